```python
import jax, jax.numpy as jnp
from jax import lax
import numpy as np

D_MODEL = 1024
BATCH = 8
SEQ = 2048
DEPTH = 1

CONV_DIM = 512
CONV_WIDTH = 31
GLA_HEADS = 4
GLA_DK = 128
GLA_DV = 256
GLA_LOWRANK = 16
GLA_TAU = 16.0
GLA_CHUNK = 64
QK_DIM = GLA_HEADS * GLA_DK
V_DIM = GLA_HEADS * GLA_DV
N_BRANCH = 2
N_GROUPS = 4
EXPERTS_PER_GROUP = 8
N_EXPERTS = N_GROUPS * EXPERTS_PER_GROUP
TOP_K = 2
D_EXPERT = 512
EPS = 1e-6
N_MOD = 6
IN_SIZES = (2 * CONV_DIM, QK_DIM, QK_DIM, V_DIM, V_DIM, GLA_LOWRANK, N_BRANCH * D_MODEL)
IN_DIM = 2 * CONV_DIM + 2 * QK_DIM + 2 * V_DIM + GLA_LOWRANK + N_BRANCH * D_MODEL

kernel_name = "hybrid_conv_gla_hmoe_adaln_block"


def rmsnorm(x, g):
    x32 = x.astype(jnp.float32)
    y = x32 * lax.rsqrt(jnp.mean(x32 * x32, axis=-1, keepdims=True) + EPS)
    return (y * g.astype(jnp.float32)).astype(x.dtype)


def layernorm(x, g, b):
    x32 = x.astype(jnp.float32)
    mu = jnp.mean(x32, axis=-1, keepdims=True)
    xc = x32 - mu
    var = jnp.mean(xc * xc, axis=-1, keepdims=True)
    y = xc * lax.rsqrt(var + EPS) * g.astype(jnp.float32) + b.astype(jnp.float32)
    return y.astype(x.dtype)


def modulate(x, shift, scale):
    return x * (1.0 + scale[:, None, :]) + shift[:, None, :]


def conv_module(u, w_dw, b_dw, ln_g, ln_b, w_pw, b_pw):
    a, gt = jnp.split(u, 2, axis=-1)
    z = a * jax.nn.sigmoid(gt)
    z = lax.conv_general_dilated(
        z, w_dw, window_strides=(1,), padding=[(CONV_WIDTH - 1, 0)],
        dimension_numbers=("NWC", "WIO", "NWC"), feature_group_count=CONV_DIM) + b_dw
    z = jax.nn.silu(layernorm(z, ln_g, ln_b))
    return z @ w_pw + b_pw


def gla_chunked(q, k, v, logg):
    B, S, H, _ = q.shape
    n_chunks = S // GLA_CHUNK

    def to_chunks(t):
        return t.reshape(B, n_chunks, GLA_CHUNK, H, t.shape[-1]).transpose(1, 0, 3, 2, 4)

    causal = jnp.tril(jnp.ones((GLA_CHUNK, GLA_CHUNK), dtype=bool))[:, :, None]

    def step(state, inp):
        qc, kc, vc, gc = inp
        b = jnp.cumsum(gc, axis=2)
        o_inter = jnp.einsum("bhcd,bhde->bhce", qc * jnp.exp(b), state)
        diff = b[:, :, :, None, :] - b[:, :, None, :, :]
        decay = jnp.exp(jnp.where(causal, diff, -jnp.inf))
        attn = jnp.einsum("bhid,bhijd,bhjd->bhij", qc, decay, kc)
        o_intra = jnp.einsum("bhij,bhje->bhie", attn, vc)
        b_last = b[:, :, -1:, :]
        new_state = (jnp.exp(b_last[:, :, 0, :])[..., None] * state
                     + jnp.einsum("bhcd,bhce->bhde", kc * jnp.exp(b_last - b), vc))
        return new_state, o_inter + o_intra

    state0 = jnp.zeros((B, H, q.shape[-1], v.shape[-1]), jnp.float32)
    _, outs = lax.scan(step, state0, (to_chunks(q), to_chunks(k), to_chunks(v), to_chunks(logg)))
    return outs.transpose(1, 0, 3, 2, 4).reshape(B, S, H, v.shape[-1])


def hybrid_mixer(u, w_in, w_dw, b_dw, g_conv_ln, b_conv_ln, w_conv_pw, b_conv_pw,
                 w_a2, b_a2, g_gla_norm, w_gla_o, w_out):
    B, S, _ = u.shape
    proj = u @ w_in
    offs = np.cumsum(IN_SIZES)[:-1].tolist()
    p_conv, q, k, v, r, a1, gates = jnp.split(proj, offs, axis=-1)
    y_conv = conv_module(p_conv, w_dw, b_dw, g_conv_ln, b_conv_ln, w_conv_pw, b_conv_pw)
    q = q.reshape(B, S, GLA_HEADS, GLA_DK).astype(jnp.float32) * (GLA_DK ** -0.5)
    k = k.reshape(B, S, GLA_HEADS, GLA_DK).astype(jnp.float32)
    v = v.reshape(B, S, GLA_HEADS, GLA_DV).astype(jnp.float32)
    logg = jax.nn.log_sigmoid((a1 @ w_a2 + b_a2).astype(jnp.float32)) / GLA_TAU
    logg = logg.reshape(B, S, GLA_HEADS, GLA_DK)
    o = gla_chunked(q, k, v, logg)
    o = rmsnorm(o, g_gla_norm.reshape(GLA_HEADS, GLA_DV)).reshape(B, S, V_DIM).astype(u.dtype)
    y_gla = (o * jax.nn.silu(r)) @ w_gla_o
    gt = jax.nn.sigmoid(gates).reshape(B, S, N_BRANCH, D_MODEL)
    merged = gt[:, :, 0, :] * y_conv + gt[:, :, 1, :] * y_gla
    return merged @ w_out


def hierarchical_moe(h, w_rg, b_rg, w_re, b_re, w1, w3, w2):
    B, S, D = h.shape
    t = h.reshape(B * S, D)
    lg = (t @ w_rg + b_rg).astype(jnp.float32)
    pg = jax.nn.softmax(lg, axis=-1)
    _, gsel = lax.top_k(lg, 1)
    wg = jnp.take_along_axis(pg, gsel, axis=1)
    le = (t @ w_re + b_re).astype(jnp.float32).reshape(-1, N_GROUPS, EXPERTS_PER_GROUP)
    le_sel = jnp.take_along_axis(le, gsel[:, :, None], axis=1)[:, 0, :]
    top_v, top_i = lax.top_k(le_sel, TOP_K)
    w = jax.nn.softmax(top_v, axis=-1) * wg
    eid = gsel * EXPERTS_PER_GROUP + top_i
    comb = jnp.einsum("tk,tke->te", w, jax.nn.one_hot(eid, N_EXPERTS, dtype=jnp.float32))
    y = jnp.zeros((B * S, D), jnp.float32)
    for e in range(N_EXPERTS):
        hid = jax.nn.silu(t @ w1[e]) * (t @ w3[e])
        y = y + comb[:, e:e + 1] * (hid @ w2[e]).astype(jnp.float32)
    return y.astype(h.dtype).reshape(B, S, D)


def setup_inputs(seed: int = 0) -> dict:
    key = jax.random.key(seed)
    ks = jax.random.split(key, 32)
    f32 = jnp.float32
    L, D = DEPTH, D_MODEL

    def nrm(k, shape, scale):
        return jax.random.normal(k, shape, f32) * scale

    def gain(k, shape):
        return 1.0 + 0.05 * jax.random.normal(k, shape, f32)

    return {
        "x": jax.random.normal(ks[0], (BATCH, SEQ, D), f32),
        "c": jax.random.normal(ks[1], (BATCH, D), f32),
        "w_ada": nrm(ks[2], (L, D, N_MOD * D), 0.5 * D ** -0.5),
        "b_ada": nrm(ks[3], (L, N_MOD * D), 0.02),
        "g_norm1": gain(ks[4], (L, D)),
        "w_in": nrm(ks[5], (L, D, IN_DIM), D ** -0.5),
        "w_dw": nrm(ks[6], (L, CONV_WIDTH, 1, CONV_DIM), CONV_WIDTH ** -0.5),
        "b_dw": nrm(ks[7], (L, CONV_DIM), 0.02),
        "g_conv_ln": gain(ks[8], (L, CONV_DIM)),
        "b_conv_ln": nrm(ks[9], (L, CONV_DIM), 0.02),
        "w_conv_pw": nrm(ks[10], (L, CONV_DIM, D), CONV_DIM ** -0.5),
        "b_conv_pw": nrm(ks[11], (L, D), 0.02),
        "w_a2": nrm(ks[12], (L, GLA_LOWRANK, QK_DIM), GLA_LOWRANK ** -0.5),
        "b_a2": nrm(ks[13], (L, QK_DIM), 0.02),
        "g_gla_norm": gain(ks[14], (L, V_DIM)),
        "w_gla_o": nrm(ks[15], (L, V_DIM, D), V_DIM ** -0.5),
        "w_out": nrm(ks[16], (L, D, D), D ** -0.5),
        "g_norm2": gain(ks[17], (L, D)),
        "w_router_g": nrm(ks[18], (L, D, N_GROUPS), D ** -0.5),
        "b_router_g": nrm(ks[19], (L, N_GROUPS), 0.01),
        "w_router_e": nrm(ks[20], (L, D, N_EXPERTS), D ** -0.5),
        "b_router_e": nrm(ks[21], (L, N_EXPERTS), 0.01),
        "w1": nrm(ks[22], (L, N_EXPERTS, D, D_EXPERT), D ** -0.5),
        "w3": nrm(ks[23], (L, N_EXPERTS, D, D_EXPERT), D ** -0.5),
        "w2": nrm(ks[24], (L, N_EXPERTS, D_EXPERT, D), D_EXPERT ** -0.5),
        "w_ada_f": nrm(ks[25], (D, 2 * D), 0.5 * D ** -0.5),
        "b_ada_f": nrm(ks[26], (2 * D,), 0.02),
        "g_final": gain(ks[27], (D,)),
    }


def reference(x, c, w_ada, b_ada, g_norm1, w_in, w_dw, b_dw, g_conv_ln, b_conv_ln,
              w_conv_pw, b_conv_pw, w_a2, b_a2, g_gla_norm, w_gla_o, w_out, g_norm2,
              w_router_g, b_router_g, w_router_e, b_router_e, w1, w3, w2,
              w_ada_f, b_ada_f, g_final):
    h = x
    c_act = jax.nn.silu(c)
    for l in range(DEPTH):
        mod = c_act @ w_ada[l] + b_ada[l]
        sh1, sc1, ga1, sh2, sc2, ga2 = jnp.split(mod, N_MOD, axis=-1)
        u = modulate(rmsnorm(h, g_norm1[l]), sh1, sc1)
        y = hybrid_mixer(u, w_in[l], w_dw[l], b_dw[l], g_conv_ln[l], b_conv_ln[l],
                         w_conv_pw[l], b_conv_pw[l], w_a2[l], b_a2[l], g_gla_norm[l],
                         w_gla_o[l], w_out[l])
        h = h + ga1[:, None, :] * y
        u2 = modulate(rmsnorm(h, g_norm2[l]), sh2, sc2)
        y2 = hierarchical_moe(u2, w_router_g[l], b_router_g[l], w_router_e[l], b_router_e[l],
                              w1[l], w3[l], w2[l])
        h = h + ga2[:, None, :] * y2
    modf = c_act @ w_ada_f + b_ada_f
    shf, scf = jnp.split(modf, 2, axis=-1)
    return modulate(rmsnorm(h, g_final), shf, scf)
```

```python
import functools

import jax
import jax.numpy as jnp
from jax import lax
from jax.experimental import pallas as pl
from jax.experimental.pallas import tpu as pltpu

F32 = jnp.float32
BF16 = jnp.bfloat16

EPS = 1e-6
CONV_DIM = 512
CONV_WIDTH = 31
GLA_HEADS = 4
GLA_DK = 128
GLA_DV = 256
GLA_LOWRANK = 16
GLA_TAU = 16.0
QK_DIM = GLA_HEADS * GLA_DK
V_DIM = GLA_HEADS * GLA_DV
N_GROUPS = 4
EXPERTS_PER_GROUP = 8
N_EXPERTS = N_GROUPS * EXPERTS_PER_GROUP
TOP_K = 2
N_MOD = 6

LANES = 128
CONV_HALO = 32
GLA_CHUNK = 128
VMEM_LIMIT = 56 * 1024 * 1024


def _bdot(a, b):
    return jnp.dot(a, b, preferred_element_type=F32)


def _split(a):
    hi = a.astype(BF16)
    lo = (a - hi.astype(F32)).astype(BF16)
    return hi, lo


def _dot3(a, b):
    ah, al = _split(a)
    bh, bl = _split(b)
    return _bdot(ah, bh) + (_bdot(ah, bl) + _bdot(al, bh))


def _sigmoid(x):
    return 1.0 / (1.0 + jnp.exp(-x))


def _rms(x, g):
    ms = jnp.mean(x * x, axis=-1, keepdims=True)
    return x * lax.rsqrt(ms + EPS) * g


def _params(*sem):
    return pltpu.CompilerParams(dimension_semantics=sem, vmem_limit_bytes=VMEM_LIMIT)


def _ada_body(c_ref, w_ref, b_ref, o_ref):
    c = c_ref[...]
    o_ref[...] = _dot3(c * _sigmoid(c), w_ref[...]) + b_ref[...]


def _ada(c, w, b, tn=1024):
    nb, d = c.shape
    n = w.shape[1]
    return pl.pallas_call(
        _ada_body,
        grid=(n // tn,),
        in_specs=[pl.BlockSpec((nb, d), lambda j: (0, 0)),
                  pl.BlockSpec((d, tn), lambda j: (0, j)),
                  pl.BlockSpec((1, tn), lambda j: (0, j))],
        out_specs=pl.BlockSpec((nb, tn), lambda j: (0, j)),
        out_shape=jax.ShapeDtypeStruct((nb, n), F32),
        compiler_params=_params("arbitrary"),
        name="ada",
    )(c, w, b.reshape(1, n))


def _proj_body(x_ref, mod_ref, g1_ref, wm_ref, wa1_ref, wg_ref, wa2_ref, ba2_ref,
               z_ref, q_ref, k_ref, v_ref, rs_ref, lg_ref, gt_ref):
    x = x_ref[0]
    u = (_rms(x, g1_ref[...]) * (1.0 + mod_ref[0, 1:2, :]) + mod_ref[0, 0:1, :]).astype(BF16)
    c0 = 2 * CONV_DIM
    c1 = c0 + 2 * QK_DIM
    c2 = c1 + V_DIM
    c3 = c2 + V_DIM
    pc = _bdot(u, wm_ref[:, 0:c0])
    z_ref[0] = pc[:, :CONV_DIM] * _sigmoid(pc[:, CONV_DIM:])
    qk = _bdot(u, wm_ref[:, c0:c1])
    q_ref[0] = qk[:, :QK_DIM] * (GLA_DK ** -0.5)
    k_ref[0] = qk[:, QK_DIM:]
    v_ref[0] = _bdot(u, wm_ref[:, c1:c2]).astype(BF16)
    r = _bdot(u, wm_ref[:, c2:c3])
    rs_ref[0] = (r * _sigmoid(r)).astype(BF16)
    a1 = _bdot(u, wa1_ref[...])
    xg = _dot3(a1, wa2_ref[...]) + ba2_ref[...]
    lg_ref[0] = (jnp.minimum(xg, 0.0) - jnp.log1p(jnp.exp(-jnp.abs(xg)))) * (1.0 / GLA_TAU)
    gt_ref[0] = _sigmoid(_bdot(u, wg_ref[...])).astype(BF16)


def _proj(x, mod3, g1, w_in, w_a2, b_a2, tm):
    nb, s, d = x.shape
    c3 = 2 * CONV_DIM + 2 * QK_DIM + 2 * V_DIM
    wm = w_in[:, :c3].astype(BF16)
    wa1 = jnp.pad(w_in[:, c3:c3 + GLA_LOWRANK], ((0, 0), (0, LANES - GLA_LOWRANK))).astype(BF16)
    wg = w_in[:, c3 + GLA_LOWRANK:].astype(BF16)
    wa2 = jnp.pad(w_a2, ((0, LANES - GLA_LOWRANK), (0, 0)))
    ng = wg.shape[1]

    def tok(width):
        return pl.BlockSpec((1, tm, width), lambda b, i: (b, i, 0))

    def const(shape):
        return pl.BlockSpec(shape, lambda b, i: (0,) * len(shape))

    def out(width, dt):
        return jax.ShapeDtypeStruct((nb, s, width), dt)

    return pl.pallas_call(
        _proj_body,
        grid=(nb, s // tm),
        in_specs=[tok(d),
                  pl.BlockSpec((1, N_MOD, d), lambda b, i: (b, 0, 0)),
                  const((1, d)), const(wm.shape), const(wa1.shape), const(wg.shape),
                  const(wa2.shape), const((1, QK_DIM))],
        out_specs=[tok(CONV_DIM), tok(QK_DIM), tok(QK_DIM), tok(V_DIM), tok(V_DIM), tok(QK_DIM), tok(ng)],
        out_shape=[out(CONV_DIM, F32), out(QK_DIM, F32), out(QK_DIM, F32), out(V_DIM, BF16),
                   out(V_DIM, BF16), out(QK_DIM, F32), out(ng, BF16)],
        compiler_params=_params("arbitrary", "arbitrary"),
        name="proj",
    )(x, mod3, g1.reshape(1, d), wm, wa1, wg, wa2, b_a2.reshape(1, QK_DIM))


def _gla_body(q_ref, k_ref, lg_ref, v_ref, rs_ref, gn_ref, o_ref, st_ref, *, n_chunks):
    cl = GLA_CHUNK

    @pl.when(pl.program_id(1) == 0)
    def _():
        st_ref[...] = jnp.zeros_like(st_ref)

    row = lax.broadcasted_iota(jnp.int32, (cl, cl), 0)
    col = lax.broadcasted_iota(jnp.int32, (cl, cl), 1)
    causal = col <= row
    tri = jnp.where(causal, 1.0, 0.0).astype(BF16)

    for h in range(GLA_HEADS):
        ks = slice(h * GLA_DK, (h + 1) * GLA_DK)
        vs = slice(h * GLA_DV, (h + 1) * GLA_DV)
        state = st_ref[h]
        for c in range(n_chunks):
            rows = slice(c * cl, (c + 1) * cl)
            gh, gl = _split(lg_ref[0, rows, ks])
            b = _bdot(tri, gh) + _bdot(tri, gl)
            q = q_ref[0, rows, ks]
            v = v_ref[0, rows, vs]
            qe = (q * jnp.exp(b)).astype(BF16)
            kt = k_ref[0, rows, ks].T
            bt = b.T
            bl = bt[:, cl - 1:cl]
            ket = (kt * jnp.exp(-bt)).astype(BF16)
            klt = (kt * jnp.exp(bl - bt)).astype(BF16)
            att = jnp.where(causal, _bdot(qe, ket), 0.0).astype(BF16)
            o = _bdot(att, v) + _bdot(qe, state.astype(BF16))
            state = jnp.exp(bl) * state + _bdot(klt, v)
            on = _rms(o, gn_ref[:, vs])
            o_ref[0, rows, vs] = (on * rs_ref[0, rows, vs].astype(F32)).astype(BF16)
        st_ref[h] = state


def _gla(q, k, lg, v, rs, gn, tc):
    nb, s, _ = q.shape

    def tok(width):
        return pl.BlockSpec((1, tc, width), lambda b, i: (b, i, 0))

    return pl.pallas_call(
        functools.partial(_gla_body, n_chunks=tc // GLA_CHUNK),
        grid=(nb, s // tc),
        in_specs=[tok(QK_DIM), tok(QK_DIM), tok(QK_DIM), tok(V_DIM), tok(V_DIM),
                  pl.BlockSpec((1, V_DIM), lambda b, i: (0, 0))],
        out_specs=tok(V_DIM),
        out_shape=jax.ShapeDtypeStruct((nb, s, V_DIM), BF16),
        scratch_shapes=[pltpu.VMEM((GLA_HEADS, GLA_DK, GLA_DV), F32)],
        compiler_params=_params("arbitrary", "arbitrary"),
        name="gla",
    )(q, k, lg, v, rs, gn.reshape(1, V_DIM))


def _route(logits, cnt):
    ts = logits.shape[0]
    lane = lax.broadcasted_iota(jnp.int32, (ts, LANES), 1).astype(F32)
    ninf = -jnp.inf
    lgm = jnp.where(lane < N_GROUPS, logits, ninf)
    gmax = jnp.max(lgm, axis=-1, keepdims=True)
    gsel = jnp.min(jnp.where(lgm == gmax, lane, float(LANES)), axis=-1, keepdims=True)
    wg = 1.0 / jnp.sum(jnp.exp(lgm - gmax), axis=-1, keepdims=True)
    base = N_GROUPS + EXPERTS_PER_GROUP * gsel
    le = jnp.where(lane >= base, jnp.where(lane < base + EXPERTS_PER_GROUP, logits, ninf), ninf)
    v1 = jnp.max(le, axis=-1, keepdims=True)
    i1 = jnp.min(jnp.where(le == v1, lane, float(LANES)), axis=-1, keepdims=True)
    le2 = jnp.where(lane == i1, ninf, le)
    v2 = jnp.max(le2, axis=-1, keepdims=True)
    i2 = jnp.min(jnp.where(le2 == v2, lane, float(LANES)), axis=-1, keepdims=True)
    e21 = jnp.exp(v2 - v1)
    w1 = wg / (1.0 + e21)
    w2 = w1 * e21
    eid1 = i1 - N_GROUPS
    eid2 = i2 - N_GROUPS
    oh1 = jnp.where(lane == eid1, 1.0, 0.0)
    oh2 = jnp.where(lane == eid2, 1.0, 0.0)
    ohs = oh1 + oh2
    row = lax.broadcasted_iota(jnp.int32, (ts, ts), 0)
    col = lax.broadcasted_iota(jnp.int32, (ts, ts), 1)
    before = jnp.where(col < row, 1.0, 0.0).astype(BF16)
    tot = cnt + _bdot(before, ohs.astype(BF16))
    rank1 = jnp.sum(oh1 * tot, axis=-1, keepdims=True)
    rank2 = jnp.sum(oh2 * tot, axis=-1, keepdims=True)
    packed = jnp.where(lane == 0.0, eid1,
             jnp.where(lane == 1.0, eid2,
             jnp.where(lane == 2.0, w1,
             jnp.where(lane == 3.0, w2,
             jnp.where(lane == 4.0, rank1,
             jnp.where(lane == 5.0, rank2, 0.0))))))
    return packed, cnt + jnp.sum(ohs, axis=0, keepdims=True)


def _merge_body(z_ref, og_ref, gt_ref, x_ref, mod_ref, wdw_ref, bdw_ref, lng_ref, lnb_ref,
                wpw_ref, bpw_ref, wgo_ref, wout_ref, g2_ref, wr_ref, br_ref,
                h_ref, u2_ref, route_ref, cnt_ref, zbuf, cnt_sc):
    ts = z_ref.shape[1]
    d = x_ref.shape[2]
    first_tile = pl.program_id(1) == 0

    @pl.when(first_tile)
    def _():
        zbuf[0:CONV_HALO, :] = jnp.zeros((CONV_HALO, CONV_DIM), F32)

    @pl.when(jnp.logical_and(first_tile, pl.program_id(0) == 0))
    def _():
        cnt_sc[...] = jnp.zeros_like(cnt_sc)

    zbuf[CONV_HALO:CONV_HALO + ts, :] = z_ref[0]
    off = CONV_HALO - (CONV_WIDTH - 1)
    acc = zbuf[off:off + ts, :] * wdw_ref[0:1, :]
    for j in range(1, CONV_WIDTH):
        acc = acc + zbuf[off + j:off + j + ts, :] * wdw_ref[j:j + 1, :]
    conv = acc + bdw_ref[...]
    zbuf[0:CONV_HALO, :] = zbuf[ts:ts + CONV_HALO, :]

    mu = jnp.mean(conv, axis=-1, keepdims=True)
    xc = conv - mu
    var = jnp.mean(xc * xc, axis=-1, keepdims=True)
    ln = xc * lax.rsqrt(var + EPS) * lng_ref[...] + lnb_ref[...]
    y_conv = _bdot((ln * _sigmoid(ln)).astype(BF16), wpw_ref[...]) + bpw_ref[...]
    y_gla = _bdot(og_ref[0], wgo_ref[...])
    merged = gt_ref[0, :, 0:d].astype(F32) * y_conv + gt_ref[0, :, d:2 * d].astype(F32) * y_gla
    y = _bdot(merged.astype(BF16), wout_ref[...])
    h = x_ref[0] + mod_ref[0, 2:3, :] * y
    h_ref[0] = h
    u2 = _rms(h, g2_ref[...]) * (1.0 + mod_ref[0, 4:5, :]) + mod_ref[0, 3:4, :]
    u2_ref[0] = u2
    logits = _dot3(u2, wr_ref[...]) + br_ref[...]
    packed, cnt = _route(logits, cnt_sc[...])
    route_ref[0] = packed
    cnt_sc[...] = cnt
    cnt_ref[...] = jnp.broadcast_to(cnt, cnt_ref.shape)


def _merge(z, og, gt, x, mod3, w_dw, b_dw, ln_g, ln_b, w_pw, b_pw, w_go, w_out, g2, w_rg, b_rg, w_re, b_re, ts):
    nb, s, d = x.shape
    npad = LANES - N_GROUPS - N_EXPERTS
    wr = jnp.pad(jnp.concatenate([w_rg, w_re], axis=1), ((0, 0), (0, npad)))
    br = jnp.pad(jnp.concatenate([b_rg, b_re]), (0, npad)).reshape(1, LANES)

    def tok(width):
        return pl.BlockSpec((1, ts, width), lambda b, i: (b, i, 0))

    def const(shape):
        return pl.BlockSpec(shape, lambda b, i: (0,) * len(shape))

    def row(v):
        return v.reshape(1, v.shape[-1])

    return pl.pallas_call(
        _merge_body,
        grid=(nb, s // ts),
        in_specs=[tok(CONV_DIM), tok(V_DIM), tok(2 * d), tok(d),
                  pl.BlockSpec((1, N_MOD, d), lambda b, i: (b, 0, 0)),
                  const((CONV_WIDTH, CONV_DIM)), const((1, CONV_DIM)), const((1, CONV_DIM)), const((1, CONV_DIM)),
                  const((CONV_DIM, d)), const((1, d)), const((V_DIM, d)), const((d, d)), const((1, d)),
                  const((d, LANES)), const((1, LANES))],
        out_specs=[tok(d), tok(d), tok(LANES), pl.BlockSpec((8, LANES), lambda b, i: (0, 0))],
        out_shape=[jax.ShapeDtypeStruct((nb, s, d), F32), jax.ShapeDtypeStruct((nb, s, d), F32),
                   jax.ShapeDtypeStruct((nb, s, LANES), F32), jax.ShapeDtypeStruct((8, LANES), F32)],
        scratch_shapes=[pltpu.VMEM((CONV_HALO + ts, CONV_DIM), F32), pltpu.VMEM((1, LANES), F32)],
        compiler_params=_params("arbitrary", "arbitrary"),
        name="merge",
    )(z, og, gt, x, mod3, w_dw.reshape(CONV_WIDTH, CONV_DIM), row(b_dw), row(ln_g), row(ln_b),
      w_pw.astype(BF16), row(b_pw), w_go.astype(BF16), w_out.astype(BF16), row(g2), wr, br)


def _row_copy(src, i, dst, j, sem):
    return pltpu.make_async_copy(src.at[pl.ds(i, 1), :], dst.at[pl.ds(j, 1), :], sem)


def _dispatch_body(p_ref, u2_ref, xs_in_ref, xs_ref, sem):
    del xs_in_ref
    ts = u2_ref.shape[0]

    def issue(r, carry):
        _row_copy(u2_ref, r, xs_ref, p_ref[0, 0, r], sem).start()
        _row_copy(u2_ref, r, xs_ref, p_ref[0, 0, ts + r], sem).start()
        return carry

    lax.fori_loop(0, ts, issue, 0)

    def drain(r, carry):
        _row_copy(u2_ref, 0, xs_ref, 0, sem).wait()
        _row_copy(u2_ref, 0, xs_ref, 0, sem).wait()
        return carry

    lax.fori_loop(0, ts, drain, 0)


def _dispatch(slots, u2, n_rows, ts):
    t, d = u2.shape
    return pl.pallas_call(
        _dispatch_body,
        grid=(t // ts,),
        in_specs=[pl.BlockSpec((1, 1, 2 * ts), lambda i: (i, 0, 0), memory_space=pltpu.SMEM),
                  pl.BlockSpec((ts, d), lambda i: (i, 0)),
                  pl.BlockSpec(memory_space=pl.ANY)],
        out_specs=pl.BlockSpec(memory_space=pl.ANY),
        out_shape=jax.ShapeDtypeStruct((n_rows, d), F32),
        scratch_shapes=[pltpu.SemaphoreType.DMA(())],
        input_output_aliases={2: 0},
        compiler_params=_params("arbitrary"),
        name="dispatch",
    )(slots, u2, jnp.zeros((n_rows, d), F32))


def _expert_body(te_ref, nu_ref, xs_ref, w1_ref, w3_ref, w2_ref, ys_ref, w1b, w3b, w2b):
    i = pl.program_id(0)
    changed = jnp.logical_or(i == 0, te_ref[i] != te_ref[jnp.maximum(i - 1, 0)])

    @pl.when(changed)
    def _():
        w1b[...] = w1_ref[0].astype(BF16)
        w3b[...] = w3_ref[0].astype(BF16)
        w2b[...] = w2_ref[0].astype(BF16)

    @pl.when(i < nu_ref[0])
    def _():
        x = xs_ref[...].astype(BF16)
        h1 = _bdot(x, w1b[...])
        h3 = _bdot(x, w3b[...])
        hid = (h1 * _sigmoid(h1) * h3).astype(BF16)
        ys_ref[...] = _bdot(hid, w2b[...])

    @pl.when(i >= nu_ref[0])
    def _():
        ys_ref[...] = jnp.zeros_like(ys_ref)


def _experts(tile_expert, n_used, xs, w1, w3, w2, tmx):
    n_rows, d = xs.shape
    ne, _, f = w1.shape

    def tile_map(i, te, nu):
        return (jnp.minimum(i, nu[0] - 1), 0)

    def w_map(i, te, nu):
        return (te[i], 0, 0)

    return pl.pallas_call(
        _expert_body,
        grid_spec=pltpu.PrefetchScalarGridSpec(
            num_scalar_prefetch=2,
            grid=(n_rows // tmx,),
            in_specs=[pl.BlockSpec((tmx, d), tile_map),
                      pl.BlockSpec((1, d, f), w_map), pl.BlockSpec((1, d, f), w_map),
                      pl.BlockSpec((1, f, d), w_map)],
            out_specs=pl.BlockSpec((tmx, d), lambda i, te, nu: (i, 0)),
            scratch_shapes=[pltpu.VMEM((d, f), BF16), pltpu.VMEM((d, f), BF16), pltpu.VMEM((f, d), BF16)]),
        out_shape=jax.ShapeDtypeStruct((n_rows, d), F32),
        compiler_params=_params("arbitrary"),
        name="experts",
    )(tile_expert, n_used, xs, w1, w3, w2)


def _final_body(p_ref, h_ref, route_ref, mod_ref, modf_ref, gf_ref, ys_ref, o_ref, y1_buf, y2_buf, sem):
    ts = h_ref.shape[1]

    def issue(r, carry):
        _row_copy(ys_ref, p_ref[0, 0, r], y1_buf, r, sem).start()
        _row_copy(ys_ref, p_ref[0, 0, ts + r], y2_buf, r, sem).start()
        return carry

    lax.fori_loop(0, ts, issue, 0)

    def drain(r, carry):
        _row_copy(ys_ref, 0, y1_buf, 0, sem).wait()
        _row_copy(ys_ref, 0, y2_buf, 0, sem).wait()
        return carry

    lax.fori_loop(0, ts, drain, 0)

    route = route_ref[0]
    y2 = route[:, 2:3] * y1_buf[...] + route[:, 3:4] * y2_buf[...]
    h = h_ref[0] + mod_ref[0, 5:6, :] * y2
    o_ref[0] = _rms(h, gf_ref[...]) * (1.0 + modf_ref[0, 1:2, :]) + modf_ref[0, 0:1, :]


def _final(slots, h, route, mod3, modf3, gf, ys, ts):
    nb, s, d = h.shape
    nt = s // ts

    def tok(width):
        return pl.BlockSpec((1, ts, width), lambda b, i: (b, i, 0))

    return pl.pallas_call(
        _final_body,
        grid=(nb, nt),
        in_specs=[pl.BlockSpec((1, 1, 2 * ts), lambda b, i: (b * nt + i, 0, 0), memory_space=pltpu.SMEM),
                  tok(d), tok(LANES),
                  pl.BlockSpec((1, N_MOD, d), lambda b, i: (b, 0, 0)),
                  pl.BlockSpec((1, 2, d), lambda b, i: (b, 0, 0)),
                  pl.BlockSpec((1, d), lambda b, i: (0, 0)),
                  pl.BlockSpec(memory_space=pl.ANY)],
        out_specs=tok(d),
        out_shape=jax.ShapeDtypeStruct((nb, s, d), F32),
        scratch_shapes=[pltpu.VMEM((ts, d), F32), pltpu.VMEM((ts, d), F32), pltpu.SemaphoreType.DMA(())],
        compiler_params=_params("arbitrary", "arbitrary"),
        name="final",
    )(slots, h, route, mod3, modf3, gf.reshape(1, d), ys)


def _plan(route, cnt, ts, tmx, n_tiles):
    t = route.shape[0]
    counts = cnt[0, :N_EXPERTS].astype(jnp.int32)
    tiles = (counts + (tmx - 1)) // tmx
    tile_end = jnp.cumsum(tiles)
    offs = (tile_end - tiles) * tmx
    n_used = tile_end[-1:]
    tile_ids = jnp.minimum(jnp.arange(n_tiles, dtype=jnp.int32), n_used[0] - 1)
    tile_expert = jnp.sum((tile_ids[:, None] >= tile_end[None, :]).astype(jnp.int32), axis=1)
    eid = route[:, 0:2].astype(jnp.int32)
    rank = route[:, 4:6].astype(jnp.int32)
    slot = jnp.take(offs, eid) + rank
    slots = slot.reshape(t // ts, ts, 2).transpose(0, 2, 1).reshape(t // ts, 1, 2 * ts)
    return slots, tile_expert.astype(jnp.int32), n_used.astype(jnp.int32)


def kernel(x, c, w_ada, b_ada, g_norm1, w_in, w_dw, b_dw, g_conv_ln, b_conv_ln, w_conv_pw, b_conv_pw,
           w_a2, b_a2, g_gla_norm, w_gla_o, w_out, g_norm2, w_router_g, b_router_g, w_router_e,
           b_router_e, w1, w3, w2, w_ada_f, b_ada_f, g_final):
    nb, s, d = x.shape
    assert w_ada.shape[0] == 1, "single-layer block"
    tm = min(512, s)
    tc = min(256, s)
    ts = min(256, s)
    tmx = 256
    t = nb * s
    n_tiles = (t * TOP_K) // tmx + N_EXPERTS
    mod3 = _ada(c, w_ada[0], b_ada[0]).reshape(nb, N_MOD, d)
    modf3 = _ada(c, w_ada_f, b_ada_f).reshape(nb, 2, d)
    z, q, k, v, rs, lg, gt = _proj(x, mod3, g_norm1[0], w_in[0], w_a2[0], b_a2[0], tm)
    og = _gla(q, k, lg, v, rs, g_gla_norm[0], tc)
    h, u2, route, cnt = _merge(z, og, gt, x, mod3, w_dw[0], b_dw[0], g_conv_ln[0], b_conv_ln[0],
                               w_conv_pw[0], b_conv_pw[0], w_gla_o[0], w_out[0], g_norm2[0],
                               w_router_g[0], b_router_g[0], w_router_e[0], b_router_e[0], ts)
    route2 = route.reshape(t, LANES)
    slots, tile_expert, n_used = _plan(route2, cnt, ts, tmx, n_tiles)
    xs = _dispatch(slots, u2.reshape(t, d), n_tiles * tmx, ts)
    ys = _experts(tile_expert, n_used, xs, w1[0], w3[0], w2[0], tmx)
    return _final(slots, h, route, mod3, modf3, g_final, ys, ts)
```

```python
import functools

import jax
import jax.numpy as jnp
from jax import lax
from jax.experimental import pallas as pl
from jax.experimental.pallas import tpu as pltpu

F32 = jnp.float32
BF16 = jnp.bfloat16

EPS = 1e-6
CONV_DIM = 512
CONV_WIDTH = 31
GLA_HEADS = 4
GLA_DK = 128
GLA_DV = 256
GLA_LOWRANK = 16
GLA_TAU = 16.0
QK_DIM = GLA_HEADS * GLA_DK
V_DIM = GLA_HEADS * GLA_DV
N_GROUPS = 4
EXPERTS_PER_GROUP = 8
N_EXPERTS = N_GROUPS * EXPERTS_PER_GROUP
TOP_K = 2
N_MOD = 6

LANES = 128
SUBLANES = 8
CONV_ROWS = 32
CONV_HALO = 32
GLA_CHUNK = 128
VMEM_LIMIT = 56 * 1024 * 1024


def _bdot(a, b):
    return jnp.dot(a, b, preferred_element_type=F32)


def _split(a):
    hi = a.astype(BF16)
    lo = (a - hi.astype(F32)).astype(BF16)
    return hi, lo


def _dot3(a, b):
    ah, al = _split(a)
    bh, bl = _split(b)
    return _bdot(ah, bh) + (_bdot(ah, bl) + _bdot(al, bh))


def _sigmoid(x):
    return 1.0 / (1.0 + jnp.exp(-x))


def _rms(x, g):
    ms = jnp.mean(x * x, axis=-1, keepdims=True)
    return x * lax.rsqrt(ms + EPS) * g


def _rows_to_tiles(ref, val):
    n = val.shape[0]
    for j in range(val.shape[1] // LANES):
        ref[pl.ds(j, n, stride=SUBLANES), :] = val[:, j * LANES:(j + 1) * LANES]


def _tiles_to_rows(ref, n):
    return jnp.concatenate([ref[pl.ds(j, n, stride=SUBLANES), :] for j in range(SUBLANES)], axis=-1)


def _params(*sem):
    return pltpu.CompilerParams(dimension_semantics=sem, vmem_limit_bytes=VMEM_LIMIT)


def _ada_body(c_ref, w_ref, b_ref, o_ref):
    c = c_ref[...]
    o_ref[...] = _dot3(c * _sigmoid(c), w_ref[...]) + b_ref[...]


def _ada(c, w, b, tn=1024):
    nb, d = c.shape
    n = w.shape[1]
    return pl.pallas_call(
        _ada_body,
        grid=(n // tn,),
        in_specs=[pl.BlockSpec((nb, d), lambda j: (0, 0)),
                  pl.BlockSpec((d, tn), lambda j: (0, j)),
                  pl.BlockSpec((1, tn), lambda j: (0, j))],
        out_specs=pl.BlockSpec((nb, tn), lambda j: (0, j)),
        out_shape=jax.ShapeDtypeStruct((nb, n), F32),
        compiler_params=_params("arbitrary"),
        name="ada",
    )(c, w, b.reshape(1, n))


def _proj_body(x_ref, mod_ref, g1_ref, wm_ref, wa1_ref, wg_ref, wa2_ref, ba2_ref,
               z_ref, q_ref, k_ref, v_ref, rs_ref, lg_ref, gt_ref):
    x = x_ref[0]
    u = (_rms(x, g1_ref[...]) * (1.0 + mod_ref[0, 1:2, :]) + mod_ref[0, 0:1, :]).astype(BF16)
    c0 = 2 * CONV_DIM
    c1 = c0 + 2 * QK_DIM
    c2 = c1 + V_DIM
    c3 = c2 + V_DIM
    pc = _bdot(u, wm_ref[:, 0:c0])
    z_ref[0] = pc[:, :CONV_DIM] * _sigmoid(pc[:, CONV_DIM:])
    qk = _bdot(u, wm_ref[:, c0:c1])
    q_ref[0] = qk[:, :QK_DIM] * (GLA_DK ** -0.5)
    k_ref[0] = qk[:, QK_DIM:]
    v_ref[0] = _bdot(u, wm_ref[:, c1:c2]).astype(BF16)
    r = _bdot(u, wm_ref[:, c2:c3])
    rs_ref[0] = (r * _sigmoid(r)).astype(BF16)
    a1 = _bdot(u, wa1_ref[...])
    xg = _dot3(a1, wa2_ref[...]) + ba2_ref[...]
    lg_ref[0] = (jnp.minimum(xg, 0.0) - jnp.log1p(jnp.exp(-jnp.abs(xg)))) * (1.0 / GLA_TAU)
    gt_ref[0] = _sigmoid(_bdot(u, wg_ref[...])).astype(BF16)


def _proj(x, mod3, g1, w_in, w_a2, b_a2, tm):
    nb, s, d = x.shape
    c3 = 2 * CONV_DIM + 2 * QK_DIM + 2 * V_DIM
    wm = w_in[:, :c3].astype(BF16)
    wa1 = jnp.pad(w_in[:, c3:c3 + GLA_LOWRANK], ((0, 0), (0, LANES - GLA_LOWRANK))).astype(BF16)
    wg = w_in[:, c3 + GLA_LOWRANK:].astype(BF16)
    wa2 = jnp.pad(w_a2, ((0, LANES - GLA_LOWRANK), (0, 0)))
    ng = wg.shape[1]

    def tok(width):
        return pl.BlockSpec((1, tm, width), lambda b, i: (b, i, 0))

    def const(shape):
        return pl.BlockSpec(shape, lambda b, i: (0,) * len(shape))

    def out(width, dt):
        return jax.ShapeDtypeStruct((nb, s, width), dt)

    return pl.pallas_call(
        _proj_body,
        grid=(nb, s // tm),
        in_specs=[tok(d),
                  pl.BlockSpec((1, N_MOD, d), lambda b, i: (b, 0, 0)),
                  const((1, d)), const(wm.shape), const(wa1.shape), const(wg.shape),
                  const(wa2.shape), const((1, QK_DIM))],
        out_specs=[tok(CONV_DIM), tok(QK_DIM), tok(QK_DIM), tok(V_DIM), tok(V_DIM), tok(QK_DIM), tok(ng)],
        out_shape=[out(CONV_DIM, F32), out(QK_DIM, F32), out(QK_DIM, F32), out(V_DIM, BF16),
                   out(V_DIM, BF16), out(QK_DIM, F32), out(ng, BF16)],
        compiler_params=_params("arbitrary", "arbitrary"),
        name="proj",
    )(x, mod3, g1.reshape(1, d), wm, wa1, wg, wa2, b_a2.reshape(1, QK_DIM))


def _gla_body(q_ref, k_ref, lg_ref, v_ref, rs_ref, gn_ref, o_ref, st_ref, *, n_chunks):
    cl = GLA_CHUNK

    @pl.when(pl.program_id(1) == 0)
    def _():
        st_ref[...] = jnp.zeros_like(st_ref)

    row = lax.broadcasted_iota(jnp.int32, (cl, cl), 0)
    col = lax.broadcasted_iota(jnp.int32, (cl, cl), 1)
    causal = col <= row
    tri = jnp.where(causal, 1.0, 0.0).astype(BF16)

    for h in range(GLA_HEADS):
        ks = slice(h * GLA_DK, (h + 1) * GLA_DK)
        vs = slice(h * GLA_DV, (h + 1) * GLA_DV)
        state = st_ref[h]
        for c in range(n_chunks):
            rows = slice(c * cl, (c + 1) * cl)
            gh, gl = _split(lg_ref[0, rows, ks])
            b = _bdot(tri, gh) + _bdot(tri, gl)
            q = q_ref[0, rows, ks]
            v = v_ref[0, rows, vs]
            qe = (q * jnp.exp(b)).astype(BF16)
            kt = k_ref[0, rows, ks].T
            bt = b.T
            bl = bt[:, cl - 1:cl]
            ket = (kt * jnp.exp(-bt)).astype(BF16)
            klt = (kt * jnp.exp(bl - bt)).astype(BF16)
            att = jnp.where(causal, _bdot(qe, ket), 0.0).astype(BF16)
            o = _bdot(att, v) + _bdot(qe, state.astype(BF16))
            state = jnp.exp(bl) * state + _bdot(klt, v)
            on = _rms(o, gn_ref[:, vs])
            o_ref[0, rows, vs] = (on * rs_ref[0, rows, vs].astype(F32)).astype(BF16)
        st_ref[h] = state


def _gla(q, k, lg, v, rs, gn, tc):
    nb, s, _ = q.shape

    def tok(width):
        return pl.BlockSpec((1, tc, width), lambda b, i: (b, i, 0))

    return pl.pallas_call(
        functools.partial(_gla_body, n_chunks=tc // GLA_CHUNK),
        grid=(nb, s // tc),
        in_specs=[tok(QK_DIM), tok(QK_DIM), tok(QK_DIM), tok(V_DIM), tok(V_DIM),
                  pl.BlockSpec((1, V_DIM), lambda b, i: (0, 0))],
        out_specs=tok(V_DIM),
        out_shape=jax.ShapeDtypeStruct((nb, s, V_DIM), BF16),
        scratch_shapes=[pltpu.VMEM((GLA_HEADS, GLA_DK, GLA_DV), F32)],
        compiler_params=_params("arbitrary", "arbitrary"),
        name="gla",
    )(q, k, lg, v, rs, gn.reshape(1, V_DIM))


def _route(logits, cnt):
    ts = logits.shape[0]
    lane = lax.broadcasted_iota(jnp.int32, (ts, LANES), 1).astype(F32)
    ninf = -jnp.inf
    lgm = jnp.where(lane < N_GROUPS, logits, ninf)
    gmax = jnp.max(lgm, axis=-1, keepdims=True)
    gsel = jnp.min(jnp.where(lgm == gmax, lane, float(LANES)), axis=-1, keepdims=True)
    wg = 1.0 / jnp.sum(jnp.exp(lgm - gmax), axis=-1, keepdims=True)
    base = N_GROUPS + EXPERTS_PER_GROUP * gsel
    le = jnp.where(lane >= base, jnp.where(lane < base + EXPERTS_PER_GROUP, logits, ninf), ninf)
    v1 = jnp.max(le, axis=-1, keepdims=True)
    i1 = jnp.min(jnp.where(le == v1, lane, float(LANES)), axis=-1, keepdims=True)
    le2 = jnp.where(lane == i1, ninf, le)
    v2 = jnp.max(le2, axis=-1, keepdims=True)
    i2 = jnp.min(jnp.where(le2 == v2, lane, float(LANES)), axis=-1, keepdims=True)
    e21 = jnp.exp(v2 - v1)
    w1 = wg / (1.0 + e21)
    w2 = w1 * e21
    eid1 = i1 - N_GROUPS
    eid2 = i2 - N_GROUPS
    oh1 = jnp.where(lane == eid1, 1.0, 0.0)
    oh2 = jnp.where(lane == eid2, 1.0, 0.0)
    ohs = oh1 + oh2
    row = lax.broadcasted_iota(jnp.int32, (ts, ts), 0)
    col = lax.broadcasted_iota(jnp.int32, (ts, ts), 1)
    before = jnp.where(col < row, 1.0, 0.0).astype(BF16)
    tot = cnt + _bdot(before, ohs.astype(BF16))
    rank1 = jnp.sum(oh1 * tot, axis=-1, keepdims=True)
    rank2 = jnp.sum(oh2 * tot, axis=-1, keepdims=True)
    packed = jnp.where(lane == 0.0, eid1,
             jnp.where(lane == 1.0, eid2,
             jnp.where(lane == 2.0, w1,
             jnp.where(lane == 3.0, w2,
             jnp.where(lane == 4.0, rank1,
             jnp.where(lane == 5.0, rank2, 0.0))))))
    return packed, cnt + jnp.sum(ohs, axis=0, keepdims=True)


def _merge_body(z_ref, og_ref, gt_ref, x_ref, mod_ref, wdw_ref, bdw_ref, lng_ref, lnb_ref,
                wpw_ref, bpw_ref, wgo_ref, wout_ref, g2_ref, wr_ref, br_ref,
                h_ref, u2_ref, route_ref, cnt_ref, zbuf, zsh, cnt_sc):
    ts = z_ref.shape[1]
    d = x_ref.shape[2]
    first_tile = pl.program_id(1) == 0

    @pl.when(first_tile)
    def _():
        zbuf[0:CONV_HALO, :] = jnp.zeros((CONV_HALO, CONV_DIM), F32)

    @pl.when(jnp.logical_and(first_tile, pl.program_id(0) == 0))
    def _():
        cnt_sc[...] = jnp.zeros_like(cnt_sc)

    zbuf[CONV_HALO:CONV_HALO + ts, :] = z_ref[0]
    span = ts + CONV_HALO - SUBLANES
    for r in range(1, SUBLANES):
        zsh[r - 1] = zbuf[r:r + span, :]
    off = CONV_HALO - (CONV_WIDTH - 1)
    pieces = []
    for blk in range(ts // CONV_ROWS):
        acc = None
        for j in range(CONV_WIDTH):
            a, r = divmod(off + j, SUBLANES)
            lo = a * SUBLANES + blk * CONV_ROWS
            src = zbuf[lo:lo + CONV_ROWS, :] if r == 0 else zsh[r - 1, lo:lo + CONV_ROWS, :]
            term = src * jnp.concatenate([wdw_ref[j]] * (CONV_ROWS // SUBLANES), axis=0)
            acc = term if acc is None else acc + term
        conv = acc + bdw_ref[...]
        mu = jnp.mean(conv, axis=-1, keepdims=True)
        xc = conv - mu
        var = jnp.mean(xc * xc, axis=-1, keepdims=True)
        ln = xc * lax.rsqrt(var + EPS) * lng_ref[...] + lnb_ref[...]
        pieces.append((ln * _sigmoid(ln)).astype(BF16))
    zbuf[0:CONV_HALO, :] = zbuf[ts:ts + CONV_HALO, :]
    y_conv = _bdot(jnp.concatenate(pieces, axis=0), wpw_ref[...]) + bpw_ref[...]
    y_gla = _bdot(og_ref[0], wgo_ref[...])
    merged = gt_ref[0, :, 0:d].astype(F32) * y_conv + gt_ref[0, :, d:2 * d].astype(F32) * y_gla
    y = _bdot(merged.astype(BF16), wout_ref[...])
    h = x_ref[0] + mod_ref[0, 2:3, :] * y
    h_ref[0] = h
    u2 = _rms(h, g2_ref[...]) * (1.0 + mod_ref[0, 4:5, :]) + mod_ref[0, 3:4, :]
    _rows_to_tiles(u2_ref, u2)
    logits = _dot3(u2, wr_ref[...]) + br_ref[...]
    packed, cnt = _route(logits, cnt_sc[...])
    route_ref[0] = packed
    cnt_sc[...] = cnt
    cnt_ref[...] = jnp.broadcast_to(cnt, cnt_ref.shape)


def _merge(z, og, gt, x, mod3, w_dw, b_dw, ln_g, ln_b, w_pw, b_pw, w_go, w_out, g2, w_rg, b_rg, w_re, b_re, ts):
    nb, s, d = x.shape
    npad = LANES - N_GROUPS - N_EXPERTS
    wr = jnp.pad(jnp.concatenate([w_rg, w_re], axis=1), ((0, 0), (0, npad)))
    br = jnp.pad(jnp.concatenate([b_rg, b_re]), (0, npad)).reshape(1, LANES)

    def tok(width):
        return pl.BlockSpec((1, ts, width), lambda b, i: (b, i, 0))

    def const(shape):
        return pl.BlockSpec(shape, lambda b, i: (0,) * len(shape))

    def row(v):
        return v.reshape(1, v.shape[-1])

    return pl.pallas_call(
        _merge_body,
        grid=(nb, s // ts),
        in_specs=[tok(CONV_DIM), tok(V_DIM), tok(2 * d), tok(d),
                  pl.BlockSpec((1, N_MOD, d), lambda b, i: (b, 0, 0)),
                  const((CONV_WIDTH, SUBLANES, CONV_DIM)), const((1, CONV_DIM)), const((1, CONV_DIM)), const((1, CONV_DIM)),
                  const((CONV_DIM, d)), const((1, d)), const((V_DIM, d)), const((d, d)), const((1, d)),
                  const((d, LANES)), const((1, LANES))],
        out_specs=[tok(d), pl.BlockSpec((ts * SUBLANES, LANES), lambda b, i: (b * (s // ts) + i, 0)), tok(LANES),
                   pl.BlockSpec((8, LANES), lambda b, i: (0, 0))],
        out_shape=[jax.ShapeDtypeStruct((nb, s, d), F32), jax.ShapeDtypeStruct((nb * s * SUBLANES, LANES), F32),
                   jax.ShapeDtypeStruct((nb, s, LANES), F32), jax.ShapeDtypeStruct((8, LANES), F32)],
        scratch_shapes=[pltpu.VMEM((CONV_HALO + ts, CONV_DIM), F32),
                        pltpu.VMEM((SUBLANES - 1, CONV_HALO + ts - SUBLANES, CONV_DIM), F32),
                        pltpu.VMEM((1, LANES), F32)],
        compiler_params=_params("arbitrary", "arbitrary"),
        name="merge",
    )(z, og, gt, x, mod3, jnp.broadcast_to(w_dw.reshape(CONV_WIDTH, 1, CONV_DIM), (CONV_WIDTH, SUBLANES, CONV_DIM)), row(b_dw), row(ln_g), row(ln_b),
      w_pw.astype(BF16), row(b_pw), w_go.astype(BF16), w_out.astype(BF16), row(g2), wr, br)


def _row_copy(src, i, dst, j, sem):
    return pltpu.make_async_copy(src.at[pl.ds(pl.multiple_of(i, SUBLANES), SUBLANES), :],
                                 dst.at[pl.ds(pl.multiple_of(j, SUBLANES), SUBLANES), :], sem)


def _dispatch_body(p_ref, u2_ref, xs_in_ref, xs_ref, sem):
    del xs_in_ref
    ts = u2_ref.shape[0] // SUBLANES
    for r in range(ts):
        _row_copy(u2_ref, r * SUBLANES, xs_ref, p_ref[0, 0, r], sem).start()
        _row_copy(u2_ref, r * SUBLANES, xs_ref, p_ref[0, 0, ts + r], sem).start()
    for _ in range(TOP_K):
        pltpu.make_async_copy(u2_ref, xs_ref.at[pl.ds(0, ts * SUBLANES), :], sem).wait()


def _dispatch(slots, u2, n_rows, ts):
    t = u2.shape[0] // SUBLANES
    return pl.pallas_call(
        _dispatch_body,
        grid=(t // ts,),
        in_specs=[pl.BlockSpec((1, 1, 2 * ts), lambda i: (i, 0, 0), memory_space=pltpu.SMEM),
                  pl.BlockSpec((ts * SUBLANES, LANES), lambda i: (i, 0)),
                  pl.BlockSpec(memory_space=pl.ANY)],
        out_specs=pl.BlockSpec(memory_space=pl.ANY),
        out_shape=jax.ShapeDtypeStruct((n_rows * SUBLANES, LANES), F32),
        scratch_shapes=[pltpu.SemaphoreType.DMA(())],
        input_output_aliases={2: 0},
        compiler_params=_params("arbitrary"),
        name="dispatch",
    )(slots, u2, jnp.zeros((n_rows * SUBLANES, LANES), F32))


def _expert_body(te_ref, nu_ref, xs_ref, w1_ref, w3_ref, w2_ref, ys_ref, w1b, w3b, w2b):
    i = pl.program_id(0)
    changed = jnp.logical_or(i == 0, te_ref[i] != te_ref[jnp.maximum(i - 1, 0)])

    @pl.when(changed)
    def _():
        w1b[...] = w1_ref[0].astype(BF16)
        w3b[...] = w3_ref[0].astype(BF16)
        w2b[...] = w2_ref[0].astype(BF16)

    @pl.when(i < nu_ref[0])
    def _():
        x = _tiles_to_rows(xs_ref, xs_ref.shape[0] // SUBLANES).astype(BF16)
        h1 = _bdot(x, w1b[...])
        h3 = _bdot(x, w3b[...])
        hid = (h1 * _sigmoid(h1) * h3).astype(BF16)
        _rows_to_tiles(ys_ref, _bdot(hid, w2b[...]))

    @pl.when(i >= nu_ref[0])
    def _():
        ys_ref[...] = jnp.zeros_like(ys_ref)


def _experts(tile_expert, n_used, xs, w1, w3, w2, tmx):
    n_rows = xs.shape[0] // SUBLANES
    ne, d, f = w1.shape

    def tile_map(i, te, nu):
        return (jnp.minimum(i, nu[0] - 1), 0)

    def w_map(i, te, nu):
        return (te[i], 0, 0)

    return pl.pallas_call(
        _expert_body,
        grid_spec=pltpu.PrefetchScalarGridSpec(
            num_scalar_prefetch=2,
            grid=(n_rows // tmx,),
            in_specs=[pl.BlockSpec((tmx * SUBLANES, LANES), tile_map),
                      pl.BlockSpec((1, d, f), w_map), pl.BlockSpec((1, d, f), w_map),
                      pl.BlockSpec((1, f, d), w_map)],
            out_specs=pl.BlockSpec((tmx * SUBLANES, LANES), lambda i, te, nu: (i, 0)),
            scratch_shapes=[pltpu.VMEM((d, f), BF16), pltpu.VMEM((d, f), BF16), pltpu.VMEM((f, d), BF16)]),
        out_shape=jax.ShapeDtypeStruct((n_rows * SUBLANES, LANES), F32),
        compiler_params=_params("arbitrary"),
        name="experts",
    )(tile_expert, n_used, xs, w1, w3, w2)


def _final_body(p_ref, h_ref, route_ref, mod_ref, modf_ref, gf_ref, ys_ref, o_ref, y1_buf, y2_buf, sem):
    ts = h_ref.shape[1]

    for r in range(ts):
        _row_copy(ys_ref, p_ref[0, 0, r], y1_buf, r * SUBLANES, sem).start()
        _row_copy(ys_ref, p_ref[0, 0, ts + r], y2_buf, r * SUBLANES, sem).start()
    pltpu.make_async_copy(ys_ref.at[pl.ds(0, ts * SUBLANES), :], y1_buf, sem).wait()
    pltpu.make_async_copy(ys_ref.at[pl.ds(0, ts * SUBLANES), :], y2_buf, sem).wait()

    route = route_ref[0]
    y2 = route[:, 2:3] * _tiles_to_rows(y1_buf, ts) + route[:, 3:4] * _tiles_to_rows(y2_buf, ts)
    h = h_ref[0] + mod_ref[0, 5:6, :] * y2
    o_ref[0] = _rms(h, gf_ref[...]) * (1.0 + modf_ref[0, 1:2, :]) + modf_ref[0, 0:1, :]


def _final(slots, h, route, mod3, modf3, gf, ys, ts):
    nb, s, d = h.shape
    nt = s // ts

    def tok(width):
        return pl.BlockSpec((1, ts, width), lambda b, i: (b, i, 0))

    return pl.pallas_call(
        _final_body,
        grid=(nb, nt),
        in_specs=[pl.BlockSpec((1, 1, 2 * ts), lambda b, i: (b * nt + i, 0, 0), memory_space=pltpu.SMEM),
                  tok(d), tok(LANES),
                  pl.BlockSpec((1, N_MOD, d), lambda b, i: (b, 0, 0)),
                  pl.BlockSpec((1, 2, d), lambda b, i: (b, 0, 0)),
                  pl.BlockSpec((1, d), lambda b, i: (0, 0)),
                  pl.BlockSpec(memory_space=pl.ANY)],
        out_specs=tok(d),
        out_shape=jax.ShapeDtypeStruct((nb, s, d), F32),
        scratch_shapes=[pltpu.VMEM((ts * SUBLANES, LANES), F32), pltpu.VMEM((ts * SUBLANES, LANES), F32),
                        pltpu.SemaphoreType.DMA(())],
        compiler_params=_params("arbitrary", "arbitrary"),
        name="final",
    )(slots, h, route, mod3, modf3, gf.reshape(1, d), ys)


def _plan(route, cnt, ts, tmx, n_tiles):
    t = route.shape[0]
    counts = cnt[0, :N_EXPERTS].astype(jnp.int32)
    tiles = (counts + (tmx - 1)) // tmx
    tile_end = jnp.cumsum(tiles)
    offs = (tile_end - tiles) * tmx
    n_used = tile_end[-1:]
    tile_ids = jnp.minimum(jnp.arange(n_tiles, dtype=jnp.int32), n_used[0] - 1)
    tile_expert = jnp.sum((tile_ids[:, None] >= tile_end[None, :]).astype(jnp.int32), axis=1)
    eid = route[:, 0:2].astype(jnp.int32)
    rank = route[:, 4:6].astype(jnp.int32)
    slot = (jnp.take(offs, eid) + rank) * SUBLANES
    slots = slot.reshape(t // ts, ts, 2).transpose(0, 2, 1).reshape(t // ts, 1, 2 * ts)
    return slots, tile_expert.astype(jnp.int32), n_used.astype(jnp.int32)


def kernel(x, c, w_ada, b_ada, g_norm1, w_in, w_dw, b_dw, g_conv_ln, b_conv_ln, w_conv_pw, b_conv_pw,
           w_a2, b_a2, g_gla_norm, w_gla_o, w_out, g_norm2, w_router_g, b_router_g, w_router_e,
           b_router_e, w1, w3, w2, w_ada_f, b_ada_f, g_final):
    nb, s, d = x.shape
    assert w_ada.shape[0] == 1, "single-layer block"
    tm = min(512, s)
    tc = min(256, s)
    ts = min(256, s)
    tmx = 256
    t = nb * s
    n_tiles = (t * TOP_K) // tmx + N_EXPERTS
    mod3 = _ada(c, w_ada[0], b_ada[0]).reshape(nb, N_MOD, d)
    modf3 = _ada(c, w_ada_f, b_ada_f).reshape(nb, 2, d)
    z, q, k, v, rs, lg, gt = _proj(x, mod3, g_norm1[0], w_in[0], w_a2[0], b_a2[0], tm)
    og = _gla(q, k, lg, v, rs, g_gla_norm[0], tc)
    h, u2, route, cnt = _merge(z, og, gt, x, mod3, w_dw[0], b_dw[0], g_conv_ln[0], b_conv_ln[0],
                               w_conv_pw[0], b_conv_pw[0], w_gla_o[0], w_out[0], g_norm2[0],
                               w_router_g[0], b_router_g[0], w_router_e[0], b_router_e[0], ts)
    route2 = route.reshape(t, LANES)
    slots, tile_expert, n_used = _plan(route2, cnt, ts, tmx, n_tiles)
    xs = _dispatch(slots, u2, n_tiles * tmx, ts)
    ys = _experts(tile_expert, n_used, xs, w1[0], w3[0], w2[0], tmx)
    return _final(slots, h, route, mod3, modf3, g_final, ys, ts)
```

```python
import functools

import jax
import jax.numpy as jnp
from jax import lax
from jax.experimental import pallas as pl
from jax.experimental.pallas import tpu as pltpu

F32 = jnp.float32
BF16 = jnp.bfloat16

EPS = 1e-6
CONV_DIM = 512
CONV_WIDTH = 31
GLA_HEADS = 4
GLA_DK = 128
GLA_DV = 256
GLA_LOWRANK = 16
GLA_TAU = 16.0
QK_DIM = GLA_HEADS * GLA_DK
V_DIM = GLA_HEADS * GLA_DV
N_GROUPS = 4
EXPERTS_PER_GROUP = 8
N_EXPERTS = N_GROUPS * EXPERTS_PER_GROUP
TOP_K = 2
N_MOD = 6

LANES = 128
SUBLANES = 8
CONV_ROWS = 32
CONV_HALO = 32
GLA_CHUNK = 128
VMEM_LIMIT = 56 * 1024 * 1024


def _bdot(a, b):
    return jnp.dot(a, b, preferred_element_type=F32)


def _split(a):
    hi = a.astype(BF16)
    lo = (a - hi.astype(F32)).astype(BF16)
    return hi, lo


def _dot3(a, b):
    ah, al = _split(a)
    bh, bl = _split(b)
    return _bdot(ah, bh) + (_bdot(ah, bl) + _bdot(al, bh))


def _sigmoid(x):
    return 1.0 / (1.0 + jnp.exp(-x))


def _rms(x, g):
    ms = jnp.mean(x * x, axis=-1, keepdims=True)
    return x * lax.rsqrt(ms + EPS) * g


def _rows_to_tiles(ref, val):
    n = val.shape[0]
    for j in range(val.shape[1] // LANES):
        ref[pl.ds(j, n, stride=SUBLANES), :] = val[:, j * LANES:(j + 1) * LANES]


def _tiles_to_rows(ref, n):
    return jnp.concatenate([ref[pl.ds(j, n, stride=SUBLANES), :] for j in range(SUBLANES)], axis=-1)


def _params(*sem):
    return pltpu.CompilerParams(dimension_semantics=sem, vmem_limit_bytes=VMEM_LIMIT)


def _ada_body(c_ref, w_ref, b_ref, o_ref):
    c = c_ref[...]
    o_ref[...] = _dot3(c * _sigmoid(c), w_ref[...]) + b_ref[...]


def _ada(c, w, b, tn=1024):
    nb, d = c.shape
    n = w.shape[1]
    return pl.pallas_call(
        _ada_body,
        grid=(n // tn,),
        in_specs=[pl.BlockSpec((nb, d), lambda j: (0, 0)),
                  pl.BlockSpec((d, tn), lambda j: (0, j)),
                  pl.BlockSpec((1, tn), lambda j: (0, j))],
        out_specs=pl.BlockSpec((nb, tn), lambda j: (0, j)),
        out_shape=jax.ShapeDtypeStruct((nb, n), F32),
        compiler_params=_params("arbitrary"),
        name="ada",
    )(c, w, b.reshape(1, n))


def _proj_body(x_ref, mod_ref, g1_ref, wm_ref, wa1_ref, wg_ref, wa2_ref, ba2_ref,
               z_ref, q_ref, k_ref, v_ref, rs_ref, lg_ref, gt_ref):
    x = x_ref[0]
    u = (_rms(x, g1_ref[...]) * (1.0 + mod_ref[0, 1:2, :]) + mod_ref[0, 0:1, :]).astype(BF16)
    c0 = 2 * CONV_DIM
    c1 = c0 + 2 * QK_DIM
    c2 = c1 + V_DIM
    c3 = c2 + V_DIM
    pc = _bdot(u, wm_ref[:, 0:c0])
    z_ref[0] = pc[:, :CONV_DIM] * _sigmoid(pc[:, CONV_DIM:])
    qk = _bdot(u, wm_ref[:, c0:c1])
    q_ref[0] = qk[:, :QK_DIM] * (GLA_DK ** -0.5)
    k_ref[0] = qk[:, QK_DIM:]
    v_ref[0] = _bdot(u, wm_ref[:, c1:c2]).astype(BF16)
    r = _bdot(u, wm_ref[:, c2:c3])
    rs_ref[0] = (r * _sigmoid(r)).astype(BF16)
    a1 = _bdot(u, wa1_ref[...])
    xg = _dot3(a1, wa2_ref[...]) + ba2_ref[...]
    lg_ref[0] = (jnp.minimum(xg, 0.0) - jnp.log1p(jnp.exp(-jnp.abs(xg)))) * (1.0 / GLA_TAU)
    gt_ref[0] = _sigmoid(_bdot(u, wg_ref[...])).astype(BF16)


def _proj(x, mod3, g1, w_in, w_a2, b_a2, tm):
    nb, s, d = x.shape
    c3 = 2 * CONV_DIM + 2 * QK_DIM + 2 * V_DIM
    wm = w_in[:, :c3].astype(BF16)
    wa1 = jnp.pad(w_in[:, c3:c3 + GLA_LOWRANK], ((0, 0), (0, LANES - GLA_LOWRANK))).astype(BF16)
    wg = w_in[:, c3 + GLA_LOWRANK:].astype(BF16)
    wa2 = jnp.pad(w_a2, ((0, LANES - GLA_LOWRANK), (0, 0)))
    ng = wg.shape[1]

    def tok(width):
        return pl.BlockSpec((1, tm, width), lambda b, i: (b, i, 0))

    def const(shape):
        return pl.BlockSpec(shape, lambda b, i: (0,) * len(shape))

    def out(width, dt):
        return jax.ShapeDtypeStruct((nb, s, width), dt)

    return pl.pallas_call(
        _proj_body,
        grid=(nb, s // tm),
        in_specs=[tok(d),
                  pl.BlockSpec((1, N_MOD, d), lambda b, i: (b, 0, 0)),
                  const((1, d)), const(wm.shape), const(wa1.shape), const(wg.shape),
                  const(wa2.shape), const((1, QK_DIM))],
        out_specs=[tok(CONV_DIM), tok(QK_DIM), tok(QK_DIM), tok(V_DIM), tok(V_DIM), tok(QK_DIM), tok(ng)],
        out_shape=[out(CONV_DIM, F32), out(QK_DIM, F32), out(QK_DIM, F32), out(V_DIM, BF16),
                   out(V_DIM, BF16), out(QK_DIM, F32), out(ng, BF16)],
        compiler_params=_params("arbitrary", "arbitrary"),
        name="proj",
    )(x, mod3, g1.reshape(1, d), wm, wa1, wg, wa2, b_a2.reshape(1, QK_DIM))


def _gla_body(q_ref, k_ref, lg_ref, v_ref, rs_ref, gn_ref, o_ref, st_ref, *, n_chunks):
    cl = GLA_CHUNK

    @pl.when(pl.program_id(1) == 0)
    def _():
        st_ref[...] = jnp.zeros_like(st_ref)

    row = lax.broadcasted_iota(jnp.int32, (cl, cl), 0)
    col = lax.broadcasted_iota(jnp.int32, (cl, cl), 1)
    causal = col <= row
    tri = jnp.where(causal, 1.0, 0.0).astype(BF16)

    for h in range(GLA_HEADS):
        ks = slice(h * GLA_DK, (h + 1) * GLA_DK)
        vs = slice(h * GLA_DV, (h + 1) * GLA_DV)
        state = st_ref[h]
        for c in range(n_chunks):
            rows = slice(c * cl, (c + 1) * cl)
            gh, gl = _split(lg_ref[0, rows, ks])
            b = _bdot(tri, gh) + _bdot(tri, gl)
            q = q_ref[0, rows, ks]
            v = v_ref[0, rows, vs]
            qe = (q * jnp.exp(b)).astype(BF16)
            kt = k_ref[0, rows, ks].T
            bt = b.T
            bl = bt[:, cl - 1:cl]
            ket = (kt * jnp.exp(-bt)).astype(BF16)
            klt = (kt * jnp.exp(bl - bt)).astype(BF16)
            att = jnp.where(causal, _bdot(qe, ket), 0.0).astype(BF16)
            o = _bdot(att, v) + _bdot(qe, state.astype(BF16))
            state = jnp.exp(bl) * state + _bdot(klt, v)
            on = _rms(o, gn_ref[:, vs])
            o_ref[0, rows, vs] = (on * rs_ref[0, rows, vs].astype(F32)).astype(BF16)
        st_ref[h] = state


def _gla(q, k, lg, v, rs, gn, tc):
    nb, s, _ = q.shape

    def tok(width):
        return pl.BlockSpec((1, tc, width), lambda b, i: (b, i, 0))

    return pl.pallas_call(
        functools.partial(_gla_body, n_chunks=tc // GLA_CHUNK),
        grid=(nb, s // tc),
        in_specs=[tok(QK_DIM), tok(QK_DIM), tok(QK_DIM), tok(V_DIM), tok(V_DIM),
                  pl.BlockSpec((1, V_DIM), lambda b, i: (0, 0))],
        out_specs=tok(V_DIM),
        out_shape=jax.ShapeDtypeStruct((nb, s, V_DIM), BF16),
        scratch_shapes=[pltpu.VMEM((GLA_HEADS, GLA_DK, GLA_DV), F32)],
        compiler_params=_params("arbitrary", "arbitrary"),
        name="gla",
    )(q, k, lg, v, rs, gn.reshape(1, V_DIM))


def _route(logits, cnt):
    ts = logits.shape[0]
    lane = lax.broadcasted_iota(jnp.int32, (ts, LANES), 1).astype(F32)
    ninf = -jnp.inf
    lgm = jnp.where(lane < N_GROUPS, logits, ninf)
    gmax = jnp.max(lgm, axis=-1, keepdims=True)
    gsel = jnp.min(jnp.where(lgm == gmax, lane, float(LANES)), axis=-1, keepdims=True)
    wg = 1.0 / jnp.sum(jnp.exp(lgm - gmax), axis=-1, keepdims=True)
    base = N_GROUPS + EXPERTS_PER_GROUP * gsel
    le = jnp.where(lane >= base, jnp.where(lane < base + EXPERTS_PER_GROUP, logits, ninf), ninf)
    v1 = jnp.max(le, axis=-1, keepdims=True)
    i1 = jnp.min(jnp.where(le == v1, lane, float(LANES)), axis=-1, keepdims=True)
    le2 = jnp.where(lane == i1, ninf, le)
    v2 = jnp.max(le2, axis=-1, keepdims=True)
    i2 = jnp.min(jnp.where(le2 == v2, lane, float(LANES)), axis=-1, keepdims=True)
    e21 = jnp.exp(v2 - v1)
    w1 = wg / (1.0 + e21)
    w2 = w1 * e21
    eid1 = i1 - N_GROUPS
    eid2 = i2 - N_GROUPS
    oh1 = jnp.where(lane == eid1, 1.0, 0.0)
    oh2 = jnp.where(lane == eid2, 1.0, 0.0)
    ohs = oh1 + oh2
    row = lax.broadcasted_iota(jnp.int32, (ts, ts), 0)
    col = lax.broadcasted_iota(jnp.int32, (ts, ts), 1)
    before = jnp.where(col < row, 1.0, 0.0).astype(BF16)
    tot = cnt + _bdot(before, ohs.astype(BF16))
    rank1 = jnp.sum(oh1 * tot, axis=-1, keepdims=True)
    rank2 = jnp.sum(oh2 * tot, axis=-1, keepdims=True)
    packed = jnp.where(lane == 0.0, eid1,
             jnp.where(lane == 1.0, eid2,
             jnp.where(lane == 2.0, w1,
             jnp.where(lane == 3.0, w2,
             jnp.where(lane == 4.0, rank1,
             jnp.where(lane == 5.0, rank2, 0.0))))))
    return packed, cnt + jnp.sum(ohs, axis=0, keepdims=True)


def _merge_body(z_ref, og_ref, gt_ref, x_ref, mod_ref, wdw_ref, bdw_ref, lng_ref, lnb_ref,
                wpw_ref, bpw_ref, wgo_ref, wout_ref, g2_ref, wr_ref, br_ref,
                h_ref, u2_ref, route_ref, cnt_ref, zbuf, zsh, cnt_sc):
    ts = z_ref.shape[1]
    d = x_ref.shape[2]
    first_tile = pl.program_id(1) == 0

    @pl.when(first_tile)
    def _():
        zbuf[0:CONV_HALO, :] = jnp.zeros((CONV_HALO, CONV_DIM), F32)

    @pl.when(jnp.logical_and(first_tile, pl.program_id(0) == 0))
    def _():
        cnt_sc[...] = jnp.zeros_like(cnt_sc)

    zbuf[CONV_HALO:CONV_HALO + ts, :] = z_ref[0]
    span = ts + CONV_HALO - SUBLANES
    for r in range(1, SUBLANES):
        zsh[r - 1] = zbuf[r:r + span, :]
    off = CONV_HALO - (CONV_WIDTH - 1)
    pieces = []
    for blk in range(ts // CONV_ROWS):
        acc = None
        for j in range(CONV_WIDTH):
            a, r = divmod(off + j, SUBLANES)
            lo = a * SUBLANES + blk * CONV_ROWS
            src = zbuf[lo:lo + CONV_ROWS, :] if r == 0 else zsh[r - 1, lo:lo + CONV_ROWS, :]
            term = src * jnp.concatenate([wdw_ref[j]] * (CONV_ROWS // SUBLANES), axis=0)
            acc = term if acc is None else acc + term
        conv = acc + bdw_ref[...]
        mu = jnp.mean(conv, axis=-1, keepdims=True)
        xc = conv - mu
        var = jnp.mean(xc * xc, axis=-1, keepdims=True)
        ln = xc * lax.rsqrt(var + EPS) * lng_ref[...] + lnb_ref[...]
        pieces.append((ln * _sigmoid(ln)).astype(BF16))
    zbuf[0:CONV_HALO, :] = zbuf[ts:ts + CONV_HALO, :]
    y_conv = _bdot(jnp.concatenate(pieces, axis=0), wpw_ref[...]) + bpw_ref[...]
    y_gla = _bdot(og_ref[0], wgo_ref[...])
    merged = gt_ref[0, :, 0:d].astype(F32) * y_conv + gt_ref[0, :, d:2 * d].astype(F32) * y_gla
    y = _bdot(merged.astype(BF16), wout_ref[...])
    h = x_ref[0] + mod_ref[0, 2:3, :] * y
    h_ref[0] = h
    u2 = _rms(h, g2_ref[...]) * (1.0 + mod_ref[0, 4:5, :]) + mod_ref[0, 3:4, :]
    _rows_to_tiles(u2_ref, u2)
    logits = _dot3(u2, wr_ref[...]) + br_ref[...]
    packed, cnt = _route(logits, cnt_sc[...])
    route_ref[0] = packed
    cnt_sc[...] = cnt
    cnt_ref[...] = jnp.broadcast_to(cnt, cnt_ref.shape)


def _merge(z, og, gt, x, mod3, w_dw, b_dw, ln_g, ln_b, w_pw, b_pw, w_go, w_out, g2, w_rg, b_rg, w_re, b_re, ts):
    nb, s, d = x.shape
    npad = LANES - N_GROUPS - N_EXPERTS
    wr = jnp.pad(jnp.concatenate([w_rg, w_re], axis=1), ((0, 0), (0, npad)))
    br = jnp.pad(jnp.concatenate([b_rg, b_re]), (0, npad)).reshape(1, LANES)

    def tok(width):
        return pl.BlockSpec((1, ts, width), lambda b, i: (b, i, 0))

    def const(shape):
        return pl.BlockSpec(shape, lambda b, i: (0,) * len(shape))

    def row(v):
        return v.reshape(1, v.shape[-1])

    return pl.pallas_call(
        _merge_body,
        grid=(nb, s // ts),
        in_specs=[tok(CONV_DIM), tok(V_DIM), tok(2 * d), tok(d),
                  pl.BlockSpec((1, N_MOD, d), lambda b, i: (b, 0, 0)),
                  const((CONV_WIDTH, SUBLANES, CONV_DIM)), const((1, CONV_DIM)), const((1, CONV_DIM)), const((1, CONV_DIM)),
                  const((CONV_DIM, d)), const((1, d)), const((V_DIM, d)), const((d, d)), const((1, d)),
                  const((d, LANES)), const((1, LANES))],
        out_specs=[tok(d), pl.BlockSpec((ts * SUBLANES, LANES), lambda b, i: (b * (s // ts) + i, 0)), tok(LANES),
                   pl.BlockSpec((8, LANES), lambda b, i: (0, 0))],
        out_shape=[jax.ShapeDtypeStruct((nb, s, d), F32), jax.ShapeDtypeStruct((nb * s * SUBLANES, LANES), F32),
                   jax.ShapeDtypeStruct((nb, s, LANES), F32), jax.ShapeDtypeStruct((8, LANES), F32)],
        scratch_shapes=[pltpu.VMEM((CONV_HALO + ts, CONV_DIM), F32),
                        pltpu.VMEM((SUBLANES - 1, CONV_HALO + ts - SUBLANES, CONV_DIM), F32),
                        pltpu.VMEM((1, LANES), F32)],
        compiler_params=_params("arbitrary", "arbitrary"),
        name="merge",
    )(z, og, gt, x, mod3, jnp.broadcast_to(w_dw.reshape(CONV_WIDTH, 1, CONV_DIM), (CONV_WIDTH, SUBLANES, CONV_DIM)), row(b_dw), row(ln_g), row(ln_b),
      w_pw.astype(BF16), row(b_pw), w_go.astype(BF16), w_out.astype(BF16), row(g2), wr, br)


def _row_copy(src, i, dst, j, sem):
    return pltpu.make_async_copy(src.at[pl.ds(pl.multiple_of(i, SUBLANES), SUBLANES), :],
                                 dst.at[pl.ds(pl.multiple_of(j, SUBLANES), SUBLANES), :], sem)


def _gather_rows(idx_ref, lo, n, src, dst, sem):
    for r in range(n):
        _row_copy(src, idx_ref[0, 0, lo + r], dst, r * SUBLANES, sem).start(priority=r % 2)


def _expert_body(te_ref, nu_ref, src_ref, nxt_ref, u2_ref, w1_ref, w3_ref, w2_ref, ys_ref,
                 xbuf, w1b, w3b, w2b, sem):
    i = pl.program_id(0)
    n_used = nu_ref[0]
    cur = i % 2
    tmx = src_ref.shape[-1]

    @pl.when(i == 0)
    def _():
        _gather_rows(src_ref, 0, tmx, u2_ref, xbuf.at[0], sem.at[0])

    @pl.when(i + 1 < n_used)
    def _():
        _gather_rows(nxt_ref, 0, tmx, u2_ref, xbuf.at[1 - cur], sem.at[1 - cur])

    @pl.when(jnp.logical_or(i == 0, te_ref[i] != te_ref[jnp.maximum(i - 1, 0)]))
    def _():
        w1b[...] = w1_ref[0].astype(BF16)
        w3b[...] = w3_ref[0].astype(BF16)
        w2b[...] = w2_ref[0].astype(BF16)

    @pl.when(i < n_used)
    def _():
        pltpu.make_async_copy(u2_ref.at[pl.ds(0, tmx * SUBLANES), :], xbuf.at[cur], sem.at[cur]).wait()
        x = _tiles_to_rows(xbuf.at[cur], tmx).astype(BF16)
        h1 = _bdot(x, w1b[...])
        h3 = _bdot(x, w3b[...])
        hid = (h1 * _sigmoid(h1) * h3).astype(BF16)
        _rows_to_tiles(ys_ref, _bdot(hid, w2b[...]))

    @pl.when(i >= n_used)
    def _():
        ys_ref[...] = jnp.zeros_like(ys_ref)


def _experts(tile_expert, n_used, src_rows, u2, w1, w3, w2):
    n_tiles, _, tmx = src_rows.shape
    ne, d, f = w1.shape

    def w_map(i, te, nu):
        return (te[i], 0, 0)

    return pl.pallas_call(
        _expert_body,
        grid_spec=pltpu.PrefetchScalarGridSpec(
            num_scalar_prefetch=2,
            grid=(n_tiles,),
            in_specs=[pl.BlockSpec((1, 1, tmx), lambda i, te, nu: (i, 0, 0), memory_space=pltpu.SMEM),
                      pl.BlockSpec((1, 1, tmx), lambda i, te, nu: (jnp.minimum(i + 1, n_tiles - 1), 0, 0),
                                   memory_space=pltpu.SMEM),
                      pl.BlockSpec(memory_space=pl.ANY),
                      pl.BlockSpec((1, d, f), w_map), pl.BlockSpec((1, d, f), w_map),
                      pl.BlockSpec((1, f, d), w_map)],
            out_specs=pl.BlockSpec((tmx * SUBLANES, LANES), lambda i, te, nu: (i, 0)),
            scratch_shapes=[pltpu.VMEM((2, tmx * SUBLANES, LANES), F32),
                            pltpu.VMEM((d, f), BF16), pltpu.VMEM((d, f), BF16), pltpu.VMEM((f, d), BF16),
                            pltpu.SemaphoreType.DMA((2,))]),
        out_shape=jax.ShapeDtypeStruct((n_tiles * tmx * SUBLANES, LANES), F32),
        compiler_params=_params("arbitrary"),
        name="experts",
    )(tile_expert, n_used, src_rows, src_rows, u2, w1, w3, w2)


def _final_body(p_ref, h_ref, route_ref, mod_ref, modf_ref, gf_ref, ys_ref, o_ref, y1_buf, y2_buf, sem):
    ts = h_ref.shape[1]

    _gather_rows(p_ref, 0, ts, ys_ref, y1_buf, sem)
    _gather_rows(p_ref, ts, ts, ys_ref, y2_buf, sem)
    pltpu.make_async_copy(ys_ref.at[pl.ds(0, ts * SUBLANES), :], y1_buf, sem).wait()
    pltpu.make_async_copy(ys_ref.at[pl.ds(0, ts * SUBLANES), :], y2_buf, sem).wait()

    route = route_ref[0]
    y2 = route[:, 2:3] * _tiles_to_rows(y1_buf, ts) + route[:, 3:4] * _tiles_to_rows(y2_buf, ts)
    h = h_ref[0] + mod_ref[0, 5:6, :] * y2
    o_ref[0] = _rms(h, gf_ref[...]) * (1.0 + modf_ref[0, 1:2, :]) + modf_ref[0, 0:1, :]


def _final(slots, h, route, mod3, modf3, gf, ys, ts):
    nb, s, d = h.shape
    nt = s // ts

    def tok(width):
        return pl.BlockSpec((1, ts, width), lambda b, i: (b, i, 0))

    return pl.pallas_call(
        _final_body,
        grid=(nb, nt),
        in_specs=[pl.BlockSpec((1, 1, 2 * ts), lambda b, i: (b * nt + i, 0, 0), memory_space=pltpu.SMEM),
                  tok(d), tok(LANES),
                  pl.BlockSpec((1, N_MOD, d), lambda b, i: (b, 0, 0)),
                  pl.BlockSpec((1, 2, d), lambda b, i: (b, 0, 0)),
                  pl.BlockSpec((1, d), lambda b, i: (0, 0)),
                  pl.BlockSpec(memory_space=pl.ANY)],
        out_specs=tok(d),
        out_shape=jax.ShapeDtypeStruct((nb, s, d), F32),
        scratch_shapes=[pltpu.VMEM((ts * SUBLANES, LANES), F32), pltpu.VMEM((ts * SUBLANES, LANES), F32),
                        pltpu.SemaphoreType.DMA(())],
        compiler_params=_params("arbitrary", "arbitrary"),
        name="final",
    )(slots, h, route, mod3, modf3, gf.reshape(1, d), ys)


def _plan(route, cnt, ts, tmx, n_tiles):
    t = route.shape[0]
    counts = cnt[0, :N_EXPERTS].astype(jnp.int32)
    tiles = (counts + (tmx - 1)) // tmx
    tile_end = jnp.cumsum(tiles)
    offs = (tile_end - tiles) * tmx
    n_used = tile_end[-1:]
    tile_ids = jnp.minimum(jnp.arange(n_tiles, dtype=jnp.int32), n_used[0] - 1)
    tile_expert = jnp.sum((tile_ids[:, None] >= tile_end[None, :]).astype(jnp.int32), axis=1)
    eid = route[:, 0:2].astype(jnp.int32)
    rank = route[:, 4:6].astype(jnp.int32)
    slot = jnp.take(offs, eid) + rank
    token = jnp.broadcast_to(jnp.arange(t, dtype=jnp.int32)[:, None], (t, TOP_K))
    src = jnp.zeros((n_tiles * tmx,), jnp.int32).at[slot.reshape(-1)].set(token.reshape(-1))
    src_rows = (src * SUBLANES).reshape(n_tiles, 1, tmx)
    slot_rows = slot * SUBLANES
    slots = slot_rows.reshape(t // ts, ts, 2).transpose(0, 2, 1).reshape(t // ts, 1, 2 * ts)
    return slots, src_rows, tile_expert.astype(jnp.int32), n_used.astype(jnp.int32)


def kernel(x, c, w_ada, b_ada, g_norm1, w_in, w_dw, b_dw, g_conv_ln, b_conv_ln, w_conv_pw, b_conv_pw,
           w_a2, b_a2, g_gla_norm, w_gla_o, w_out, g_norm2, w_router_g, b_router_g, w_router_e,
           b_router_e, w1, w3, w2, w_ada_f, b_ada_f, g_final):
    nb, s, d = x.shape
    assert w_ada.shape[0] == 1, "single-layer block"
    tm = min(512, s)
    tc = min(256, s)
    ts = min(256, s)
    tmx = 256
    t = nb * s
    n_tiles = (t * TOP_K) // tmx + N_EXPERTS
    mod3 = _ada(c, w_ada[0], b_ada[0]).reshape(nb, N_MOD, d)
    modf3 = _ada(c, w_ada_f, b_ada_f).reshape(nb, 2, d)
    z, q, k, v, rs, lg, gt = _proj(x, mod3, g_norm1[0], w_in[0], w_a2[0], b_a2[0], tm)
    og = _gla(q, k, lg, v, rs, g_gla_norm[0], tc)
    h, u2, route, cnt = _merge(z, og, gt, x, mod3, w_dw[0], b_dw[0], g_conv_ln[0], b_conv_ln[0],
                               w_conv_pw[0], b_conv_pw[0], w_gla_o[0], w_out[0], g_norm2[0],
                               w_router_g[0], b_router_g[0], w_router_e[0], b_router_e[0], ts)
    route2 = route.reshape(t, LANES)
    slots, src_rows, tile_expert, n_used = _plan(route2, cnt, ts, tmx, n_tiles)
    ys = _experts(tile_expert, n_used, src_rows, u2, w1[0], w3[0], w2[0])
    return _final(slots, h, route, mod3, modf3, g_final, ys, ts)
```

```python
import functools

import jax
import jax.numpy as jnp
from jax import lax
from jax.experimental import pallas as pl
from jax.experimental.pallas import tpu as pltpu

F32 = jnp.float32
BF16 = jnp.bfloat16

EPS = 1e-6
CONV_DIM = 512
CONV_WIDTH = 31
GLA_HEADS = 4
GLA_DK = 128
GLA_DV = 256
GLA_LOWRANK = 16
GLA_TAU = 16.0
QK_DIM = GLA_HEADS * GLA_DK
V_DIM = GLA_HEADS * GLA_DV
N_GROUPS = 4
EXPERTS_PER_GROUP = 8
N_EXPERTS = N_GROUPS * EXPERTS_PER_GROUP
TOP_K = 2
N_MOD = 6

LANES = 128
SUBLANES = 8
CONV_ROWS = 32
CONV_HALO = 32
GLA_CHUNK = 128
VMEM_LIMIT = 56 * 1024 * 1024


def _bdot(a, b):
    return jnp.dot(a, b, preferred_element_type=F32)


def _split(a):
    hi = a.astype(BF16)
    lo = (a - hi.astype(F32)).astype(BF16)
    return hi, lo


def _dot3(a, b):
    ah, al = _split(a)
    bh, bl = _split(b)
    return _bdot(ah, bh) + (_bdot(ah, bl) + _bdot(al, bh))


def _sigmoid(x):
    return 1.0 / (1.0 + jnp.exp(-x))


def _rms(x, g):
    ms = jnp.mean(x * x, axis=-1, keepdims=True)
    return x * lax.rsqrt(ms + EPS) * g


def _rows_to_tiles(ref, val):
    n = val.shape[0]
    for j in range(val.shape[1] // LANES):
        ref[pl.ds(j, n, stride=SUBLANES), :] = val[:, j * LANES:(j + 1) * LANES]


def _tiles_to_rows(ref, n):
    return jnp.concatenate([ref[pl.ds(j, n, stride=SUBLANES), :] for j in range(SUBLANES)], axis=-1)


def _params(*sem):
    return pltpu.CompilerParams(dimension_semantics=sem, vmem_limit_bytes=VMEM_LIMIT)


def _ada_body(c_ref, w_ref, b_ref, o_ref):
    c = c_ref[...]
    o_ref[...] = _dot3(c * _sigmoid(c), w_ref[...]) + b_ref[...]


def _ada(c, w, b, tn=1024):
    nb, d = c.shape
    n = w.shape[1]
    return pl.pallas_call(
        _ada_body,
        grid=(n // tn,),
        in_specs=[pl.BlockSpec((nb, d), lambda j: (0, 0)),
                  pl.BlockSpec((d, tn), lambda j: (0, j)),
                  pl.BlockSpec((1, tn), lambda j: (0, j))],
        out_specs=pl.BlockSpec((nb, tn), lambda j: (0, j)),
        out_shape=jax.ShapeDtypeStruct((nb, n), F32),
        compiler_params=_params("arbitrary"),
        name="ada",
    )(c, w, b.reshape(1, n))


def _proj_body(x_ref, mod_ref, g1_ref, wm_ref, wa1_ref, wg_ref, wa2_ref, ba2_ref,
               z_ref, q_ref, k_ref, v_ref, rs_ref, lg_ref, gt_ref):
    x = x_ref[0]
    u = (_rms(x, g1_ref[...]) * (1.0 + mod_ref[0, 1:2, :]) + mod_ref[0, 0:1, :]).astype(BF16)
    c0 = 2 * CONV_DIM
    c1 = c0 + 2 * QK_DIM
    c2 = c1 + V_DIM
    c3 = c2 + V_DIM
    pc = _bdot(u, wm_ref[:, 0:c0])
    z_ref[0] = pc[:, :CONV_DIM] * _sigmoid(pc[:, CONV_DIM:])
    qk = _bdot(u, wm_ref[:, c0:c1])
    q_ref[0] = qk[:, :QK_DIM] * (GLA_DK ** -0.5)
    k_ref[0] = qk[:, QK_DIM:]
    v_ref[0] = _bdot(u, wm_ref[:, c1:c2]).astype(BF16)
    r = _bdot(u, wm_ref[:, c2:c3])
    rs_ref[0] = (r * _sigmoid(r)).astype(BF16)
    a1 = _bdot(u, wa1_ref[...])
    xg = _dot3(a1, wa2_ref[...]) + ba2_ref[...]
    lg_ref[0] = (jnp.minimum(xg, 0.0) - jnp.log1p(jnp.exp(-jnp.abs(xg)))) * (1.0 / GLA_TAU)
    gt_ref[0] = _sigmoid(_bdot(u, wg_ref[...])).astype(BF16)


def _proj(x, mod3, g1, w_in, w_a2, b_a2, tm):
    nb, s, d = x.shape
    c3 = 2 * CONV_DIM + 2 * QK_DIM + 2 * V_DIM
    wm = w_in[:, :c3].astype(BF16)
    wa1 = jnp.pad(w_in[:, c3:c3 + GLA_LOWRANK], ((0, 0), (0, LANES - GLA_LOWRANK))).astype(BF16)
    wg = w_in[:, c3 + GLA_LOWRANK:].astype(BF16)
    wa2 = jnp.pad(w_a2, ((0, LANES - GLA_LOWRANK), (0, 0)))
    ng = wg.shape[1]

    def tok(width):
        return pl.BlockSpec((1, tm, width), lambda b, i: (b, i, 0))

    def const(shape):
        return pl.BlockSpec(shape, lambda b, i: (0,) * len(shape))

    def out(width, dt):
        return jax.ShapeDtypeStruct((nb, s, width), dt)

    return pl.pallas_call(
        _proj_body,
        grid=(nb, s // tm),
        in_specs=[tok(d),
                  pl.BlockSpec((1, N_MOD, d), lambda b, i: (b, 0, 0)),
                  const((1, d)), const(wm.shape), const(wa1.shape), const(wg.shape),
                  const(wa2.shape), const((1, QK_DIM))],
        out_specs=[tok(CONV_DIM), tok(QK_DIM), tok(QK_DIM), tok(V_DIM), tok(V_DIM), tok(QK_DIM), tok(ng)],
        out_shape=[out(CONV_DIM, F32), out(QK_DIM, F32), out(QK_DIM, F32), out(V_DIM, BF16),
                   out(V_DIM, BF16), out(QK_DIM, F32), out(ng, BF16)],
        compiler_params=_params("arbitrary", "arbitrary"),
        name="proj",
    )(x, mod3, g1.reshape(1, d), wm, wa1, wg, wa2, b_a2.reshape(1, QK_DIM))


def _gla_body(q_ref, k_ref, lg_ref, v_ref, rs_ref, gn_ref, o_ref, st_ref, *, n_chunks):
    cl = GLA_CHUNK

    @pl.when(pl.program_id(1) == 0)
    def _():
        st_ref[...] = jnp.zeros_like(st_ref)

    row = lax.broadcasted_iota(jnp.int32, (cl, cl), 0)
    col = lax.broadcasted_iota(jnp.int32, (cl, cl), 1)
    causal = col <= row
    tri = jnp.where(causal, 1.0, 0.0).astype(BF16)

    for h in range(GLA_HEADS):
        ks = slice(h * GLA_DK, (h + 1) * GLA_DK)
        vs = slice(h * GLA_DV, (h + 1) * GLA_DV)
        state = st_ref[h]
        for c in range(n_chunks):
            rows = slice(c * cl, (c + 1) * cl)
            gh, gl = _split(lg_ref[0, rows, ks])
            b = _bdot(tri, gh) + _bdot(tri, gl)
            q = q_ref[0, rows, ks]
            v = v_ref[0, rows, vs]
            qe = (q * jnp.exp(b)).astype(BF16)
            kt = k_ref[0, rows, ks].T
            bt = b.T
            bl = bt[:, cl - 1:cl]
            ket = (kt * jnp.exp(-bt)).astype(BF16)
            klt = (kt * jnp.exp(bl - bt)).astype(BF16)
            att = jnp.where(causal, _bdot(qe, ket), 0.0).astype(BF16)
            o = _bdot(att, v) + _bdot(qe, state.astype(BF16))
            state = jnp.exp(bl) * state + _bdot(klt, v)
            on = _rms(o, gn_ref[:, vs])
            o_ref[0, rows, vs] = (on * rs_ref[0, rows, vs].astype(F32)).astype(BF16)
        st_ref[h] = state


def _gla(q, k, lg, v, rs, gn, tc):
    nb, s, _ = q.shape

    def tok(width):
        return pl.BlockSpec((1, tc, width), lambda b, i: (b, i, 0))

    return pl.pallas_call(
        functools.partial(_gla_body, n_chunks=tc // GLA_CHUNK),
        grid=(nb, s // tc),
        in_specs=[tok(QK_DIM), tok(QK_DIM), tok(QK_DIM), tok(V_DIM), tok(V_DIM),
                  pl.BlockSpec((1, V_DIM), lambda b, i: (0, 0))],
        out_specs=tok(V_DIM),
        out_shape=jax.ShapeDtypeStruct((nb, s, V_DIM), BF16),
        scratch_shapes=[pltpu.VMEM((GLA_HEADS, GLA_DK, GLA_DV), F32)],
        compiler_params=_params("arbitrary", "arbitrary"),
        name="gla",
    )(q, k, lg, v, rs, gn.reshape(1, V_DIM))


def _route(logits, cnt):
    ts = logits.shape[0]
    lane = lax.broadcasted_iota(jnp.int32, (ts, LANES), 1).astype(F32)
    ninf = -jnp.inf
    lgm = jnp.where(lane < N_GROUPS, logits, ninf)
    gmax = jnp.max(lgm, axis=-1, keepdims=True)
    gsel = jnp.min(jnp.where(lgm == gmax, lane, float(LANES)), axis=-1, keepdims=True)
    wg = 1.0 / jnp.sum(jnp.exp(lgm - gmax), axis=-1, keepdims=True)
    base = N_GROUPS + EXPERTS_PER_GROUP * gsel
    le = jnp.where(lane >= base, jnp.where(lane < base + EXPERTS_PER_GROUP, logits, ninf), ninf)
    v1 = jnp.max(le, axis=-1, keepdims=True)
    i1 = jnp.min(jnp.where(le == v1, lane, float(LANES)), axis=-1, keepdims=True)
    le2 = jnp.where(lane == i1, ninf, le)
    v2 = jnp.max(le2, axis=-1, keepdims=True)
    i2 = jnp.min(jnp.where(le2 == v2, lane, float(LANES)), axis=-1, keepdims=True)
    e21 = jnp.exp(v2 - v1)
    w1 = wg / (1.0 + e21)
    w2 = w1 * e21
    eid1 = i1 - N_GROUPS
    eid2 = i2 - N_GROUPS
    oh1 = jnp.where(lane == eid1, 1.0, 0.0)
    oh2 = jnp.where(lane == eid2, 1.0, 0.0)
    ohs = oh1 + oh2
    row = lax.broadcasted_iota(jnp.int32, (ts, ts), 0)
    col = lax.broadcasted_iota(jnp.int32, (ts, ts), 1)
    before = jnp.where(col < row, 1.0, 0.0).astype(BF16)
    tot = cnt + _bdot(before, ohs.astype(BF16))
    rank1 = jnp.sum(oh1 * tot, axis=-1, keepdims=True)
    rank2 = jnp.sum(oh2 * tot, axis=-1, keepdims=True)
    packed = jnp.where(lane == 0.0, eid1,
             jnp.where(lane == 1.0, eid2,
             jnp.where(lane == 2.0, w1,
             jnp.where(lane == 3.0, w2,
             jnp.where(lane == 4.0, rank1,
             jnp.where(lane == 5.0, rank2, 0.0))))))
    return packed, cnt + jnp.sum(ohs, axis=0, keepdims=True)


def _merge_body(z_ref, og_ref, gt_ref, x_ref, mod_ref, wdw_ref, bdw_ref, lng_ref, lnb_ref,
                wpw_ref, bpw_ref, wgo_ref, wout_ref, g2_ref, wr_ref, br_ref,
                h_ref, u2_ref, route_ref, cnt_ref, zbuf, zsh, cnt_sc):
    ts = z_ref.shape[1]
    d = x_ref.shape[2]
    first_tile = pl.program_id(1) == 0

    @pl.when(first_tile)
    def _():
        zbuf[0:CONV_HALO, :] = jnp.zeros((CONV_HALO, CONV_DIM), F32)

    @pl.when(jnp.logical_and(first_tile, pl.program_id(0) == 0))
    def _():
        cnt_sc[...] = jnp.zeros_like(cnt_sc)

    zbuf[CONV_HALO:CONV_HALO + ts, :] = z_ref[0]
    span = ts + CONV_HALO - SUBLANES
    for r in range(1, SUBLANES):
        zsh[r - 1] = zbuf[r:r + span, :]
    off = CONV_HALO - (CONV_WIDTH - 1)
    pieces = []
    for blk in range(ts // CONV_ROWS):
        acc = None
        for j in range(CONV_WIDTH):
            a, r = divmod(off + j, SUBLANES)
            lo = a * SUBLANES + blk * CONV_ROWS
            src = zbuf[lo:lo + CONV_ROWS, :] if r == 0 else zsh[r - 1, lo:lo + CONV_ROWS, :]
            term = src * jnp.concatenate([wdw_ref[j]] * (CONV_ROWS // SUBLANES), axis=0)
            acc = term if acc is None else acc + term
        conv = acc + bdw_ref[...]
        mu = jnp.mean(conv, axis=-1, keepdims=True)
        xc = conv - mu
        var = jnp.mean(xc * xc, axis=-1, keepdims=True)
        ln = xc * lax.rsqrt(var + EPS) * lng_ref[...] + lnb_ref[...]
        pieces.append((ln * _sigmoid(ln)).astype(BF16))
    zbuf[0:CONV_HALO, :] = zbuf[ts:ts + CONV_HALO, :]
    y_conv = _bdot(jnp.concatenate(pieces, axis=0), wpw_ref[...]) + bpw_ref[...]
    y_gla = _bdot(og_ref[0], wgo_ref[...])
    merged = gt_ref[0, :, 0:d].astype(F32) * y_conv + gt_ref[0, :, d:2 * d].astype(F32) * y_gla
    y = _bdot(merged.astype(BF16), wout_ref[...])
    h = x_ref[0] + mod_ref[0, 2:3, :] * y
    h_ref[0] = h
    u2 = _rms(h, g2_ref[...]) * (1.0 + mod_ref[0, 4:5, :]) + mod_ref[0, 3:4, :]
    _rows_to_tiles(u2_ref, u2)
    logits = _dot3(u2, wr_ref[...]) + br_ref[...]
    packed, cnt = _route(logits, cnt_sc[...])
    route_ref[0] = packed
    cnt_sc[...] = cnt
    cnt_ref[...] = jnp.broadcast_to(cnt, cnt_ref.shape)


def _merge(z, og, gt, x, mod3, w_dw, b_dw, ln_g, ln_b, w_pw, b_pw, w_go, w_out, g2, w_rg, b_rg, w_re, b_re, ts):
    nb, s, d = x.shape
    npad = LANES - N_GROUPS - N_EXPERTS
    wr = jnp.pad(jnp.concatenate([w_rg, w_re], axis=1), ((0, 0), (0, npad)))
    br = jnp.pad(jnp.concatenate([b_rg, b_re]), (0, npad)).reshape(1, LANES)

    def tok(width):
        return pl.BlockSpec((1, ts, width), lambda b, i: (b, i, 0))

    def const(shape):
        return pl.BlockSpec(shape, lambda b, i: (0,) * len(shape))

    def row(v):
        return v.reshape(1, v.shape[-1])

    return pl.pallas_call(
        _merge_body,
        grid=(nb, s // ts),
        in_specs=[tok(CONV_DIM), tok(V_DIM), tok(2 * d), tok(d),
                  pl.BlockSpec((1, N_MOD, d), lambda b, i: (b, 0, 0)),
                  const((CONV_WIDTH, SUBLANES, CONV_DIM)), const((1, CONV_DIM)), const((1, CONV_DIM)), const((1, CONV_DIM)),
                  const((CONV_DIM, d)), const((1, d)), const((V_DIM, d)), const((d, d)), const((1, d)),
                  const((d, LANES)), const((1, LANES))],
        out_specs=[tok(d), pl.BlockSpec((ts * SUBLANES, LANES), lambda b, i: (b * (s // ts) + i, 0)), tok(LANES),
                   pl.BlockSpec((8, LANES), lambda b, i: (0, 0))],
        out_shape=[jax.ShapeDtypeStruct((nb, s, d), F32), jax.ShapeDtypeStruct((nb * s * SUBLANES, LANES), F32),
                   jax.ShapeDtypeStruct((nb, s, LANES), F32), jax.ShapeDtypeStruct((8, LANES), F32)],
        scratch_shapes=[pltpu.VMEM((CONV_HALO + ts, CONV_DIM), F32),
                        pltpu.VMEM((SUBLANES - 1, CONV_HALO + ts - SUBLANES, CONV_DIM), F32),
                        pltpu.VMEM((1, LANES), F32)],
        compiler_params=_params("arbitrary", "arbitrary"),
        name="merge",
    )(z, og, gt, x, mod3, jnp.broadcast_to(w_dw.reshape(CONV_WIDTH, 1, CONV_DIM), (CONV_WIDTH, SUBLANES, CONV_DIM)), row(b_dw), row(ln_g), row(ln_b),
      w_pw.astype(BF16), row(b_pw), w_go.astype(BF16), w_out.astype(BF16), row(g2), wr, br)


def _row_copy(src, i, dst, j, sem):
    return pltpu.make_async_copy(src.at[pl.ds(pl.multiple_of(i, SUBLANES), SUBLANES), :],
                                 dst.at[pl.ds(pl.multiple_of(j, SUBLANES), SUBLANES), :], sem)


def _gather_rows(idx_ref, lo, n, src, dst, sem, row0=0):
    for r in range(n):
        _row_copy(src, idx_ref[0, 0, lo + r], dst, (row0 + r) * SUBLANES, sem).start(priority=r % 2)


def _invert_body(lo_ref, hi_ref, slot_ref, src_ref):
    i = pl.program_id(0)
    ts = slot_ref.shape[-1] // TOP_K

    @pl.when(i == 0)
    def _():
        def clear(p, carry):
            src_ref[p] = 0
            return carry

        for e in range(lo_ref.shape[0]):
            lax.fori_loop(lo_ref[e], hi_ref[e], clear, 0)

    for r in range(ts):
        row = (i * ts + r) * SUBLANES
        for k in range(TOP_K):
            src_ref[slot_ref[0, 0, k * ts + r]] = row


def _invert(fill_lo, fill_hi, slots, n_slots):
    nt, _, width = slots.shape
    return pl.pallas_call(
        _invert_body,
        grid_spec=pltpu.PrefetchScalarGridSpec(
            num_scalar_prefetch=2,
            grid=(nt,),
            in_specs=[pl.BlockSpec((1, 1, width), lambda i, lo, hi: (i, 0, 0), memory_space=pltpu.SMEM)],
            out_specs=pl.BlockSpec(memory_space=pltpu.SMEM)),
        out_shape=jax.ShapeDtypeStruct((n_slots,), jnp.int32),
        compiler_params=_params("arbitrary"),
        name="invert",
    )(fill_lo, fill_hi, slots)


def _expert_body(te_ref, nu_ref, src_ref, nxt_ref, u2_ref, w1_ref, w3_ref, w2_ref, ys_ref,
                 xbuf, w1b, w3b, w2b, sem):
    i = pl.program_id(0)
    n_used = nu_ref[0]
    cur = i % 2
    tmx = src_ref.shape[-1]

    def tile_wait(slot):
        pltpu.make_async_copy(u2_ref.at[pl.ds(0, tmx * SUBLANES), :], xbuf.at[slot], sem.at[slot]).wait()

    @pl.when(i == 0)
    def _():
        _gather_rows(src_ref, 0, tmx, u2_ref, xbuf.at[0], sem.at[0])

    @pl.when(jnp.logical_or(i == 0, te_ref[i] != te_ref[jnp.maximum(i - 1, 0)]))
    def _():
        w1b[...] = w1_ref[0].astype(BF16)
        w3b[...] = w3_ref[0].astype(BF16)
        w2b[...] = w2_ref[0].astype(BF16)

    @pl.when(i < n_used)
    def _():
        tile_wait(cur)
        x = _tiles_to_rows(xbuf.at[cur], tmx).astype(BF16)
        f = w1b.shape[1]
        d = w2b.shape[1]
        stages = []
        fh = f // 2
        h1 = []
        h3 = []
        for c in range(2):
            stages.append(lambda c=c: h1.append(_bdot(x, w1b[:, c * fh:(c + 1) * fh])))
        for c in range(2):
            stages.append(lambda c=c: h3.append(_bdot(x, w3b[:, c * fh:(c + 1) * fh])))
        hid = []
        dq = d // 4

        def out_piece(c):
            if not hid:
                a = jnp.concatenate(h1, axis=-1)
                hid.append((a * _sigmoid(a) * jnp.concatenate(h3, axis=-1)).astype(BF16))
            y = _bdot(hid[0], w2b[:, c * dq:(c + 1) * dq])
            for j in range(dq // LANES):
                col = c * (dq // LANES) + j
                ys_ref[pl.ds(col, tmx, stride=SUBLANES), :] = y[:, j * LANES:(j + 1) * LANES]

        for c in range(4):
            stages.append(lambda c=c: out_piece(c))
        burst = tmx // len(stages)
        for s, stage in enumerate(stages):
            _gather_rows(nxt_ref, s * burst, burst, u2_ref, xbuf.at[1 - cur], sem.at[1 - cur], row0=s * burst)
            stage()

    @pl.when(i == n_used - 1)
    def _():
        tile_wait(1 - cur)

    @pl.when(i >= n_used)
    def _():
        ys_ref[...] = jnp.zeros_like(ys_ref)


def _experts(tile_expert, n_used, src_rows, u2, w1, w3, w2):
    n_tiles, _, tmx = src_rows.shape
    ne, d, f = w1.shape

    def w_map(i, te, nu):
        return (te[i], 0, 0)

    return pl.pallas_call(
        _expert_body,
        grid_spec=pltpu.PrefetchScalarGridSpec(
            num_scalar_prefetch=2,
            grid=(n_tiles,),
            in_specs=[pl.BlockSpec((1, 1, tmx), lambda i, te, nu: (i, 0, 0), memory_space=pltpu.SMEM),
                      pl.BlockSpec((1, 1, tmx), lambda i, te, nu: (jnp.minimum(i + 1, n_tiles - 1), 0, 0),
                                   memory_space=pltpu.SMEM),
                      pl.BlockSpec(memory_space=pl.ANY),
                      pl.BlockSpec((1, d, f), w_map), pl.BlockSpec((1, d, f), w_map),
                      pl.BlockSpec((1, f, d), w_map)],
            out_specs=pl.BlockSpec((tmx * SUBLANES, LANES), lambda i, te, nu: (i, 0)),
            scratch_shapes=[pltpu.VMEM((2, tmx * SUBLANES, LANES), F32),
                            pltpu.VMEM((d, f), BF16), pltpu.VMEM((d, f), BF16), pltpu.VMEM((f, d), BF16),
                            pltpu.SemaphoreType.DMA((2,))]),
        out_shape=jax.ShapeDtypeStruct((n_tiles * tmx * SUBLANES, LANES), F32),
        compiler_params=_params("arbitrary"),
        name="experts",
    )(tile_expert, n_used, src_rows, src_rows, u2, w1, w3, w2)


def _final_body(p_ref, h_ref, route_ref, mod_ref, modf_ref, gf_ref, ys_ref, o_ref, y1_buf, y2_buf, sem):
    ts = h_ref.shape[1]

    _gather_rows(p_ref, 0, ts, ys_ref, y1_buf, sem)
    _gather_rows(p_ref, ts, ts, ys_ref, y2_buf, sem)
    pltpu.make_async_copy(ys_ref.at[pl.ds(0, ts * SUBLANES), :], y1_buf, sem).wait()
    pltpu.make_async_copy(ys_ref.at[pl.ds(0, ts * SUBLANES), :], y2_buf, sem).wait()

    route = route_ref[0]
    y2 = route[:, 2:3] * _tiles_to_rows(y1_buf, ts) + route[:, 3:4] * _tiles_to_rows(y2_buf, ts)
    h = h_ref[0] + mod_ref[0, 5:6, :] * y2
    o_ref[0] = _rms(h, gf_ref[...]) * (1.0 + modf_ref[0, 1:2, :]) + modf_ref[0, 0:1, :]


def _final(slots, h, route, mod3, modf3, gf, ys, ts):
    nb, s, d = h.shape
    nt = s // ts

    def tok(width):
        return pl.BlockSpec((1, ts, width), lambda b, i: (b, i, 0))

    return pl.pallas_call(
        _final_body,
        grid=(nb, nt),
        in_specs=[pl.BlockSpec((1, 1, 2 * ts), lambda b, i: (b * nt + i, 0, 0), memory_space=pltpu.SMEM),
                  tok(d), tok(LANES),
                  pl.BlockSpec((1, N_MOD, d), lambda b, i: (b, 0, 0)),
                  pl.BlockSpec((1, 2, d), lambda b, i: (b, 0, 0)),
                  pl.BlockSpec((1, d), lambda b, i: (0, 0)),
                  pl.BlockSpec(memory_space=pl.ANY)],
        out_specs=tok(d),
        out_shape=jax.ShapeDtypeStruct((nb, s, d), F32),
        scratch_shapes=[pltpu.VMEM((ts * SUBLANES, LANES), F32), pltpu.VMEM((ts * SUBLANES, LANES), F32),
                        pltpu.SemaphoreType.DMA(())],
        compiler_params=_params("arbitrary", "arbitrary"),
        name="final",
    )(slots, h, route, mod3, modf3, gf.reshape(1, d), ys)


def _plan(route, cnt, ts, tmx, n_tiles):
    t = route.shape[0]
    counts = cnt[0, :N_EXPERTS].astype(jnp.int32)
    tiles = (counts + (tmx - 1)) // tmx
    tile_end = jnp.cumsum(tiles)
    offs = (tile_end - tiles) * tmx
    n_used = tile_end[-1:]
    tile_ids = jnp.minimum(jnp.arange(n_tiles, dtype=jnp.int32), n_used[0] - 1)
    tile_expert = jnp.sum((tile_ids[:, None] >= tile_end[None, :]).astype(jnp.int32), axis=1)
    eid = route[:, 0:2].astype(jnp.int32)
    rank = route[:, 4:6].astype(jnp.int32)
    slot = jnp.take(offs, eid) + rank
    slots = slot.reshape(t // ts, ts, 2).transpose(0, 2, 1).reshape(t // ts, 1, 2 * ts)
    fill_lo = jnp.concatenate([offs + counts, tile_end[-1:] * tmx]).astype(jnp.int32)
    fill_hi = jnp.concatenate([tile_end * tmx, jnp.full((1,), n_tiles * tmx, jnp.int32)]).astype(jnp.int32)
    src = _invert(fill_lo, fill_hi, slots, n_tiles * tmx)
    return slots * SUBLANES, src.reshape(n_tiles, 1, tmx), tile_expert.astype(jnp.int32), n_used.astype(jnp.int32)


def kernel(x, c, w_ada, b_ada, g_norm1, w_in, w_dw, b_dw, g_conv_ln, b_conv_ln, w_conv_pw, b_conv_pw,
           w_a2, b_a2, g_gla_norm, w_gla_o, w_out, g_norm2, w_router_g, b_router_g, w_router_e,
           b_router_e, w1, w3, w2, w_ada_f, b_ada_f, g_final):
    nb, s, d = x.shape
    assert w_ada.shape[0] == 1, "single-layer block"
    tm = min(512, s)
    tc = min(256, s)
    ts = min(256, s)
    tmx = 256
    t = nb * s
    n_tiles = (t * TOP_K) // tmx + N_EXPERTS
    mod3 = _ada(c, w_ada[0], b_ada[0]).reshape(nb, N_MOD, d)
    modf3 = _ada(c, w_ada_f, b_ada_f).reshape(nb, 2, d)
    z, q, k, v, rs, lg, gt = _proj(x, mod3, g_norm1[0], w_in[0], w_a2[0], b_a2[0], tm)
    og = _gla(q, k, lg, v, rs, g_gla_norm[0], tc)
    h, u2, route, cnt = _merge(z, og, gt, x, mod3, w_dw[0], b_dw[0], g_conv_ln[0], b_conv_ln[0],
                               w_conv_pw[0], b_conv_pw[0], w_gla_o[0], w_out[0], g_norm2[0],
                               w_router_g[0], b_router_g[0], w_router_e[0], b_router_e[0], ts)
    route2 = route.reshape(t, LANES)
    slots, src_rows, tile_expert, n_used = _plan(route2, cnt, ts, tmx, n_tiles)
    ys = _experts(tile_expert, n_used, src_rows, u2, w1[0], w3[0], w2[0])
    return _final(slots, h, route, mod3, modf3, g_final, ys, ts)
```

```python
import functools

import jax
import jax.numpy as jnp
from jax import lax
from jax.experimental import pallas as pl
from jax.experimental.pallas import tpu as pltpu

F32 = jnp.float32
BF16 = jnp.bfloat16

EPS = 1e-6
CONV_DIM = 512
CONV_WIDTH = 31
GLA_HEADS = 4
GLA_DK = 128
GLA_DV = 256
GLA_LOWRANK = 16
GLA_TAU = 16.0
QK_DIM = GLA_HEADS * GLA_DK
V_DIM = GLA_HEADS * GLA_DV
N_GROUPS = 4
EXPERTS_PER_GROUP = 8
N_EXPERTS = N_GROUPS * EXPERTS_PER_GROUP
TOP_K = 2
N_MOD = 6

LANES = 128
SUBLANES = 8
PACK_ROWS = 4
CONV_ROWS = 32
CONV_HALO = 32
GLA_CHUNK = 128
VMEM_LIMIT = 56 * 1024 * 1024


def _bdot(a, b):
    return jnp.dot(a, b, preferred_element_type=F32)


def _split(a):
    hi = a.astype(BF16)
    lo = (a - hi.astype(F32)).astype(BF16)
    return hi, lo


def _dot3(a, b):
    ah, al = _split(a)
    bh, bl = _split(b)
    return _bdot(ah, bh) + (_bdot(ah, bl) + _bdot(al, bh))


def _sigmoid(x):
    return 1.0 / (1.0 + jnp.exp(-x))


def _rms(x, g):
    ms = jnp.mean(x * x, axis=-1, keepdims=True)
    return x * lax.rsqrt(ms + EPS) * g


def _rows_to_tiles(ref, val):
    n = val.shape[0]
    for j in range(val.shape[1] // LANES):
        ref[pl.ds(j, n, stride=SUBLANES), :] = val[:, j * LANES:(j + 1) * LANES]


def _tiles_to_rows(ref, n):
    return jnp.concatenate([ref[pl.ds(j, n, stride=SUBLANES), :] for j in range(SUBLANES)], axis=-1)


def _pack_rows(ref, val, tiles):
    _rows_to_tiles(tiles, val)
    ref[...] = pltpu.bitcast(tiles[...].astype(BF16), jnp.uint32)


def _unpack_rows(ref, n, tiles):
    tiles[...] = pltpu.bitcast(ref[...], BF16).astype(F32)
    return _tiles_to_rows(tiles, n).astype(BF16)


def _params(*sem):
    return pltpu.CompilerParams(dimension_semantics=sem, vmem_limit_bytes=VMEM_LIMIT)


def _ada_body(c_ref, w_ref, b_ref, o_ref):
    c = c_ref[...]
    o_ref[...] = _dot3(c * _sigmoid(c), w_ref[...]) + b_ref[...]


def _ada(c, w, b, tn=1024):
    nb, d = c.shape
    n = w.shape[1]
    return pl.pallas_call(
        _ada_body,
        grid=(n // tn,),
        in_specs=[pl.BlockSpec((nb, d), lambda j: (0, 0)),
                  pl.BlockSpec((d, tn), lambda j: (0, j)),
                  pl.BlockSpec((1, tn), lambda j: (0, j))],
        out_specs=pl.BlockSpec((nb, tn), lambda j: (0, j)),
        out_shape=jax.ShapeDtypeStruct((nb, n), F32),
        compiler_params=_params("arbitrary"),
        name="ada",
    )(c, w, b.reshape(1, n))


def _proj_body(x_ref, mod_ref, g1_ref, wm_ref, wa1_ref, wg_ref, wa2_ref, ba2_ref,
               z_ref, q_ref, k_ref, v_ref, rs_ref, lg_ref, gt_ref):
    x = x_ref[0]
    u = (_rms(x, g1_ref[...]) * (1.0 + mod_ref[0, 1:2, :]) + mod_ref[0, 0:1, :]).astype(BF16)
    c0 = 2 * CONV_DIM
    c1 = c0 + 2 * QK_DIM
    c2 = c1 + V_DIM
    c3 = c2 + V_DIM
    pc = _bdot(u, wm_ref[:, 0:c0])
    z_ref[0] = pc[:, :CONV_DIM] * _sigmoid(pc[:, CONV_DIM:])
    qk = _bdot(u, wm_ref[:, c0:c1])
    q_ref[0] = qk[:, :QK_DIM] * (GLA_DK ** -0.5)
    k_ref[0] = qk[:, QK_DIM:]
    v_ref[0] = _bdot(u, wm_ref[:, c1:c2]).astype(BF16)
    r = _bdot(u, wm_ref[:, c2:c3])
    rs_ref[0] = (r * _sigmoid(r)).astype(BF16)
    a1 = _bdot(u, wa1_ref[...])
    xg = _dot3(a1, wa2_ref[...]) + ba2_ref[...]
    lg_ref[0] = (jnp.minimum(xg, 0.0) - jnp.log1p(jnp.exp(-jnp.abs(xg)))) * (1.0 / GLA_TAU)
    gt_ref[0] = _sigmoid(_bdot(u, wg_ref[...])).astype(BF16)


def _proj(x, mod3, g1, w_in, w_a2, b_a2, tm):
    nb, s, d = x.shape
    c3 = 2 * CONV_DIM + 2 * QK_DIM + 2 * V_DIM
    wm = w_in[:, :c3].astype(BF16)
    wa1 = jnp.pad(w_in[:, c3:c3 + GLA_LOWRANK], ((0, 0), (0, LANES - GLA_LOWRANK))).astype(BF16)
    wg = w_in[:, c3 + GLA_LOWRANK:].astype(BF16)
    wa2 = jnp.pad(w_a2, ((0, LANES - GLA_LOWRANK), (0, 0)))
    ng = wg.shape[1]

    def tok(width):
        return pl.BlockSpec((1, tm, width), lambda b, i: (b, i, 0))

    def const(shape):
        return pl.BlockSpec(shape, lambda b, i: (0,) * len(shape))

    def out(width, dt):
        return jax.ShapeDtypeStruct((nb, s, width), dt)

    return pl.pallas_call(
        _proj_body,
        grid=(nb, s // tm),
        in_specs=[tok(d),
                  pl.BlockSpec((1, N_MOD, d), lambda b, i: (b, 0, 0)),
                  const((1, d)), const(wm.shape), const(wa1.shape), const(wg.shape),
                  const(wa2.shape), const((1, QK_DIM))],
        out_specs=[tok(CONV_DIM), tok(QK_DIM), tok(QK_DIM), tok(V_DIM), tok(V_DIM), tok(QK_DIM), tok(ng)],
        out_shape=[out(CONV_DIM, F32), out(QK_DIM, F32), out(QK_DIM, F32), out(V_DIM, BF16),
                   out(V_DIM, BF16), out(QK_DIM, F32), out(ng, BF16)],
        compiler_params=_params("arbitrary", "arbitrary"),
        name="proj",
    )(x, mod3, g1.reshape(1, d), wm, wa1, wg, wa2, b_a2.reshape(1, QK_DIM))


def _gla_body(q_ref, k_ref, lg_ref, v_ref, rs_ref, gn_ref, o_ref, st_ref, *, n_chunks):
    cl = GLA_CHUNK

    @pl.when(pl.program_id(1) == 0)
    def _():
        st_ref[...] = jnp.zeros_like(st_ref)

    row = lax.broadcasted_iota(jnp.int32, (cl, cl), 0)
    col = lax.broadcasted_iota(jnp.int32, (cl, cl), 1)
    causal = col <= row
    tri = jnp.where(causal, 1.0, 0.0).astype(BF16)

    for h in range(GLA_HEADS):
        ks = slice(h * GLA_DK, (h + 1) * GLA_DK)
        vs = slice(h * GLA_DV, (h + 1) * GLA_DV)
        state = st_ref[h]
        for c in range(n_chunks):
            rows = slice(c * cl, (c + 1) * cl)
            gh, gl = _split(lg_ref[0, rows, ks])
            b = _bdot(tri, gh) + _bdot(tri, gl)
            q = q_ref[0, rows, ks]
            v = v_ref[0, rows, vs]
            qe = (q * jnp.exp(b)).astype(BF16)
            kt = k_ref[0, rows, ks].T
            bt = b.T
            bl = bt[:, cl - 1:cl]
            ket = (kt * jnp.exp(-bt)).astype(BF16)
            klt = (kt * jnp.exp(bl - bt)).astype(BF16)
            att = jnp.where(causal, _bdot(qe, ket), 0.0).astype(BF16)
            o = _bdot(att, v) + _bdot(qe, state.astype(BF16))
            state = jnp.exp(bl) * state + _bdot(klt, v)
            on = _rms(o, gn_ref[:, vs])
            o_ref[0, rows, vs] = (on * rs_ref[0, rows, vs].astype(F32)).astype(BF16)
        st_ref[h] = state


def _gla(q, k, lg, v, rs, gn, tc):
    nb, s, _ = q.shape

    def tok(width):
        return pl.BlockSpec((1, tc, width), lambda b, i: (b, i, 0))

    return pl.pallas_call(
        functools.partial(_gla_body, n_chunks=tc // GLA_CHUNK),
        grid=(nb, s // tc),
        in_specs=[tok(QK_DIM), tok(QK_DIM), tok(QK_DIM), tok(V_DIM), tok(V_DIM),
                  pl.BlockSpec((1, V_DIM), lambda b, i: (0, 0))],
        out_specs=tok(V_DIM),
        out_shape=jax.ShapeDtypeStruct((nb, s, V_DIM), BF16),
        scratch_shapes=[pltpu.VMEM((GLA_HEADS, GLA_DK, GLA_DV), F32)],
        compiler_params=_params("arbitrary", "arbitrary"),
        name="gla",
    )(q, k, lg, v, rs, gn.reshape(1, V_DIM))


def _route(logits, cnt):
    ts = logits.shape[0]
    lane = lax.broadcasted_iota(jnp.int32, (ts, LANES), 1).astype(F32)
    ninf = -jnp.inf
    lgm = jnp.where(lane < N_GROUPS, logits, ninf)
    gmax = jnp.max(lgm, axis=-1, keepdims=True)
    gsel = jnp.min(jnp.where(lgm == gmax, lane, float(LANES)), axis=-1, keepdims=True)
    wg = 1.0 / jnp.sum(jnp.exp(lgm - gmax), axis=-1, keepdims=True)
    base = N_GROUPS + EXPERTS_PER_GROUP * gsel
    le = jnp.where(lane >= base, jnp.where(lane < base + EXPERTS_PER_GROUP, logits, ninf), ninf)
    v1 = jnp.max(le, axis=-1, keepdims=True)
    i1 = jnp.min(jnp.where(le == v1, lane, float(LANES)), axis=-1, keepdims=True)
    le2 = jnp.where(lane == i1, ninf, le)
    v2 = jnp.max(le2, axis=-1, keepdims=True)
    i2 = jnp.min(jnp.where(le2 == v2, lane, float(LANES)), axis=-1, keepdims=True)
    e21 = jnp.exp(v2 - v1)
    w1 = wg / (1.0 + e21)
    w2 = w1 * e21
    eid1 = i1 - N_GROUPS
    eid2 = i2 - N_GROUPS
    oh1 = jnp.where(lane == eid1, 1.0, 0.0)
    oh2 = jnp.where(lane == eid2, 1.0, 0.0)
    ohs = oh1 + oh2
    row = lax.broadcasted_iota(jnp.int32, (ts, ts), 0)
    col = lax.broadcasted_iota(jnp.int32, (ts, ts), 1)
    before = jnp.where(col < row, 1.0, 0.0).astype(BF16)
    tot = cnt + _bdot(before, ohs.astype(BF16))
    rank1 = jnp.sum(oh1 * tot, axis=-1, keepdims=True)
    rank2 = jnp.sum(oh2 * tot, axis=-1, keepdims=True)
    packed = jnp.where(lane == 0.0, eid1,
             jnp.where(lane == 1.0, eid2,
             jnp.where(lane == 2.0, w1,
             jnp.where(lane == 3.0, w2,
             jnp.where(lane == 4.0, rank1,
             jnp.where(lane == 5.0, rank2, 0.0))))))
    return packed, cnt + jnp.sum(ohs, axis=0, keepdims=True)


def _merge_body(z_ref, og_ref, gt_ref, x_ref, mod_ref, wdw_ref, bdw_ref, lng_ref, lnb_ref,
                wpw_ref, bpw_ref, wgo_ref, wout_ref, g2_ref, wr_ref, br_ref,
                h_ref, u2_ref, route_ref, cnt_ref, zbuf, zsh, u2t, cnt_sc):
    ts = z_ref.shape[1]
    d = x_ref.shape[2]
    first_tile = pl.program_id(1) == 0

    @pl.when(first_tile)
    def _():
        zbuf[0:CONV_HALO, :] = jnp.zeros((CONV_HALO, CONV_DIM), F32)

    @pl.when(jnp.logical_and(first_tile, pl.program_id(0) == 0))
    def _():
        cnt_sc[...] = jnp.zeros_like(cnt_sc)

    zbuf[CONV_HALO:CONV_HALO + ts, :] = z_ref[0]
    span = ts + CONV_HALO - SUBLANES
    for r in range(1, SUBLANES):
        zsh[r - 1] = zbuf[r:r + span, :]
    off = CONV_HALO - (CONV_WIDTH - 1)
    pieces = []
    for blk in range(ts // CONV_ROWS):
        acc = None
        for j in range(CONV_WIDTH):
            a, r = divmod(off + j, SUBLANES)
            lo = a * SUBLANES + blk * CONV_ROWS
            src = zbuf[lo:lo + CONV_ROWS, :] if r == 0 else zsh[r - 1, lo:lo + CONV_ROWS, :]
            term = src * jnp.concatenate([wdw_ref[j]] * (CONV_ROWS // SUBLANES), axis=0)
            acc = term if acc is None else acc + term
        conv = acc + bdw_ref[...]
        mu = jnp.mean(conv, axis=-1, keepdims=True)
        xc = conv - mu
        var = jnp.mean(xc * xc, axis=-1, keepdims=True)
        ln = xc * lax.rsqrt(var + EPS) * lng_ref[...] + lnb_ref[...]
        pieces.append((ln * _sigmoid(ln)).astype(BF16))
    zbuf[0:CONV_HALO, :] = zbuf[ts:ts + CONV_HALO, :]
    y_conv = _bdot(jnp.concatenate(pieces, axis=0), wpw_ref[...]) + bpw_ref[...]
    y_gla = _bdot(og_ref[0], wgo_ref[...])
    merged = gt_ref[0, :, 0:d].astype(F32) * y_conv + gt_ref[0, :, d:2 * d].astype(F32) * y_gla
    y = _bdot(merged.astype(BF16), wout_ref[...])
    h = x_ref[0] + mod_ref[0, 2:3, :] * y
    h_ref[0] = h
    u2 = _rms(h, g2_ref[...]) * (1.0 + mod_ref[0, 4:5, :]) + mod_ref[0, 3:4, :]
    _pack_rows(u2_ref, u2, u2t)
    logits = _dot3(u2, wr_ref[...]) + br_ref[...]
    packed, cnt = _route(logits, cnt_sc[...])
    route_ref[0] = packed
    cnt_sc[...] = cnt
    cnt_ref[...] = jnp.broadcast_to(cnt, cnt_ref.shape)


def _merge(z, og, gt, x, mod3, w_dw, b_dw, ln_g, ln_b, w_pw, b_pw, w_go, w_out, g2, w_rg, b_rg, w_re, b_re, ts):
    nb, s, d = x.shape
    npad = LANES - N_GROUPS - N_EXPERTS
    wr = jnp.pad(jnp.concatenate([w_rg, w_re], axis=1), ((0, 0), (0, npad)))
    br = jnp.pad(jnp.concatenate([b_rg, b_re]), (0, npad)).reshape(1, LANES)

    def tok(width):
        return pl.BlockSpec((1, ts, width), lambda b, i: (b, i, 0))

    def const(shape):
        return pl.BlockSpec(shape, lambda b, i: (0,) * len(shape))

    def row(v):
        return v.reshape(1, v.shape[-1])

    return pl.pallas_call(
        _merge_body,
        grid=(nb, s // ts),
        in_specs=[tok(CONV_DIM), tok(V_DIM), tok(2 * d), tok(d),
                  pl.BlockSpec((1, N_MOD, d), lambda b, i: (b, 0, 0)),
                  const((CONV_WIDTH, SUBLANES, CONV_DIM)), const((1, CONV_DIM)), const((1, CONV_DIM)), const((1, CONV_DIM)),
                  const((CONV_DIM, d)), const((1, d)), const((V_DIM, d)), const((d, d)), const((1, d)),
                  const((d, LANES)), const((1, LANES))],
        out_specs=[tok(d), pl.BlockSpec((ts * PACK_ROWS, LANES), lambda b, i: (b * (s // ts) + i, 0)), tok(LANES),
                   pl.BlockSpec((8, LANES), lambda b, i: (0, 0))],
        out_shape=[jax.ShapeDtypeStruct((nb, s, d), F32),
                   jax.ShapeDtypeStruct((nb * s * PACK_ROWS, LANES), jnp.uint32),
                   jax.ShapeDtypeStruct((nb, s, LANES), F32), jax.ShapeDtypeStruct((8, LANES), F32)],
        scratch_shapes=[pltpu.VMEM((CONV_HALO + ts, CONV_DIM), F32),
                        pltpu.VMEM((SUBLANES - 1, CONV_HALO + ts - SUBLANES, CONV_DIM), F32),
                        pltpu.VMEM((ts * SUBLANES, LANES), F32),
                        pltpu.VMEM((1, LANES), F32)],
        compiler_params=_params("arbitrary", "arbitrary"),
        name="merge",
    )(z, og, gt, x, mod3, jnp.broadcast_to(w_dw.reshape(CONV_WIDTH, 1, CONV_DIM), (CONV_WIDTH, SUBLANES, CONV_DIM)), row(b_dw), row(ln_g), row(ln_b),
      w_pw.astype(BF16), row(b_pw), w_go.astype(BF16), w_out.astype(BF16), row(g2), wr, br)


def _row_copy(src, i, dst, j, sem):
    return pltpu.make_async_copy(src.at[pl.ds(pl.multiple_of(i, SUBLANES), SUBLANES), :],
                                 dst.at[pl.ds(pl.multiple_of(j, SUBLANES), SUBLANES), :], sem)


def _gather_rows(idx_ref, lo, n, src, dst, sem, row0=0):
    for r in range(n):
        _row_copy(src, idx_ref[0, 0, lo + r], dst, (row0 + r) * SUBLANES, sem).start(priority=r % 2)


def _invert_body(lo_ref, hi_ref, slot_ref, src_ref):
    i = pl.program_id(0)
    ts = slot_ref.shape[-1] // TOP_K

    @pl.when(i == 0)
    def _():
        def clear(p, carry):
            src_ref[p] = 0
            return carry

        for e in range(lo_ref.shape[0]):
            lax.fori_loop(lo_ref[e], hi_ref[e], clear, 0)

    for r in range(ts):
        row = (i * ts + r) * PACK_ROWS
        for k in range(TOP_K):
            src_ref[slot_ref[0, 0, k * ts + r]] = row


def _invert(fill_lo, fill_hi, slots, n_slots):
    nt, _, width = slots.shape
    return pl.pallas_call(
        _invert_body,
        grid_spec=pltpu.PrefetchScalarGridSpec(
            num_scalar_prefetch=2,
            grid=(nt,),
            in_specs=[pl.BlockSpec((1, 1, width), lambda i, lo, hi: (i, 0, 0), memory_space=pltpu.SMEM)],
            out_specs=pl.BlockSpec(memory_space=pltpu.SMEM)),
        out_shape=jax.ShapeDtypeStruct((n_slots,), jnp.int32),
        compiler_params=_params("arbitrary"),
        name="invert",
    )(fill_lo, fill_hi, slots)


def _expert_body(te_ref, nu_ref, src_ref, u2_ref, w1_ref, w3_ref, w2_ref, ys_ref,
                 u2v, xg, xt, w1b, w3b, w2b, sem):
    i = pl.program_id(0)
    tmx = src_ref.shape[-1]

    @pl.when(i == 0)
    def _():
        load = pltpu.make_async_copy(u2_ref, u2v, sem)
        load.start()
        load.wait()

    @pl.when(jnp.logical_or(i == 0, te_ref[i] != te_ref[jnp.maximum(i - 1, 0)]))
    def _():
        w1b[...] = w1_ref[0].astype(BF16)
        w3b[...] = w3_ref[0].astype(BF16)
        w2b[...] = w2_ref[0].astype(BF16)

    @pl.when(i < nu_ref[0])
    def _():
        for r in range(tmx):
            row = pl.multiple_of(src_ref[0, 0, r], PACK_ROWS)
            xg[r * PACK_ROWS:(r + 1) * PACK_ROWS, :] = u2v[pl.ds(row, PACK_ROWS), :]
        x = _unpack_rows(xg, tmx, xt)
        h1 = _bdot(x, w1b[...])
        h3 = _bdot(x, w3b[...])
        hid = (h1 * _sigmoid(h1) * h3).astype(BF16)
        _rows_to_tiles(ys_ref, _bdot(hid, w2b[...]))

    @pl.when(i >= nu_ref[0])
    def _():
        ys_ref[...] = jnp.zeros_like(ys_ref)


def _experts(tile_expert, n_used, src_rows, u2p, w1, w3, w2):
    n_tiles, _, tmx = src_rows.shape
    ne, d, f = w1.shape

    def w_map(i, te, nu):
        return (te[i], 0, 0)

    return pl.pallas_call(
        _expert_body,
        grid_spec=pltpu.PrefetchScalarGridSpec(
            num_scalar_prefetch=2,
            grid=(n_tiles,),
            in_specs=[pl.BlockSpec((1, 1, tmx), lambda i, te, nu: (i, 0, 0), memory_space=pltpu.SMEM),
                      pl.BlockSpec(memory_space=pl.ANY),
                      pl.BlockSpec((1, d, f), w_map), pl.BlockSpec((1, d, f), w_map),
                      pl.BlockSpec((1, f, d), w_map)],
            out_specs=pl.BlockSpec((tmx * SUBLANES, LANES), lambda i, te, nu: (i, 0)),
            scratch_shapes=[pltpu.VMEM(u2p.shape, jnp.uint32),
                            pltpu.VMEM((tmx * PACK_ROWS, LANES), jnp.uint32),
                            pltpu.VMEM((tmx * SUBLANES, LANES), F32),
                            pltpu.VMEM((d, f), BF16), pltpu.VMEM((d, f), BF16), pltpu.VMEM((f, d), BF16),
                            pltpu.SemaphoreType.DMA(())]),
        out_shape=jax.ShapeDtypeStruct((n_tiles * tmx * SUBLANES, LANES), F32),
        compiler_params=_params("arbitrary"),
        name="experts",
    )(tile_expert, n_used, src_rows, u2p, w1, w3, w2)


def _final_body(p_ref, h_ref, route_ref, mod_ref, modf_ref, gf_ref, ys_ref, o_ref, y1_buf, y2_buf, sem):
    ts = h_ref.shape[1]

    _gather_rows(p_ref, 0, ts, ys_ref, y1_buf, sem)
    _gather_rows(p_ref, ts, ts, ys_ref, y2_buf, sem)
    pltpu.make_async_copy(ys_ref.at[pl.ds(0, ts * SUBLANES), :], y1_buf, sem).wait()
    pltpu.make_async_copy(ys_ref.at[pl.ds(0, ts * SUBLANES), :], y2_buf, sem).wait()

    route = route_ref[0]
    y2 = route[:, 2:3] * _tiles_to_rows(y1_buf, ts) + route[:, 3:4] * _tiles_to_rows(y2_buf, ts)
    h = h_ref[0] + mod_ref[0, 5:6, :] * y2
    o_ref[0] = _rms(h, gf_ref[...]) * (1.0 + modf_ref[0, 1:2, :]) + modf_ref[0, 0:1, :]


def _final(slots, h, route, mod3, modf3, gf, ys, ts):
    nb, s, d = h.shape
    nt = s // ts

    def tok(width):
        return pl.BlockSpec((1, ts, width), lambda b, i: (b, i, 0))

    return pl.pallas_call(
        _final_body,
        grid=(nb, nt),
        in_specs=[pl.BlockSpec((1, 1, 2 * ts), lambda b, i: (b * nt + i, 0, 0), memory_space=pltpu.SMEM),
                  tok(d), tok(LANES),
                  pl.BlockSpec((1, N_MOD, d), lambda b, i: (b, 0, 0)),
                  pl.BlockSpec((1, 2, d), lambda b, i: (b, 0, 0)),
                  pl.BlockSpec((1, d), lambda b, i: (0, 0)),
                  pl.BlockSpec(memory_space=pl.ANY)],
        out_specs=tok(d),
        out_shape=jax.ShapeDtypeStruct((nb, s, d), F32),
        scratch_shapes=[pltpu.VMEM((ts * SUBLANES, LANES), F32), pltpu.VMEM((ts * SUBLANES, LANES), F32),
                        pltpu.SemaphoreType.DMA(())],
        compiler_params=_params("arbitrary", "arbitrary"),
        name="final",
    )(slots, h, route, mod3, modf3, gf.reshape(1, d), ys)


def _plan(route, cnt, ts, tmx, n_tiles):
    t = route.shape[0]
    counts = cnt[0, :N_EXPERTS].astype(jnp.int32)
    tiles = (counts + (tmx - 1)) // tmx
    tile_end = jnp.cumsum(tiles)
    offs = (tile_end - tiles) * tmx
    n_used = tile_end[-1:]
    tile_ids = jnp.minimum(jnp.arange(n_tiles, dtype=jnp.int32), n_used[0] - 1)
    tile_expert = jnp.sum((tile_ids[:, None] >= tile_end[None, :]).astype(jnp.int32), axis=1)
    eid = route[:, 0:2].astype(jnp.int32)
    rank = route[:, 4:6].astype(jnp.int32)
    slot = jnp.take(offs, eid) + rank
    slots = slot.reshape(t // ts, ts, 2).transpose(0, 2, 1).reshape(t // ts, 1, 2 * ts)
    fill_lo = jnp.concatenate([offs + counts, tile_end[-1:] * tmx]).astype(jnp.int32)
    fill_hi = jnp.concatenate([tile_end * tmx, jnp.full((1,), n_tiles * tmx, jnp.int32)]).astype(jnp.int32)
    src = _invert(fill_lo, fill_hi, slots, n_tiles * tmx)
    return slots * SUBLANES, src.reshape(n_tiles, 1, tmx), tile_expert.astype(jnp.int32), n_used.astype(jnp.int32)


def kernel(x, c, w_ada, b_ada, g_norm1, w_in, w_dw, b_dw, g_conv_ln, b_conv_ln, w_conv_pw, b_conv_pw,
           w_a2, b_a2, g_gla_norm, w_gla_o, w_out, g_norm2, w_router_g, b_router_g, w_router_e,
           b_router_e, w1, w3, w2, w_ada_f, b_ada_f, g_final):
    nb, s, d = x.shape
    assert w_ada.shape[0] == 1, "single-layer block"
    assert d == 2 * PACK_ROWS * LANES, "packed token rows assume D_MODEL = 1024"
    tm = min(512, s)
    tc = min(256, s)
    ts = min(256, s)
    tmx = 256
    t = nb * s
    n_tiles = (t * TOP_K) // tmx + N_EXPERTS
    mod3 = _ada(c, w_ada[0], b_ada[0]).reshape(nb, N_MOD, d)
    modf3 = _ada(c, w_ada_f, b_ada_f).reshape(nb, 2, d)
    z, q, k, v, rs, lg, gt = _proj(x, mod3, g_norm1[0], w_in[0], w_a2[0], b_a2[0], tm)
    og = _gla(q, k, lg, v, rs, g_gla_norm[0], tc)
    h, u2, route, cnt = _merge(z, og, gt, x, mod3, w_dw[0], b_dw[0], g_conv_ln[0], b_conv_ln[0],
                               w_conv_pw[0], b_conv_pw[0], w_gla_o[0], w_out[0], g_norm2[0],
                               w_router_g[0], b_router_g[0], w_router_e[0], b_router_e[0], ts)
    route2 = route.reshape(t, LANES)
    slots, src_rows, tile_expert, n_used = _plan(route2, cnt, ts, tmx, n_tiles)
    ys = _experts(tile_expert, n_used, src_rows, u2, w1[0], w3[0], w2[0])
    return _final(slots, h, route, mod3, modf3, g_final, ys, ts)
```

```python
import functools

import jax
import jax.numpy as jnp
from jax import lax
from jax.experimental import pallas as pl
from jax.experimental.pallas import tpu as pltpu

F32 = jnp.float32
BF16 = jnp.bfloat16

EPS = 1e-6
CONV_DIM = 512
CONV_WIDTH = 31
GLA_HEADS = 4
GLA_DK = 128
GLA_DV = 256
GLA_LOWRANK = 16
GLA_TAU = 16.0
QK_DIM = GLA_HEADS * GLA_DK
V_DIM = GLA_HEADS * GLA_DV
N_GROUPS = 4
EXPERTS_PER_GROUP = 8
N_EXPERTS = N_GROUPS * EXPERTS_PER_GROUP
TOP_K = 2
N_MOD = 6

LANES = 128
SUBLANES = 8
PACK_ROWS = 4
CONV_ROWS = 32
MERGE_ROWS = 128
CLEAR_UNROLL = 16
CONV_HALO = 32
GLA_CHUNK = 128
VMEM_LIMIT = 56 * 1024 * 1024


def _bdot(a, b):
    return jnp.dot(a, b, preferred_element_type=F32)


def _split(a):
    hi = a.astype(BF16)
    lo = (a - hi.astype(F32)).astype(BF16)
    return hi, lo


def _dot3(a, b):
    ah, al = _split(a)
    bh, bl = _split(b)
    return _bdot(ah, bh) + (_bdot(ah, bl) + _bdot(al, bh))


def _sigmoid(x):
    return 1.0 / (1.0 + jnp.exp(-x))


def _rms(x, g):
    ms = jnp.mean(x * x, axis=-1, keepdims=True)
    return x * lax.rsqrt(ms + EPS) * g


def _rows_to_tiles(ref, val):
    n = val.shape[0]
    for j in range(val.shape[1] // LANES):
        ref[pl.ds(j, n, stride=SUBLANES), :] = val[:, j * LANES:(j + 1) * LANES]


def _tiles_to_rows(ref, n):
    return jnp.concatenate([ref[pl.ds(j, n, stride=SUBLANES), :] for j in range(SUBLANES)], axis=-1)


def _pack_rows(ref, val, tiles):
    _rows_to_tiles(tiles, val)
    ref[...] = pltpu.bitcast(tiles[...].astype(BF16), jnp.uint32)


def _unpack_rows(ref, n, tiles):
    tiles[...] = pltpu.bitcast(ref[...], BF16).astype(F32)
    return _tiles_to_rows(tiles, n).astype(BF16)


def _params(*sem):
    return pltpu.CompilerParams(dimension_semantics=sem, vmem_limit_bytes=VMEM_LIMIT)


def _ada_body(c_ref, w_ref, b_ref, o_ref):
    c = c_ref[...]
    o_ref[...] = _dot3(c * _sigmoid(c), w_ref[...]) + b_ref[...]


def _ada(c, w, b, tn=1024):
    nb, d = c.shape
    n = w.shape[1]
    return pl.pallas_call(
        _ada_body,
        grid=(n // tn,),
        in_specs=[pl.BlockSpec((nb, d), lambda j: (0, 0)),
                  pl.BlockSpec((d, tn), lambda j: (0, j)),
                  pl.BlockSpec((1, tn), lambda j: (0, j))],
        out_specs=pl.BlockSpec((nb, tn), lambda j: (0, j)),
        out_shape=jax.ShapeDtypeStruct((nb, n), F32),
        compiler_params=_params("arbitrary"),
        name="ada",
    )(c, w, b.reshape(1, n))


def _proj_body(x_ref, mod_ref, g1_ref, wm_ref, wa1_ref, wg_ref, wa2_ref, ba2_ref,
               z_ref, q_ref, k_ref, v_ref, rs_ref, lg_ref, gt_ref):
    x = x_ref[0]
    u = (_rms(x, g1_ref[...]) * (1.0 + mod_ref[0, 1:2, :]) + mod_ref[0, 0:1, :]).astype(BF16)
    c0 = 2 * CONV_DIM
    c1 = c0 + 2 * QK_DIM
    c2 = c1 + V_DIM
    c3 = c2 + V_DIM
    pc = _bdot(u, wm_ref[:, 0:c0])
    z_ref[0] = pc[:, :CONV_DIM] * _sigmoid(pc[:, CONV_DIM:])
    qk = _bdot(u, wm_ref[:, c0:c1])
    q_ref[0] = qk[:, :QK_DIM] * (GLA_DK ** -0.5)
    k_ref[0] = qk[:, QK_DIM:]
    v_ref[0] = _bdot(u, wm_ref[:, c1:c2]).astype(BF16)
    r = _bdot(u, wm_ref[:, c2:c3])
    rs_ref[0] = (r * _sigmoid(r)).astype(BF16)
    a1 = _bdot(u, wa1_ref[...])
    xg = _dot3(a1, wa2_ref[...]) + ba2_ref[...]
    lg_ref[0] = (jnp.minimum(xg, 0.0) - jnp.log1p(jnp.exp(-jnp.abs(xg)))) * (1.0 / GLA_TAU)
    gt_ref[0] = _sigmoid(_bdot(u, wg_ref[...])).astype(BF16)


def _proj(x, mod3, g1, w_in, w_a2, b_a2, tm):
    nb, s, d = x.shape
    c3 = 2 * CONV_DIM + 2 * QK_DIM + 2 * V_DIM
    wm = w_in[:, :c3].astype(BF16)
    wa1 = jnp.pad(w_in[:, c3:c3 + GLA_LOWRANK], ((0, 0), (0, LANES - GLA_LOWRANK))).astype(BF16)
    wg = w_in[:, c3 + GLA_LOWRANK:].astype(BF16)
    wa2 = jnp.pad(w_a2, ((0, LANES - GLA_LOWRANK), (0, 0)))
    ng = wg.shape[1]

    def tok(width):
        return pl.BlockSpec((1, tm, width), lambda b, i: (b, i, 0))

    def const(shape):
        return pl.BlockSpec(shape, lambda b, i: (0,) * len(shape))

    def out(width, dt):
        return jax.ShapeDtypeStruct((nb, s, width), dt)

    return pl.pallas_call(
        _proj_body,
        grid=(nb, s // tm),
        in_specs=[tok(d),
                  pl.BlockSpec((1, N_MOD, d), lambda b, i: (b, 0, 0)),
                  const((1, d)), const(wm.shape), const(wa1.shape), const(wg.shape),
                  const(wa2.shape), const((1, QK_DIM))],
        out_specs=[tok(CONV_DIM), tok(QK_DIM), tok(QK_DIM), tok(V_DIM), tok(V_DIM), tok(QK_DIM), tok(ng)],
        out_shape=[out(CONV_DIM, F32), out(QK_DIM, F32), out(QK_DIM, F32), out(V_DIM, BF16),
                   out(V_DIM, BF16), out(QK_DIM, F32), out(ng, BF16)],
        compiler_params=_params("arbitrary", "arbitrary"),
        name="proj",
    )(x, mod3, g1.reshape(1, d), wm, wa1, wg, wa2, b_a2.reshape(1, QK_DIM))


def _gla_body(q_ref, k_ref, lg_ref, v_ref, rs_ref, gn_ref, o_ref, st_ref, *, n_chunks):
    cl = GLA_CHUNK

    @pl.when(pl.program_id(1) == 0)
    def _():
        st_ref[...] = jnp.zeros_like(st_ref)

    row = lax.broadcasted_iota(jnp.int32, (cl, cl), 0)
    col = lax.broadcasted_iota(jnp.int32, (cl, cl), 1)
    causal = col <= row
    tri = jnp.where(causal, 1.0, 0.0).astype(BF16)

    chunks = [slice(c * cl, (c + 1) * cl) for c in range(n_chunks)]
    heads = range(GLA_HEADS)
    ksl = [slice(h * GLA_DK, (h + 1) * GLA_DK) for h in heads]
    vsl = [slice(h * GLA_DV, (h + 1) * GLA_DV) for h in heads]
    cums = []
    for rows in chunks:
        gh, gl = _split(lg_ref[0, rows, :])
        cums.append(_bdot(tri, gh) + _bdot(tri, gl))
    qes, kets, klts, decays = [], [], [], []
    for rows, b in zip(chunks, cums):
        qes.append((q_ref[0, rows, :] * jnp.exp(b)).astype(BF16))
        kt = k_ref[0, rows, :].T
        bt = b.T
        bl = bt[:, cl - 1:cl]
        kets.append((kt * jnp.exp(-bt)).astype(BF16))
        klts.append((kt * jnp.exp(bl - bt)).astype(BF16))
        decays.append(jnp.exp(bl))
    atts = {}
    for c in range(n_chunks):
        for h in heads:
            atts[h, c] = jnp.where(causal, _bdot(qes[c][:, ksl[h]], kets[c][ksl[h], :]), 0.0).astype(BF16)
    parts = {}
    for c, rows in enumerate(chunks):
        for h in heads:
            v = v_ref[0, rows, vsl[h]]
            parts[h, c] = (qes[c][:, ksl[h]], _bdot(atts[h, c], v), decays[c][ksl[h], :],
                           _bdot(klts[c][ksl[h], :], v))

    outs = {}
    states = []
    for h in range(GLA_HEADS):
        vs = slice(h * GLA_DV, (h + 1) * GLA_DV)
        state = st_ref[h]
        for c in range(n_chunks):
            rows = slice(c * cl, (c + 1) * cl)
            qe, o_intra, decay, update = parts[h, c]
            o = o_intra + _bdot(qe, state.astype(BF16))
            state = decay * state + update
            outs[h, c] = (_rms(o, gn_ref[:, vs]) * rs_ref[0, rows, vs].astype(F32)).astype(BF16)
        states.append(state)
    for c in range(n_chunks):
        o_ref[0, c * cl:(c + 1) * cl, :] = jnp.concatenate([outs[h, c] for h in range(GLA_HEADS)], axis=-1)
    for h in range(GLA_HEADS):
        st_ref[h] = states[h]


def _gla(q, k, lg, v, rs, gn, tc):
    nb, s, _ = q.shape

    def tok(width):
        return pl.BlockSpec((1, tc, width), lambda b, i: (b, i, 0))

    return pl.pallas_call(
        functools.partial(_gla_body, n_chunks=tc // GLA_CHUNK),
        grid=(nb, s // tc),
        in_specs=[tok(QK_DIM), tok(QK_DIM), tok(QK_DIM), tok(V_DIM), tok(V_DIM),
                  pl.BlockSpec((1, V_DIM), lambda b, i: (0, 0))],
        out_specs=tok(V_DIM),
        out_shape=jax.ShapeDtypeStruct((nb, s, V_DIM), BF16),
        scratch_shapes=[pltpu.VMEM((GLA_HEADS, GLA_DK, GLA_DV), F32)],
        compiler_params=_params("arbitrary", "arbitrary"),
        name="gla",
    )(q, k, lg, v, rs, gn.reshape(1, V_DIM))


def _route(logits, cnt):
    ts = logits.shape[0]
    lane = lax.broadcasted_iota(jnp.int32, (ts, LANES), 1).astype(F32)
    ninf = -jnp.inf
    lgm = jnp.where(lane < N_GROUPS, logits, ninf)
    gmax = jnp.max(lgm, axis=-1, keepdims=True)
    gsel = jnp.min(jnp.where(lgm == gmax, lane, float(LANES)), axis=-1, keepdims=True)
    wg = 1.0 / jnp.sum(jnp.exp(lgm - gmax), axis=-1, keepdims=True)
    base = N_GROUPS + EXPERTS_PER_GROUP * gsel
    le = jnp.where(lane >= base, jnp.where(lane < base + EXPERTS_PER_GROUP, logits, ninf), ninf)
    v1 = jnp.max(le, axis=-1, keepdims=True)
    i1 = jnp.min(jnp.where(le == v1, lane, float(LANES)), axis=-1, keepdims=True)
    le2 = jnp.where(lane == i1, ninf, le)
    v2 = jnp.max(le2, axis=-1, keepdims=True)
    i2 = jnp.min(jnp.where(le2 == v2, lane, float(LANES)), axis=-1, keepdims=True)
    e21 = jnp.exp(v2 - v1)
    w1 = wg / (1.0 + e21)
    w2 = w1 * e21
    eid1 = i1 - N_GROUPS
    eid2 = i2 - N_GROUPS
    oh1 = jnp.where(lane == eid1, 1.0, 0.0)
    oh2 = jnp.where(lane == eid2, 1.0, 0.0)
    ohs = oh1 + oh2
    row = lax.broadcasted_iota(jnp.int32, (ts, ts), 0)
    col = lax.broadcasted_iota(jnp.int32, (ts, ts), 1)
    before = jnp.where(col < row, 1.0, 0.0).astype(BF16)
    tot = cnt + _bdot(before, ohs.astype(BF16))
    rank1 = jnp.sum(oh1 * tot, axis=-1, keepdims=True)
    rank2 = jnp.sum(oh2 * tot, axis=-1, keepdims=True)
    packed = jnp.where(lane == 0.0, eid1,
             jnp.where(lane == 1.0, eid2,
             jnp.where(lane == 2.0, w1,
             jnp.where(lane == 3.0, w2,
             jnp.where(lane == 4.0, rank1,
             jnp.where(lane == 5.0, rank2, 0.0))))))
    return packed, cnt + jnp.sum(ohs, axis=0, keepdims=True)


def _merge_body(z_ref, og_ref, gt_ref, x_ref, mod_ref, wdw_ref, bdw_ref, lng_ref, lnb_ref,
                wpw_ref, bpw_ref, wgo_ref, wout_ref, g2_ref, wr_ref, br_ref,
                h_ref, u2_ref, route_ref, cnt_ref, zbuf, zsh, u2t, cnt_sc):
    ts = z_ref.shape[1]
    d = x_ref.shape[2]
    first_tile = pl.program_id(1) == 0

    @pl.when(first_tile)
    def _():
        zbuf[0:CONV_HALO, :] = jnp.zeros((CONV_HALO, CONV_DIM), F32)

    @pl.when(jnp.logical_and(first_tile, pl.program_id(0) == 0))
    def _():
        cnt_sc[...] = jnp.zeros_like(cnt_sc)

    zbuf[CONV_HALO:CONV_HALO + ts, :] = z_ref[0]
    span = ts + CONV_HALO - SUBLANES
    for r in range(1, SUBLANES):
        zsh[r - 1] = zbuf[r:r + span, :]
    off = CONV_HALO - (CONV_WIDTH - 1)
    spans = [slice(p * MERGE_ROWS, (p + 1) * MERGE_ROWS) for p in range(ts // MERGE_ROWS)]
    y_gla = [_bdot(og_ref[0, rows, :], wgo_ref[...]) for rows in spans]
    acts = []
    for rows in spans:
        blocks = []
        for blk in range(rows.start, rows.stop, CONV_ROWS):
            acc = None
            for j in range(CONV_WIDTH):
                a, r = divmod(off + j, SUBLANES)
                lo = a * SUBLANES + blk
                src = zbuf[lo:lo + CONV_ROWS, :] if r == 0 else zsh[r - 1, lo:lo + CONV_ROWS, :]
                term = src * jnp.concatenate([wdw_ref[j]] * (CONV_ROWS // SUBLANES), axis=0)
                acc = term if acc is None else acc + term
            conv = acc + bdw_ref[...]
            mu = jnp.mean(conv, axis=-1, keepdims=True)
            xc = conv - mu
            var = jnp.mean(xc * xc, axis=-1, keepdims=True)
            ln = xc * lax.rsqrt(var + EPS) * lng_ref[...] + lnb_ref[...]
            blocks.append((ln * _sigmoid(ln)).astype(BF16))
        acts.append(jnp.concatenate(blocks, axis=0))
    hs, u2s, logits = [], [], []
    for p, rows in enumerate(spans):
        y_conv = _bdot(acts[p], wpw_ref[...]) + bpw_ref[...]
        merged = gt_ref[0, rows, 0:d].astype(F32) * y_conv + gt_ref[0, rows, d:2 * d].astype(F32) * y_gla[p]
        y = _bdot(merged.astype(BF16), wout_ref[...])
        h = x_ref[0, rows, :] + mod_ref[0, 2:3, :] * y
        u2 = _rms(h, g2_ref[...]) * (1.0 + mod_ref[0, 4:5, :]) + mod_ref[0, 3:4, :]
        hs.append(h)
        u2s.append(u2)
        logits.append(_dot3(u2, wr_ref[...]) + br_ref[...])
    cnt = cnt_sc[...]
    routes = []
    for p in range(len(spans)):
        packed, cnt = _route(logits[p], cnt)
        routes.append(packed)
    zbuf[0:CONV_HALO, :] = zbuf[ts:ts + CONV_HALO, :]
    h_ref[0] = jnp.concatenate(hs, axis=0)
    _pack_rows(u2_ref, jnp.concatenate(u2s, axis=0), u2t)
    route_ref[0] = jnp.concatenate(routes, axis=0)
    cnt_sc[...] = cnt
    cnt_ref[...] = jnp.broadcast_to(cnt, cnt_ref.shape)


def _merge(z, og, gt, x, mod3, w_dw, b_dw, ln_g, ln_b, w_pw, b_pw, w_go, w_out, g2, w_rg, b_rg, w_re, b_re, ts):
    nb, s, d = x.shape
    npad = LANES - N_GROUPS - N_EXPERTS
    wr = jnp.pad(jnp.concatenate([w_rg, w_re], axis=1), ((0, 0), (0, npad)))
    br = jnp.pad(jnp.concatenate([b_rg, b_re]), (0, npad)).reshape(1, LANES)

    def tok(width):
        return pl.BlockSpec((1, ts, width), lambda b, i: (b, i, 0))

    def const(shape):
        return pl.BlockSpec(shape, lambda b, i: (0,) * len(shape))

    def row(v):
        return v.reshape(1, v.shape[-1])

    return pl.pallas_call(
        _merge_body,
        grid=(nb, s // ts),
        in_specs=[tok(CONV_DIM), tok(V_DIM), tok(2 * d), tok(d),
                  pl.BlockSpec((1, N_MOD, d), lambda b, i: (b, 0, 0)),
                  const((CONV_WIDTH, SUBLANES, CONV_DIM)), const((1, CONV_DIM)), const((1, CONV_DIM)), const((1, CONV_DIM)),
                  const((CONV_DIM, d)), const((1, d)), const((V_DIM, d)), const((d, d)), const((1, d)),
                  const((d, LANES)), const((1, LANES))],
        out_specs=[tok(d), pl.BlockSpec((ts * PACK_ROWS, LANES), lambda b, i: (b * (s // ts) + i, 0)), tok(LANES),
                   pl.BlockSpec((8, LANES), lambda b, i: (0, 0))],
        out_shape=[jax.ShapeDtypeStruct((nb, s, d), F32),
                   jax.ShapeDtypeStruct((nb * s * PACK_ROWS, LANES), jnp.uint32),
                   jax.ShapeDtypeStruct((nb, s, LANES), F32), jax.ShapeDtypeStruct((8, LANES), F32)],
        scratch_shapes=[pltpu.VMEM((CONV_HALO + ts, CONV_DIM), F32),
                        pltpu.VMEM((SUBLANES - 1, CONV_HALO + ts - SUBLANES, CONV_DIM), F32),
                        pltpu.VMEM((ts * SUBLANES, LANES), F32),
                        pltpu.VMEM((1, LANES), F32)],
        compiler_params=_params("arbitrary", "arbitrary"),
        name="merge",
    )(z, og, gt, x, mod3, jnp.broadcast_to(w_dw.reshape(CONV_WIDTH, 1, CONV_DIM), (CONV_WIDTH, SUBLANES, CONV_DIM)), row(b_dw), row(ln_g), row(ln_b),
      w_pw.astype(BF16), row(b_pw), w_go.astype(BF16), w_out.astype(BF16), row(g2), wr, br)


def _row_copy(src, i, dst, j, sem):
    return pltpu.make_async_copy(src.at[pl.ds(pl.multiple_of(i, SUBLANES), SUBLANES), :],
                                 dst.at[pl.ds(pl.multiple_of(j, SUBLANES), SUBLANES), :], sem)


def _gather_rows(idx_ref, lo, n, src, dst, sem, row0=0):
    for r in range(n):
        _row_copy(src, idx_ref[0, 0, lo + r], dst, (row0 + r) * SUBLANES, sem).start(priority=r % 2)


def _invert_body(lo_ref, hi_ref, slot_ref, src_ref):
    i = pl.program_id(0)
    ts = slot_ref.shape[-1] // TOP_K

    @pl.when(i == 0)
    def _():
        for e in range(lo_ref.shape[0]):
            lo = lo_ref[e]
            hi = hi_ref[e]

            def clear(p, carry, lo=lo, hi=hi):
                for j in range(CLEAR_UNROLL):
                    src_ref[jnp.minimum(lo + p * CLEAR_UNROLL + j, hi - 1)] = 0
                return carry

            trips = lax.shift_right_logical(hi - lo + (CLEAR_UNROLL - 1), CLEAR_UNROLL.bit_length() - 1)
            lax.fori_loop(0, trips, clear, 0)

    for r in range(ts):
        row = (i * ts + r) * PACK_ROWS
        for k in range(TOP_K):
            src_ref[slot_ref[0, 0, k * ts + r]] = row


def _invert(fill_lo, fill_hi, slots, n_slots):
    nt, _, width = slots.shape
    return pl.pallas_call(
        _invert_body,
        grid_spec=pltpu.PrefetchScalarGridSpec(
            num_scalar_prefetch=2,
            grid=(nt,),
            in_specs=[pl.BlockSpec((1, 1, width), lambda i, lo, hi: (i, 0, 0), memory_space=pltpu.SMEM)],
            out_specs=pl.BlockSpec(memory_space=pltpu.SMEM)),
        out_shape=jax.ShapeDtypeStruct((n_slots,), jnp.int32),
        compiler_params=_params("arbitrary"),
        name="invert",
    )(fill_lo, fill_hi, slots)


def _expert_body(te_ref, nu_ref, src_ref, u2_ref, w1_ref, w3_ref, w2_ref, ys_ref,
                 u2v, xg, xt, w1b, w3b, w2b, sem):
    i = pl.program_id(0)
    tmx = src_ref.shape[-1]

    @pl.when(i == 0)
    def _():
        load = pltpu.make_async_copy(u2_ref, u2v, sem)
        load.start()
        load.wait()

    @pl.when(jnp.logical_or(i == 0, te_ref[i] != te_ref[jnp.maximum(i - 1, 0)]))
    def _():
        w1b[...] = w1_ref[0].astype(BF16)
        w3b[...] = w3_ref[0].astype(BF16)
        w2b[...] = w2_ref[0].astype(BF16)

    @pl.when(i < nu_ref[0])
    def _():
        for r in range(tmx):
            row = pl.multiple_of(src_ref[0, 0, r], PACK_ROWS)
            xg[r * PACK_ROWS:(r + 1) * PACK_ROWS, :] = u2v[pl.ds(row, PACK_ROWS), :]
        x = _unpack_rows(xg, tmx, xt)
        h1 = _bdot(x, w1b[...])
        h3 = _bdot(x, w3b[...])
        hid = (h1 * _sigmoid(h1) * h3).astype(BF16)
        _rows_to_tiles(ys_ref, _bdot(hid, w2b[...]))

    @pl.when(i >= nu_ref[0])
    def _():
        ys_ref[...] = jnp.zeros_like(ys_ref)


def _experts(tile_expert, n_used, src_rows, u2p, w1, w3, w2):
    n_tiles, _, tmx = src_rows.shape
    ne, d, f = w1.shape

    def w_map(i, te, nu):
        return (te[i], 0, 0)

    return pl.pallas_call(
        _expert_body,
        grid_spec=pltpu.PrefetchScalarGridSpec(
            num_scalar_prefetch=2,
            grid=(n_tiles,),
            in_specs=[pl.BlockSpec((1, 1, tmx), lambda i, te, nu: (i, 0, 0), memory_space=pltpu.SMEM),
                      pl.BlockSpec(memory_space=pl.ANY),
                      pl.BlockSpec((1, d, f), w_map), pl.BlockSpec((1, d, f), w_map),
                      pl.BlockSpec((1, f, d), w_map)],
            out_specs=pl.BlockSpec((tmx * SUBLANES, LANES), lambda i, te, nu: (i, 0)),
            scratch_shapes=[pltpu.VMEM(u2p.shape, jnp.uint32),
                            pltpu.VMEM((tmx * PACK_ROWS, LANES), jnp.uint32),
                            pltpu.VMEM((tmx * SUBLANES, LANES), F32),
                            pltpu.VMEM((d, f), BF16), pltpu.VMEM((d, f), BF16), pltpu.VMEM((f, d), BF16),
                            pltpu.SemaphoreType.DMA(())]),
        out_shape=jax.ShapeDtypeStruct((n_tiles * tmx * SUBLANES, LANES), F32),
        compiler_params=_params("arbitrary"),
        name="experts",
    )(tile_expert, n_used, src_rows, u2p, w1, w3, w2)


def _final_body(p_ref, h_ref, route_ref, mod_ref, modf_ref, gf_ref, ys_ref, o_ref, y1_buf, y2_buf, sem):
    ts = h_ref.shape[1]

    _gather_rows(p_ref, 0, ts, ys_ref, y1_buf, sem)
    _gather_rows(p_ref, ts, ts, ys_ref, y2_buf, sem)
    pltpu.make_async_copy(ys_ref.at[pl.ds(0, ts * SUBLANES), :], y1_buf, sem).wait()
    pltpu.make_async_copy(ys_ref.at[pl.ds(0, ts * SUBLANES), :], y2_buf, sem).wait()

    route = route_ref[0]
    y2 = route[:, 2:3] * _tiles_to_rows(y1_buf, ts) + route[:, 3:4] * _tiles_to_rows(y2_buf, ts)
    h = h_ref[0] + mod_ref[0, 5:6, :] * y2
    o_ref[0] = _rms(h, gf_ref[...]) * (1.0 + modf_ref[0, 1:2, :]) + modf_ref[0, 0:1, :]


def _final(slots, h, route, mod3, modf3, gf, ys, ts):
    nb, s, d = h.shape
    nt = s // ts

    def tok(width):
        return pl.BlockSpec((1, ts, width), lambda b, i: (b, i, 0))

    return pl.pallas_call(
        _final_body,
        grid=(nb, nt),
        in_specs=[pl.BlockSpec((1, 1, 2 * ts), lambda b, i: (b * nt + i, 0, 0), memory_space=pltpu.SMEM),
                  tok(d), tok(LANES),
                  pl.BlockSpec((1, N_MOD, d), lambda b, i: (b, 0, 0)),
                  pl.BlockSpec((1, 2, d), lambda b, i: (b, 0, 0)),
                  pl.BlockSpec((1, d), lambda b, i: (0, 0)),
                  pl.BlockSpec(memory_space=pl.ANY)],
        out_specs=tok(d),
        out_shape=jax.ShapeDtypeStruct((nb, s, d), F32),
        scratch_shapes=[pltpu.VMEM((ts * SUBLANES, LANES), F32), pltpu.VMEM((ts * SUBLANES, LANES), F32),
                        pltpu.SemaphoreType.DMA(())],
        compiler_params=_params("arbitrary", "arbitrary"),
        name="final",
    )(slots, h, route, mod3, modf3, gf.reshape(1, d), ys)


def _plan(route, cnt, ts, tmx, n_tiles):
    t = route.shape[0]
    counts = cnt[0, :N_EXPERTS].astype(jnp.int32)
    tiles = (counts + (tmx - 1)) // tmx
    tile_end = jnp.cumsum(tiles)
    offs = (tile_end - tiles) * tmx
    n_used = tile_end[-1:]
    tile_ids = jnp.minimum(jnp.arange(n_tiles, dtype=jnp.int32), n_used[0] - 1)
    tile_expert = jnp.sum((tile_ids[:, None] >= tile_end[None, :]).astype(jnp.int32), axis=1)
    eid = route[:, 0:2].astype(jnp.int32)
    rank = route[:, 4:6].astype(jnp.int32)
    slot = jnp.take(offs, eid) + rank
    slots = slot.reshape(t // ts, ts, 2).transpose(0, 2, 1).reshape(t // ts, 1, 2 * ts)
    fill_lo = jnp.concatenate([offs + counts, tile_end[-1:] * tmx]).astype(jnp.int32)
    fill_hi = jnp.concatenate([tile_end * tmx, jnp.full((1,), n_tiles * tmx, jnp.int32)]).astype(jnp.int32)
    src = _invert(fill_lo, fill_hi, slots, n_tiles * tmx)
    return slots * SUBLANES, src.reshape(n_tiles, 1, tmx), tile_expert.astype(jnp.int32), n_used.astype(jnp.int32)


def kernel(x, c, w_ada, b_ada, g_norm1, w_in, w_dw, b_dw, g_conv_ln, b_conv_ln, w_conv_pw, b_conv_pw,
           w_a2, b_a2, g_gla_norm, w_gla_o, w_out, g_norm2, w_router_g, b_router_g, w_router_e,
           b_router_e, w1, w3, w2, w_ada_f, b_ada_f, g_final):
    nb, s, d = x.shape
    assert w_ada.shape[0] == 1, "single-layer block"
    assert d == 2 * PACK_ROWS * LANES, "packed token rows assume D_MODEL = 1024"
    tm = min(512, s)
    tc = min(256, s)
    ts = min(256, s)
    tmx = 256
    t = nb * s
    n_tiles = (t * TOP_K) // tmx + N_EXPERTS
    mod3 = _ada(c, w_ada[0], b_ada[0]).reshape(nb, N_MOD, d)
    modf3 = _ada(c, w_ada_f, b_ada_f).reshape(nb, 2, d)
    z, q, k, v, rs, lg, gt = _proj(x, mod3, g_norm1[0], w_in[0], w_a2[0], b_a2[0], tm)
    og = _gla(q, k, lg, v, rs, g_gla_norm[0], tc)
    h, u2, route, cnt = _merge(z, og, gt, x, mod3, w_dw[0], b_dw[0], g_conv_ln[0], b_conv_ln[0],
                               w_conv_pw[0], b_conv_pw[0], w_gla_o[0], w_out[0], g_norm2[0],
                               w_router_g[0], b_router_g[0], w_router_e[0], b_router_e[0], ts)
    route2 = route.reshape(t, LANES)
    slots, src_rows, tile_expert, n_used = _plan(route2, cnt, ts, tmx, n_tiles)
    ys = _experts(tile_expert, n_used, src_rows, u2, w1[0], w3[0], w2[0])
    return _final(slots, h, route, mod3, modf3, g_final, ys, ts)
```

```python
import functools

import jax
import jax.numpy as jnp
from jax import lax
from jax.experimental import pallas as pl
from jax.experimental.pallas import tpu as pltpu

F32 = jnp.float32
BF16 = jnp.bfloat16

EPS = 1e-6
CONV_DIM = 512
CONV_WIDTH = 31
GLA_HEADS = 4
GLA_DK = 128
GLA_DV = 256
GLA_LOWRANK = 16
GLA_TAU = 16.0
QK_DIM = GLA_HEADS * GLA_DK
V_DIM = GLA_HEADS * GLA_DV
N_GROUPS = 4
EXPERTS_PER_GROUP = 8
N_EXPERTS = N_GROUPS * EXPERTS_PER_GROUP
TOP_K = 2
N_MOD = 6

LANES = 128
SUBLANES = 8
PACK_ROWS = 4
CONV_ROWS = 32
CLEAR_UNROLL = 16
CONV_HALO = 32
GLA_CHUNK = 128
GLA_FACTOR_RANGE = 60.0
VMEM_LIMIT = 56 * 1024 * 1024


def _bdot(a, b):
    return jnp.dot(a, b, preferred_element_type=F32)


def _split(a):
    hi = a.astype(BF16)
    lo = (a - hi.astype(F32)).astype(BF16)
    return hi, lo


def _dot3(a, b):
    ah, al = _split(a)
    bh, bl = _split(b)
    return _bdot(ah, bh) + (_bdot(ah, bl) + _bdot(al, bh))


def _sigmoid(x):
    return 1.0 / (1.0 + jnp.exp(-x))


def _rms(x, g):
    ms = jnp.mean(x * x, axis=-1, keepdims=True)
    return x * lax.rsqrt(ms + EPS) * g


def _rows_to_tiles(ref, val):
    n = val.shape[0]
    for j in range(val.shape[1] // LANES):
        ref[pl.ds(j, n, stride=SUBLANES), :] = val[:, j * LANES:(j + 1) * LANES]


def _tiles_to_rows(ref, n):
    return jnp.concatenate([ref[pl.ds(j, n, stride=SUBLANES), :] for j in range(SUBLANES)], axis=-1)


def _pack_rows(ref, val, tiles):
    _rows_to_tiles(tiles, val)
    ref[...] = pltpu.bitcast(tiles[...].astype(BF16), jnp.uint32)


def _unpack_rows(ref, n, tiles):
    tiles[...] = pltpu.bitcast(ref[...], BF16).astype(F32)
    return _tiles_to_rows(tiles, n).astype(BF16)


def _params(*sem):
    return pltpu.CompilerParams(dimension_semantics=sem, vmem_limit_bytes=VMEM_LIMIT)


def _ada_body(c_ref, w_ref, b_ref, o_ref):
    c = c_ref[...]
    o_ref[...] = _dot3(c * _sigmoid(c), w_ref[...]) + b_ref[...]


def _ada(c, w, b, tn=1024):
    nb, d = c.shape
    n = w.shape[1]
    return pl.pallas_call(
        _ada_body,
        grid=(n // tn,),
        in_specs=[pl.BlockSpec((nb, d), lambda j: (0, 0)),
                  pl.BlockSpec((d, tn), lambda j: (0, j)),
                  pl.BlockSpec((1, tn), lambda j: (0, j))],
        out_specs=pl.BlockSpec((nb, tn), lambda j: (0, j)),
        out_shape=jax.ShapeDtypeStruct((nb, n), F32),
        compiler_params=_params("arbitrary"),
        name="ada",
    )(c, w, b.reshape(1, n))


def _proj_body(x_ref, mod_ref, g1_ref, wm_ref, wa1_ref, wg_ref, wa2_ref, ba2_ref,
               z_ref, q_ref, k_ref, v_ref, rs_ref, lg_ref, gt_ref):
    x = x_ref[0]
    u = (_rms(x, g1_ref[...]) * (1.0 + mod_ref[0, 1:2, :]) + mod_ref[0, 0:1, :]).astype(BF16)
    c0 = 2 * CONV_DIM
    c1 = c0 + 2 * QK_DIM
    c2 = c1 + V_DIM
    c3 = c2 + V_DIM
    pc = _bdot(u, wm_ref[:, 0:c0])
    z_ref[0] = pc[:, :CONV_DIM] * _sigmoid(pc[:, CONV_DIM:])
    qk = _bdot(u, wm_ref[:, c0:c1])
    q_ref[0] = qk[:, :QK_DIM] * (GLA_DK ** -0.5)
    k_ref[0] = qk[:, QK_DIM:]
    v_ref[0] = _bdot(u, wm_ref[:, c1:c2]).astype(BF16)
    r = _bdot(u, wm_ref[:, c2:c3])
    rs_ref[0] = (r * _sigmoid(r)).astype(BF16)
    a1 = _bdot(u, wa1_ref[...])
    xg = _dot3(a1, wa2_ref[...]) + ba2_ref[...]
    lg_ref[0] = (jnp.minimum(xg, 0.0) - jnp.log1p(jnp.exp(-jnp.abs(xg)))) * (1.0 / GLA_TAU)
    gt_ref[0] = _sigmoid(_bdot(u, wg_ref[...])).astype(BF16)


def _proj(x, mod3, g1, w_in, w_a2, b_a2, tm):
    nb, s, d = x.shape
    c3 = 2 * CONV_DIM + 2 * QK_DIM + 2 * V_DIM
    wm = w_in[:, :c3].astype(BF16)
    wa1 = jnp.pad(w_in[:, c3:c3 + GLA_LOWRANK], ((0, 0), (0, LANES - GLA_LOWRANK))).astype(BF16)
    wg = w_in[:, c3 + GLA_LOWRANK:].astype(BF16)
    wa2 = jnp.pad(w_a2, ((0, LANES - GLA_LOWRANK), (0, 0)))
    ng = wg.shape[1]

    def tok(width):
        return pl.BlockSpec((1, tm, width), lambda b, i: (b, i, 0))

    def const(shape):
        return pl.BlockSpec(shape, lambda b, i: (0,) * len(shape))

    def out(width, dt):
        return jax.ShapeDtypeStruct((nb, s, width), dt)

    return pl.pallas_call(
        _proj_body,
        grid=(nb, s // tm),
        in_specs=[tok(d),
                  pl.BlockSpec((1, N_MOD, d), lambda b, i: (b, 0, 0)),
                  const((1, d)), const(wm.shape), const(wa1.shape), const(wg.shape),
                  const(wa2.shape), const((1, QK_DIM))],
        out_specs=[tok(CONV_DIM), tok(QK_DIM), tok(QK_DIM), tok(V_DIM), tok(V_DIM), tok(QK_DIM), tok(ng)],
        out_shape=[out(CONV_DIM, F32), out(QK_DIM, F32), out(QK_DIM, F32), out(V_DIM, BF16),
                   out(V_DIM, BF16), out(QK_DIM, F32), out(ng, BF16)],
        compiler_params=_params("arbitrary", "arbitrary"),
        name="proj",
    )(x, mod3, g1.reshape(1, d), wm, wa1, wg, wa2, b_a2.reshape(1, QK_DIM))


def _gla_body(q_ref, k_ref, lg_ref, v_ref, rs_ref, gn_ref, o_ref, st_ref, b_s, oi_s, *, n_chunks):
    cl = GLA_CHUNK

    @pl.when(pl.program_id(1) == 0)
    def _():
        st_ref[...] = jnp.zeros_like(st_ref)

    row = lax.broadcasted_iota(jnp.int32, (cl, cl), 0)
    col = lax.broadcasted_iota(jnp.int32, (cl, cl), 1)
    causal = col <= row
    tri = jnp.where(causal, 1.0, 0.0).astype(BF16)
    chunks = [slice(c * cl, (c + 1) * cl) for c in range(n_chunks)]
    heads = range(GLA_HEADS)
    ksl = [slice(h * GLA_DK, (h + 1) * GLA_DK) for h in heads]
    vsl = [slice(h * GLA_DV, (h + 1) * GLA_DV) for h in heads]

    lowest = None
    for rows in chunks:
        gh, gl = _split(lg_ref[0, rows, :])
        b = _bdot(tri, gh) + _bdot(tri, gl)
        b_s[rows, :] = b
        low = jnp.min(b[cl - 1:cl, :])
        lowest = low if lowest is None else jnp.minimum(lowest, low)

    def intra_pairwise(h, c):
        kf = k_ref[0, chunks[c], ksl[h]]
        vf = v_ref[0, chunks[c], vsl[h]].astype(F32)
        bh = b_s[chunks[c], ksl[h]]
        key = lax.broadcasted_iota(jnp.int32, (cl, 1), 0)

        def group(g, carry):
            r0 = pl.multiple_of(g * SUBLANES, SUBLANES)
            q8 = q_ref[0, pl.ds(c * cl + r0, SUBLANES), ksl[h]]
            b8 = b_s[pl.ds(c * cl + r0, SUBLANES), ksl[h]]
            out_rows = []
            for r in range(SUBLANES):
                diff = jnp.where(key <= r0 + r, b8[r:r + 1, :] - bh, -jnp.inf)
                att = jnp.sum(jnp.exp(diff) * kf * q8[r:r + 1, :], axis=-1, keepdims=True)
                out_rows.append(jnp.sum(att * vf, axis=0, keepdims=True))
            oi_s[h, pl.ds(c * cl + r0, SUBLANES), :] = jnp.concatenate(out_rows, axis=0)
            return carry

        lax.fori_loop(0, cl // SUBLANES, group, 0)
        return oi_s[h, chunks[c], :]

    def run(factored):
        cums = [b_s[rows, :] for rows in chunks]
        qes, kts, bts, klts, decays = [], [], [], [], []
        for rows, b in zip(chunks, cums):
            qes.append((q_ref[0, rows, :] * jnp.exp(b)).astype(BF16))
            kt = k_ref[0, rows, :].T
            bt = b.T
            bl = bt[:, cl - 1:cl]
            kts.append(kt)
            bts.append(bt)
            klts.append((kt * jnp.exp(bl - bt)).astype(BF16))
            decays.append(jnp.exp(bl))
        intra = {}
        if factored:
            kets = [(kt * jnp.exp(-bt)).astype(BF16) for kt, bt in zip(kts, bts)]
            atts = {}
            for c in range(n_chunks):
                for h in heads:
                    att = _bdot(qes[c][:, ksl[h]], kets[c][ksl[h], :])
                    atts[h, c] = jnp.where(causal, att, 0.0).astype(BF16)
            for c, rows in enumerate(chunks):
                for h in heads:
                    intra[h, c] = _bdot(atts[h, c], v_ref[0, rows, vsl[h]])
        else:
            for c in range(n_chunks):
                for h in heads:
                    intra[h, c] = intra_pairwise(h, c)
        updates = {}
        for c, rows in enumerate(chunks):
            for h in heads:
                updates[h, c] = _bdot(klts[c][ksl[h], :], v_ref[0, rows, vsl[h]])

        outs = {}
        states = []
        for h in heads:
            state = st_ref[h]
            for c, rows in enumerate(chunks):
                o = intra[h, c] + _bdot(qes[c][:, ksl[h]], state.astype(BF16))
                state = decays[c][ksl[h], :] * state + updates[h, c]
                outs[h, c] = (_rms(o, gn_ref[:, vsl[h]]) * rs_ref[0, rows, vsl[h]].astype(F32)).astype(BF16)
            states.append(state)
        for c, rows in enumerate(chunks):
            o_ref[0, rows, :] = jnp.concatenate([outs[h, c] for h in heads], axis=-1)
        for h in heads:
            st_ref[h] = states[h]

    in_range = lowest > -GLA_FACTOR_RANGE
    pl.when(in_range)(functools.partial(run, True))
    pl.when(jnp.logical_not(in_range))(functools.partial(run, False))


def _gla(q, k, lg, v, rs, gn, tc):
    nb, s, _ = q.shape

    def tok(width):
        return pl.BlockSpec((1, tc, width), lambda b, i: (b, i, 0))

    return pl.pallas_call(
        functools.partial(_gla_body, n_chunks=tc // GLA_CHUNK),
        grid=(nb, s // tc),
        in_specs=[tok(QK_DIM), tok(QK_DIM), tok(QK_DIM), tok(V_DIM), tok(V_DIM),
                  pl.BlockSpec((1, V_DIM), lambda b, i: (0, 0))],
        out_specs=tok(V_DIM),
        out_shape=jax.ShapeDtypeStruct((nb, s, V_DIM), BF16),
        scratch_shapes=[pltpu.VMEM((GLA_HEADS, GLA_DK, GLA_DV), F32),
                        pltpu.VMEM((tc, QK_DIM), F32),
                        pltpu.VMEM((GLA_HEADS, tc, GLA_DV), F32)],
        compiler_params=_params("arbitrary", "arbitrary"),
        name="gla",
    )(q, k, lg, v, rs, gn.reshape(1, V_DIM))


def _route(logits, cnt):
    ts = logits.shape[0]
    lane = lax.broadcasted_iota(jnp.int32, (ts, LANES), 1).astype(F32)
    ninf = -jnp.inf
    lgm = jnp.where(lane < N_GROUPS, logits, ninf)
    gmax = jnp.max(lgm, axis=-1, keepdims=True)
    gsel = jnp.min(jnp.where(lgm == gmax, lane, float(LANES)), axis=-1, keepdims=True)
    wg = 1.0 / jnp.sum(jnp.exp(lgm - gmax), axis=-1, keepdims=True)
    base = N_GROUPS + EXPERTS_PER_GROUP * gsel
    le = jnp.where(lane >= base, jnp.where(lane < base + EXPERTS_PER_GROUP, logits, ninf), ninf)
    v1 = jnp.max(le, axis=-1, keepdims=True)
    i1 = jnp.min(jnp.where(le == v1, lane, float(LANES)), axis=-1, keepdims=True)
    le2 = jnp.where(lane == i1, ninf, le)
    v2 = jnp.max(le2, axis=-1, keepdims=True)
    i2 = jnp.min(jnp.where(le2 == v2, lane, float(LANES)), axis=-1, keepdims=True)
    e21 = jnp.exp(v2 - v1)
    w1 = wg / (1.0 + e21)
    w2 = w1 * e21
    eid1 = i1 - N_GROUPS
    eid2 = i2 - N_GROUPS
    oh1 = jnp.where(lane == eid1, 1.0, 0.0)
    oh2 = jnp.where(lane == eid2, 1.0, 0.0)
    ohs = oh1 + oh2
    row = lax.broadcasted_iota(jnp.int32, (ts, ts), 0)
    col = lax.broadcasted_iota(jnp.int32, (ts, ts), 1)
    before = jnp.where(col < row, 1.0, 0.0).astype(BF16)
    tot = cnt + _bdot(before, ohs.astype(BF16))
    rank1 = jnp.sum(oh1 * tot, axis=-1, keepdims=True)
    rank2 = jnp.sum(oh2 * tot, axis=-1, keepdims=True)
    packed = jnp.where(lane == 0.0, eid1,
             jnp.where(lane == 1.0, eid2,
             jnp.where(lane == 2.0, w1,
             jnp.where(lane == 3.0, w2,
             jnp.where(lane == 4.0, rank1,
             jnp.where(lane == 5.0, rank2, 0.0))))))
    return packed, cnt + jnp.sum(ohs, axis=0, keepdims=True)


def _merge_body(z_ref, og_ref, gt_ref, x_ref, mod_ref, wdw_ref, bdw_ref, lng_ref, lnb_ref,
                wpw_ref, bpw_ref, wgo_ref, wout_ref, g2_ref, wr_ref, br_ref,
                h_ref, u2_ref, route_ref, cnt_ref, zbuf, zsh, u2t, cnt_sc):
    ts = z_ref.shape[1]
    d = x_ref.shape[2]
    first_tile = pl.program_id(1) == 0

    @pl.when(first_tile)
    def _():
        zbuf[0:CONV_HALO, :] = jnp.zeros((CONV_HALO, CONV_DIM), F32)

    @pl.when(jnp.logical_and(first_tile, pl.program_id(0) == 0))
    def _():
        cnt_sc[...] = jnp.zeros_like(cnt_sc)

    zbuf[CONV_HALO:CONV_HALO + ts, :] = z_ref[0]
    span = ts + CONV_HALO - SUBLANES
    for r in range(1, SUBLANES):
        zsh[r - 1] = zbuf[r:r + span, :]
    off = CONV_HALO - (CONV_WIDTH - 1)
    pieces = []
    for blk in range(ts // CONV_ROWS):
        acc = None
        for j in range(CONV_WIDTH):
            a, r = divmod(off + j, SUBLANES)
            lo = a * SUBLANES + blk * CONV_ROWS
            src = zbuf[lo:lo + CONV_ROWS, :] if r == 0 else zsh[r - 1, lo:lo + CONV_ROWS, :]
            term = src * jnp.concatenate([wdw_ref[j]] * (CONV_ROWS // SUBLANES), axis=0)
            acc = term if acc is None else acc + term
        conv = acc + bdw_ref[...]
        mu = jnp.mean(conv, axis=-1, keepdims=True)
        xc = conv - mu
        var = jnp.mean(xc * xc, axis=-1, keepdims=True)
        ln = xc * lax.rsqrt(var + EPS) * lng_ref[...] + lnb_ref[...]
        pieces.append((ln * _sigmoid(ln)).astype(BF16))
    zbuf[0:CONV_HALO, :] = zbuf[ts:ts + CONV_HALO, :]
    y_conv = _bdot(jnp.concatenate(pieces, axis=0), wpw_ref[...]) + bpw_ref[...]
    y_gla = _bdot(og_ref[0], wgo_ref[...])
    merged = gt_ref[0, :, 0:d].astype(F32) * y_conv + gt_ref[0, :, d:2 * d].astype(F32) * y_gla
    y = _bdot(merged.astype(BF16), wout_ref[...])
    h = x_ref[0] + mod_ref[0, 2:3, :] * y
    h_ref[0] = h
    u2 = _rms(h, g2_ref[...]) * (1.0 + mod_ref[0, 4:5, :]) + mod_ref[0, 3:4, :]
    _pack_rows(u2_ref, u2, u2t)
    logits = _dot3(u2, wr_ref[...]) + br_ref[...]
    packed, cnt = _route(logits, cnt_sc[...])
    route_ref[0] = packed
    cnt_sc[...] = cnt
    cnt_ref[...] = jnp.broadcast_to(cnt, cnt_ref.shape)


def _merge(z, og, gt, x, mod3, w_dw, b_dw, ln_g, ln_b, w_pw, b_pw, w_go, w_out, g2, w_rg, b_rg, w_re, b_re, ts):
    nb, s, d = x.shape
    npad = LANES - N_GROUPS - N_EXPERTS
    wr = jnp.pad(jnp.concatenate([w_rg, w_re], axis=1), ((0, 0), (0, npad)))
    br = jnp.pad(jnp.concatenate([b_rg, b_re]), (0, npad)).reshape(1, LANES)

    def tok(width):
        return pl.BlockSpec((1, ts, width), lambda b, i: (b, i, 0))

    def const(shape):
        return pl.BlockSpec(shape, lambda b, i: (0,) * len(shape))

    def row(v):
        return v.reshape(1, v.shape[-1])

    return pl.pallas_call(
        _merge_body,
        grid=(nb, s // ts),
        in_specs=[tok(CONV_DIM), tok(V_DIM), tok(2 * d), tok(d),
                  pl.BlockSpec((1, N_MOD, d), lambda b, i: (b, 0, 0)),
                  const((CONV_WIDTH, SUBLANES, CONV_DIM)), const((1, CONV_DIM)), const((1, CONV_DIM)), const((1, CONV_DIM)),
                  const((CONV_DIM, d)), const((1, d)), const((V_DIM, d)), const((d, d)), const((1, d)),
                  const((d, LANES)), const((1, LANES))],
        out_specs=[tok(d), pl.BlockSpec((ts * PACK_ROWS, LANES), lambda b, i: (b * (s // ts) + i, 0)), tok(LANES),
                   pl.BlockSpec((8, LANES), lambda b, i: (0, 0))],
        out_shape=[jax.ShapeDtypeStruct((nb, s, d), F32),
                   jax.ShapeDtypeStruct((nb * s * PACK_ROWS, LANES), jnp.uint32),
                   jax.ShapeDtypeStruct((nb, s, LANES), F32), jax.ShapeDtypeStruct((8, LANES), F32)],
        scratch_shapes=[pltpu.VMEM((CONV_HALO + ts, CONV_DIM), F32),
                        pltpu.VMEM((SUBLANES - 1, CONV_HALO + ts - SUBLANES, CONV_DIM), F32),
                        pltpu.VMEM((ts * SUBLANES, LANES), F32),
                        pltpu.VMEM((1, LANES), F32)],
        compiler_params=_params("arbitrary", "arbitrary"),
        name="merge",
    )(z, og, gt, x, mod3, jnp.broadcast_to(w_dw.reshape(CONV_WIDTH, 1, CONV_DIM), (CONV_WIDTH, SUBLANES, CONV_DIM)), row(b_dw), row(ln_g), row(ln_b),
      w_pw.astype(BF16), row(b_pw), w_go.astype(BF16), w_out.astype(BF16), row(g2), wr, br)


def _row_copy(src, i, dst, j, sem):
    return pltpu.make_async_copy(src.at[pl.ds(pl.multiple_of(i, SUBLANES), SUBLANES), :],
                                 dst.at[pl.ds(pl.multiple_of(j, SUBLANES), SUBLANES), :], sem)


def _gather_rows(idx_ref, lo, n, src, dst, sem, row0=0):
    for r in range(n):
        _row_copy(src, idx_ref[0, 0, lo + r], dst, (row0 + r) * SUBLANES, sem).start(priority=r % 2)


def _invert_body(lo_ref, hi_ref, slot_ref, src_ref):
    i = pl.program_id(0)
    ts = slot_ref.shape[-1] // TOP_K

    @pl.when(i == 0)
    def _():
        for e in range(lo_ref.shape[0]):
            lo = lo_ref[e]
            hi = hi_ref[e]

            def clear(p, carry, lo=lo, hi=hi):
                for j in range(CLEAR_UNROLL):
                    src_ref[jnp.minimum(lo + p * CLEAR_UNROLL + j, hi - 1)] = 0
                return carry

            trips = lax.shift_right_logical(hi - lo + (CLEAR_UNROLL - 1), CLEAR_UNROLL.bit_length() - 1)
            lax.fori_loop(0, trips, clear, 0)

    for r in range(ts):
        row = (i * ts + r) * PACK_ROWS
        for k in range(TOP_K):
            src_ref[slot_ref[0, 0, k * ts + r]] = row


def _invert(fill_lo, fill_hi, slots, n_slots):
    nt, _, width = slots.shape
    return pl.pallas_call(
        _invert_body,
        grid_spec=pltpu.PrefetchScalarGridSpec(
            num_scalar_prefetch=2,
            grid=(nt,),
            in_specs=[pl.BlockSpec((1, 1, width), lambda i, lo, hi: (i, 0, 0), memory_space=pltpu.SMEM)],
            out_specs=pl.BlockSpec(memory_space=pltpu.SMEM)),
        out_shape=jax.ShapeDtypeStruct((n_slots,), jnp.int32),
        compiler_params=_params("arbitrary"),
        name="invert",
    )(fill_lo, fill_hi, slots)


def _expert_body(te_ref, nu_ref, src_ref, u2_ref, w1_ref, w3_ref, w2_ref, ys_ref,
                 u2v, xg, xt, w1b, w3b, w2b, sem):
    i = pl.program_id(0)
    tmx = src_ref.shape[-1]

    @pl.when(i == 0)
    def _():
        load = pltpu.make_async_copy(u2_ref, u2v, sem)
        load.start()
        load.wait()

    @pl.when(jnp.logical_or(i == 0, te_ref[i] != te_ref[jnp.maximum(i - 1, 0)]))
    def _():
        w1b[...] = w1_ref[0].astype(BF16)
        w3b[...] = w3_ref[0].astype(BF16)
        w2b[...] = w2_ref[0].astype(BF16)

    @pl.when(i < nu_ref[0])
    def _():
        for r in range(tmx):
            row = pl.multiple_of(src_ref[0, 0, r], PACK_ROWS)
            xg[r * PACK_ROWS:(r + 1) * PACK_ROWS, :] = u2v[pl.ds(row, PACK_ROWS), :]
        x = _unpack_rows(xg, tmx, xt)
        h1 = _bdot(x, w1b[...])
        h3 = _bdot(x, w3b[...])
        hid = (h1 * _sigmoid(h1) * h3).astype(BF16)
        _rows_to_tiles(ys_ref, _bdot(hid, w2b[...]))

    @pl.when(i >= nu_ref[0])
    def _():
        ys_ref[...] = jnp.zeros_like(ys_ref)


def _experts(tile_expert, n_used, src_rows, u2p, w1, w3, w2):
    n_tiles, _, tmx = src_rows.shape
    ne, d, f = w1.shape

    def w_map(i, te, nu):
        return (te[i], 0, 0)

    return pl.pallas_call(
        _expert_body,
        grid_spec=pltpu.PrefetchScalarGridSpec(
            num_scalar_prefetch=2,
            grid=(n_tiles,),
            in_specs=[pl.BlockSpec((1, 1, tmx), lambda i, te, nu: (i, 0, 0), memory_space=pltpu.SMEM),
                      pl.BlockSpec(memory_space=pl.ANY),
                      pl.BlockSpec((1, d, f), w_map), pl.BlockSpec((1, d, f), w_map),
                      pl.BlockSpec((1, f, d), w_map)],
            out_specs=pl.BlockSpec((tmx * SUBLANES, LANES), lambda i, te, nu: (i, 0)),
            scratch_shapes=[pltpu.VMEM(u2p.shape, jnp.uint32),
                            pltpu.VMEM((tmx * PACK_ROWS, LANES), jnp.uint32),
                            pltpu.VMEM((tmx * SUBLANES, LANES), F32),
                            pltpu.VMEM((d, f), BF16), pltpu.VMEM((d, f), BF16), pltpu.VMEM((f, d), BF16),
                            pltpu.SemaphoreType.DMA(())]),
        out_shape=jax.ShapeDtypeStruct((n_tiles * tmx * SUBLANES, LANES), F32),
        compiler_params=_params("arbitrary"),
        name="experts",
    )(tile_expert, n_used, src_rows, u2p, w1, w3, w2)


def _final_body(p_ref, h_ref, route_ref, mod_ref, modf_ref, gf_ref, ys_ref, o_ref, y1_buf, y2_buf, sem):
    ts = h_ref.shape[1]

    _gather_rows(p_ref, 0, ts, ys_ref, y1_buf, sem)
    _gather_rows(p_ref, ts, ts, ys_ref, y2_buf, sem)
    pltpu.make_async_copy(ys_ref.at[pl.ds(0, ts * SUBLANES), :], y1_buf, sem).wait()
    pltpu.make_async_copy(ys_ref.at[pl.ds(0, ts * SUBLANES), :], y2_buf, sem).wait()

    route = route_ref[0]
    y2 = route[:, 2:3] * _tiles_to_rows(y1_buf, ts) + route[:, 3:4] * _tiles_to_rows(y2_buf, ts)
    h = h_ref[0] + mod_ref[0, 5:6, :] * y2
    o_ref[0] = _rms(h, gf_ref[...]) * (1.0 + modf_ref[0, 1:2, :]) + modf_ref[0, 0:1, :]


def _final(slots, h, route, mod3, modf3, gf, ys, ts):
    nb, s, d = h.shape
    nt = s // ts

    def tok(width):
        return pl.BlockSpec((1, ts, width), lambda b, i: (b, i, 0))

    return pl.pallas_call(
        _final_body,
        grid=(nb, nt),
        in_specs=[pl.BlockSpec((1, 1, 2 * ts), lambda b, i: (b * nt + i, 0, 0), memory_space=pltpu.SMEM),
                  tok(d), tok(LANES),
                  pl.BlockSpec((1, N_MOD, d), lambda b, i: (b, 0, 0)),
                  pl.BlockSpec((1, 2, d), lambda b, i: (b, 0, 0)),
                  pl.BlockSpec((1, d), lambda b, i: (0, 0)),
                  pl.BlockSpec(memory_space=pl.ANY)],
        out_specs=tok(d),
        out_shape=jax.ShapeDtypeStruct((nb, s, d), F32),
        scratch_shapes=[pltpu.VMEM((ts * SUBLANES, LANES), F32), pltpu.VMEM((ts * SUBLANES, LANES), F32),
                        pltpu.SemaphoreType.DMA(())],
        compiler_params=_params("arbitrary", "arbitrary"),
        name="final",
    )(slots, h, route, mod3, modf3, gf.reshape(1, d), ys)


def _plan(route, cnt, ts, tmx, n_tiles):
    t = route.shape[0]
    counts = cnt[0, :N_EXPERTS].astype(jnp.int32)
    tiles = (counts + (tmx - 1)) // tmx
    tile_end = jnp.cumsum(tiles)
    offs = (tile_end - tiles) * tmx
    n_used = tile_end[-1:]
    tile_ids = jnp.minimum(jnp.arange(n_tiles, dtype=jnp.int32), n_used[0] - 1)
    tile_expert = jnp.sum((tile_ids[:, None] >= tile_end[None, :]).astype(jnp.int32), axis=1)
    eid = route[:, 0:2].astype(jnp.int32)
    rank = route[:, 4:6].astype(jnp.int32)
    slot = jnp.take(offs, eid) + rank
    slots = slot.reshape(t // ts, ts, 2).transpose(0, 2, 1).reshape(t // ts, 1, 2 * ts)
    fill_lo = jnp.concatenate([offs + counts, tile_end[-1:] * tmx]).astype(jnp.int32)
    fill_hi = jnp.concatenate([tile_end * tmx, jnp.full((1,), n_tiles * tmx, jnp.int32)]).astype(jnp.int32)
    src = _invert(fill_lo, fill_hi, slots, n_tiles * tmx)
    return slots * SUBLANES, src.reshape(n_tiles, 1, tmx), tile_expert.astype(jnp.int32), n_used.astype(jnp.int32)


def kernel(x, c, w_ada, b_ada, g_norm1, w_in, w_dw, b_dw, g_conv_ln, b_conv_ln, w_conv_pw, b_conv_pw,
           w_a2, b_a2, g_gla_norm, w_gla_o, w_out, g_norm2, w_router_g, b_router_g, w_router_e,
           b_router_e, w1, w3, w2, w_ada_f, b_ada_f, g_final):
    nb, s, d = x.shape
    assert w_ada.shape[0] == 1, "single-layer block"
    assert d == 2 * PACK_ROWS * LANES, "packed token rows assume D_MODEL = 1024"
    tm = min(512, s)
    tc = min(256, s)
    ts = min(256, s)
    tmx = 256
    t = nb * s
    n_tiles = (t * TOP_K) // tmx + N_EXPERTS
    mod3 = _ada(c, w_ada[0], b_ada[0]).reshape(nb, N_MOD, d)
    modf3 = _ada(c, w_ada_f, b_ada_f).reshape(nb, 2, d)
    z, q, k, v, rs, lg, gt = _proj(x, mod3, g_norm1[0], w_in[0], w_a2[0], b_a2[0], tm)
    og = _gla(q, k, lg, v, rs, g_gla_norm[0], tc)
    h, u2, route, cnt = _merge(z, og, gt, x, mod3, w_dw[0], b_dw[0], g_conv_ln[0], b_conv_ln[0],
                               w_conv_pw[0], b_conv_pw[0], w_gla_o[0], w_out[0], g_norm2[0],
                               w_router_g[0], b_router_g[0], w_router_e[0], b_router_e[0], ts)
    route2 = route.reshape(t, LANES)
    slots, src_rows, tile_expert, n_used = _plan(route2, cnt, ts, tmx, n_tiles)
    ys = _experts(tile_expert, n_used, src_rows, u2, w1[0], w3[0], w2[0])
    return _final(slots, h, route, mod3, modf3, g_final, ys, ts)
```

```python
import functools

import jax
import jax.numpy as jnp
from jax import lax
from jax.experimental import pallas as pl
from jax.experimental.pallas import tpu as pltpu

F32 = jnp.float32
BF16 = jnp.bfloat16

EPS = 1e-6
CONV_DIM = 512
CONV_WIDTH = 31
GLA_HEADS = 4
GLA_DK = 128
GLA_DV = 256
GLA_LOWRANK = 16
GLA_TAU = 16.0
QK_DIM = GLA_HEADS * GLA_DK
V_DIM = GLA_HEADS * GLA_DV
N_GROUPS = 4
EXPERTS_PER_GROUP = 8
N_EXPERTS = N_GROUPS * EXPERTS_PER_GROUP
TOP_K = 2
N_MOD = 6

LANES = 128
SUBLANES = 8
PACK_ROWS = 4
CONV_ROWS = 32
ROUTE_ROWS = 2048
CLEAR_UNROLL = 16
CONV_HALO = 32
GLA_CHUNK = 128
GLA_FACTOR_RANGE = 60.0
VMEM_LIMIT = 56 * 1024 * 1024


def _bdot(a, b):
    return jnp.dot(a, b, preferred_element_type=F32)


def _split(a):
    hi = a.astype(BF16)
    lo = (a - hi.astype(F32)).astype(BF16)
    return hi, lo


def _dot3(a, b):
    ah, al = _split(a)
    bh, bl = _split(b)
    return _bdot(ah, bh) + (_bdot(ah, bl) + _bdot(al, bh))


def _sigmoid(x):
    return 1.0 / (1.0 + jnp.exp(-x))


def _rms(x, g):
    ms = jnp.mean(x * x, axis=-1, keepdims=True)
    return x * lax.rsqrt(ms + EPS) * g


def _rows_to_tiles(ref, val):
    n = val.shape[0]
    for j in range(val.shape[1] // LANES):
        ref[pl.ds(j, n, stride=SUBLANES), :] = val[:, j * LANES:(j + 1) * LANES]


def _tiles_to_rows(ref, n):
    return jnp.concatenate([ref[pl.ds(j, n, stride=SUBLANES), :] for j in range(SUBLANES)], axis=-1)


def _pack_rows(ref, val, tiles):
    _rows_to_tiles(tiles, val)
    ref[...] = pltpu.bitcast(tiles[...].astype(BF16), jnp.uint32)


def _unpack_rows(ref, n, tiles):
    tiles[...] = pltpu.bitcast(ref[...], BF16).astype(F32)
    return _tiles_to_rows(tiles, n).astype(BF16)


def _params(*sem):
    return pltpu.CompilerParams(dimension_semantics=sem, vmem_limit_bytes=VMEM_LIMIT)


def _ada_body(c_ref, w_ref, b_ref, o_ref):
    c = c_ref[...]
    o_ref[...] = _dot3(c * _sigmoid(c), w_ref[...]) + b_ref[...]


def _ada(c, w, b, tn=1024):
    nb, d = c.shape
    n = w.shape[1]
    return pl.pallas_call(
        _ada_body,
        grid=(n // tn,),
        in_specs=[pl.BlockSpec((nb, d), lambda j: (0, 0)),
                  pl.BlockSpec((d, tn), lambda j: (0, j)),
                  pl.BlockSpec((1, tn), lambda j: (0, j))],
        out_specs=pl.BlockSpec((nb, tn), lambda j: (0, j)),
        out_shape=jax.ShapeDtypeStruct((nb, n), F32),
        compiler_params=_params("arbitrary"),
        name="ada",
    )(c, w, b.reshape(1, n))


def _proj_body(x_ref, mod_ref, g1_ref, wm_ref, wa1_ref, wg_ref, wa2_ref, ba2_ref,
               z_ref, q_ref, k_ref, v_ref, rs_ref, lg_ref, gt_ref):
    x = x_ref[0]
    u = (_rms(x, g1_ref[...]) * (1.0 + mod_ref[0, 1:2, :]) + mod_ref[0, 0:1, :]).astype(BF16)
    c0 = 2 * CONV_DIM
    c1 = c0 + 2 * QK_DIM
    c2 = c1 + V_DIM
    c3 = c2 + V_DIM
    pc = _bdot(u, wm_ref[:, 0:c0])
    z_ref[0] = pc[:, :CONV_DIM] * _sigmoid(pc[:, CONV_DIM:])
    qk = _bdot(u, wm_ref[:, c0:c1])
    q_ref[0] = qk[:, :QK_DIM] * (GLA_DK ** -0.5)
    k_ref[0] = qk[:, QK_DIM:]
    v_ref[0] = _bdot(u, wm_ref[:, c1:c2]).astype(BF16)
    r = _bdot(u, wm_ref[:, c2:c3])
    rs_ref[0] = (r * _sigmoid(r)).astype(BF16)
    a1 = _bdot(u, wa1_ref[...])
    xg = _dot3(a1, wa2_ref[...]) + ba2_ref[...]
    lg_ref[0] = (jnp.minimum(xg, 0.0) - jnp.log1p(jnp.exp(-jnp.abs(xg)))) * (1.0 / GLA_TAU)
    gt_ref[0] = _sigmoid(_bdot(u, wg_ref[...])).astype(BF16)


def _wsplit_body(w_ref, wm_ref, wa1_ref, wg_ref):
    c3 = wm_ref.shape[1]
    wm_ref[...] = w_ref[:, :c3].astype(BF16)
    lane = lax.broadcasted_iota(jnp.int32, wa1_ref.shape, 1)
    wa1_ref[...] = jnp.where(lane < GLA_LOWRANK, w_ref[:, c3:c3 + LANES], 0.0).astype(BF16)
    wg_ref[...] = w_ref[:, c3 + GLA_LOWRANK:].astype(BF16)


def _wsplit(w_in, c3, tr=128):
    d, n = w_in.shape
    ng = n - c3 - GLA_LOWRANK
    return pl.pallas_call(
        _wsplit_body,
        grid=(d // tr,),
        in_specs=[pl.BlockSpec((tr, n), lambda i: (i, 0))],
        out_specs=[pl.BlockSpec((tr, c3), lambda i: (i, 0)), pl.BlockSpec((tr, LANES), lambda i: (i, 0)),
                   pl.BlockSpec((tr, ng), lambda i: (i, 0))],
        out_shape=[jax.ShapeDtypeStruct((d, c3), BF16), jax.ShapeDtypeStruct((d, LANES), BF16),
                   jax.ShapeDtypeStruct((d, ng), BF16)],
        compiler_params=_params("arbitrary"),
        name="wsplit",
    )(w_in)


def _proj(x, mod3, g1, w_in, w_a2, b_a2, tm):
    nb, s, d = x.shape
    c3 = 2 * CONV_DIM + 2 * QK_DIM + 2 * V_DIM
    wm, wa1, wg = _wsplit(w_in, c3)
    wa2 = jnp.pad(w_a2, ((0, LANES - GLA_LOWRANK), (0, 0)))
    ng = wg.shape[1]

    def tok(width):
        return pl.BlockSpec((1, tm, width), lambda b, i: (b, i, 0))

    def const(shape):
        return pl.BlockSpec(shape, lambda b, i: (0,) * len(shape))

    def out(width, dt):
        return jax.ShapeDtypeStruct((nb, s, width), dt)

    return pl.pallas_call(
        _proj_body,
        grid=(nb, s // tm),
        in_specs=[tok(d),
                  pl.BlockSpec((1, N_MOD, d), lambda b, i: (b, 0, 0)),
                  const((1, d)), const(wm.shape), const(wa1.shape), const(wg.shape),
                  const(wa2.shape), const((1, QK_DIM))],
        out_specs=[tok(CONV_DIM), tok(QK_DIM), tok(QK_DIM), tok(V_DIM), tok(V_DIM), tok(QK_DIM), tok(ng)],
        out_shape=[out(CONV_DIM, F32), out(QK_DIM, F32), out(QK_DIM, F32), out(V_DIM, BF16),
                   out(V_DIM, BF16), out(QK_DIM, F32), out(ng, BF16)],
        compiler_params=_params("arbitrary", "arbitrary"),
        name="proj",
    )(x, mod3, g1.reshape(1, d), wm, wa1, wg, wa2, b_a2.reshape(1, QK_DIM))


def _gla_body(q_ref, k_ref, lg_ref, v_ref, rs_ref, gn_ref, o_ref, st_ref, b_s, oi_s, *, n_chunks):
    cl = GLA_CHUNK

    @pl.when(pl.program_id(1) == 0)
    def _():
        st_ref[...] = jnp.zeros_like(st_ref)

    row = lax.broadcasted_iota(jnp.int32, (cl, cl), 0)
    col = lax.broadcasted_iota(jnp.int32, (cl, cl), 1)
    causal = col <= row
    tri = jnp.where(causal, 1.0, 0.0).astype(BF16)
    chunks = [slice(c * cl, (c + 1) * cl) for c in range(n_chunks)]
    heads = range(GLA_HEADS)
    ksl = [slice(h * GLA_DK, (h + 1) * GLA_DK) for h in heads]
    vsl = [slice(h * GLA_DV, (h + 1) * GLA_DV) for h in heads]

    lowest = None
    for rows in chunks:
        gh, gl = _split(lg_ref[0, rows, :])
        b = _bdot(tri, gh) + _bdot(tri, gl)
        b_s[rows, :] = b
        low = jnp.min(b[cl - 1:cl, :])
        lowest = low if lowest is None else jnp.minimum(lowest, low)

    def intra_pairwise(h, c):
        kf = k_ref[0, chunks[c], ksl[h]]
        vf = v_ref[0, chunks[c], vsl[h]].astype(F32)
        bh = b_s[chunks[c], ksl[h]]
        key = lax.broadcasted_iota(jnp.int32, (cl, 1), 0)

        def group(g, carry):
            r0 = pl.multiple_of(g * SUBLANES, SUBLANES)
            q8 = q_ref[0, pl.ds(c * cl + r0, SUBLANES), ksl[h]]
            b8 = b_s[pl.ds(c * cl + r0, SUBLANES), ksl[h]]
            out_rows = []
            for r in range(SUBLANES):
                diff = jnp.where(key <= r0 + r, b8[r:r + 1, :] - bh, -jnp.inf)
                att = jnp.sum(jnp.exp(diff) * kf * q8[r:r + 1, :], axis=-1, keepdims=True)
                out_rows.append(jnp.sum(att * vf, axis=0, keepdims=True))
            oi_s[h, pl.ds(c * cl + r0, SUBLANES), :] = jnp.concatenate(out_rows, axis=0)
            return carry

        lax.fori_loop(0, cl // SUBLANES, group, 0)
        return oi_s[h, chunks[c], :]

    def run(factored):
        cums = [b_s[rows, :] for rows in chunks]
        qes, kts, bts, klts, decays = [], [], [], [], []
        for rows, b in zip(chunks, cums):
            qes.append((q_ref[0, rows, :] * jnp.exp(b)).astype(BF16))
            kt = k_ref[0, rows, :].T
            bt = b.T
            bl = bt[:, cl - 1:cl]
            kts.append(kt)
            bts.append(bt)
            klts.append((kt * jnp.exp(bl - bt)).astype(BF16))
            decays.append(jnp.exp(bl))
        intra = {}
        if factored:
            kets = [(kt * jnp.exp(-bt)).astype(BF16) for kt, bt in zip(kts, bts)]
            atts = {}
            for c in range(n_chunks):
                for h in heads:
                    att = _bdot(qes[c][:, ksl[h]], kets[c][ksl[h], :])
                    atts[h, c] = jnp.where(causal, att, 0.0).astype(BF16)
            for c, rows in enumerate(chunks):
                for h in heads:
                    intra[h, c] = _bdot(atts[h, c], v_ref[0, rows, vsl[h]])
        else:
            for c in range(n_chunks):
                for h in heads:
                    intra[h, c] = intra_pairwise(h, c)
        updates = {}
        for c, rows in enumerate(chunks):
            for h in heads:
                updates[h, c] = _bdot(klts[c][ksl[h], :], v_ref[0, rows, vsl[h]])

        outs = {}
        states = []
        for h in heads:
            state = st_ref[h]
            for c, rows in enumerate(chunks):
                o = intra[h, c] + _bdot(qes[c][:, ksl[h]], state.astype(BF16))
                state = decays[c][ksl[h], :] * state + updates[h, c]
                outs[h, c] = (_rms(o, gn_ref[:, vsl[h]]) * rs_ref[0, rows, vsl[h]].astype(F32)).astype(BF16)
            states.append(state)
        for c, rows in enumerate(chunks):
            o_ref[0, rows, :] = jnp.concatenate([outs[h, c] for h in heads], axis=-1)
        for h in heads:
            st_ref[h] = states[h]

    in_range = lowest > -GLA_FACTOR_RANGE
    pl.when(in_range)(functools.partial(run, True))
    pl.when(jnp.logical_not(in_range))(functools.partial(run, False))


def _gla(q, k, lg, v, rs, gn, tc):
    nb, s, _ = q.shape

    def tok(width):
        return pl.BlockSpec((1, tc, width), lambda b, i: (b, i, 0))

    return pl.pallas_call(
        functools.partial(_gla_body, n_chunks=tc // GLA_CHUNK),
        grid=(nb, s // tc),
        in_specs=[tok(QK_DIM), tok(QK_DIM), tok(QK_DIM), tok(V_DIM), tok(V_DIM),
                  pl.BlockSpec((1, V_DIM), lambda b, i: (0, 0))],
        out_specs=tok(V_DIM),
        out_shape=jax.ShapeDtypeStruct((nb, s, V_DIM), BF16),
        scratch_shapes=[pltpu.VMEM((GLA_HEADS, GLA_DK, GLA_DV), F32),
                        pltpu.VMEM((tc, QK_DIM), F32),
                        pltpu.VMEM((GLA_HEADS, tc, GLA_DV), F32)],
        compiler_params=_params("arbitrary", "arbitrary"),
        name="gla",
    )(q, k, lg, v, rs, gn.reshape(1, V_DIM))


def _route(logits, cnt, sub):
    ts = logits.shape[0]
    lane = lax.broadcasted_iota(jnp.int32, (ts, LANES), 1).astype(F32)
    ninf = -jnp.inf
    lgm = jnp.where(lane < N_GROUPS, logits, ninf)
    gmax = jnp.max(lgm, axis=-1, keepdims=True)
    gsel = jnp.min(jnp.where(lgm == gmax, lane, float(LANES)), axis=-1, keepdims=True)
    wg = 1.0 / jnp.sum(jnp.exp(lgm - gmax), axis=-1, keepdims=True)
    base = N_GROUPS + EXPERTS_PER_GROUP * gsel
    le = jnp.where(lane >= base, jnp.where(lane < base + EXPERTS_PER_GROUP, logits, ninf), ninf)
    v1 = jnp.max(le, axis=-1, keepdims=True)
    i1 = jnp.min(jnp.where(le == v1, lane, float(LANES)), axis=-1, keepdims=True)
    le2 = jnp.where(lane == i1, ninf, le)
    v2 = jnp.max(le2, axis=-1, keepdims=True)
    i2 = jnp.min(jnp.where(le2 == v2, lane, float(LANES)), axis=-1, keepdims=True)
    e21 = jnp.exp(v2 - v1)
    w1 = wg / (1.0 + e21)
    w2 = w1 * e21
    eid1 = i1 - N_GROUPS
    eid2 = i2 - N_GROUPS
    oh1 = jnp.where(lane == eid1, 1.0, 0.0)
    oh2 = jnp.where(lane == eid2, 1.0, 0.0)
    ohs = oh1 + oh2
    row = lax.broadcasted_iota(jnp.int32, (sub, sub), 0)
    col = lax.broadcasted_iota(jnp.int32, (sub, sub), 1)
    before = jnp.where(col < row, 1.0, 0.0).astype(BF16)
    tots = []
    for lo in range(0, ts, sub):
        piece = ohs[lo:lo + sub, :]
        tots.append(cnt + _bdot(before, piece.astype(BF16)))
        cnt = cnt + jnp.sum(piece, axis=0, keepdims=True)
    tot = jnp.concatenate(tots, axis=0)
    rank1 = jnp.sum(oh1 * tot, axis=-1, keepdims=True)
    rank2 = jnp.sum(oh2 * tot, axis=-1, keepdims=True)
    packed = jnp.where(lane == 0.0, eid1,
             jnp.where(lane == 1.0, eid2,
             jnp.where(lane == 2.0, w1,
             jnp.where(lane == 3.0, w2,
             jnp.where(lane == 4.0, rank1,
             jnp.where(lane == 5.0, rank2, 0.0))))))
    return packed, cnt


def _merge_body(z_ref, og_ref, gt_ref, x_ref, mod_ref, wdw_ref, bdw_ref, lng_ref, lnb_ref,
                wpw_ref, bpw_ref, wgo_ref, wout_ref, g2_ref, wr_ref, br_ref,
                h_ref, u2_ref, logit_ref, zbuf, zsh, u2t):
    ts = z_ref.shape[1]
    d = x_ref.shape[2]
    first_tile = pl.program_id(1) == 0

    @pl.when(first_tile)
    def _():
        zbuf[0:CONV_HALO, :] = jnp.zeros((CONV_HALO, CONV_DIM), F32)

    zbuf[CONV_HALO:CONV_HALO + ts, :] = z_ref[0]
    span = ts + CONV_HALO - SUBLANES
    for r in range(1, SUBLANES):
        zsh[r - 1] = zbuf[r:r + span, :]
    off = CONV_HALO - (CONV_WIDTH - 1)
    pieces = []
    for blk in range(ts // CONV_ROWS):
        acc = None
        for j in range(CONV_WIDTH):
            a, r = divmod(off + j, SUBLANES)
            lo = a * SUBLANES + blk * CONV_ROWS
            src = zbuf[lo:lo + CONV_ROWS, :] if r == 0 else zsh[r - 1, lo:lo + CONV_ROWS, :]
            term = src * jnp.concatenate([wdw_ref[j]] * (CONV_ROWS // SUBLANES), axis=0)
            acc = term if acc is None else acc + term
        conv = acc + bdw_ref[...]
        mu = jnp.mean(conv, axis=-1, keepdims=True)
        xc = conv - mu
        var = jnp.mean(xc * xc, axis=-1, keepdims=True)
        ln = xc * lax.rsqrt(var + EPS) * lng_ref[...] + lnb_ref[...]
        pieces.append((ln * _sigmoid(ln)).astype(BF16))
    zbuf[0:CONV_HALO, :] = zbuf[ts:ts + CONV_HALO, :]
    y_conv = _bdot(jnp.concatenate(pieces, axis=0), wpw_ref[...]) + bpw_ref[...]
    y_gla = _bdot(og_ref[0], wgo_ref[...])
    merged = gt_ref[0, :, 0:d].astype(F32) * y_conv + gt_ref[0, :, d:2 * d].astype(F32) * y_gla
    y = _bdot(merged.astype(BF16), wout_ref[...])
    h = x_ref[0] + mod_ref[0, 2:3, :] * y
    h_ref[0] = h
    u2 = _rms(h, g2_ref[...]) * (1.0 + mod_ref[0, 4:5, :]) + mod_ref[0, 3:4, :]
    _pack_rows(u2_ref, u2, u2t)
    logit_ref[0] = _dot3(u2, wr_ref[...]) + br_ref[...]


def _merge(z, og, gt, x, mod3, w_dw, b_dw, ln_g, ln_b, w_pw, b_pw, w_go, w_out, g2, w_rg, b_rg, w_re, b_re, ts):
    nb, s, d = x.shape
    npad = LANES - N_GROUPS - N_EXPERTS
    wr = jnp.pad(jnp.concatenate([w_rg, w_re], axis=1), ((0, 0), (0, npad)))
    br = jnp.pad(jnp.concatenate([b_rg, b_re]), (0, npad)).reshape(1, LANES)

    def tok(width):
        return pl.BlockSpec((1, ts, width), lambda b, i: (b, i, 0))

    def const(shape):
        return pl.BlockSpec(shape, lambda b, i: (0,) * len(shape))

    def row(v):
        return v.reshape(1, v.shape[-1])

    return pl.pallas_call(
        _merge_body,
        grid=(nb, s // ts),
        in_specs=[tok(CONV_DIM), tok(V_DIM), tok(2 * d), tok(d),
                  pl.BlockSpec((1, N_MOD, d), lambda b, i: (b, 0, 0)),
                  const((CONV_WIDTH, SUBLANES, CONV_DIM)), const((1, CONV_DIM)), const((1, CONV_DIM)), const((1, CONV_DIM)),
                  const((CONV_DIM, d)), const((1, d)), const((V_DIM, d)), const((d, d)), const((1, d)),
                  const((d, LANES)), const((1, LANES))],
        out_specs=[tok(d), pl.BlockSpec((ts * PACK_ROWS, LANES), lambda b, i: (b * (s // ts) + i, 0)), tok(LANES)],
        out_shape=[jax.ShapeDtypeStruct((nb, s, d), F32),
                   jax.ShapeDtypeStruct((nb * s * PACK_ROWS, LANES), jnp.uint32),
                   jax.ShapeDtypeStruct((nb, s, LANES), F32)],
        scratch_shapes=[pltpu.VMEM((CONV_HALO + ts, CONV_DIM), F32),
                        pltpu.VMEM((SUBLANES - 1, CONV_HALO + ts - SUBLANES, CONV_DIM), F32),
                        pltpu.VMEM((ts * SUBLANES, LANES), F32)],
        compiler_params=_params("arbitrary", "arbitrary"),
        name="merge",
    )(z, og, gt, x, mod3, jnp.broadcast_to(w_dw.reshape(CONV_WIDTH, 1, CONV_DIM), (CONV_WIDTH, SUBLANES, CONV_DIM)), row(b_dw), row(ln_g), row(ln_b),
      w_pw.astype(BF16), row(b_pw), w_go.astype(BF16), w_out.astype(BF16), row(g2), wr, br)


def _route_body(logit_ref, route_ref, cnt_ref, cnt_sc, *, sub):
    @pl.when(pl.program_id(0) == 0)
    def _():
        cnt_sc[...] = jnp.zeros_like(cnt_sc)

    packed, cnt = _route(logit_ref[...], cnt_sc[...], sub)
    route_ref[...] = packed
    cnt_sc[...] = cnt
    cnt_ref[...] = jnp.broadcast_to(cnt, cnt_ref.shape)


def _route_call(logits, tr, sub):
    t = logits.shape[0]
    return pl.pallas_call(
        functools.partial(_route_body, sub=sub),
        grid=(t // tr,),
        in_specs=[pl.BlockSpec((tr, LANES), lambda i: (i, 0))],
        out_specs=[pl.BlockSpec((tr, LANES), lambda i: (i, 0)), pl.BlockSpec((SUBLANES, LANES), lambda i: (0, 0))],
        out_shape=[jax.ShapeDtypeStruct((t, LANES), F32), jax.ShapeDtypeStruct((SUBLANES, LANES), F32)],
        scratch_shapes=[pltpu.VMEM((1, LANES), F32)],
        compiler_params=_params("arbitrary"),
        name="route",
    )(logits)


def _row_copy(src, i, dst, j, sem):
    return pltpu.make_async_copy(src.at[pl.ds(pl.multiple_of(i, SUBLANES), SUBLANES), :],
                                 dst.at[pl.ds(pl.multiple_of(j, SUBLANES), SUBLANES), :], sem)


def _gather_rows(idx_ref, lo, n, src, dst, sem, row0=0):
    for r in range(n):
        _row_copy(src, idx_ref[0, 0, lo + r], dst, (row0 + r) * SUBLANES, sem).start(priority=r % 2)


def _invert_body(lo_ref, hi_ref, slot_ref, src_ref):
    i = pl.program_id(0)
    ts = slot_ref.shape[-1] // TOP_K

    @pl.when(i == 0)
    def _():
        for e in range(lo_ref.shape[0]):
            lo = lo_ref[e]
            hi = hi_ref[e]

            def clear(p, carry, lo=lo, hi=hi):
                for j in range(CLEAR_UNROLL):
                    src_ref[jnp.minimum(lo + p * CLEAR_UNROLL + j, hi - 1)] = 0
                return carry

            trips = lax.shift_right_logical(hi - lo + (CLEAR_UNROLL - 1), CLEAR_UNROLL.bit_length() - 1)
            lax.fori_loop(0, trips, clear, 0)

    for r in range(ts):
        row = (i * ts + r) * PACK_ROWS
        for k in range(TOP_K):
            src_ref[slot_ref[0, 0, k * ts + r]] = row


def _invert(fill_lo, fill_hi, slots, n_slots):
    nt, _, width = slots.shape
    return pl.pallas_call(
        _invert_body,
        grid_spec=pltpu.PrefetchScalarGridSpec(
            num_scalar_prefetch=2,
            grid=(nt,),
            in_specs=[pl.BlockSpec((1, 1, width), lambda i, lo, hi: (i, 0, 0), memory_space=pltpu.SMEM)],
            out_specs=pl.BlockSpec(memory_space=pltpu.SMEM)),
        out_shape=jax.ShapeDtypeStruct((n_slots,), jnp.int32),
        compiler_params=_params("arbitrary"),
        name="invert",
    )(fill_lo, fill_hi, slots)


def _expert_body(te_ref, nu_ref, src_ref, u2_ref, w1_ref, w3_ref, w2_ref, ys_ref,
                 u2v, xg, xt, w1b, w3b, w2b, sem):
    i = pl.program_id(0)
    tmx = src_ref.shape[-1]

    load = pltpu.make_async_copy(u2_ref, u2v, sem)

    @pl.when(i == 0)
    def _():
        load.start()

    @pl.when(jnp.logical_or(i == 0, te_ref[i] != te_ref[jnp.maximum(i - 1, 0)]))
    def _():
        w1b[...] = w1_ref[0].astype(BF16)
        w3b[...] = w3_ref[0].astype(BF16)
        w2b[...] = w2_ref[0].astype(BF16)

    @pl.when(i == 0)
    def _():
        load.wait()

    @pl.when(i < nu_ref[0])
    def _():
        for r in range(tmx):
            row = pl.multiple_of(src_ref[0, 0, r], PACK_ROWS)
            xg[r * PACK_ROWS:(r + 1) * PACK_ROWS, :] = u2v[pl.ds(row, PACK_ROWS), :]
        x = _unpack_rows(xg, tmx, xt)
        h1 = _bdot(x, w1b[...])
        h3 = _bdot(x, w3b[...])
        hid = (h1 * _sigmoid(h1) * h3).astype(BF16)
        _rows_to_tiles(ys_ref, _bdot(hid, w2b[...]))

    @pl.when(i >= nu_ref[0])
    def _():
        ys_ref[...] = jnp.zeros_like(ys_ref)


def _experts(tile_expert, n_used, src_rows, u2p, w1, w3, w2):
    n_tiles, _, tmx = src_rows.shape
    ne, d, f = w1.shape

    def w_map(i, te, nu):
        return (te[i], 0, 0)

    return pl.pallas_call(
        _expert_body,
        grid_spec=pltpu.PrefetchScalarGridSpec(
            num_scalar_prefetch=2,
            grid=(n_tiles,),
            in_specs=[pl.BlockSpec((1, 1, tmx), lambda i, te, nu: (i, 0, 0), memory_space=pltpu.SMEM),
                      pl.BlockSpec(memory_space=pl.ANY),
                      pl.BlockSpec((1, d, f), w_map), pl.BlockSpec((1, d, f), w_map),
                      pl.BlockSpec((1, f, d), w_map)],
            out_specs=pl.BlockSpec((tmx * SUBLANES, LANES), lambda i, te, nu: (i, 0)),
            scratch_shapes=[pltpu.VMEM(u2p.shape, jnp.uint32),
                            pltpu.VMEM((tmx * PACK_ROWS, LANES), jnp.uint32),
                            pltpu.VMEM((tmx * SUBLANES, LANES), F32),
                            pltpu.VMEM((d, f), BF16), pltpu.VMEM((d, f), BF16), pltpu.VMEM((f, d), BF16),
                            pltpu.SemaphoreType.DMA(())]),
        out_shape=jax.ShapeDtypeStruct((n_tiles * tmx * SUBLANES, LANES), F32),
        compiler_params=_params("arbitrary"),
        name="experts",
    )(tile_expert, n_used, src_rows, u2p, w1, w3, w2)


def _final_body(p_ref, h_ref, route_ref, mod_ref, modf_ref, gf_ref, ys_ref, o_ref, y1_buf, y2_buf, sem):
    ts = h_ref.shape[1]

    _gather_rows(p_ref, 0, ts, ys_ref, y1_buf, sem)
    _gather_rows(p_ref, ts, ts, ys_ref, y2_buf, sem)
    pltpu.make_async_copy(ys_ref.at[pl.ds(0, ts * SUBLANES), :], y1_buf, sem).wait()
    pltpu.make_async_copy(ys_ref.at[pl.ds(0, ts * SUBLANES), :], y2_buf, sem).wait()

    route = route_ref[0]
    y2 = route[:, 2:3] * _tiles_to_rows(y1_buf, ts) + route[:, 3:4] * _tiles_to_rows(y2_buf, ts)
    h = h_ref[0] + mod_ref[0, 5:6, :] * y2
    o_ref[0] = _rms(h, gf_ref[...]) * (1.0 + modf_ref[0, 1:2, :]) + modf_ref[0, 0:1, :]


def _final(slots, h, route, mod3, modf3, gf, ys, ts):
    nb, s, d = h.shape
    nt = s // ts

    def tok(width):
        return pl.BlockSpec((1, ts, width), lambda b, i: (b, i, 0))

    return pl.pallas_call(
        _final_body,
        grid=(nb, nt),
        in_specs=[pl.BlockSpec((1, 1, 2 * ts), lambda b, i: (b * nt + i, 0, 0), memory_space=pltpu.SMEM),
                  tok(d), tok(LANES),
                  pl.BlockSpec((1, N_MOD, d), lambda b, i: (b, 0, 0)),
                  pl.BlockSpec((1, 2, d), lambda b, i: (b, 0, 0)),
                  pl.BlockSpec((1, d), lambda b, i: (0, 0)),
                  pl.BlockSpec(memory_space=pl.ANY)],
        out_specs=tok(d),
        out_shape=jax.ShapeDtypeStruct((nb, s, d), F32),
        scratch_shapes=[pltpu.VMEM((ts * SUBLANES, LANES), F32), pltpu.VMEM((ts * SUBLANES, LANES), F32),
                        pltpu.SemaphoreType.DMA(())],
        compiler_params=_params("arbitrary", "arbitrary"),
        name="final",
    )(slots, h, route, mod3, modf3, gf.reshape(1, d), ys)


def _plan(route, cnt, ts, tmx, n_tiles):
    t = route.shape[0]
    counts = cnt[0, :N_EXPERTS].astype(jnp.int32)
    tiles = (counts + (tmx - 1)) // tmx
    tile_end = jnp.cumsum(tiles)
    offs = (tile_end - tiles) * tmx
    n_used = tile_end[-1:]
    tile_ids = jnp.minimum(jnp.arange(n_tiles, dtype=jnp.int32), n_used[0] - 1)
    tile_expert = jnp.sum((tile_ids[:, None] >= tile_end[None, :]).astype(jnp.int32), axis=1)
    eid = route[:, 0:2].astype(jnp.int32)
    rank = route[:, 4:6].astype(jnp.int32)
    slot = jnp.take(offs, eid) + rank
    slots = slot.reshape(t // ts, ts, 2).transpose(0, 2, 1).reshape(t // ts, 1, 2 * ts)
    fill_lo = jnp.concatenate([offs + counts, tile_end[-1:] * tmx]).astype(jnp.int32)
    fill_hi = jnp.concatenate([tile_end * tmx, jnp.full((1,), n_tiles * tmx, jnp.int32)]).astype(jnp.int32)
    src = _invert(fill_lo, fill_hi, slots, n_tiles * tmx)
    return slots * SUBLANES, src.reshape(n_tiles, 1, tmx), tile_expert.astype(jnp.int32), n_used.astype(jnp.int32)


def kernel(x, c, w_ada, b_ada, g_norm1, w_in, w_dw, b_dw, g_conv_ln, b_conv_ln, w_conv_pw, b_conv_pw,
           w_a2, b_a2, g_gla_norm, w_gla_o, w_out, g_norm2, w_router_g, b_router_g, w_router_e,
           b_router_e, w1, w3, w2, w_ada_f, b_ada_f, g_final):
    nb, s, d = x.shape
    assert w_ada.shape[0] == 1, "single-layer block"
    assert d == 2 * PACK_ROWS * LANES, "packed token rows assume D_MODEL = 1024"
    tm = min(512, s)
    tc = min(256, s)
    ts = min(256, s)
    tmx = 256
    t = nb * s
    n_tiles = (t * TOP_K) // tmx + N_EXPERTS
    mod3 = _ada(c, w_ada[0], b_ada[0]).reshape(nb, N_MOD, d)
    modf3 = _ada(c, w_ada_f, b_ada_f).reshape(nb, 2, d)
    z, q, k, v, rs, lg, gt = _proj(x, mod3, g_norm1[0], w_in[0], w_a2[0], b_a2[0], tm)
    og = _gla(q, k, lg, v, rs, g_gla_norm[0], tc)
    h, u2, logits = _merge(z, og, gt, x, mod3, w_dw[0], b_dw[0], g_conv_ln[0], b_conv_ln[0],
                           w_conv_pw[0], b_conv_pw[0], w_gla_o[0], w_out[0], g_norm2[0],
                           w_router_g[0], b_router_g[0], w_router_e[0], b_router_e[0], ts)
    route, cnt = _route_call(logits.reshape(t, LANES), min(ROUTE_ROWS, t), ts)
    slots, src_rows, tile_expert, n_used = _plan(route, cnt, ts, tmx, n_tiles)
    ys = _experts(tile_expert, n_used, src_rows, u2, w1[0], w3[0], w2[0])
    return _final(slots, h, route.reshape(nb, s, LANES), mod3, modf3, g_final, ys, ts)
```

```python
import functools

import jax
import jax.numpy as jnp
from jax import lax
from jax.experimental import pallas as pl
from jax.experimental.pallas import tpu as pltpu

F32 = jnp.float32
BF16 = jnp.bfloat16

EPS = 1e-6
CONV_DIM = 512
CONV_WIDTH = 31
GLA_HEADS = 4
GLA_DK = 128
GLA_DV = 256
GLA_LOWRANK = 16
GLA_TAU = 16.0
QK_DIM = GLA_HEADS * GLA_DK
V_DIM = GLA_HEADS * GLA_DV
N_GROUPS = 4
EXPERTS_PER_GROUP = 8
N_EXPERTS = N_GROUPS * EXPERTS_PER_GROUP
TOP_K = 2
N_MOD = 6

LANES = 128
SUBLANES = 8
PACK_ROWS = 4
CONV_ROWS = 32
ROUTE_ROWS = 2048
CLEAR_UNROLL = 16
CONV_HALO = 32
GLA_CHUNK = 128
GLA_FACTOR_RANGE = 60.0
VMEM_LIMIT = 56 * 1024 * 1024


def _bdot(a, b):
    return jnp.dot(a, b, preferred_element_type=F32)


def _split(a):
    hi = a.astype(BF16)
    lo = (a - hi.astype(F32)).astype(BF16)
    return hi, lo


def _dot3(a, b):
    ah, al = _split(a)
    bh, bl = _split(b)
    return _bdot(ah, bh) + (_bdot(ah, bl) + _bdot(al, bh))


def _sigmoid(x):
    return 1.0 / (1.0 + jnp.exp(-x))


def _rms(x, g):
    ms = jnp.mean(x * x, axis=-1, keepdims=True)
    return x * lax.rsqrt(ms + EPS) * g


def _rows_to_tiles(ref, val):
    n = val.shape[0]
    for j in range(val.shape[1] // LANES):
        ref[pl.ds(j, n, stride=SUBLANES), :] = val[:, j * LANES:(j + 1) * LANES]


def _tiles_to_rows(ref, n):
    return jnp.concatenate([ref[pl.ds(j, n, stride=SUBLANES), :] for j in range(SUBLANES)], axis=-1)


def _pack_rows(ref, val, tiles):
    _rows_to_tiles(tiles, val)
    ref[...] = pltpu.bitcast(tiles[...].astype(BF16), jnp.uint32)


def _unpack_rows(ref, n, tiles):
    tiles[...] = pltpu.bitcast(ref[...], BF16).astype(F32)
    return _tiles_to_rows(tiles, n).astype(BF16)


def _params(*sem):
    return pltpu.CompilerParams(dimension_semantics=sem, vmem_limit_bytes=VMEM_LIMIT)


def _ada_body(c_ref, w_ref, b_ref, o_ref):
    c = c_ref[...]
    o_ref[...] = _dot3(c * _sigmoid(c), w_ref[...]) + b_ref[...]


def _ada(c, w, b, tn=1024):
    nb, d = c.shape
    n = w.shape[1]
    return pl.pallas_call(
        _ada_body,
        grid=(n // tn,),
        in_specs=[pl.BlockSpec((nb, d), lambda j: (0, 0)),
                  pl.BlockSpec((d, tn), lambda j: (0, j)),
                  pl.BlockSpec((1, tn), lambda j: (0, j))],
        out_specs=pl.BlockSpec((nb, tn), lambda j: (0, j)),
        out_shape=jax.ShapeDtypeStruct((nb, n), F32),
        compiler_params=_params("arbitrary"),
        name="ada",
    )(c, w, b.reshape(1, n))


def _proj_body(x_ref, mod_ref, g1_ref, wm_ref, wa1_ref, wg_ref, wa2_ref, ba2_ref,
               z_ref, q_ref, k_ref, v_ref, rs_ref, lg_ref, gt_ref):
    x = x_ref[0]
    u = (_rms(x, g1_ref[...]) * (1.0 + mod_ref[0, 1:2, :]) + mod_ref[0, 0:1, :]).astype(BF16)
    c0 = 2 * CONV_DIM
    c1 = c0 + 2 * QK_DIM
    c2 = c1 + V_DIM
    c3 = c2 + V_DIM
    pc = _bdot(u, wm_ref[:, 0:c0])
    z_ref[0] = pc[:, :CONV_DIM] * _sigmoid(pc[:, CONV_DIM:])
    qk = _bdot(u, wm_ref[:, c0:c1])
    q_ref[0] = qk[:, :QK_DIM] * (GLA_DK ** -0.5)
    k_ref[0] = qk[:, QK_DIM:]
    v_ref[0] = _bdot(u, wm_ref[:, c1:c2]).astype(BF16)
    r = _bdot(u, wm_ref[:, c2:c3])
    rs_ref[0] = (r * _sigmoid(r)).astype(BF16)
    a1 = _bdot(u, wa1_ref[...])
    xg = _dot3(a1, wa2_ref[...]) + ba2_ref[...]
    lg_ref[0] = (jnp.minimum(xg, 0.0) - jnp.log1p(jnp.exp(-jnp.abs(xg)))) * (1.0 / GLA_TAU)
    gt_ref[0] = _sigmoid(_bdot(u, wg_ref[...])).astype(BF16)


def _wsplit_body(wt_ref, wm_ref, wa1_ref, wg_ref, buf, sem, *, c3):
    cw = buf.shape[0]
    n_main = c3 // cw
    j = pl.program_id(0)
    start = jnp.where(j <= n_main, j * cw, c3 + GLA_LOWRANK + (j - n_main - 1) * cw)
    chunk = pltpu.make_async_copy(wt_ref.at[pl.ds(pl.multiple_of(start, SUBLANES), cw), :], buf, sem)
    chunk.start()
    chunk.wait()

    @pl.when(j < n_main)
    def _():
        wm_ref[...] = buf[...].T.astype(BF16)

    @pl.when(j == n_main)
    def _():
        lane = lax.broadcasted_iota(jnp.int32, wa1_ref.shape, 1)
        wa1_ref[...] = jnp.where(lane < GLA_LOWRANK, buf[0:LANES, :].T, 0.0).astype(BF16)

    @pl.when(j > n_main)
    def _():
        wg_ref[...] = buf[...].T.astype(BF16)


def _wsplit(wt, c3, cw=512):
    n, d = wt.shape
    ng = n - c3 - GLA_LOWRANK
    n_main = c3 // cw
    return pl.pallas_call(
        functools.partial(_wsplit_body, c3=c3),
        grid=(n_main + 1 + ng // cw,),
        in_specs=[pl.BlockSpec(memory_space=pl.ANY)],
        out_specs=[pl.BlockSpec((d, cw), lambda j: (0, jnp.minimum(j, n_main - 1))),
                   pl.BlockSpec((d, LANES), lambda j: (0, 0)),
                   pl.BlockSpec((d, cw), lambda j: (0, jnp.clip(j - n_main - 1, 0, ng // cw - 1)))],
        out_shape=[jax.ShapeDtypeStruct((d, c3), BF16), jax.ShapeDtypeStruct((d, LANES), BF16),
                   jax.ShapeDtypeStruct((d, ng), BF16)],
        scratch_shapes=[pltpu.VMEM((cw, d), F32), pltpu.SemaphoreType.DMA(())],
        compiler_params=_params("arbitrary"),
        name="wsplit",
    )(wt)


def _proj(x, mod3, g1, w_in, w_a2, b_a2, tm):
    nb, s, d = x.shape
    c3 = 2 * CONV_DIM + 2 * QK_DIM + 2 * V_DIM
    wm, wa1, wg = _wsplit(jnp.swapaxes(w_in, 0, 1), c3)
    wa2 = jnp.pad(w_a2, ((0, LANES - GLA_LOWRANK), (0, 0)))
    ng = wg.shape[1]

    def tok(width):
        return pl.BlockSpec((1, tm, width), lambda b, i: (b, i, 0))

    def const(shape):
        return pl.BlockSpec(shape, lambda b, i: (0,) * len(shape))

    def out(width, dt):
        return jax.ShapeDtypeStruct((nb, s, width), dt)

    return pl.pallas_call(
        _proj_body,
        grid=(nb, s // tm),
        in_specs=[tok(d),
                  pl.BlockSpec((1, N_MOD, d), lambda b, i: (b, 0, 0)),
                  const((1, d)), const(wm.shape), const(wa1.shape), const(wg.shape),
                  const(wa2.shape), const((1, QK_DIM))],
        out_specs=[tok(CONV_DIM), tok(QK_DIM), tok(QK_DIM), tok(V_DIM), tok(V_DIM), tok(QK_DIM), tok(ng)],
        out_shape=[out(CONV_DIM, F32), out(QK_DIM, F32), out(QK_DIM, F32), out(V_DIM, BF16),
                   out(V_DIM, BF16), out(QK_DIM, F32), out(ng, BF16)],
        compiler_params=_params("arbitrary", "arbitrary"),
        name="proj",
    )(x, mod3, g1.reshape(1, d), wm, wa1, wg, wa2, b_a2.reshape(1, QK_DIM))


def _gla_body(q_ref, k_ref, lg_ref, v_ref, rs_ref, gn_ref, o_ref, st_ref, b_s, oi_s, *, n_chunks):
    cl = GLA_CHUNK

    @pl.when(pl.program_id(1) == 0)
    def _():
        st_ref[...] = jnp.zeros_like(st_ref)

    row = lax.broadcasted_iota(jnp.int32, (cl, cl), 0)
    col = lax.broadcasted_iota(jnp.int32, (cl, cl), 1)
    causal = col <= row
    tri = jnp.where(causal, 1.0, 0.0).astype(BF16)
    chunks = [slice(c * cl, (c + 1) * cl) for c in range(n_chunks)]
    heads = range(GLA_HEADS)
    ksl = [slice(h * GLA_DK, (h + 1) * GLA_DK) for h in heads]
    vsl = [slice(h * GLA_DV, (h + 1) * GLA_DV) for h in heads]

    lowest = None
    for rows in chunks:
        gh, gl = _split(lg_ref[0, rows, :])
        b = _bdot(tri, gh) + _bdot(tri, gl)
        b_s[rows, :] = b
        low = jnp.min(b[cl - 1:cl, :])
        lowest = low if lowest is None else jnp.minimum(lowest, low)

    def intra_pairwise(h, c):
        kf = k_ref[0, chunks[c], ksl[h]]
        vf = v_ref[0, chunks[c], vsl[h]].astype(F32)
        bh = b_s[chunks[c], ksl[h]]
        key = lax.broadcasted_iota(jnp.int32, (cl, 1), 0)

        def group(g, carry):
            r0 = pl.multiple_of(g * SUBLANES, SUBLANES)
            q8 = q_ref[0, pl.ds(c * cl + r0, SUBLANES), ksl[h]]
            b8 = b_s[pl.ds(c * cl + r0, SUBLANES), ksl[h]]
            out_rows = []
            for r in range(SUBLANES):
                diff = jnp.where(key <= r0 + r, b8[r:r + 1, :] - bh, -jnp.inf)
                att = jnp.sum(jnp.exp(diff) * kf * q8[r:r + 1, :], axis=-1, keepdims=True)
                out_rows.append(jnp.sum(att * vf, axis=0, keepdims=True))
            oi_s[h, pl.ds(c * cl + r0, SUBLANES), :] = jnp.concatenate(out_rows, axis=0)
            return carry

        lax.fori_loop(0, cl // SUBLANES, group, 0)
        return oi_s[h, chunks[c], :]

    def run(factored):
        cums = [b_s[rows, :] for rows in chunks]
        qes, kts, bts, klts, decays = [], [], [], [], []
        for rows, b in zip(chunks, cums):
            qes.append((q_ref[0, rows, :] * jnp.exp(b)).astype(BF16))
            kt = k_ref[0, rows, :].T
            bt = b.T
            bl = bt[:, cl - 1:cl]
            kts.append(kt)
            bts.append(bt)
            klts.append((kt * jnp.exp(bl - bt)).astype(BF16))
            decays.append(jnp.exp(bl))
        intra = {}
        if factored:
            kets = [(kt * jnp.exp(-bt)).astype(BF16) for kt, bt in zip(kts, bts)]
            atts = {}
            for c in range(n_chunks):
                for h in heads:
                    att = _bdot(qes[c][:, ksl[h]], kets[c][ksl[h], :])
                    atts[h, c] = jnp.where(causal, att, 0.0).astype(BF16)
            for c, rows in enumerate(chunks):
                for h in heads:
                    intra[h, c] = _bdot(atts[h, c], v_ref[0, rows, vsl[h]])
        else:
            for c in range(n_chunks):
                for h in heads:
                    intra[h, c] = intra_pairwise(h, c)
        updates = {}
        for c, rows in enumerate(chunks):
            for h in heads:
                updates[h, c] = _bdot(klts[c][ksl[h], :], v_ref[0, rows, vsl[h]])

        outs = {}
        states = []
        for h in heads:
            state = st_ref[h]
            for c, rows in enumerate(chunks):
                o = intra[h, c] + _bdot(qes[c][:, ksl[h]], state.astype(BF16))
                state = decays[c][ksl[h], :] * state + updates[h, c]
                outs[h, c] = (_rms(o, gn_ref[:, vsl[h]]) * rs_ref[0, rows, vsl[h]].astype(F32)).astype(BF16)
            states.append(state)
        for c, rows in enumerate(chunks):
            o_ref[0, rows, :] = jnp.concatenate([outs[h, c] for h in heads], axis=-1)
        for h in heads:
            st_ref[h] = states[h]

    in_range = lowest > -GLA_FACTOR_RANGE
    pl.when(in_range)(functools.partial(run, True))
    pl.when(jnp.logical_not(in_range))(functools.partial(run, False))


def _gla(q, k, lg, v, rs, gn, tc):
    nb, s, _ = q.shape

    def tok(width):
        return pl.BlockSpec((1, tc, width), lambda b, i: (b, i, 0))

    return pl.pallas_call(
        functools.partial(_gla_body, n_chunks=tc // GLA_CHUNK),
        grid=(nb, s // tc),
        in_specs=[tok(QK_DIM), tok(QK_DIM), tok(QK_DIM), tok(V_DIM), tok(V_DIM),
                  pl.BlockSpec((1, V_DIM), lambda b, i: (0, 0))],
        out_specs=tok(V_DIM),
        out_shape=jax.ShapeDtypeStruct((nb, s, V_DIM), BF16),
        scratch_shapes=[pltpu.VMEM((GLA_HEADS, GLA_DK, GLA_DV), F32),
                        pltpu.VMEM((tc, QK_DIM), F32),
                        pltpu.VMEM((GLA_HEADS, tc, GLA_DV), F32)],
        compiler_params=_params("arbitrary", "arbitrary"),
        name="gla",
    )(q, k, lg, v, rs, gn.reshape(1, V_DIM))


def _route(logits, cnt, sub):
    ts = logits.shape[0]
    lane = lax.broadcasted_iota(jnp.int32, (ts, LANES), 1).astype(F32)
    ninf = -jnp.inf
    lgm = jnp.where(lane < N_GROUPS, logits, ninf)
    gmax = jnp.max(lgm, axis=-1, keepdims=True)
    gsel = jnp.min(jnp.where(lgm == gmax, lane, float(LANES)), axis=-1, keepdims=True)
    wg = 1.0 / jnp.sum(jnp.exp(lgm - gmax), axis=-1, keepdims=True)
    base = N_GROUPS + EXPERTS_PER_GROUP * gsel
    le = jnp.where(lane >= base, jnp.where(lane < base + EXPERTS_PER_GROUP, logits, ninf), ninf)
    v1 = jnp.max(le, axis=-1, keepdims=True)
    i1 = jnp.min(jnp.where(le == v1, lane, float(LANES)), axis=-1, keepdims=True)
    le2 = jnp.where(lane == i1, ninf, le)
    v2 = jnp.max(le2, axis=-1, keepdims=True)
    i2 = jnp.min(jnp.where(le2 == v2, lane, float(LANES)), axis=-1, keepdims=True)
    e21 = jnp.exp(v2 - v1)
    w1 = wg / (1.0 + e21)
    w2 = w1 * e21
    eid1 = i1 - N_GROUPS
    eid2 = i2 - N_GROUPS
    oh1 = jnp.where(lane == eid1, 1.0, 0.0)
    oh2 = jnp.where(lane == eid2, 1.0, 0.0)
    ohs = oh1 + oh2
    row = lax.broadcasted_iota(jnp.int32, (sub, sub), 0)
    col = lax.broadcasted_iota(jnp.int32, (sub, sub), 1)
    before = jnp.where(col < row, 1.0, 0.0).astype(BF16)
    tots = []
    for lo in range(0, ts, sub):
        piece = ohs[lo:lo + sub, :]
        tots.append(cnt + _bdot(before, piece.astype(BF16)))
        cnt = cnt + jnp.sum(piece, axis=0, keepdims=True)
    tot = jnp.concatenate(tots, axis=0)
    rank1 = jnp.sum(oh1 * tot, axis=-1, keepdims=True)
    rank2 = jnp.sum(oh2 * tot, axis=-1, keepdims=True)
    packed = jnp.where(lane == 0.0, eid1,
             jnp.where(lane == 1.0, eid2,
             jnp.where(lane == 2.0, w1,
             jnp.where(lane == 3.0, w2,
             jnp.where(lane == 4.0, rank1,
             jnp.where(lane == 5.0, rank2, 0.0))))))
    return packed, cnt


def _merge_body(z_ref, og_ref, gt_ref, x_ref, mod_ref, wdw_ref, bdw_ref, lng_ref, lnb_ref,
                wpw_ref, bpw_ref, wgo_ref, wout_ref, g2_ref, wr_ref, br_ref,
                h_ref, u2_ref, logit_ref, zbuf, zsh, u2t):
    ts = z_ref.shape[1]
    d = x_ref.shape[2]
    first_tile = pl.program_id(1) == 0

    @pl.when(first_tile)
    def _():
        zbuf[0:CONV_HALO, :] = jnp.zeros((CONV_HALO, CONV_DIM), F32)

    zbuf[CONV_HALO:CONV_HALO + ts, :] = z_ref[0]
    span = ts + CONV_HALO - SUBLANES
    for r in range(1, SUBLANES):
        zsh[r - 1] = zbuf[r:r + span, :]
    off = CONV_HALO - (CONV_WIDTH - 1)
    pieces = []
    for blk in range(ts // CONV_ROWS):
        acc = None
        for j in range(CONV_WIDTH):
            a, r = divmod(off + j, SUBLANES)
            lo = a * SUBLANES + blk * CONV_ROWS
            src = zbuf[lo:lo + CONV_ROWS, :] if r == 0 else zsh[r - 1, lo:lo + CONV_ROWS, :]
            term = src * jnp.concatenate([wdw_ref[j]] * (CONV_ROWS // SUBLANES), axis=0)
            acc = term if acc is None else acc + term
        conv = acc + bdw_ref[...]
        mu = jnp.mean(conv, axis=-1, keepdims=True)
        xc = conv - mu
        var = jnp.mean(xc * xc, axis=-1, keepdims=True)
        ln = xc * lax.rsqrt(var + EPS) * lng_ref[...] + lnb_ref[...]
        pieces.append((ln * _sigmoid(ln)).astype(BF16))
    zbuf[0:CONV_HALO, :] = zbuf[ts:ts + CONV_HALO, :]
    y_conv = _bdot(jnp.concatenate(pieces, axis=0), wpw_ref[...]) + bpw_ref[...]
    y_gla = _bdot(og_ref[0], wgo_ref[...])
    merged = gt_ref[0, :, 0:d].astype(F32) * y_conv + gt_ref[0, :, d:2 * d].astype(F32) * y_gla
    y = _bdot(merged.astype(BF16), wout_ref[...])
    h = x_ref[0] + mod_ref[0, 2:3, :] * y
    h_ref[0] = h
    u2 = _rms(h, g2_ref[...]) * (1.0 + mod_ref[0, 4:5, :]) + mod_ref[0, 3:4, :]
    _pack_rows(u2_ref, u2, u2t)
    logit_ref[0] = _dot3(u2, wr_ref[...]) + br_ref[...]


def _merge(z, og, gt, x, mod3, w_dw, b_dw, ln_g, ln_b, w_pw, b_pw, w_go, w_out, g2, w_rg, b_rg, w_re, b_re, ts):
    nb, s, d = x.shape
    npad = LANES - N_GROUPS - N_EXPERTS
    wr = jnp.pad(jnp.concatenate([w_rg, w_re], axis=1), ((0, 0), (0, npad)))
    br = jnp.pad(jnp.concatenate([b_rg, b_re]), (0, npad)).reshape(1, LANES)

    def tok(width):
        return pl.BlockSpec((1, ts, width), lambda b, i: (b, i, 0))

    def const(shape):
        return pl.BlockSpec(shape, lambda b, i: (0,) * len(shape))

    def row(v):
        return v.reshape(1, v.shape[-1])

    return pl.pallas_call(
        _merge_body,
        grid=(nb, s // ts),
        in_specs=[tok(CONV_DIM), tok(V_DIM), tok(2 * d), tok(d),
                  pl.BlockSpec((1, N_MOD, d), lambda b, i: (b, 0, 0)),
                  const((CONV_WIDTH, SUBLANES, CONV_DIM)), const((1, CONV_DIM)), const((1, CONV_DIM)), const((1, CONV_DIM)),
                  const((CONV_DIM, d)), const((1, d)), const((V_DIM, d)), const((d, d)), const((1, d)),
                  const((d, LANES)), const((1, LANES))],
        out_specs=[tok(d), pl.BlockSpec((ts * PACK_ROWS, LANES), lambda b, i: (b * (s // ts) + i, 0)), tok(LANES)],
        out_shape=[jax.ShapeDtypeStruct((nb, s, d), F32),
                   jax.ShapeDtypeStruct((nb * s * PACK_ROWS, LANES), jnp.uint32),
                   jax.ShapeDtypeStruct((nb, s, LANES), F32)],
        scratch_shapes=[pltpu.VMEM((CONV_HALO + ts, CONV_DIM), F32),
                        pltpu.VMEM((SUBLANES - 1, CONV_HALO + ts - SUBLANES, CONV_DIM), F32),
                        pltpu.VMEM((ts * SUBLANES, LANES), F32)],
        compiler_params=_params("arbitrary", "arbitrary"),
        name="merge",
    )(z, og, gt, x, mod3, jnp.broadcast_to(w_dw.reshape(CONV_WIDTH, 1, CONV_DIM), (CONV_WIDTH, SUBLANES, CONV_DIM)), row(b_dw), row(ln_g), row(ln_b),
      w_pw.astype(BF16), row(b_pw), w_go.astype(BF16), w_out.astype(BF16), row(g2), wr, br)


def _route_body(logit_ref, route_ref, routet_ref, cnt_ref, cnt_sc, *, sub):
    @pl.when(pl.program_id(0) == 0)
    def _():
        cnt_sc[...] = jnp.zeros_like(cnt_sc)

    packed, cnt = _route(logit_ref[...], cnt_sc[...], sub)
    route_ref[...] = packed
    routet_ref[...] = packed.T[0:SUBLANES, :]
    cnt_sc[...] = cnt
    cnt_ref[...] = jnp.broadcast_to(cnt, cnt_ref.shape)


def _route_call(logits, tr, sub):
    t = logits.shape[0]
    return pl.pallas_call(
        functools.partial(_route_body, sub=sub),
        grid=(t // tr,),
        in_specs=[pl.BlockSpec((tr, LANES), lambda i: (i, 0))],
        out_specs=[pl.BlockSpec((tr, LANES), lambda i: (i, 0)), pl.BlockSpec((SUBLANES, tr), lambda i: (0, i)),
                   pl.BlockSpec((SUBLANES, LANES), lambda i: (0, 0))],
        out_shape=[jax.ShapeDtypeStruct((t, LANES), F32), jax.ShapeDtypeStruct((SUBLANES, t), F32),
                   jax.ShapeDtypeStruct((SUBLANES, LANES), F32)],
        scratch_shapes=[pltpu.VMEM((1, LANES), F32)],
        compiler_params=_params("arbitrary"),
        name="route",
    )(logits)


def _row_copy(src, i, dst, j, sem):
    return pltpu.make_async_copy(src.at[pl.ds(pl.multiple_of(i, SUBLANES), SUBLANES), :],
                                 dst.at[pl.ds(pl.multiple_of(j, SUBLANES), SUBLANES), :], sem)


def _invert_body(lo_ref, hi_ref, slot_ref, src_ref):
    i = pl.program_id(0)
    ts = slot_ref.shape[-1] // TOP_K

    @pl.when(i == 0)
    def _():
        for e in range(lo_ref.shape[0]):
            lo = lo_ref[e]
            hi = hi_ref[e]

            def clear(p, carry, lo=lo, hi=hi):
                for j in range(CLEAR_UNROLL):
                    src_ref[jnp.minimum(lo + p * CLEAR_UNROLL + j, hi - 1)] = 0
                return carry

            trips = lax.shift_right_logical(hi - lo + (CLEAR_UNROLL - 1), CLEAR_UNROLL.bit_length() - 1)
            lax.fori_loop(0, trips, clear, 0)

    for r in range(ts):
        row = (i * ts + r) * PACK_ROWS
        for k in range(TOP_K):
            src_ref[slot_ref[0, 0, k * ts + r]] = row


def _invert(fill_lo, fill_hi, slots, n_slots):
    nt, _, width = slots.shape
    return pl.pallas_call(
        _invert_body,
        grid_spec=pltpu.PrefetchScalarGridSpec(
            num_scalar_prefetch=2,
            grid=(nt,),
            in_specs=[pl.BlockSpec((1, 1, width), lambda i, lo, hi: (i, 0, 0), memory_space=pltpu.SMEM)],
            out_specs=pl.BlockSpec(memory_space=pltpu.SMEM)),
        out_shape=jax.ShapeDtypeStruct((n_slots,), jnp.int32),
        compiler_params=_params("arbitrary"),
        name="invert",
    )(fill_lo, fill_hi, slots)


def _expert_body(te_ref, nu_ref, src_ref, u2_ref, w1_ref, w3_ref, w2_ref, ys_ref,
                 u2v, xg, xt, w1b, w3b, w2b, sem):
    i = pl.program_id(0)
    tmx = src_ref.shape[-1]

    load = pltpu.make_async_copy(u2_ref, u2v, sem)

    @pl.when(i == 0)
    def _():
        load.start()

    @pl.when(jnp.logical_or(i == 0, te_ref[i] != te_ref[jnp.maximum(i - 1, 0)]))
    def _():
        w1b[...] = w1_ref[0].astype(BF16)
        w3b[...] = w3_ref[0].astype(BF16)
        w2b[...] = w2_ref[0].astype(BF16)

    @pl.when(i == 0)
    def _():
        load.wait()

    @pl.when(i < nu_ref[0])
    def _():
        for r in range(tmx):
            row = pl.multiple_of(src_ref[0, 0, r], PACK_ROWS)
            xg[r * PACK_ROWS:(r + 1) * PACK_ROWS, :] = u2v[pl.ds(row, PACK_ROWS), :]
        x = _unpack_rows(xg, tmx, xt)
        h1 = _bdot(x, w1b[...])
        h3 = _bdot(x, w3b[...])
        hid = (h1 * _sigmoid(h1) * h3).astype(BF16)
        _rows_to_tiles(ys_ref, _bdot(hid, w2b[...]))

    @pl.when(i >= nu_ref[0])
    def _():
        ys_ref[...] = jnp.zeros_like(ys_ref)


def _experts(tile_expert, n_used, src_rows, u2p, w1, w3, w2):
    n_tiles, _, tmx = src_rows.shape
    ne, d, f = w1.shape

    def w_map(i, te, nu):
        return (te[i], 0, 0)

    return pl.pallas_call(
        _expert_body,
        grid_spec=pltpu.PrefetchScalarGridSpec(
            num_scalar_prefetch=2,
            grid=(n_tiles,),
            in_specs=[pl.BlockSpec((1, 1, tmx), lambda i, te, nu: (i, 0, 0), memory_space=pltpu.SMEM),
                      pl.BlockSpec(memory_space=pl.ANY),
                      pl.BlockSpec((1, d, f), w_map), pl.BlockSpec((1, d, f), w_map),
                      pl.BlockSpec((1, f, d), w_map)],
            out_specs=pl.BlockSpec((tmx * SUBLANES, LANES), lambda i, te, nu: (i, 0)),
            scratch_shapes=[pltpu.VMEM(u2p.shape, jnp.uint32),
                            pltpu.VMEM((tmx * PACK_ROWS, LANES), jnp.uint32),
                            pltpu.VMEM((tmx * SUBLANES, LANES), F32),
                            pltpu.VMEM((d, f), BF16), pltpu.VMEM((d, f), BF16), pltpu.VMEM((f, d), BF16),
                            pltpu.SemaphoreType.DMA(())]),
        out_shape=jax.ShapeDtypeStruct((n_tiles * tmx * SUBLANES, LANES), F32),
        compiler_params=_params("arbitrary"),
        name="experts",
    )(tile_expert, n_used, src_rows, u2p, w1, w3, w2)


def _final_body(p_ref, h_ref, route_ref, mod_ref, modf_ref, gf_ref, ys_ref, o_ref, y1_buf, y2_buf, sem):
    ts = h_ref.shape[1]

    for k, buf in enumerate((y1_buf, y2_buf)):
        for r in range(ts):
            _row_copy(ys_ref, p_ref[0, 0, k * ts + r], buf, r * SUBLANES, sem).start(priority=r % 2)
    pltpu.make_async_copy(ys_ref.at[pl.ds(0, ts * SUBLANES), :], y1_buf, sem).wait()
    pltpu.make_async_copy(ys_ref.at[pl.ds(0, ts * SUBLANES), :], y2_buf, sem).wait()

    route = route_ref[0]
    y2 = route[:, 2:3] * _tiles_to_rows(y1_buf, ts) + route[:, 3:4] * _tiles_to_rows(y2_buf, ts)
    h = h_ref[0] + mod_ref[0, 5:6, :] * y2
    o_ref[0] = _rms(h, gf_ref[...]) * (1.0 + modf_ref[0, 1:2, :]) + modf_ref[0, 0:1, :]


def _final(slot_rows, h, route, mod3, modf3, gf, ys, ts):
    nb, s, d = h.shape
    nt = s // ts

    def tok(width):
        return pl.BlockSpec((1, ts, width), lambda b, i: (b, i, 0))

    return pl.pallas_call(
        _final_body,
        grid=(nb, nt),
        in_specs=[pl.BlockSpec((1, 1, TOP_K * ts), lambda b, i: (b * nt + i, 0, 0), memory_space=pltpu.SMEM),
                  tok(d), tok(LANES),
                  pl.BlockSpec((1, N_MOD, d), lambda b, i: (b, 0, 0)),
                  pl.BlockSpec((1, 2, d), lambda b, i: (b, 0, 0)),
                  pl.BlockSpec((1, d), lambda b, i: (0, 0)),
                  pl.BlockSpec(memory_space=pl.ANY)],
        out_specs=tok(d),
        out_shape=jax.ShapeDtypeStruct((nb, s, d), F32),
        scratch_shapes=[pltpu.VMEM((ts * SUBLANES, LANES), F32), pltpu.VMEM((ts * SUBLANES, LANES), F32),
                        pltpu.SemaphoreType.DMA(())],
        compiler_params=_params("arbitrary", "arbitrary"),
        name="final",
    )(slot_rows, h, route, mod3, modf3, gf.reshape(1, d), ys)


def _plan(routet, cnt, ts, tmx, n_tiles):
    t = routet.shape[1]
    counts = cnt[0, :N_EXPERTS].astype(jnp.int32)
    tiles = (counts + (tmx - 1)) // tmx
    tile_end = jnp.cumsum(tiles)
    offs = ((tile_end - tiles) * tmx).astype(jnp.int32)
    n_used = tile_end[-1:]
    tile_ids = jnp.minimum(jnp.arange(n_tiles, dtype=jnp.int32), n_used[0] - 1)
    tile_expert = jnp.sum((tile_ids[:, None] >= tile_end[None, :]).astype(jnp.int32), axis=1)

    slot = jnp.take(offs, routet[0:2].astype(jnp.int32)) + routet[4:6].astype(jnp.int32)
    slots = slot.reshape(TOP_K, t // ts, ts).transpose(1, 0, 2).reshape(t // ts, 1, TOP_K * ts)
    fill_lo = jnp.concatenate([offs + counts, tile_end[-1:] * tmx]).astype(jnp.int32)
    fill_hi = jnp.concatenate([tile_end * tmx, jnp.full((1,), n_tiles * tmx, jnp.int32)]).astype(jnp.int32)
    src = _invert(fill_lo, fill_hi, slots, n_tiles * tmx)
    return slots * SUBLANES, src.reshape(n_tiles, 1, tmx), tile_expert.astype(jnp.int32), n_used.astype(jnp.int32)


def kernel(x, c, w_ada, b_ada, g_norm1, w_in, w_dw, b_dw, g_conv_ln, b_conv_ln, w_conv_pw, b_conv_pw,
           w_a2, b_a2, g_gla_norm, w_gla_o, w_out, g_norm2, w_router_g, b_router_g, w_router_e,
           b_router_e, w1, w3, w2, w_ada_f, b_ada_f, g_final):
    nb, s, d = x.shape
    assert w_ada.shape[0] == 1, "single-layer block"
    assert d == 2 * PACK_ROWS * LANES, "packed token rows assume D_MODEL = 1024"
    tm = min(512, s)
    tc = min(256, s)
    ts = min(256, s)
    tmx = 256
    t = nb * s
    n_tiles = (t * TOP_K) // tmx + N_EXPERTS
    mod3 = _ada(c, w_ada[0], b_ada[0]).reshape(nb, N_MOD, d)
    modf3 = _ada(c, w_ada_f, b_ada_f).reshape(nb, 2, d)
    z, q, k, v, rs, lg, gt = _proj(x, mod3, g_norm1[0], w_in[0], w_a2[0], b_a2[0], tm)
    og = _gla(q, k, lg, v, rs, g_gla_norm[0], tc)
    h, u2, logits = _merge(z, og, gt, x, mod3, w_dw[0], b_dw[0], g_conv_ln[0], b_conv_ln[0],
                           w_conv_pw[0], b_conv_pw[0], w_gla_o[0], w_out[0], g_norm2[0],
                           w_router_g[0], b_router_g[0], w_router_e[0], b_router_e[0], ts)
    route, routet, cnt = _route_call(logits.reshape(t, LANES), min(ROUTE_ROWS, t), ts)
    slot_rows, src_rows, tile_expert, n_used = _plan(routet, cnt, ts, tmx, n_tiles)
    ys = _experts(tile_expert, n_used, src_rows, u2, w1[0], w3[0], w2[0])
    return _final(slot_rows, h, route.reshape(nb, s, LANES), mod3, modf3, g_final, ys, ts)
```

```python
import functools

import jax
import jax.numpy as jnp
from jax import lax
from jax.experimental import pallas as pl
from jax.experimental.pallas import tpu as pltpu

F32 = jnp.float32
BF16 = jnp.bfloat16

EPS = 1e-6
CONV_DIM = 512
CONV_WIDTH = 31
GLA_HEADS = 4
GLA_DK = 128
GLA_DV = 256
GLA_LOWRANK = 16
GLA_TAU = 16.0
QK_DIM = GLA_HEADS * GLA_DK
V_DIM = GLA_HEADS * GLA_DV
N_GROUPS = 4
EXPERTS_PER_GROUP = 8
N_EXPERTS = N_GROUPS * EXPERTS_PER_GROUP
TOP_K = 2
N_MOD = 6

LANES = 128
SUBLANES = 8
PACK_ROWS = 4
CONV_ROWS = 32
ROUTE_ROWS = 2048
CLEAR_UNROLL = 16
CONV_HALO = 32
GLA_CHUNK = 128
GLA_FACTOR_RANGE = 60.0
VMEM_LIMIT = 56 * 1024 * 1024


def _bdot(a, b):
    return jnp.dot(a, b, preferred_element_type=F32)


def _split(a):
    hi = a.astype(BF16)
    lo = (a - hi.astype(F32)).astype(BF16)
    return hi, lo


def _dot3(a, b):
    ah, al = _split(a)
    bh, bl = _split(b)
    return _bdot(ah, bh) + (_bdot(ah, bl) + _bdot(al, bh))


def _sigmoid(x):
    return 1.0 / (1.0 + jnp.exp(-x))


def _rms(x, g):
    ms = jnp.mean(x * x, axis=-1, keepdims=True)
    return x * lax.rsqrt(ms + EPS) * g


def _rows_to_tiles(ref, val):
    n = val.shape[0]
    for j in range(val.shape[1] // LANES):
        ref[pl.ds(j, n, stride=SUBLANES), :] = val[:, j * LANES:(j + 1) * LANES]


def _tiles_to_rows(ref, n):
    return jnp.concatenate([ref[pl.ds(j, n, stride=SUBLANES), :] for j in range(SUBLANES)], axis=-1)


def _pack_rows(ref, val, tiles):
    _rows_to_tiles(tiles, val)
    ref[...] = pltpu.bitcast(tiles[...].astype(BF16), jnp.uint32)


def _unpack_rows(ref, n, tiles):
    tiles[...] = pltpu.bitcast(ref[...], BF16).astype(F32)
    return _tiles_to_rows(tiles, n).astype(BF16)


def _params(*sem):
    return pltpu.CompilerParams(dimension_semantics=sem, vmem_limit_bytes=VMEM_LIMIT)


def _ada_body(c_ref, w_ref, b_ref, o_ref):
    c = c_ref[...]
    o_ref[...] = _dot3(c * _sigmoid(c), w_ref[...]) + b_ref[...]


def _ada(c, w, b, tn=1024):
    nb, d = c.shape
    n = w.shape[1]
    return pl.pallas_call(
        _ada_body,
        grid=(n // tn,),
        in_specs=[pl.BlockSpec((nb, d), lambda j: (0, 0)),
                  pl.BlockSpec((d, tn), lambda j: (0, j)),
                  pl.BlockSpec((1, tn), lambda j: (0, j))],
        out_specs=pl.BlockSpec((nb, tn), lambda j: (0, j)),
        out_shape=jax.ShapeDtypeStruct((nb, n), F32),
        compiler_params=_params("arbitrary"),
        name="ada",
    )(c, w, b.reshape(1, n))


def _proj_body(x_ref, mod_ref, g1_ref, wm_ref, wa1_ref, wg_ref, wa2_ref, ba2_ref,
               z_ref, q_ref, k_ref, v_ref, rs_ref, lg_ref, gt_ref):
    x = x_ref[0]
    u = (_rms(x, g1_ref[...]) * (1.0 + mod_ref[0, 1:2, :]) + mod_ref[0, 0:1, :]).astype(BF16)
    c0 = 2 * CONV_DIM
    c1 = c0 + 2 * QK_DIM
    c2 = c1 + V_DIM
    c3 = c2 + V_DIM
    pc = _bdot(u, wm_ref[:, 0:c0])
    z_ref[0] = pc[:, :CONV_DIM] * _sigmoid(pc[:, CONV_DIM:])
    qk = _bdot(u, wm_ref[:, c0:c1])
    q_ref[0] = qk[:, :QK_DIM] * (GLA_DK ** -0.5)
    k_ref[0] = qk[:, QK_DIM:]
    v_ref[0] = _bdot(u, wm_ref[:, c1:c2]).astype(BF16)
    r = _bdot(u, wm_ref[:, c2:c3])
    rs_ref[0] = (r * _sigmoid(r)).astype(BF16)
    a1 = _bdot(u, wa1_ref[...])
    xg = _dot3(a1, wa2_ref[...]) + ba2_ref[...]
    lg_ref[0] = (jnp.minimum(xg, 0.0) - jnp.log1p(jnp.exp(-jnp.abs(xg)))) * (1.0 / GLA_TAU)
    gt_ref[0] = _sigmoid(_bdot(u, wg_ref[...])).astype(BF16)


def _wsplit_body(wt_ref, wm_ref, wa1_ref, wg_ref, buf, sem, *, c3):
    cw = buf.shape[0]
    n_main = c3 // cw
    j = pl.program_id(0)
    start = jnp.where(j <= n_main, j * cw, c3 + GLA_LOWRANK + (j - n_main - 1) * cw)
    chunk = pltpu.make_async_copy(wt_ref.at[pl.ds(pl.multiple_of(start, SUBLANES), cw), :], buf, sem)
    chunk.start()
    chunk.wait()

    @pl.when(j < n_main)
    def _():
        wm_ref[...] = buf[...].T.astype(BF16)

    @pl.when(j == n_main)
    def _():
        lane = lax.broadcasted_iota(jnp.int32, wa1_ref.shape, 1)
        wa1_ref[...] = jnp.where(lane < GLA_LOWRANK, buf[0:LANES, :].T, 0.0).astype(BF16)

    @pl.when(j > n_main)
    def _():
        wg_ref[...] = buf[...].T.astype(BF16)


def _wsplit(wt, c3, cw=512):
    n, d = wt.shape
    ng = n - c3 - GLA_LOWRANK
    n_main = c3 // cw
    return pl.pallas_call(
        functools.partial(_wsplit_body, c3=c3),
        grid=(n_main + 1 + ng // cw,),
        in_specs=[pl.BlockSpec(memory_space=pl.ANY)],
        out_specs=[pl.BlockSpec((d, cw), lambda j: (0, jnp.minimum(j, n_main - 1))),
                   pl.BlockSpec((d, LANES), lambda j: (0, 0)),
                   pl.BlockSpec((d, cw), lambda j: (0, jnp.clip(j - n_main - 1, 0, ng // cw - 1)))],
        out_shape=[jax.ShapeDtypeStruct((d, c3), BF16), jax.ShapeDtypeStruct((d, LANES), BF16),
                   jax.ShapeDtypeStruct((d, ng), BF16)],
        scratch_shapes=[pltpu.VMEM((cw, d), F32), pltpu.SemaphoreType.DMA(())],
        compiler_params=_params("arbitrary"),
        name="wsplit",
    )(wt)


def _proj(x, mod3, g1, w_in, w_a2, b_a2, tm):
    nb, s, d = x.shape
    c3 = 2 * CONV_DIM + 2 * QK_DIM + 2 * V_DIM
    wm, wa1, wg = _wsplit(jnp.swapaxes(w_in, 0, 1), c3)
    wa2 = jnp.pad(w_a2, ((0, LANES - GLA_LOWRANK), (0, 0)))
    ng = wg.shape[1]

    def tok(width):
        return pl.BlockSpec((1, tm, width), lambda b, i: (b, i, 0))

    def const(shape):
        return pl.BlockSpec(shape, lambda b, i: (0,) * len(shape))

    def out(width, dt):
        return jax.ShapeDtypeStruct((nb, s, width), dt)

    return pl.pallas_call(
        _proj_body,
        grid=(nb, s // tm),
        in_specs=[tok(d),
                  pl.BlockSpec((1, N_MOD, d), lambda b, i: (b, 0, 0)),
                  const((1, d)), const(wm.shape), const(wa1.shape), const(wg.shape),
                  const(wa2.shape), const((1, QK_DIM))],
        out_specs=[tok(CONV_DIM), tok(QK_DIM), tok(QK_DIM), tok(V_DIM), tok(V_DIM), tok(QK_DIM), tok(ng)],
        out_shape=[out(CONV_DIM, F32), out(QK_DIM, F32), out(QK_DIM, F32), out(V_DIM, BF16),
                   out(V_DIM, BF16), out(QK_DIM, F32), out(ng, BF16)],
        compiler_params=_params("arbitrary", "arbitrary"),
        name="proj",
    )(x, mod3, g1.reshape(1, d), wm, wa1, wg, wa2, b_a2.reshape(1, QK_DIM))


def _gla_body(q_ref, k_ref, lg_ref, v_ref, rs_ref, gn_ref, o_ref, st_ref, b_s, oi_s, *, n_chunks):
    cl = GLA_CHUNK

    @pl.when(pl.program_id(1) == 0)
    def _():
        st_ref[...] = jnp.zeros_like(st_ref)

    row = lax.broadcasted_iota(jnp.int32, (cl, cl), 0)
    col = lax.broadcasted_iota(jnp.int32, (cl, cl), 1)
    causal = col <= row
    tri = jnp.where(causal, 1.0, 0.0).astype(BF16)
    chunks = [slice(c * cl, (c + 1) * cl) for c in range(n_chunks)]
    heads = range(GLA_HEADS)
    ksl = [slice(h * GLA_DK, (h + 1) * GLA_DK) for h in heads]
    vsl = [slice(h * GLA_DV, (h + 1) * GLA_DV) for h in heads]

    lowest = None
    for rows in chunks:
        gh, gl = _split(lg_ref[0, rows, :])
        b = _bdot(tri, gh) + _bdot(tri, gl)
        b_s[rows, :] = b
        low = jnp.min(b[cl - 1:cl, :])
        lowest = low if lowest is None else jnp.minimum(lowest, low)

    def intra_pairwise(h, c):
        kf = k_ref[0, chunks[c], ksl[h]]
        vf = v_ref[0, chunks[c], vsl[h]].astype(F32)
        bh = b_s[chunks[c], ksl[h]]
        key = lax.broadcasted_iota(jnp.int32, (cl, 1), 0)

        def group(g, carry):
            r0 = pl.multiple_of(g * SUBLANES, SUBLANES)
            q8 = q_ref[0, pl.ds(c * cl + r0, SUBLANES), ksl[h]]
            b8 = b_s[pl.ds(c * cl + r0, SUBLANES), ksl[h]]
            out_rows = []
            for r in range(SUBLANES):
                diff = jnp.where(key <= r0 + r, b8[r:r + 1, :] - bh, -jnp.inf)
                att = jnp.sum(jnp.exp(diff) * kf * q8[r:r + 1, :], axis=-1, keepdims=True)
                out_rows.append(jnp.sum(att * vf, axis=0, keepdims=True))
            oi_s[h, pl.ds(c * cl + r0, SUBLANES), :] = jnp.concatenate(out_rows, axis=0)
            return carry

        lax.fori_loop(0, cl // SUBLANES, group, 0)
        return oi_s[h, chunks[c], :]

    def run(factored):
        cums = [b_s[rows, :] for rows in chunks]
        qes, kts, bts, klts, decays = [], [], [], [], []
        for rows, b in zip(chunks, cums):
            qes.append((q_ref[0, rows, :] * jnp.exp(b)).astype(BF16))
            kt = k_ref[0, rows, :].T
            bt = b.T
            bl = bt[:, cl - 1:cl]
            kts.append(kt)
            bts.append(bt)
            klts.append((kt * jnp.exp(bl - bt)).astype(BF16))
            decays.append(jnp.exp(bl))
        intra = {}
        if factored:
            kets = [(kt * jnp.exp(-bt)).astype(BF16) for kt, bt in zip(kts, bts)]
            atts = {}
            for c in range(n_chunks):
                for h in heads:
                    att = _bdot(qes[c][:, ksl[h]], kets[c][ksl[h], :])
                    atts[h, c] = jnp.where(causal, att, 0.0).astype(BF16)
            for c, rows in enumerate(chunks):
                for h in heads:
                    intra[h, c] = _bdot(atts[h, c], v_ref[0, rows, vsl[h]])
        else:
            for c in range(n_chunks):
                for h in heads:
                    intra[h, c] = intra_pairwise(h, c)
        updates = {}
        for c, rows in enumerate(chunks):
            for h in heads:
                updates[h, c] = _bdot(klts[c][ksl[h], :], v_ref[0, rows, vsl[h]])

        outs = {}
        states = []
        for h in heads:
            state = st_ref[h]
            for c, rows in enumerate(chunks):
                o = intra[h, c] + _bdot(qes[c][:, ksl[h]], state.astype(BF16))
                state = decays[c][ksl[h], :] * state + updates[h, c]
                outs[h, c] = (_rms(o, gn_ref[:, vsl[h]]) * rs_ref[0, rows, vsl[h]].astype(F32)).astype(BF16)
            states.append(state)
        for c, rows in enumerate(chunks):
            o_ref[0, rows, :] = jnp.concatenate([outs[h, c] for h in heads], axis=-1)
        for h in heads:
            st_ref[h] = states[h]

    in_range = lowest > -GLA_FACTOR_RANGE
    pl.when(in_range)(functools.partial(run, True))
    pl.when(jnp.logical_not(in_range))(functools.partial(run, False))


def _gla(q, k, lg, v, rs, gn, tc):
    nb, s, _ = q.shape

    def tok(width):
        return pl.BlockSpec((1, tc, width), lambda b, i: (b, i, 0))

    return pl.pallas_call(
        functools.partial(_gla_body, n_chunks=tc // GLA_CHUNK),
        grid=(nb, s // tc),
        in_specs=[tok(QK_DIM), tok(QK_DIM), tok(QK_DIM), tok(V_DIM), tok(V_DIM),
                  pl.BlockSpec((1, V_DIM), lambda b, i: (0, 0))],
        out_specs=tok(V_DIM),
        out_shape=jax.ShapeDtypeStruct((nb, s, V_DIM), BF16),
        scratch_shapes=[pltpu.VMEM((GLA_HEADS, GLA_DK, GLA_DV), F32),
                        pltpu.VMEM((tc, QK_DIM), F32),
                        pltpu.VMEM((GLA_HEADS, tc, GLA_DV), F32)],
        compiler_params=_params("arbitrary", "arbitrary"),
        name="gla",
    )(q, k, lg, v, rs, gn.reshape(1, V_DIM))


def _route(logits, cnt, sub):
    ts = logits.shape[0]
    lane = lax.broadcasted_iota(jnp.int32, (ts, LANES), 1).astype(F32)
    ninf = -jnp.inf
    lgm = jnp.where(lane < N_GROUPS, logits, ninf)
    gmax = jnp.max(lgm, axis=-1, keepdims=True)
    gsel = jnp.min(jnp.where(lgm == gmax, lane, float(LANES)), axis=-1, keepdims=True)
    wg = 1.0 / jnp.sum(jnp.exp(lgm - gmax), axis=-1, keepdims=True)
    base = N_GROUPS + EXPERTS_PER_GROUP * gsel
    le = jnp.where(lane >= base, jnp.where(lane < base + EXPERTS_PER_GROUP, logits, ninf), ninf)
    v1 = jnp.max(le, axis=-1, keepdims=True)
    i1 = jnp.min(jnp.where(le == v1, lane, float(LANES)), axis=-1, keepdims=True)
    le2 = jnp.where(lane == i1, ninf, le)
    v2 = jnp.max(le2, axis=-1, keepdims=True)
    i2 = jnp.min(jnp.where(le2 == v2, lane, float(LANES)), axis=-1, keepdims=True)
    e21 = jnp.exp(v2 - v1)
    w1 = wg / (1.0 + e21)
    w2 = w1 * e21
    eid1 = i1 - N_GROUPS
    eid2 = i2 - N_GROUPS
    oh1 = jnp.where(lane == eid1, 1.0, 0.0)
    oh2 = jnp.where(lane == eid2, 1.0, 0.0)
    ohs = oh1 + oh2
    row = lax.broadcasted_iota(jnp.int32, (sub, sub), 0)
    col = lax.broadcasted_iota(jnp.int32, (sub, sub), 1)
    before = jnp.where(col < row, 1.0, 0.0).astype(BF16)
    tots = []
    for lo in range(0, ts, sub):
        piece = ohs[lo:lo + sub, :]
        tots.append(cnt + _bdot(before, piece.astype(BF16)))
        cnt = cnt + jnp.sum(piece, axis=0, keepdims=True)
    tot = jnp.concatenate(tots, axis=0)
    rank1 = jnp.sum(oh1 * tot, axis=-1, keepdims=True)
    rank2 = jnp.sum(oh2 * tot, axis=-1, keepdims=True)
    packed = jnp.where(lane == 0.0, eid1,
             jnp.where(lane == 1.0, eid2,
             jnp.where(lane == 2.0, w1,
             jnp.where(lane == 3.0, w2,
             jnp.where(lane == 4.0, rank1,
             jnp.where(lane == 5.0, rank2, 0.0))))))
    return packed, cnt


def _merge_body(z_ref, og_ref, gt_ref, x_ref, mod_ref, wdw_ref, bdw_ref, lng_ref, lnb_ref,
                wpw_ref, bpw_ref, wgo_ref, wout_ref, g2_ref, wr_ref, br_ref,
                h_ref, u2_ref, logit_ref, zbuf, zsh, u2t):
    ts = z_ref.shape[1]
    d = x_ref.shape[2]
    first_tile = pl.program_id(1) == 0

    @pl.when(first_tile)
    def _():
        zbuf[0:CONV_HALO, :] = jnp.zeros((CONV_HALO, CONV_DIM), F32)

    zbuf[CONV_HALO:CONV_HALO + ts, :] = z_ref[0]
    span = ts + CONV_HALO - SUBLANES
    for r in range(1, SUBLANES):
        zsh[r - 1] = zbuf[r:r + span, :]
    off = CONV_HALO - (CONV_WIDTH - 1)
    pieces = []
    for blk in range(ts // CONV_ROWS):
        acc = None
        for j in range(CONV_WIDTH):
            a, r = divmod(off + j, SUBLANES)
            lo = a * SUBLANES + blk * CONV_ROWS
            src = zbuf[lo:lo + CONV_ROWS, :] if r == 0 else zsh[r - 1, lo:lo + CONV_ROWS, :]
            term = src * jnp.concatenate([wdw_ref[j]] * (CONV_ROWS // SUBLANES), axis=0)
            acc = term if acc is None else acc + term
        conv = acc + bdw_ref[...]
        mu = jnp.mean(conv, axis=-1, keepdims=True)
        xc = conv - mu
        var = jnp.mean(xc * xc, axis=-1, keepdims=True)
        ln = xc * lax.rsqrt(var + EPS) * lng_ref[...] + lnb_ref[...]
        pieces.append((ln * _sigmoid(ln)).astype(BF16))
    zbuf[0:CONV_HALO, :] = zbuf[ts:ts + CONV_HALO, :]
    y_conv = _bdot(jnp.concatenate(pieces, axis=0), wpw_ref[...]) + bpw_ref[...]
    y_gla = _bdot(og_ref[0], wgo_ref[...])
    merged = gt_ref[0, :, 0:d].astype(F32) * y_conv + gt_ref[0, :, d:2 * d].astype(F32) * y_gla
    y = _bdot(merged.astype(BF16), wout_ref[...])
    h = x_ref[0] + mod_ref[0, 2:3, :] * y
    h_ref[0] = h
    u2 = _rms(h, g2_ref[...]) * (1.0 + mod_ref[0, 4:5, :]) + mod_ref[0, 3:4, :]
    _pack_rows(u2_ref, u2, u2t)
    logit_ref[0] = _dot3(u2, wr_ref[...]) + br_ref[...]


def _merge(z, og, gt, x, mod3, w_dw, b_dw, ln_g, ln_b, w_pw, b_pw, w_go, w_out, g2, w_rg, b_rg, w_re, b_re, ts):
    nb, s, d = x.shape
    npad = LANES - N_GROUPS - N_EXPERTS
    wr = jnp.pad(jnp.concatenate([w_rg, w_re], axis=1), ((0, 0), (0, npad)))
    br = jnp.pad(jnp.concatenate([b_rg, b_re]), (0, npad)).reshape(1, LANES)

    def tok(width):
        return pl.BlockSpec((1, ts, width), lambda b, i: (b, i, 0))

    def const(shape):
        return pl.BlockSpec(shape, lambda b, i: (0,) * len(shape))

    def row(v):
        return v.reshape(1, v.shape[-1])

    return pl.pallas_call(
        _merge_body,
        grid=(nb, s // ts),
        in_specs=[tok(CONV_DIM), tok(V_DIM), tok(2 * d), tok(d),
                  pl.BlockSpec((1, N_MOD, d), lambda b, i: (b, 0, 0)),
                  const((CONV_WIDTH, SUBLANES, CONV_DIM)), const((1, CONV_DIM)), const((1, CONV_DIM)), const((1, CONV_DIM)),
                  const((CONV_DIM, d)), const((1, d)), const((V_DIM, d)), const((d, d)), const((1, d)),
                  const((d, LANES)), const((1, LANES))],
        out_specs=[tok(d), pl.BlockSpec((ts * PACK_ROWS, LANES), lambda b, i: (b * (s // ts) + i, 0)), tok(LANES)],
        out_shape=[jax.ShapeDtypeStruct((nb, s, d), F32),
                   jax.ShapeDtypeStruct((nb * s * PACK_ROWS, LANES), jnp.uint32),
                   jax.ShapeDtypeStruct((nb, s, LANES), F32)],
        scratch_shapes=[pltpu.VMEM((CONV_HALO + ts, CONV_DIM), F32),
                        pltpu.VMEM((SUBLANES - 1, CONV_HALO + ts - SUBLANES, CONV_DIM), F32),
                        pltpu.VMEM((ts * SUBLANES, LANES), F32)],
        compiler_params=_params("arbitrary", "arbitrary"),
        name="merge",
    )(z, og, gt, x, mod3, jnp.broadcast_to(w_dw.reshape(CONV_WIDTH, 1, CONV_DIM), (CONV_WIDTH, SUBLANES, CONV_DIM)), row(b_dw), row(ln_g), row(ln_b),
      w_pw.astype(BF16), row(b_pw), w_go.astype(BF16), w_out.astype(BF16), row(g2), wr, br)


def _route_body(logit_ref, route_ref, routet_ref, cnt_ref, cnt_sc, *, sub):
    @pl.when(pl.program_id(0) == 0)
    def _():
        cnt_sc[...] = jnp.zeros_like(cnt_sc)

    packed, cnt = _route(logit_ref[...], cnt_sc[...], sub)
    route_ref[...] = packed
    routet_ref[...] = packed.T[0:SUBLANES, :]
    cnt_sc[...] = cnt
    cnt_ref[...] = jnp.broadcast_to(cnt, cnt_ref.shape)


def _route_call(logits, tr, sub):
    t = logits.shape[0]
    return pl.pallas_call(
        functools.partial(_route_body, sub=sub),
        grid=(t // tr,),
        in_specs=[pl.BlockSpec((tr, LANES), lambda i: (i, 0))],
        out_specs=[pl.BlockSpec((tr, LANES), lambda i: (i, 0)), pl.BlockSpec((SUBLANES, tr), lambda i: (0, i)),
                   pl.BlockSpec((SUBLANES, LANES), lambda i: (0, 0))],
        out_shape=[jax.ShapeDtypeStruct((t, LANES), F32), jax.ShapeDtypeStruct((SUBLANES, t), F32),
                   jax.ShapeDtypeStruct((SUBLANES, LANES), F32)],
        scratch_shapes=[pltpu.VMEM((1, LANES), F32)],
        compiler_params=_params("arbitrary"),
        name="route",
    )(logits)


def _row_copy(src, i, dst, j, sem):
    return pltpu.make_async_copy(src.at[pl.ds(pl.multiple_of(i, SUBLANES), SUBLANES), :],
                                 dst.at[pl.ds(pl.multiple_of(j, SUBLANES), SUBLANES), :], sem)


def _invert_body(lo_ref, hi_ref, slot_ref, src_ref):
    i = pl.program_id(0)
    ts = slot_ref.shape[-1] // TOP_K

    @pl.when(i == 0)
    def _():
        for e in range(lo_ref.shape[0]):
            lo = lo_ref[e]
            hi = hi_ref[e]

            def clear(p, carry, lo=lo, hi=hi):
                for j in range(CLEAR_UNROLL):
                    src_ref[jnp.minimum(lo + p * CLEAR_UNROLL + j, hi - 1)] = 0
                return carry

            trips = lax.shift_right_logical(hi - lo + (CLEAR_UNROLL - 1), CLEAR_UNROLL.bit_length() - 1)
            lax.fori_loop(0, trips, clear, 0)

    for r in range(ts):
        row = (i * ts + r) * PACK_ROWS
        for k in range(TOP_K):
            src_ref[slot_ref[0, 0, k * ts + r]] = row


def _invert(fill_lo, fill_hi, slots, n_slots):
    nt, _, width = slots.shape
    return pl.pallas_call(
        _invert_body,
        grid_spec=pltpu.PrefetchScalarGridSpec(
            num_scalar_prefetch=2,
            grid=(nt,),
            in_specs=[pl.BlockSpec((1, 1, width), lambda i, lo, hi: (i, 0, 0), memory_space=pltpu.SMEM)],
            out_specs=pl.BlockSpec(memory_space=pltpu.SMEM)),
        out_shape=jax.ShapeDtypeStruct((n_slots,), jnp.int32),
        compiler_params=_params("arbitrary"),
        name="invert",
    )(fill_lo, fill_hi, slots)


def _expert_body(te_ref, nu_ref, src_ref, u2_ref, w1_ref, w3_ref, w2_ref, ys_ref,
                 u2v, xg, xt, w1b, w3b, w2b, sem):
    i = pl.program_id(0)
    tmx = src_ref.shape[-1]

    load = pltpu.make_async_copy(u2_ref, u2v, sem)

    @pl.when(i == 0)
    def _():
        load.start()

    @pl.when(jnp.logical_or(i == 0, te_ref[i] != te_ref[jnp.maximum(i - 1, 0)]))
    def _():
        w1b[...] = w1_ref[0].astype(BF16)
        w3b[...] = w3_ref[0].astype(BF16)
        w2b[...] = w2_ref[0].astype(BF16)

    @pl.when(i == 0)
    def _():
        load.wait()

    @pl.when(i < nu_ref[0])
    def _():
        for r in range(tmx):
            row = pl.multiple_of(src_ref[0, 0, r], PACK_ROWS)
            xg[r * PACK_ROWS:(r + 1) * PACK_ROWS, :] = u2v[pl.ds(row, PACK_ROWS), :]
        x = _unpack_rows(xg, tmx, xt)
        h1 = _bdot(x, w1b[...])
        h3 = _bdot(x, w3b[...])
        hid = (h1 * _sigmoid(h1) * h3).astype(BF16)
        _rows_to_tiles(ys_ref, _bdot(hid, w2b[...]))

    @pl.when(i >= nu_ref[0])
    def _():
        ys_ref[...] = jnp.zeros_like(ys_ref)


def _experts(tile_expert, n_used, src_rows, u2p, w1, w3, w2):
    n_tiles, _, tmx = src_rows.shape
    ne, d, f = w1.shape

    def w_map(i, te, nu):
        return (te[i], 0, 0)

    return pl.pallas_call(
        _expert_body,
        grid_spec=pltpu.PrefetchScalarGridSpec(
            num_scalar_prefetch=2,
            grid=(n_tiles,),
            in_specs=[pl.BlockSpec((1, 1, tmx), lambda i, te, nu: (i, 0, 0), memory_space=pltpu.SMEM),
                      pl.BlockSpec(memory_space=pl.ANY),
                      pl.BlockSpec((1, d, f), w_map), pl.BlockSpec((1, d, f), w_map),
                      pl.BlockSpec((1, f, d), w_map)],
            out_specs=pl.BlockSpec((tmx * SUBLANES, LANES), lambda i, te, nu: (i, 0)),
            scratch_shapes=[pltpu.VMEM(u2p.shape, jnp.uint32),
                            pltpu.VMEM((tmx * PACK_ROWS, LANES), jnp.uint32),
                            pltpu.VMEM((tmx * SUBLANES, LANES), F32),
                            pltpu.VMEM((d, f), BF16), pltpu.VMEM((d, f), BF16), pltpu.VMEM((f, d), BF16),
                            pltpu.SemaphoreType.DMA(())]),
        out_shape=jax.ShapeDtypeStruct((n_tiles * tmx * SUBLANES, LANES), F32),
        compiler_params=_params("arbitrary"),
        name="experts",
    )(tile_expert, n_used, src_rows, u2p, w1, w3, w2)


def _final_body(p_ref, h_ref, route_ref, mod_ref, modf_ref, gf_ref, ys_ref, o_ref, y1_buf, y2_buf, sem):
    ts = h_ref.shape[1]

    for k, buf in enumerate((y1_buf, y2_buf)):
        for r in range(ts):
            _row_copy(ys_ref, p_ref[0, 0, k * ts + r], buf, r * SUBLANES, sem).start(priority=r % 2)
    pltpu.make_async_copy(ys_ref.at[pl.ds(0, ts * SUBLANES), :], y1_buf, sem).wait()
    pltpu.make_async_copy(ys_ref.at[pl.ds(0, ts * SUBLANES), :], y2_buf, sem).wait()

    route = route_ref[0]
    y2 = route[:, 2:3] * _tiles_to_rows(y1_buf, ts) + route[:, 3:4] * _tiles_to_rows(y2_buf, ts)
    h = h_ref[0] + mod_ref[0, 5:6, :] * y2
    o_ref[0] = _rms(h, gf_ref[...]) * (1.0 + modf_ref[0, 1:2, :]) + modf_ref[0, 0:1, :]


def _final(slot_rows, h, route, mod3, modf3, gf, ys, ts):
    nb, s, d = h.shape
    nt = s // ts

    def tok(width):
        return pl.BlockSpec((1, ts, width), lambda b, i: (b, i, 0))

    return pl.pallas_call(
        _final_body,
        grid=(nb, nt),
        in_specs=[pl.BlockSpec((1, 1, TOP_K * ts), lambda b, i: (b * nt + i, 0, 0), memory_space=pltpu.SMEM),
                  tok(d), tok(LANES),
                  pl.BlockSpec((1, N_MOD, d), lambda b, i: (b, 0, 0)),
                  pl.BlockSpec((1, 2, d), lambda b, i: (b, 0, 0)),
                  pl.BlockSpec((1, d), lambda b, i: (0, 0)),
                  pl.BlockSpec(memory_space=pl.ANY)],
        out_specs=tok(d),
        out_shape=jax.ShapeDtypeStruct((nb, s, d), F32),
        scratch_shapes=[pltpu.VMEM((ts * SUBLANES, LANES), F32), pltpu.VMEM((ts * SUBLANES, LANES), F32),
                        pltpu.SemaphoreType.DMA(())],
        compiler_params=_params("arbitrary", "arbitrary"),
        name="final",
    )(slot_rows, h, route, mod3, modf3, gf.reshape(1, d), ys)


def _plan(routet, cnt, ts, tmx, n_tiles):
    t = routet.shape[1]
    counts = cnt[0, :N_EXPERTS].astype(jnp.int32)
    tiles = (counts + (tmx - 1)) // tmx
    tile_end = jnp.cumsum(tiles)
    offs = ((tile_end - tiles) * tmx).astype(jnp.int32)
    n_used = tile_end[-1:]
    tile_ids = jnp.minimum(jnp.arange(n_tiles, dtype=jnp.int32), n_used[0] - 1)
    tile_expert = jnp.sum((tile_ids[:, None] >= tile_end[None, :]).astype(jnp.int32), axis=1)

    eid = routet[0:2].astype(jnp.int32)
    experts = jnp.arange(N_EXPERTS, dtype=jnp.int32)[:, None, None]
    slot = routet[4:6].astype(jnp.int32) + jnp.sum(jnp.where(eid[None] == experts, offs[:, None, None], 0), axis=0)
    slots = slot.reshape(TOP_K, t // ts, ts).transpose(1, 0, 2).reshape(t // ts, 1, TOP_K * ts)
    fill_lo = jnp.concatenate([offs + counts, tile_end[-1:] * tmx]).astype(jnp.int32)
    fill_hi = jnp.concatenate([tile_end * tmx, jnp.full((1,), n_tiles * tmx, jnp.int32)]).astype(jnp.int32)
    src = _invert(fill_lo, fill_hi, slots, n_tiles * tmx)
    return slots * SUBLANES, src.reshape(n_tiles, 1, tmx), tile_expert.astype(jnp.int32), n_used.astype(jnp.int32)


def kernel(x, c, w_ada, b_ada, g_norm1, w_in, w_dw, b_dw, g_conv_ln, b_conv_ln, w_conv_pw, b_conv_pw,
           w_a2, b_a2, g_gla_norm, w_gla_o, w_out, g_norm2, w_router_g, b_router_g, w_router_e,
           b_router_e, w1, w3, w2, w_ada_f, b_ada_f, g_final):
    nb, s, d = x.shape
    assert w_ada.shape[0] == 1, "single-layer block"
    assert d == 2 * PACK_ROWS * LANES, "packed token rows assume D_MODEL = 1024"
    tm = min(512, s)
    tc = min(256, s)
    ts = min(256, s)
    tmx = 256
    t = nb * s
    n_tiles = (t * TOP_K) // tmx + N_EXPERTS
    mod3 = _ada(c, w_ada[0], b_ada[0]).reshape(nb, N_MOD, d)
    modf3 = _ada(c, w_ada_f, b_ada_f).reshape(nb, 2, d)
    z, q, k, v, rs, lg, gt = _proj(x, mod3, g_norm1[0], w_in[0], w_a2[0], b_a2[0], tm)
    og = _gla(q, k, lg, v, rs, g_gla_norm[0], tc)
    h, u2, logits = _merge(z, og, gt, x, mod3, w_dw[0], b_dw[0], g_conv_ln[0], b_conv_ln[0],
                           w_conv_pw[0], b_conv_pw[0], w_gla_o[0], w_out[0], g_norm2[0],
                           w_router_g[0], b_router_g[0], w_router_e[0], b_router_e[0], ts)
    route, routet, cnt = _route_call(logits.reshape(t, LANES), min(ROUTE_ROWS, t), ts)
    slot_rows, src_rows, tile_expert, n_used = _plan(routet, cnt, ts, tmx, n_tiles)
    ys = _experts(tile_expert, n_used, src_rows, u2, w1[0], w3[0], w2[0])
    return _final(slot_rows, h, route.reshape(nb, s, LANES), mod3, modf3, g_final, ys, ts)
```

```python
import functools

import jax
import jax.numpy as jnp
from jax import lax
from jax.experimental import pallas as pl
from jax.experimental.pallas import tpu as pltpu

F32 = jnp.float32
BF16 = jnp.bfloat16

EPS = 1e-6
CONV_DIM = 512
CONV_WIDTH = 31
GLA_HEADS = 4
GLA_DK = 128
GLA_DV = 256
GLA_LOWRANK = 16
GLA_TAU = 16.0
QK_DIM = GLA_HEADS * GLA_DK
V_DIM = GLA_HEADS * GLA_DV
N_GROUPS = 4
EXPERTS_PER_GROUP = 8
N_EXPERTS = N_GROUPS * EXPERTS_PER_GROUP
TOP_K = 2
N_MOD = 6

LANES = 128
SUBLANES = 8
PACK_ROWS = 4
CONV_ROWS = 32
ROUTE_ROWS = 2048
CLEAR_UNROLL = 16
CONV_HALO = 32
GLA_CHUNK = 128
GLA_FACTOR_RANGE = 60.0
VMEM_LIMIT = 56 * 1024 * 1024


def _bdot(a, b):
    return jnp.dot(a, b, preferred_element_type=F32)


def _split(a):
    hi = a.astype(BF16)
    lo = (a - hi.astype(F32)).astype(BF16)
    return hi, lo


def _dot3(a, b):
    ah, al = _split(a)
    bh, bl = _split(b)
    return _bdot(ah, bh) + (_bdot(ah, bl) + _bdot(al, bh))


def _sigmoid(x):
    return 1.0 / (1.0 + jnp.exp(-x))


def _rms(x, g):
    ms = jnp.mean(x * x, axis=-1, keepdims=True)
    return x * lax.rsqrt(ms + EPS) * g


def _rows_to_tiles(ref, val):
    n = val.shape[0]
    for j in range(val.shape[1] // LANES):
        ref[pl.ds(j, n, stride=SUBLANES), :] = val[:, j * LANES:(j + 1) * LANES]


def _tiles_to_rows(ref, n):
    return jnp.concatenate([ref[pl.ds(j, n, stride=SUBLANES), :] for j in range(SUBLANES)], axis=-1)


def _pack_rows(ref, val, tiles):
    _rows_to_tiles(tiles, val)
    ref[...] = pltpu.bitcast(tiles[...].astype(BF16), jnp.uint32)


def _unpack_rows(ref, n, tiles):
    tiles[...] = pltpu.bitcast(ref[...], BF16).astype(F32)
    return _tiles_to_rows(tiles, n).astype(BF16)


def _params(*sem):
    return pltpu.CompilerParams(dimension_semantics=sem, vmem_limit_bytes=VMEM_LIMIT)


def _ada_body(c_ref, w_ref, b_ref, o_ref):
    c = c_ref[...]
    o_ref[...] = _dot3(c * _sigmoid(c), w_ref[...]) + b_ref[...]


def _ada(c, w, b, tn=1024):
    nb, d = c.shape
    n = w.shape[1]
    return pl.pallas_call(
        _ada_body,
        grid=(n // tn,),
        in_specs=[pl.BlockSpec((nb, d), lambda j: (0, 0)),
                  pl.BlockSpec((d, tn), lambda j: (0, j)),
                  pl.BlockSpec((1, tn), lambda j: (0, j))],
        out_specs=pl.BlockSpec((nb, tn), lambda j: (0, j)),
        out_shape=jax.ShapeDtypeStruct((nb, n), F32),
        compiler_params=_params("arbitrary"),
        name="ada",
    )(c, w, b.reshape(1, n))


def _proj_body(x_ref, mod_ref, g1_ref, wm_ref, wa1_ref, wg_ref, wa2_ref, ba2_ref,
               z_ref, q_ref, k_ref, v_ref, rs_ref, lg_ref, gt_ref):
    x = x_ref[0]
    u = (_rms(x, g1_ref[...]) * (1.0 + mod_ref[0, 1:2, :]) + mod_ref[0, 0:1, :]).astype(BF16)
    c0 = 2 * CONV_DIM
    c1 = c0 + 2 * QK_DIM
    c2 = c1 + V_DIM
    c3 = c2 + V_DIM
    pc = _bdot(u, wm_ref[:, 0:c0])
    z_ref[0] = pc[:, :CONV_DIM] * _sigmoid(pc[:, CONV_DIM:])
    qk = _bdot(u, wm_ref[:, c0:c1])
    q_ref[0] = qk[:, :QK_DIM] * (GLA_DK ** -0.5)
    k_ref[0] = qk[:, QK_DIM:]
    v_ref[0] = _bdot(u, wm_ref[:, c1:c2]).astype(BF16)
    r = _bdot(u, wm_ref[:, c2:c3])
    rs_ref[0] = (r * _sigmoid(r)).astype(BF16)
    a1 = _bdot(u, wa1_ref[...])
    xg = _dot3(a1, wa2_ref[...]) + ba2_ref[...]
    lg_ref[0] = (jnp.minimum(xg, 0.0) - jnp.log1p(jnp.exp(-jnp.abs(xg)))) * (1.0 / GLA_TAU)
    gt_ref[0] = _sigmoid(_bdot(u, wg_ref[...])).astype(BF16)


def _wsplit_body(wt_ref, wm_ref, wa1_ref, wg_ref, buf, sem, *, c3):
    cw = buf.shape[0]
    n_main = c3 // cw
    j = pl.program_id(0)
    start = jnp.where(j <= n_main, j * cw, c3 + GLA_LOWRANK + (j - n_main - 1) * cw)
    chunk = pltpu.make_async_copy(wt_ref.at[pl.ds(pl.multiple_of(start, SUBLANES), cw), :], buf, sem)
    chunk.start()
    chunk.wait()

    @pl.when(j < n_main)
    def _():
        wm_ref[...] = buf[...].T.astype(BF16)

    @pl.when(j == n_main)
    def _():
        lane = lax.broadcasted_iota(jnp.int32, wa1_ref.shape, 1)
        wa1_ref[...] = jnp.where(lane < GLA_LOWRANK, buf[0:LANES, :].T, 0.0).astype(BF16)

    @pl.when(j > n_main)
    def _():
        wg_ref[...] = buf[...].T.astype(BF16)


def _wsplit(wt, c3, cw=512):
    n, d = wt.shape
    ng = n - c3 - GLA_LOWRANK
    n_main = c3 // cw
    return pl.pallas_call(
        functools.partial(_wsplit_body, c3=c3),
        grid=(n_main + 1 + ng // cw,),
        in_specs=[pl.BlockSpec(memory_space=pl.ANY)],
        out_specs=[pl.BlockSpec((d, cw), lambda j: (0, jnp.minimum(j, n_main - 1))),
                   pl.BlockSpec((d, LANES), lambda j: (0, 0)),
                   pl.BlockSpec((d, cw), lambda j: (0, jnp.clip(j - n_main - 1, 0, ng // cw - 1)))],
        out_shape=[jax.ShapeDtypeStruct((d, c3), BF16), jax.ShapeDtypeStruct((d, LANES), BF16),
                   jax.ShapeDtypeStruct((d, ng), BF16)],
        scratch_shapes=[pltpu.VMEM((cw, d), F32), pltpu.SemaphoreType.DMA(())],
        compiler_params=_params("arbitrary"),
        name="wsplit",
    )(wt)


def _proj(x, mod3, g1, w_in, w_a2, b_a2, tm):
    nb, s, d = x.shape
    c3 = 2 * CONV_DIM + 2 * QK_DIM + 2 * V_DIM
    wm, wa1, wg = _wsplit(jnp.swapaxes(w_in, 0, 1), c3)
    wa2 = jnp.pad(w_a2, ((0, LANES - GLA_LOWRANK), (0, 0)))
    ng = wg.shape[1]

    def tok(width):
        return pl.BlockSpec((1, tm, width), lambda b, i: (b, i, 0))

    def const(shape):
        return pl.BlockSpec(shape, lambda b, i: (0,) * len(shape))

    def out(width, dt):
        return jax.ShapeDtypeStruct((nb, s, width), dt)

    return pl.pallas_call(
        _proj_body,
        grid=(nb, s // tm),
        in_specs=[tok(d),
                  pl.BlockSpec((1, N_MOD, d), lambda b, i: (b, 0, 0)),
                  const((1, d)), const(wm.shape), const(wa1.shape), const(wg.shape),
                  const(wa2.shape), const((1, QK_DIM))],
        out_specs=[tok(CONV_DIM), tok(QK_DIM), tok(QK_DIM), tok(V_DIM), tok(V_DIM), tok(QK_DIM), tok(ng)],
        out_shape=[out(CONV_DIM, F32), out(QK_DIM, F32), out(QK_DIM, F32), out(V_DIM, BF16),
                   out(V_DIM, BF16), out(QK_DIM, F32), out(ng, BF16)],
        compiler_params=_params("arbitrary", "arbitrary"),
        name="proj",
    )(x, mod3, g1.reshape(1, d), wm, wa1, wg, wa2, b_a2.reshape(1, QK_DIM))


def _gla_body(q_ref, k_ref, lg_ref, v_ref, rs_ref, gn_ref, o_ref, st_ref, b_s, oi_s, *, n_chunks):
    cl = GLA_CHUNK

    @pl.when(pl.program_id(1) == 0)
    def _():
        st_ref[...] = jnp.zeros_like(st_ref)

    row = lax.broadcasted_iota(jnp.int32, (cl, cl), 0)
    col = lax.broadcasted_iota(jnp.int32, (cl, cl), 1)
    causal = col <= row
    tri = jnp.where(causal, 1.0, 0.0).astype(BF16)
    chunks = [slice(c * cl, (c + 1) * cl) for c in range(n_chunks)]
    heads = range(GLA_HEADS)
    ksl = [slice(h * GLA_DK, (h + 1) * GLA_DK) for h in heads]
    vsl = [slice(h * GLA_DV, (h + 1) * GLA_DV) for h in heads]

    lowest = None
    for rows in chunks:
        gh, gl = _split(lg_ref[0, rows, :])
        b = _bdot(tri, gh) + _bdot(tri, gl)
        b_s[rows, :] = b
        low = jnp.min(b[cl - 1:cl, :])
        lowest = low if lowest is None else jnp.minimum(lowest, low)

    def intra_pairwise(h, c):
        kf = k_ref[0, chunks[c], ksl[h]]
        vf = v_ref[0, chunks[c], vsl[h]].astype(F32)
        bh = b_s[chunks[c], ksl[h]]
        key = lax.broadcasted_iota(jnp.int32, (cl, 1), 0)

        def group(g, carry):
            r0 = pl.multiple_of(g * SUBLANES, SUBLANES)
            q8 = q_ref[0, pl.ds(c * cl + r0, SUBLANES), ksl[h]]
            b8 = b_s[pl.ds(c * cl + r0, SUBLANES), ksl[h]]
            out_rows = []
            for r in range(SUBLANES):
                diff = jnp.where(key <= r0 + r, b8[r:r + 1, :] - bh, -jnp.inf)
                att = jnp.sum(jnp.exp(diff) * kf * q8[r:r + 1, :], axis=-1, keepdims=True)
                out_rows.append(jnp.sum(att * vf, axis=0, keepdims=True))
            oi_s[h, pl.ds(c * cl + r0, SUBLANES), :] = jnp.concatenate(out_rows, axis=0)
            return carry

        lax.fori_loop(0, cl // SUBLANES, group, 0)
        return oi_s[h, chunks[c], :]

    def run(factored):
        cums = [b_s[rows, :] for rows in chunks]
        qes, kts, bts, klts, decays = [], [], [], [], []
        for rows, b in zip(chunks, cums):
            qes.append((q_ref[0, rows, :] * jnp.exp(b)).astype(BF16))
            kt = k_ref[0, rows, :].T
            bt = b.T
            bl = bt[:, cl - 1:cl]
            kts.append(kt)
            bts.append(bt)
            klts.append((kt * jnp.exp(bl - bt)).astype(BF16))
            decays.append(jnp.exp(bl))
        intra = {}
        if factored:
            kets = [(kt * jnp.exp(-bt)).astype(BF16) for kt, bt in zip(kts, bts)]
            atts = {}
            for c in range(n_chunks):
                for h in heads:
                    att = _bdot(qes[c][:, ksl[h]], kets[c][ksl[h], :])
                    atts[h, c] = jnp.where(causal, att, 0.0).astype(BF16)
            for c, rows in enumerate(chunks):
                for h in heads:
                    intra[h, c] = _bdot(atts[h, c], v_ref[0, rows, vsl[h]])
        else:
            for c in range(n_chunks):
                for h in heads:
                    intra[h, c] = intra_pairwise(h, c)
        updates = {}
        for c, rows in enumerate(chunks):
            for h in heads:
                updates[h, c] = _bdot(klts[c][ksl[h], :], v_ref[0, rows, vsl[h]])

        outs = {}
        states = []
        for h in heads:
            state = st_ref[h]
            for c, rows in enumerate(chunks):
                o = intra[h, c] + _bdot(qes[c][:, ksl[h]], state.astype(BF16))
                state = decays[c][ksl[h], :] * state + updates[h, c]
                outs[h, c] = (_rms(o, gn_ref[:, vsl[h]]) * rs_ref[0, rows, vsl[h]].astype(F32)).astype(BF16)
            states.append(state)
        for c, rows in enumerate(chunks):
            o_ref[0, rows, :] = jnp.concatenate([outs[h, c] for h in heads], axis=-1)
        for h in heads:
            st_ref[h] = states[h]

    in_range = lowest > -GLA_FACTOR_RANGE
    pl.when(in_range)(functools.partial(run, True))
    pl.when(jnp.logical_not(in_range))(functools.partial(run, False))


def _gla(q, k, lg, v, rs, gn, tc):
    nb, s, _ = q.shape

    def tok(width):
        return pl.BlockSpec((1, tc, width), lambda b, i: (b, i, 0))

    return pl.pallas_call(
        functools.partial(_gla_body, n_chunks=tc // GLA_CHUNK),
        grid=(nb, s // tc),
        in_specs=[tok(QK_DIM), tok(QK_DIM), tok(QK_DIM), tok(V_DIM), tok(V_DIM),
                  pl.BlockSpec((1, V_DIM), lambda b, i: (0, 0))],
        out_specs=tok(V_DIM),
        out_shape=jax.ShapeDtypeStruct((nb, s, V_DIM), BF16),
        scratch_shapes=[pltpu.VMEM((GLA_HEADS, GLA_DK, GLA_DV), F32),
                        pltpu.VMEM((tc, QK_DIM), F32),
                        pltpu.VMEM((GLA_HEADS, tc, GLA_DV), F32)],
        compiler_params=_params("arbitrary", "arbitrary"),
        name="gla",
    )(q, k, lg, v, rs, gn.reshape(1, V_DIM))


def _route(logits, cnt, sub):
    ts = logits.shape[0]
    lane = lax.broadcasted_iota(jnp.int32, (ts, LANES), 1).astype(F32)
    ninf = -jnp.inf
    lgm = jnp.where(lane < N_GROUPS, logits, ninf)
    gmax = jnp.max(lgm, axis=-1, keepdims=True)
    gsel = jnp.min(jnp.where(lgm == gmax, lane, float(LANES)), axis=-1, keepdims=True)
    wg = 1.0 / jnp.sum(jnp.exp(lgm - gmax), axis=-1, keepdims=True)
    base = N_GROUPS + EXPERTS_PER_GROUP * gsel
    le = jnp.where(lane >= base, jnp.where(lane < base + EXPERTS_PER_GROUP, logits, ninf), ninf)
    v1 = jnp.max(le, axis=-1, keepdims=True)
    i1 = jnp.min(jnp.where(le == v1, lane, float(LANES)), axis=-1, keepdims=True)
    le2 = jnp.where(lane == i1, ninf, le)
    v2 = jnp.max(le2, axis=-1, keepdims=True)
    i2 = jnp.min(jnp.where(le2 == v2, lane, float(LANES)), axis=-1, keepdims=True)
    e21 = jnp.exp(v2 - v1)
    w1 = wg / (1.0 + e21)
    w2 = w1 * e21
    eid1 = i1 - N_GROUPS
    eid2 = i2 - N_GROUPS
    oh1 = jnp.where(lane == eid1, 1.0, 0.0)
    oh2 = jnp.where(lane == eid2, 1.0, 0.0)
    ohs = oh1 + oh2
    row = lax.broadcasted_iota(jnp.int32, (sub, sub), 0)
    col = lax.broadcasted_iota(jnp.int32, (sub, sub), 1)
    before = jnp.where(col < row, 1.0, 0.0).astype(BF16)
    tots = []
    for lo in range(0, ts, sub):
        piece = ohs[lo:lo + sub, :]
        tots.append(cnt + _bdot(before, piece.astype(BF16)))
        cnt = cnt + jnp.sum(piece, axis=0, keepdims=True)
    tot = jnp.concatenate(tots, axis=0)
    rank1 = jnp.sum(oh1 * tot, axis=-1, keepdims=True)
    rank2 = jnp.sum(oh2 * tot, axis=-1, keepdims=True)
    packed = jnp.where(lane == 0.0, eid1,
             jnp.where(lane == 1.0, eid2,
             jnp.where(lane == 2.0, w1,
             jnp.where(lane == 3.0, w2,
             jnp.where(lane == 4.0, rank1,
             jnp.where(lane == 5.0, rank2, 0.0))))))
    return packed, cnt


def _merge_body(z_ref, og_ref, gt_ref, x_ref, mod_ref, wdw_ref, bdw_ref, lng_ref, lnb_ref,
                wpw_ref, bpw_ref, wgo_ref, wout_ref, g2_ref, wr_ref, br_ref,
                h_ref, u2_ref, logit_ref, zbuf, zsh, u2t):
    ts = z_ref.shape[1]
    d = x_ref.shape[2]
    first_tile = pl.program_id(1) == 0

    @pl.when(first_tile)
    def _():
        zbuf[0:CONV_HALO, :] = jnp.zeros((CONV_HALO, CONV_DIM), F32)

    zbuf[CONV_HALO:CONV_HALO + ts, :] = z_ref[0]
    span = ts + CONV_HALO - SUBLANES
    for r in range(1, SUBLANES):
        zsh[r - 1] = zbuf[r:r + span, :]
    off = CONV_HALO - (CONV_WIDTH - 1)
    pieces = []
    for blk in range(ts // CONV_ROWS):
        acc = None
        for j in range(CONV_WIDTH):
            a, r = divmod(off + j, SUBLANES)
            lo = a * SUBLANES + blk * CONV_ROWS
            src = zbuf[lo:lo + CONV_ROWS, :] if r == 0 else zsh[r - 1, lo:lo + CONV_ROWS, :]
            term = src * jnp.concatenate([wdw_ref[j]] * (CONV_ROWS // SUBLANES), axis=0)
            acc = term if acc is None else acc + term
        conv = acc + bdw_ref[...]
        mu = jnp.mean(conv, axis=-1, keepdims=True)
        xc = conv - mu
        var = jnp.mean(xc * xc, axis=-1, keepdims=True)
        ln = xc * lax.rsqrt(var + EPS) * lng_ref[...] + lnb_ref[...]
        pieces.append((ln * _sigmoid(ln)).astype(BF16))
    zbuf[0:CONV_HALO, :] = zbuf[ts:ts + CONV_HALO, :]
    y_conv = _bdot(jnp.concatenate(pieces, axis=0), wpw_ref[...]) + bpw_ref[...]
    y_gla = _bdot(og_ref[0], wgo_ref[...])
    merged = gt_ref[0, :, 0:d].astype(F32) * y_conv + gt_ref[0, :, d:2 * d].astype(F32) * y_gla
    y = _bdot(merged.astype(BF16), wout_ref[...])
    h = x_ref[0] + mod_ref[0, 2:3, :] * y
    h_ref[0] = h
    u2 = _rms(h, g2_ref[...]) * (1.0 + mod_ref[0, 4:5, :]) + mod_ref[0, 3:4, :]
    _pack_rows(u2_ref, u2, u2t)
    logit_ref[0] = _dot3(u2, wr_ref[...]) + br_ref[...]


def _merge(z, og, gt, x, mod3, w_dw, b_dw, ln_g, ln_b, w_pw, b_pw, w_go, w_out, g2, w_rg, b_rg, w_re, b_re, ts):
    nb, s, d = x.shape
    npad = LANES - N_GROUPS - N_EXPERTS
    wr = jnp.pad(jnp.concatenate([w_rg, w_re], axis=1), ((0, 0), (0, npad)))
    br = jnp.pad(jnp.concatenate([b_rg, b_re]), (0, npad)).reshape(1, LANES)

    def tok(width):
        return pl.BlockSpec((1, ts, width), lambda b, i: (b, i, 0))

    def const(shape):
        return pl.BlockSpec(shape, lambda b, i: (0,) * len(shape))

    def row(v):
        return v.reshape(1, v.shape[-1])

    return pl.pallas_call(
        _merge_body,
        grid=(nb, s // ts),
        in_specs=[tok(CONV_DIM), tok(V_DIM), tok(2 * d), tok(d),
                  pl.BlockSpec((1, N_MOD, d), lambda b, i: (b, 0, 0)),
                  const((CONV_WIDTH, SUBLANES, CONV_DIM)), const((1, CONV_DIM)), const((1, CONV_DIM)), const((1, CONV_DIM)),
                  const((CONV_DIM, d)), const((1, d)), const((V_DIM, d)), const((d, d)), const((1, d)),
                  const((d, LANES)), const((1, LANES))],
        out_specs=[tok(d), pl.BlockSpec((ts * PACK_ROWS, LANES), lambda b, i: (b * (s // ts) + i, 0)), tok(LANES)],
        out_shape=[jax.ShapeDtypeStruct((nb, s, d), F32),
                   jax.ShapeDtypeStruct((nb * s * PACK_ROWS, LANES), jnp.uint32),
                   jax.ShapeDtypeStruct((nb, s, LANES), F32)],
        scratch_shapes=[pltpu.VMEM((CONV_HALO + ts, CONV_DIM), F32),
                        pltpu.VMEM((SUBLANES - 1, CONV_HALO + ts - SUBLANES, CONV_DIM), F32),
                        pltpu.VMEM((ts * SUBLANES, LANES), F32)],
        compiler_params=_params("arbitrary", "arbitrary"),
        name="merge",
    )(z, og, gt, x, mod3, jnp.broadcast_to(w_dw.reshape(CONV_WIDTH, 1, CONV_DIM), (CONV_WIDTH, SUBLANES, CONV_DIM)), row(b_dw), row(ln_g), row(ln_b),
      w_pw.astype(BF16), row(b_pw), w_go.astype(BF16), w_out.astype(BF16), row(g2), wr, br)


def _route_body(logit_ref, route_ref, routet_ref, cnt_ref, cnt_sc, *, sub):
    @pl.when(pl.program_id(0) == 0)
    def _():
        cnt_sc[...] = jnp.zeros_like(cnt_sc)

    packed, cnt = _route(logit_ref[...], cnt_sc[...], sub)
    route_ref[...] = packed
    routet_ref[...] = packed.T[0:SUBLANES, :]
    cnt_sc[...] = cnt
    cnt_ref[...] = jnp.broadcast_to(cnt, cnt_ref.shape)


def _route_call(logits, tr, sub):
    t = logits.shape[0]
    return pl.pallas_call(
        functools.partial(_route_body, sub=sub),
        grid=(t // tr,),
        in_specs=[pl.BlockSpec((tr, LANES), lambda i: (i, 0))],
        out_specs=[pl.BlockSpec((tr, LANES), lambda i: (i, 0)), pl.BlockSpec((SUBLANES, tr), lambda i: (0, i)),
                   pl.BlockSpec((SUBLANES, LANES), lambda i: (0, 0))],
        out_shape=[jax.ShapeDtypeStruct((t, LANES), F32), jax.ShapeDtypeStruct((SUBLANES, t), F32),
                   jax.ShapeDtypeStruct((SUBLANES, LANES), F32)],
        scratch_shapes=[pltpu.VMEM((1, LANES), F32)],
        compiler_params=_params("arbitrary"),
        name="route",
    )(logits)


def _row_copy(src, i, dst, j, sem):
    return pltpu.make_async_copy(src.at[pl.ds(pl.multiple_of(i, SUBLANES), SUBLANES), :],
                                 dst.at[pl.ds(pl.multiple_of(j, SUBLANES), SUBLANES), :], sem)


def _invert_body(lo_ref, hi_ref, slot_ref, src_ref):
    i = pl.program_id(0)
    ts = slot_ref.shape[-1] // TOP_K

    @pl.when(i == 0)
    def _():
        for e in range(lo_ref.shape[0]):
            lo = lo_ref[e]
            hi = hi_ref[e]

            def clear(p, carry, lo=lo, hi=hi):
                for j in range(CLEAR_UNROLL):
                    src_ref[jnp.minimum(lo + p * CLEAR_UNROLL + j, hi - 1)] = 0
                return carry

            trips = lax.shift_right_logical(hi - lo + (CLEAR_UNROLL - 1), CLEAR_UNROLL.bit_length() - 1)
            lax.fori_loop(0, trips, clear, 0)

    for r in range(ts):
        row = (i * ts + r) * PACK_ROWS
        for k in range(TOP_K):
            src_ref[slot_ref[0, 0, k * ts + r]] = row


def _invert(fill_lo, fill_hi, slots, n_slots):
    nt, _, width = slots.shape
    return pl.pallas_call(
        _invert_body,
        grid_spec=pltpu.PrefetchScalarGridSpec(
            num_scalar_prefetch=2,
            grid=(nt,),
            in_specs=[pl.BlockSpec((1, 1, width), lambda i, lo, hi: (i, 0, 0), memory_space=pltpu.SMEM)],
            out_specs=pl.BlockSpec(memory_space=pltpu.SMEM)),
        out_shape=jax.ShapeDtypeStruct((n_slots,), jnp.int32),
        compiler_params=_params("arbitrary"),
        name="invert",
    )(fill_lo, fill_hi, slots)


def _expert_body(te_ref, nu_ref, seg_ref, nxt_ref, src_ref, u2_ref, w1_ref, w3_ref, w2_ref, ys_ref,
                 u2v, xg, xt, w1f, w3f, w2f, w1b, w3b, w2b, sem, wsem):
    i = pl.program_id(0)
    tmx = src_ref.shape[-1]
    expert = te_ref[i]
    first = i == 0
    changed = jnp.logical_or(first, expert != te_ref[jnp.maximum(i - 1, 0)])
    slot = lax.rem(seg_ref[i], 2)

    def weight_copies(e, s):
        return [pltpu.make_async_copy(src.at[e], dst.at[s], wsem.at[s])
                for src, dst in ((w1_ref, w1f), (w3_ref, w3f), (w2_ref, w2f))]

    load = pltpu.make_async_copy(u2_ref, u2v, sem)

    @pl.when(first)
    def _():
        load.start()
        for cp in weight_copies(expert, 0):
            cp.start()

    @pl.when(changed)
    def _():
        for cp in weight_copies(expert, slot):
            cp.wait()
        w1b[...] = w1f[slot].astype(BF16)
        w3b[...] = w3f[slot].astype(BF16)
        w2b[...] = w2f[slot].astype(BF16)

    @pl.when(jnp.logical_and(changed, nxt_ref[i] != expert))
    def _():
        for cp in weight_copies(nxt_ref[i], 1 - slot):
            cp.start()

    @pl.when(first)
    def _():
        load.wait()

    @pl.when(i < nu_ref[0])
    def _():
        for r in range(tmx):
            row = pl.multiple_of(src_ref[0, 0, r], PACK_ROWS)
            xg[r * PACK_ROWS:(r + 1) * PACK_ROWS, :] = u2v[pl.ds(row, PACK_ROWS), :]
        x = _unpack_rows(xg, tmx, xt)
        h1 = _bdot(x, w1b[...])
        h3 = _bdot(x, w3b[...])
        hid = (h1 * _sigmoid(h1) * h3).astype(BF16)
        _rows_to_tiles(ys_ref, _bdot(hid, w2b[...]))

    @pl.when(i >= nu_ref[0])
    def _():
        ys_ref[...] = jnp.zeros_like(ys_ref)


def _experts(tile_expert, n_used, run_index, next_expert, src_rows, u2p, w1, w3, w2):
    n_tiles, _, tmx = src_rows.shape
    ne, d, f = w1.shape
    hbm = pl.BlockSpec(memory_space=pl.ANY)
    return pl.pallas_call(
        _expert_body,
        grid_spec=pltpu.PrefetchScalarGridSpec(
            num_scalar_prefetch=4,
            grid=(n_tiles,),
            in_specs=[pl.BlockSpec((1, 1, tmx), lambda i, *_: (i, 0, 0), memory_space=pltpu.SMEM),
                      hbm, hbm, hbm, hbm],
            out_specs=pl.BlockSpec((tmx * SUBLANES, LANES), lambda i, *_: (i, 0)),
            scratch_shapes=[pltpu.VMEM(u2p.shape, jnp.uint32),
                            pltpu.VMEM((tmx * PACK_ROWS, LANES), jnp.uint32),
                            pltpu.VMEM((tmx * SUBLANES, LANES), F32),
                            pltpu.VMEM((2, d, f), F32), pltpu.VMEM((2, d, f), F32), pltpu.VMEM((2, f, d), F32),
                            pltpu.VMEM((d, f), BF16), pltpu.VMEM((d, f), BF16), pltpu.VMEM((f, d), BF16),
                            pltpu.SemaphoreType.DMA(()), pltpu.SemaphoreType.DMA((2,))]),
        out_shape=jax.ShapeDtypeStruct((n_tiles * tmx * SUBLANES, LANES), F32),
        compiler_params=_params("arbitrary"),
        name="experts",
    )(tile_expert, n_used, run_index, next_expert, src_rows, u2p, w1, w3, w2)


def _final_body(p_ref, h_ref, route_ref, mod_ref, modf_ref, gf_ref, ys_ref, o_ref, y1_buf, y2_buf, sem):
    ts = h_ref.shape[1]

    for k, buf in enumerate((y1_buf, y2_buf)):
        for r in range(ts):
            _row_copy(ys_ref, p_ref[0, 0, k * ts + r], buf, r * SUBLANES, sem).start(priority=r % 2)
    pltpu.make_async_copy(ys_ref.at[pl.ds(0, ts * SUBLANES), :], y1_buf, sem).wait()
    pltpu.make_async_copy(ys_ref.at[pl.ds(0, ts * SUBLANES), :], y2_buf, sem).wait()

    route = route_ref[0]
    y2 = route[:, 2:3] * _tiles_to_rows(y1_buf, ts) + route[:, 3:4] * _tiles_to_rows(y2_buf, ts)
    h = h_ref[0] + mod_ref[0, 5:6, :] * y2
    o_ref[0] = _rms(h, gf_ref[...]) * (1.0 + modf_ref[0, 1:2, :]) + modf_ref[0, 0:1, :]


def _final(slot_rows, h, route, mod3, modf3, gf, ys, ts):
    nb, s, d = h.shape
    nt = s // ts

    def tok(width):
        return pl.BlockSpec((1, ts, width), lambda b, i: (b, i, 0))

    return pl.pallas_call(
        _final_body,
        grid=(nb, nt),
        in_specs=[pl.BlockSpec((1, 1, TOP_K * ts), lambda b, i: (b * nt + i, 0, 0), memory_space=pltpu.SMEM),
                  tok(d), tok(LANES),
                  pl.BlockSpec((1, N_MOD, d), lambda b, i: (b, 0, 0)),
                  pl.BlockSpec((1, 2, d), lambda b, i: (b, 0, 0)),
                  pl.BlockSpec((1, d), lambda b, i: (0, 0)),
                  pl.BlockSpec(memory_space=pl.ANY)],
        out_specs=tok(d),
        out_shape=jax.ShapeDtypeStruct((nb, s, d), F32),
        scratch_shapes=[pltpu.VMEM((ts * SUBLANES, LANES), F32), pltpu.VMEM((ts * SUBLANES, LANES), F32),
                        pltpu.SemaphoreType.DMA(())],
        compiler_params=_params("arbitrary", "arbitrary"),
        name="final",
    )(slot_rows, h, route, mod3, modf3, gf.reshape(1, d), ys)


def _plan(routet, cnt, ts, tmx, n_tiles):
    t = routet.shape[1]
    counts = cnt[0, :N_EXPERTS].astype(jnp.int32)
    tiles = (counts + (tmx - 1)) // tmx
    tile_end = jnp.cumsum(tiles)
    offs = ((tile_end - tiles) * tmx).astype(jnp.int32)
    n_used = tile_end[-1:]
    tile_ids = jnp.minimum(jnp.arange(n_tiles, dtype=jnp.int32), n_used[0] - 1)
    tile_expert = jnp.sum((tile_ids[:, None] >= tile_end[None, :]).astype(jnp.int32), axis=1)
    run_index = jnp.cumsum(jnp.concatenate([jnp.zeros((1,), jnp.int32),
                                            (tile_expert[1:] != tile_expert[:-1]).astype(jnp.int32)]))
    same_run = run_index[:, None] + 1 == run_index[None, :]
    has_next = jnp.any(same_run, axis=1)
    next_expert = jnp.where(has_next, jnp.max(jnp.where(same_run, tile_expert[None, :], 0), axis=1), tile_expert)

    eid = routet[0:2].astype(jnp.int32)
    experts = jnp.arange(N_EXPERTS, dtype=jnp.int32)[:, None, None]
    slot = routet[4:6].astype(jnp.int32) + jnp.sum(jnp.where(eid[None] == experts, offs[:, None, None], 0), axis=0)
    slots = slot.reshape(TOP_K, t // ts, ts).transpose(1, 0, 2).reshape(t // ts, 1, TOP_K * ts)
    fill_lo = jnp.concatenate([offs + counts, tile_end[-1:] * tmx]).astype(jnp.int32)
    fill_hi = jnp.concatenate([tile_end * tmx, jnp.full((1,), n_tiles * tmx, jnp.int32)]).astype(jnp.int32)
    src = _invert(fill_lo, fill_hi, slots, n_tiles * tmx)
    tables = [a.astype(jnp.int32) for a in (tile_expert, n_used, run_index, next_expert)]
    return slots * SUBLANES, src.reshape(n_tiles, 1, tmx), tables


def kernel(x, c, w_ada, b_ada, g_norm1, w_in, w_dw, b_dw, g_conv_ln, b_conv_ln, w_conv_pw, b_conv_pw,
           w_a2, b_a2, g_gla_norm, w_gla_o, w_out, g_norm2, w_router_g, b_router_g, w_router_e,
           b_router_e, w1, w3, w2, w_ada_f, b_ada_f, g_final):
    nb, s, d = x.shape
    assert w_ada.shape[0] == 1, "single-layer block"
    assert d == 2 * PACK_ROWS * LANES, "packed token rows assume D_MODEL = 1024"
    tm = min(512, s)
    tc = min(256, s)
    ts = min(256, s)
    tmx = 256
    t = nb * s
    n_tiles = (t * TOP_K) // tmx + N_EXPERTS
    mod3 = _ada(c, w_ada[0], b_ada[0]).reshape(nb, N_MOD, d)
    modf3 = _ada(c, w_ada_f, b_ada_f).reshape(nb, 2, d)
    z, q, k, v, rs, lg, gt = _proj(x, mod3, g_norm1[0], w_in[0], w_a2[0], b_a2[0], tm)
    og = _gla(q, k, lg, v, rs, g_gla_norm[0], tc)
    h, u2, logits = _merge(z, og, gt, x, mod3, w_dw[0], b_dw[0], g_conv_ln[0], b_conv_ln[0],
                           w_conv_pw[0], b_conv_pw[0], w_gla_o[0], w_out[0], g_norm2[0],
                           w_router_g[0], b_router_g[0], w_router_e[0], b_router_e[0], ts)
    route, routet, cnt = _route_call(logits.reshape(t, LANES), min(ROUTE_ROWS, t), ts)
    slot_rows, src_rows, tables = _plan(routet, cnt, ts, tmx, n_tiles)
    ys = _experts(*tables, src_rows, u2, w1[0], w3[0], w2[0])
    return _final(slot_rows, h, route.reshape(nb, s, LANES), mod3, modf3, g_final, ys, ts)
```

```python
import functools

import jax
import jax.numpy as jnp
from jax import lax
from jax.experimental import pallas as pl
from jax.experimental.pallas import tpu as pltpu

F32 = jnp.float32
BF16 = jnp.bfloat16

EPS = 1e-6
CONV_DIM = 512
CONV_WIDTH = 31
GLA_HEADS = 4
GLA_DK = 128
GLA_DV = 256
GLA_LOWRANK = 16
GLA_TAU = 16.0
QK_DIM = GLA_HEADS * GLA_DK
V_DIM = GLA_HEADS * GLA_DV
N_GROUPS = 4
EXPERTS_PER_GROUP = 8
N_EXPERTS = N_GROUPS * EXPERTS_PER_GROUP
TOP_K = 2
N_MOD = 6

LANES = 128
SUBLANES = 8
PACK_ROWS = 4
CONV_ROWS = 32
WINDOW_ROWS = 32
ROUTE_ROWS = 2048
CLEAR_UNROLL = 16
CONV_HALO = 32
GLA_CHUNK = 128
GLA_FACTOR_RANGE = 60.0
VMEM_LIMIT = 56 * 1024 * 1024


def _bdot(a, b):
    return jnp.dot(a, b, preferred_element_type=F32)


def _split(a):
    hi = a.astype(BF16)
    lo = (a - hi.astype(F32)).astype(BF16)
    return hi, lo


def _dot3(a, b):
    ah, al = _split(a)
    bh, bl = _split(b)
    return _bdot(ah, bh) + (_bdot(ah, bl) + _bdot(al, bh))


def _sigmoid(x):
    return 1.0 / (1.0 + jnp.exp(-x))


def _rms(x, g):
    ms = jnp.mean(x * x, axis=-1, keepdims=True)
    return x * lax.rsqrt(ms + EPS) * g


def _rows_to_tiles(ref, val):
    n = val.shape[0]
    for j in range(val.shape[1] // LANES):
        ref[pl.ds(j, n, stride=SUBLANES), :] = val[:, j * LANES:(j + 1) * LANES]


def _tiles_to_rows(ref, n):
    return jnp.concatenate([ref[pl.ds(j, n, stride=SUBLANES), :] for j in range(SUBLANES)], axis=-1)


def _pack_rows(ref, val, tiles):
    _rows_to_tiles(tiles, val)
    ref[...] = pltpu.bitcast(tiles[...].astype(BF16), jnp.uint32)


def _unpack_rows(ref, n, tiles):
    tiles[...] = pltpu.bitcast(ref[...], BF16).astype(F32)
    return _tiles_to_rows(tiles, n).astype(BF16)


def _params(*sem):
    return pltpu.CompilerParams(dimension_semantics=sem, vmem_limit_bytes=VMEM_LIMIT)


def _ada_body(c_ref, w_ref, b_ref, o_ref):
    c = c_ref[...]
    o_ref[...] = _dot3(c * _sigmoid(c), w_ref[...]) + b_ref[...]


def _ada(c, w, b, tn=1024):
    nb, d = c.shape
    n = w.shape[1]
    return pl.pallas_call(
        _ada_body,
        grid=(n // tn,),
        in_specs=[pl.BlockSpec((nb, d), lambda j: (0, 0)),
                  pl.BlockSpec((d, tn), lambda j: (0, j)),
                  pl.BlockSpec((1, tn), lambda j: (0, j))],
        out_specs=pl.BlockSpec((nb, tn), lambda j: (0, j)),
        out_shape=jax.ShapeDtypeStruct((nb, n), F32),
        compiler_params=_params("arbitrary"),
        name="ada",
    )(c, w, b.reshape(1, n))


def _proj_body(x_ref, mod_ref, g1_ref, wm_ref, wa1_ref, wg_ref, wa2_ref, ba2_ref,
               z_ref, q_ref, k_ref, v_ref, rs_ref, lg_ref, gt_ref):
    x = x_ref[0]
    u = (_rms(x, g1_ref[...]) * (1.0 + mod_ref[0, 1:2, :]) + mod_ref[0, 0:1, :]).astype(BF16)
    c0 = 2 * CONV_DIM
    c1 = c0 + 2 * QK_DIM
    c2 = c1 + V_DIM
    c3 = c2 + V_DIM
    pc = _bdot(u, wm_ref[:, 0:c0])
    z_ref[0] = pc[:, :CONV_DIM] * _sigmoid(pc[:, CONV_DIM:])
    qk = _bdot(u, wm_ref[:, c0:c1])
    q_ref[0] = qk[:, :QK_DIM] * (GLA_DK ** -0.5)
    k_ref[0] = qk[:, QK_DIM:]
    v_ref[0] = _bdot(u, wm_ref[:, c1:c2]).astype(BF16)
    r = _bdot(u, wm_ref[:, c2:c3])
    rs_ref[0] = (r * _sigmoid(r)).astype(BF16)
    a1 = _bdot(u, wa1_ref[...])
    xg = _dot3(a1, wa2_ref[...]) + ba2_ref[...]
    lg_ref[0] = (jnp.minimum(xg, 0.0) - jnp.log1p(jnp.exp(-jnp.abs(xg)))) * (1.0 / GLA_TAU)
    gt_ref[0] = _sigmoid(_bdot(u, wg_ref[...])).astype(BF16)


def _wsplit_body(wt_ref, wm_ref, wa1_ref, wg_ref, buf, sem, *, c3):
    cw = buf.shape[0]
    n_main = c3 // cw
    j = pl.program_id(0)
    start = jnp.where(j <= n_main, j * cw, c3 + GLA_LOWRANK + (j - n_main - 1) * cw)
    chunk = pltpu.make_async_copy(wt_ref.at[pl.ds(pl.multiple_of(start, SUBLANES), cw), :], buf, sem)
    chunk.start()
    chunk.wait()

    @pl.when(j < n_main)
    def _():
        wm_ref[...] = buf[...].T.astype(BF16)

    @pl.when(j == n_main)
    def _():
        lane = lax.broadcasted_iota(jnp.int32, wa1_ref.shape, 1)
        wa1_ref[...] = jnp.where(lane < GLA_LOWRANK, buf[0:LANES, :].T, 0.0).astype(BF16)

    @pl.when(j > n_main)
    def _():
        wg_ref[...] = buf[...].T.astype(BF16)


def _wsplit(wt, c3, cw=512):
    n, d = wt.shape
    ng = n - c3 - GLA_LOWRANK
    n_main = c3 // cw
    return pl.pallas_call(
        functools.partial(_wsplit_body, c3=c3),
        grid=(n_main + 1 + ng // cw,),
        in_specs=[pl.BlockSpec(memory_space=pl.ANY)],
        out_specs=[pl.BlockSpec((d, cw), lambda j: (0, jnp.minimum(j, n_main - 1))),
                   pl.BlockSpec((d, LANES), lambda j: (0, 0)),
                   pl.BlockSpec((d, cw), lambda j: (0, jnp.clip(j - n_main - 1, 0, ng // cw - 1)))],
        out_shape=[jax.ShapeDtypeStruct((d, c3), BF16), jax.ShapeDtypeStruct((d, LANES), BF16),
                   jax.ShapeDtypeStruct((d, ng), BF16)],
        scratch_shapes=[pltpu.VMEM((cw, d), F32), pltpu.SemaphoreType.DMA(())],
        compiler_params=_params("arbitrary"),
        name="wsplit",
    )(wt)


def _proj(x, mod3, g1, w_in, w_a2, b_a2, tm):
    nb, s, d = x.shape
    c3 = 2 * CONV_DIM + 2 * QK_DIM + 2 * V_DIM
    wm, wa1, wg = _wsplit(jnp.swapaxes(w_in, 0, 1), c3)
    wa2 = jnp.pad(w_a2, ((0, LANES - GLA_LOWRANK), (0, 0)))
    ng = wg.shape[1]

    def tok(width):
        return pl.BlockSpec((1, tm, width), lambda b, i: (b, i, 0))

    def const(shape):
        return pl.BlockSpec(shape, lambda b, i: (0,) * len(shape))

    def out(width, dt):
        return jax.ShapeDtypeStruct((nb, s, width), dt)

    return pl.pallas_call(
        _proj_body,
        grid=(nb, s // tm),
        in_specs=[tok(d),
                  pl.BlockSpec((1, N_MOD, d), lambda b, i: (b, 0, 0)),
                  const((1, d)), const(wm.shape), const(wa1.shape), const(wg.shape),
                  const(wa2.shape), const((1, QK_DIM))],
        out_specs=[tok(CONV_DIM), tok(QK_DIM), tok(QK_DIM), tok(V_DIM), tok(V_DIM), tok(QK_DIM), tok(ng)],
        out_shape=[out(CONV_DIM, F32), out(QK_DIM, F32), out(QK_DIM, F32), out(V_DIM, BF16),
                   out(V_DIM, BF16), out(QK_DIM, F32), out(ng, BF16)],
        compiler_params=_params("arbitrary", "arbitrary"),
        name="proj",
    )(x, mod3, g1.reshape(1, d), wm, wa1, wg, wa2, b_a2.reshape(1, QK_DIM))


def _gla_body(q_ref, k_ref, lg_ref, v_ref, rs_ref, gn_ref, o_ref, st_ref, b_s, oi_s, *, n_chunks):
    cl = GLA_CHUNK

    @pl.when(pl.program_id(1) == 0)
    def _():
        st_ref[...] = jnp.zeros_like(st_ref)

    row = lax.broadcasted_iota(jnp.int32, (cl, cl), 0)
    col = lax.broadcasted_iota(jnp.int32, (cl, cl), 1)
    causal = col <= row
    tri = jnp.where(causal, 1.0, 0.0).astype(BF16)
    chunks = [slice(c * cl, (c + 1) * cl) for c in range(n_chunks)]
    heads = range(GLA_HEADS)
    ksl = [slice(h * GLA_DK, (h + 1) * GLA_DK) for h in heads]
    vsl = [slice(h * GLA_DV, (h + 1) * GLA_DV) for h in heads]

    lowest = None
    for rows in chunks:
        gh, gl = _split(lg_ref[0, rows, :])
        b = _bdot(tri, gh) + _bdot(tri, gl)
        b_s[rows, :] = b
        low = jnp.min(b[cl - 1:cl, :])
        lowest = low if lowest is None else jnp.minimum(lowest, low)

    def intra_pairwise(h, c):
        kf = k_ref[0, chunks[c], ksl[h]]
        vf = v_ref[0, chunks[c], vsl[h]].astype(F32)
        bh = b_s[chunks[c], ksl[h]]
        key = lax.broadcasted_iota(jnp.int32, (cl, 1), 0)

        def group(g, carry):
            r0 = pl.multiple_of(g * SUBLANES, SUBLANES)
            q8 = q_ref[0, pl.ds(c * cl + r0, SUBLANES), ksl[h]]
            b8 = b_s[pl.ds(c * cl + r0, SUBLANES), ksl[h]]
            out_rows = []
            for r in range(SUBLANES):
                diff = jnp.where(key <= r0 + r, b8[r:r + 1, :] - bh, -jnp.inf)
                att = jnp.sum(jnp.exp(diff) * kf * q8[r:r + 1, :], axis=-1, keepdims=True)
                out_rows.append(jnp.sum(att * vf, axis=0, keepdims=True))
            oi_s[h, pl.ds(c * cl + r0, SUBLANES), :] = jnp.concatenate(out_rows, axis=0)
            return carry

        lax.fori_loop(0, cl // SUBLANES, group, 0)
        return oi_s[h, chunks[c], :]

    def run(factored):
        cums = [b_s[rows, :] for rows in chunks]
        qes, kts, bts, klts, decays = [], [], [], [], []
        for rows, b in zip(chunks, cums):
            qes.append((q_ref[0, rows, :] * jnp.exp(b)).astype(BF16))
            kt = k_ref[0, rows, :].T
            bt = b.T
            bl = bt[:, cl - 1:cl]
            kts.append(kt)
            bts.append(bt)
            klts.append((kt * jnp.exp(bl - bt)).astype(BF16))
            decays.append(jnp.exp(bl))
        intra = {}
        if factored:
            kets = [(kt * jnp.exp(-bt)).astype(BF16) for kt, bt in zip(kts, bts)]
            atts = {}
            for c in range(n_chunks):
                for h in heads:
                    att = _bdot(qes[c][:, ksl[h]], kets[c][ksl[h], :])
                    atts[h, c] = jnp.where(causal, att, 0.0).astype(BF16)
            for c, rows in enumerate(chunks):
                for h in heads:
                    intra[h, c] = _bdot(atts[h, c], v_ref[0, rows, vsl[h]])
        else:
            for c in range(n_chunks):
                for h in heads:
                    intra[h, c] = intra_pairwise(h, c)
        updates = {}
        for c, rows in enumerate(chunks):
            for h in heads:
                updates[h, c] = _bdot(klts[c][ksl[h], :], v_ref[0, rows, vsl[h]])

        outs = {}
        states = []
        for h in heads:
            state = st_ref[h]
            for c, rows in enumerate(chunks):
                o = intra[h, c] + _bdot(qes[c][:, ksl[h]], state.astype(BF16))
                state = decays[c][ksl[h], :] * state + updates[h, c]
                outs[h, c] = (_rms(o, gn_ref[:, vsl[h]]) * rs_ref[0, rows, vsl[h]].astype(F32)).astype(BF16)
            states.append(state)
        for c, rows in enumerate(chunks):
            o_ref[0, rows, :] = jnp.concatenate([outs[h, c] for h in heads], axis=-1)
        for h in heads:
            st_ref[h] = states[h]

    in_range = lowest > -GLA_FACTOR_RANGE
    pl.when(in_range)(functools.partial(run, True))
    pl.when(jnp.logical_not(in_range))(functools.partial(run, False))


def _gla(q, k, lg, v, rs, gn, tc):
    nb, s, _ = q.shape

    def tok(width):
        return pl.BlockSpec((1, tc, width), lambda b, i: (b, i, 0))

    return pl.pallas_call(
        functools.partial(_gla_body, n_chunks=tc // GLA_CHUNK),
        grid=(nb, s // tc),
        in_specs=[tok(QK_DIM), tok(QK_DIM), tok(QK_DIM), tok(V_DIM), tok(V_DIM),
                  pl.BlockSpec((1, V_DIM), lambda b, i: (0, 0))],
        out_specs=tok(V_DIM),
        out_shape=jax.ShapeDtypeStruct((nb, s, V_DIM), BF16),
        scratch_shapes=[pltpu.VMEM((GLA_HEADS, GLA_DK, GLA_DV), F32),
                        pltpu.VMEM((tc, QK_DIM), F32),
                        pltpu.VMEM((GLA_HEADS, tc, GLA_DV), F32)],
        compiler_params=_params("arbitrary", "arbitrary"),
        name="gla",
    )(q, k, lg, v, rs, gn.reshape(1, V_DIM))


def _route(logits, cnt, sub):
    ts = logits.shape[0]
    lane = lax.broadcasted_iota(jnp.int32, (ts, LANES), 1).astype(F32)
    ninf = -jnp.inf
    lgm = jnp.where(lane < N_GROUPS, logits, ninf)
    gmax = jnp.max(lgm, axis=-1, keepdims=True)
    gsel = jnp.min(jnp.where(lgm == gmax, lane, float(LANES)), axis=-1, keepdims=True)
    wg = 1.0 / jnp.sum(jnp.exp(lgm - gmax), axis=-1, keepdims=True)
    base = N_GROUPS + EXPERTS_PER_GROUP * gsel
    le = jnp.where(lane >= base, jnp.where(lane < base + EXPERTS_PER_GROUP, logits, ninf), ninf)
    v1 = jnp.max(le, axis=-1, keepdims=True)
    i1 = jnp.min(jnp.where(le == v1, lane, float(LANES)), axis=-1, keepdims=True)
    le2 = jnp.where(lane == i1, ninf, le)
    v2 = jnp.max(le2, axis=-1, keepdims=True)
    i2 = jnp.min(jnp.where(le2 == v2, lane, float(LANES)), axis=-1, keepdims=True)
    e21 = jnp.exp(v2 - v1)
    w1 = wg / (1.0 + e21)
    w2 = w1 * e21
    eid1 = i1 - N_GROUPS
    eid2 = i2 - N_GROUPS
    oh1 = jnp.where(lane == eid1, 1.0, 0.0)
    oh2 = jnp.where(lane == eid2, 1.0, 0.0)
    ohs = oh1 + oh2
    row = lax.broadcasted_iota(jnp.int32, (sub, sub), 0)
    col = lax.broadcasted_iota(jnp.int32, (sub, sub), 1)
    before = jnp.where(col < row, 1.0, 0.0).astype(BF16)
    tots = []
    starts = []
    for lo in range(0, ts, sub):
        piece = ohs[lo:lo + sub, :]
        starts.append(cnt)
        tots.append(cnt + _bdot(before, piece.astype(BF16)))
        cnt = cnt + jnp.sum(piece, axis=0, keepdims=True)
    tot = jnp.concatenate(tots, axis=0)
    rank1 = jnp.sum(oh1 * tot, axis=-1, keepdims=True)
    rank2 = jnp.sum(oh2 * tot, axis=-1, keepdims=True)
    packed = jnp.where(lane == 0.0, eid1,
             jnp.where(lane == 1.0, eid2,
             jnp.where(lane == 2.0, w1,
             jnp.where(lane == 3.0, w2,
             jnp.where(lane == 4.0, rank1,
             jnp.where(lane == 5.0, rank2, 0.0))))))
    return packed, cnt, jnp.concatenate(starts, axis=0)


def _merge_body(z_ref, og_ref, gt_ref, x_ref, mod_ref, wdw_ref, bdw_ref, lng_ref, lnb_ref,
                wpw_ref, bpw_ref, wgo_ref, wout_ref, g2_ref, wr_ref, br_ref,
                h_ref, u2_ref, logit_ref, zbuf, zsh, u2t):
    ts = z_ref.shape[1]
    d = x_ref.shape[2]
    first_tile = pl.program_id(1) == 0

    @pl.when(first_tile)
    def _():
        zbuf[0:CONV_HALO, :] = jnp.zeros((CONV_HALO, CONV_DIM), F32)

    zbuf[CONV_HALO:CONV_HALO + ts, :] = z_ref[0]
    span = ts + CONV_HALO - SUBLANES
    for r in range(1, SUBLANES):
        zsh[r - 1] = zbuf[r:r + span, :]
    off = CONV_HALO - (CONV_WIDTH - 1)
    pieces = []
    for blk in range(ts // CONV_ROWS):
        acc = None
        for j in range(CONV_WIDTH):
            a, r = divmod(off + j, SUBLANES)
            lo = a * SUBLANES + blk * CONV_ROWS
            src = zbuf[lo:lo + CONV_ROWS, :] if r == 0 else zsh[r - 1, lo:lo + CONV_ROWS, :]
            term = src * jnp.concatenate([wdw_ref[j]] * (CONV_ROWS // SUBLANES), axis=0)
            acc = term if acc is None else acc + term
        conv = acc + bdw_ref[...]
        mu = jnp.mean(conv, axis=-1, keepdims=True)
        xc = conv - mu
        var = jnp.mean(xc * xc, axis=-1, keepdims=True)
        ln = xc * lax.rsqrt(var + EPS) * lng_ref[...] + lnb_ref[...]
        pieces.append((ln * _sigmoid(ln)).astype(BF16))
    zbuf[0:CONV_HALO, :] = zbuf[ts:ts + CONV_HALO, :]
    y_conv = _bdot(jnp.concatenate(pieces, axis=0), wpw_ref[...]) + bpw_ref[...]
    y_gla = _bdot(og_ref[0], wgo_ref[...])
    merged = gt_ref[0, :, 0:d].astype(F32) * y_conv + gt_ref[0, :, d:2 * d].astype(F32) * y_gla
    y = _bdot(merged.astype(BF16), wout_ref[...])
    h = x_ref[0] + mod_ref[0, 2:3, :] * y
    h_ref[0] = h
    u2 = _rms(h, g2_ref[...]) * (1.0 + mod_ref[0, 4:5, :]) + mod_ref[0, 3:4, :]
    _pack_rows(u2_ref, u2, u2t)
    logit_ref[0] = _dot3(u2, wr_ref[...]) + br_ref[...]


def _merge(z, og, gt, x, mod3, w_dw, b_dw, ln_g, ln_b, w_pw, b_pw, w_go, w_out, g2, w_rg, b_rg, w_re, b_re, ts):
    nb, s, d = x.shape
    npad = LANES - N_GROUPS - N_EXPERTS
    wr = jnp.pad(jnp.concatenate([w_rg, w_re], axis=1), ((0, 0), (0, npad)))
    br = jnp.pad(jnp.concatenate([b_rg, b_re]), (0, npad)).reshape(1, LANES)

    def tok(width):
        return pl.BlockSpec((1, ts, width), lambda b, i: (b, i, 0))

    def const(shape):
        return pl.BlockSpec(shape, lambda b, i: (0,) * len(shape))

    def row(v):
        return v.reshape(1, v.shape[-1])

    return pl.pallas_call(
        _merge_body,
        grid=(nb, s // ts),
        in_specs=[tok(CONV_DIM), tok(V_DIM), tok(2 * d), tok(d),
                  pl.BlockSpec((1, N_MOD, d), lambda b, i: (b, 0, 0)),
                  const((CONV_WIDTH, SUBLANES, CONV_DIM)), const((1, CONV_DIM)), const((1, CONV_DIM)), const((1, CONV_DIM)),
                  const((CONV_DIM, d)), const((1, d)), const((V_DIM, d)), const((d, d)), const((1, d)),
                  const((d, LANES)), const((1, LANES))],
        out_specs=[tok(d), pl.BlockSpec((ts * PACK_ROWS, LANES), lambda b, i: (b * (s // ts) + i, 0)), tok(LANES)],
        out_shape=[jax.ShapeDtypeStruct((nb, s, d), F32),
                   jax.ShapeDtypeStruct((nb * s * PACK_ROWS, LANES), jnp.uint32),
                   jax.ShapeDtypeStruct((nb, s, LANES), F32)],
        scratch_shapes=[pltpu.VMEM((CONV_HALO + ts, CONV_DIM), F32),
                        pltpu.VMEM((SUBLANES - 1, CONV_HALO + ts - SUBLANES, CONV_DIM), F32),
                        pltpu.VMEM((ts * SUBLANES, LANES), F32)],
        compiler_params=_params("arbitrary", "arbitrary"),
        name="merge",
    )(z, og, gt, x, mod3, jnp.broadcast_to(w_dw.reshape(CONV_WIDTH, 1, CONV_DIM), (CONV_WIDTH, SUBLANES, CONV_DIM)), row(b_dw), row(ln_g), row(ln_b),
      w_pw.astype(BF16), row(b_pw), w_go.astype(BF16), w_out.astype(BF16), row(g2), wr, br)


def _route_body(logit_ref, route_ref, routet_ref, cnt_ref, before_ref, cnt_sc, *, sub):
    @pl.when(pl.program_id(0) == 0)
    def _():
        cnt_sc[...] = jnp.zeros_like(cnt_sc)

    packed, cnt, before = _route(logit_ref[...], cnt_sc[...], sub)
    route_ref[...] = packed
    before_ref[...] = before
    routet_ref[...] = packed.T[0:SUBLANES, :]
    cnt_sc[...] = cnt
    cnt_ref[...] = jnp.broadcast_to(cnt, cnt_ref.shape)


def _route_call(logits, tr, sub):
    t = logits.shape[0]
    return pl.pallas_call(
        functools.partial(_route_body, sub=sub),
        grid=(t // tr,),
        in_specs=[pl.BlockSpec((tr, LANES), lambda i: (i, 0))],
        out_specs=[pl.BlockSpec((tr, LANES), lambda i: (i, 0)), pl.BlockSpec((SUBLANES, tr), lambda i: (0, i)),
                   pl.BlockSpec((SUBLANES, LANES), lambda i: (0, 0)),
                   pl.BlockSpec((tr // sub, LANES), lambda i: (i, 0))],
        out_shape=[jax.ShapeDtypeStruct((t, LANES), F32), jax.ShapeDtypeStruct((SUBLANES, t), F32),
                   jax.ShapeDtypeStruct((SUBLANES, LANES), F32), jax.ShapeDtypeStruct((t // sub, LANES), F32)],
        scratch_shapes=[pltpu.VMEM((1, LANES), F32)],
        compiler_params=_params("arbitrary"),
        name="route",
    )(logits)


def _row_copy(src, i, dst, j, sem):
    return pltpu.make_async_copy(src.at[pl.ds(pl.multiple_of(i, SUBLANES), SUBLANES), :],
                                 dst.at[pl.ds(pl.multiple_of(j, SUBLANES), SUBLANES), :], sem)


def _invert_body(lo_ref, hi_ref, slot_ref, src_ref):
    i = pl.program_id(0)
    ts = slot_ref.shape[-1] // TOP_K

    @pl.when(i == 0)
    def _():
        for e in range(lo_ref.shape[0]):
            lo = lo_ref[e]
            hi = hi_ref[e]

            def clear(p, carry, lo=lo, hi=hi):
                for j in range(CLEAR_UNROLL):
                    src_ref[jnp.minimum(lo + p * CLEAR_UNROLL + j, hi - 1)] = 0
                return carry

            trips = lax.shift_right_logical(hi - lo + (CLEAR_UNROLL - 1), CLEAR_UNROLL.bit_length() - 1)
            lax.fori_loop(0, trips, clear, 0)

    for r in range(ts):
        row = (i * ts + r) * PACK_ROWS
        for k in range(TOP_K):
            src_ref[slot_ref[0, 0, k * ts + r]] = row


def _invert(fill_lo, fill_hi, slots, n_slots):
    nt, _, width = slots.shape
    return pl.pallas_call(
        _invert_body,
        grid_spec=pltpu.PrefetchScalarGridSpec(
            num_scalar_prefetch=2,
            grid=(nt,),
            in_specs=[pl.BlockSpec((1, 1, width), lambda i, lo, hi: (i, 0, 0), memory_space=pltpu.SMEM)],
            out_specs=pl.BlockSpec(memory_space=pltpu.SMEM)),
        out_shape=jax.ShapeDtypeStruct((n_slots,), jnp.int32),
        compiler_params=_params("arbitrary"),
        name="invert",
    )(fill_lo, fill_hi, slots)


def _expert_body(te_ref, nu_ref, seg_ref, nxt_ref, src_ref, u2_ref, w1_ref, w3_ref, w2_ref, ys_ref,
                 u2v, xg, xt, w1f, w3f, w2f, w1b, w3b, w2b, sem, wsem):
    i = pl.program_id(0)
    tmx = src_ref.shape[-1]
    expert = te_ref[i]
    first = i == 0
    changed = jnp.logical_or(first, expert != te_ref[jnp.maximum(i - 1, 0)])
    slot = lax.rem(seg_ref[i], 2)

    def weight_copies(e, s):
        return [pltpu.make_async_copy(src.at[e], dst.at[s], wsem.at[s])
                for src, dst in ((w1_ref, w1f), (w3_ref, w3f), (w2_ref, w2f))]

    load = pltpu.make_async_copy(u2_ref, u2v, sem)

    @pl.when(first)
    def _():
        load.start()
        for cp in weight_copies(expert, 0):
            cp.start()

    @pl.when(changed)
    def _():
        for cp in weight_copies(expert, slot):
            cp.wait()
        w1b[...] = w1f[slot].astype(BF16)
        w3b[...] = w3f[slot].astype(BF16)
        w2b[...] = w2f[slot].astype(BF16)

    @pl.when(jnp.logical_and(changed, nxt_ref[i] != expert))
    def _():
        for cp in weight_copies(nxt_ref[i], 1 - slot):
            cp.start()

    @pl.when(first)
    def _():
        load.wait()

    @pl.when(i < nu_ref[0])
    def _():
        for r in range(tmx):
            row = pl.multiple_of(src_ref[0, 0, r], PACK_ROWS)
            xg[r * PACK_ROWS:(r + 1) * PACK_ROWS, :] = u2v[pl.ds(row, PACK_ROWS), :]
        x = _unpack_rows(xg, tmx, xt)
        h1 = _bdot(x, w1b[...])
        h3 = _bdot(x, w3b[...])
        hid = (h1 * _sigmoid(h1) * h3).astype(BF16)
        _rows_to_tiles(ys_ref, _bdot(hid, w2b[...]))

    @pl.when(i >= nu_ref[0])
    def _():
        ys_ref[...] = jnp.zeros_like(ys_ref)


def _experts(tile_expert, n_used, run_index, next_expert, src_rows, u2p, w1, w3, w2):
    n_tiles, _, tmx = src_rows.shape
    ne, d, f = w1.shape
    hbm = pl.BlockSpec(memory_space=pl.ANY)
    return pl.pallas_call(
        _expert_body,
        grid_spec=pltpu.PrefetchScalarGridSpec(
            num_scalar_prefetch=4,
            grid=(n_tiles,),
            in_specs=[pl.BlockSpec((1, 1, tmx), lambda i, *_: (i, 0, 0), memory_space=pltpu.SMEM),
                      hbm, hbm, hbm, hbm],
            out_specs=pl.BlockSpec((tmx * SUBLANES, LANES), lambda i, *_: (i, 0)),
            scratch_shapes=[pltpu.VMEM(u2p.shape, jnp.uint32),
                            pltpu.VMEM((tmx * PACK_ROWS, LANES), jnp.uint32),
                            pltpu.VMEM((tmx * SUBLANES, LANES), F32),
                            pltpu.VMEM((2, d, f), F32), pltpu.VMEM((2, d, f), F32), pltpu.VMEM((2, f, d), F32),
                            pltpu.VMEM((d, f), BF16), pltpu.VMEM((d, f), BF16), pltpu.VMEM((f, d), BF16),
                            pltpu.SemaphoreType.DMA(()), pltpu.SemaphoreType.DMA((2,))]),
        out_shape=jax.ShapeDtypeStruct((n_tiles * tmx * SUBLANES, LANES), F32),
        compiler_params=_params("arbitrary"),
        name="experts",
    )(tile_expert, n_used, run_index, next_expert, src_rows, u2p, w1, w3, w2)


def _final_body(p_ref, pick_ref, win_ref, nwin_ref, h_ref, route_ref, mod_ref, modf_ref, gf_ref, ys_ref, o_ref,
                wbuf, y1_buf, y2_buf, sem, wsem):
    ts = h_ref.shape[1]
    step = pl.program_id(0) * pl.num_programs(1) + pl.program_id(1)
    n_steps = pl.num_programs(0) * pl.num_programs(1)
    cur = lax.rem(step, 2)
    win_rows = WINDOW_ROWS * SUBLANES
    buf_rows = N_EXPERTS * win_rows

    def window_copies(ref, half):
        return [pltpu.make_async_copy(
            ys_ref.at[pl.ds(pl.multiple_of(ref[0, 0, e], SUBLANES), win_rows), :],
            wbuf.at[pl.ds(pl.multiple_of(half * buf_rows + e * win_rows, SUBLANES), win_rows), :],
            wsem.at[half]) for e in range(N_EXPERTS)]

    @pl.when(step == 0)
    def _():
        for cp in window_copies(win_ref, 0):
            cp.start()

    @pl.when(step + 1 < n_steps)
    def _():
        for cp in window_copies(nwin_ref, 1 - cur):
            cp.start()

    for cp in window_copies(win_ref, cur):
        cp.wait()
    overflow = win_ref[0, 0, N_EXPERTS]

    @pl.when(overflow == 0)
    def _():
        base = cur * buf_rows
        for k, buf in enumerate((y1_buf, y2_buf)):
            for r in range(ts):
                row = pl.multiple_of(base + pick_ref[0, 0, k * ts + r], SUBLANES)
                buf[r * SUBLANES:(r + 1) * SUBLANES, :] = wbuf[pl.ds(row, SUBLANES), :]

    @pl.when(overflow != 0)
    def _():
        for k, buf in enumerate((y1_buf, y2_buf)):
            for r in range(ts):
                _row_copy(ys_ref, p_ref[0, 0, k * ts + r], buf, r * SUBLANES, sem).start(priority=r % 2)
        pltpu.make_async_copy(ys_ref.at[pl.ds(0, ts * SUBLANES), :], y1_buf, sem).wait()
        pltpu.make_async_copy(ys_ref.at[pl.ds(0, ts * SUBLANES), :], y2_buf, sem).wait()

    route = route_ref[0]
    y2 = route[:, 2:3] * _tiles_to_rows(y1_buf, ts) + route[:, 3:4] * _tiles_to_rows(y2_buf, ts)
    h = h_ref[0] + mod_ref[0, 5:6, :] * y2
    o_ref[0] = _rms(h, gf_ref[...]) * (1.0 + modf_ref[0, 1:2, :]) + modf_ref[0, 0:1, :]


def _final(slot_rows, pick_rows, windows, h, route, mod3, modf3, gf, ys, ts):
    nb, s, d = h.shape
    nt = s // ts
    last = nb * nt - 1

    def tok(width):
        return pl.BlockSpec((1, ts, width), lambda b, i: (b, i, 0))

    def idx(width, ahead=0):
        return pl.BlockSpec((1, 1, width), lambda b, i: (jnp.minimum(b * nt + i + ahead, last), 0, 0),
                            memory_space=pltpu.SMEM)

    return pl.pallas_call(
        _final_body,
        grid=(nb, nt),
        in_specs=[idx(TOP_K * ts), idx(TOP_K * ts), idx(LANES), idx(LANES, ahead=1),
                  tok(d), tok(LANES),
                  pl.BlockSpec((1, N_MOD, d), lambda b, i: (b, 0, 0)),
                  pl.BlockSpec((1, 2, d), lambda b, i: (b, 0, 0)),
                  pl.BlockSpec((1, d), lambda b, i: (0, 0)),
                  pl.BlockSpec(memory_space=pl.ANY)],
        out_specs=tok(d),
        out_shape=jax.ShapeDtypeStruct((nb, s, d), F32),
        scratch_shapes=[pltpu.VMEM((2 * N_EXPERTS * WINDOW_ROWS * SUBLANES, LANES), F32),
                        pltpu.VMEM((ts * SUBLANES, LANES), F32), pltpu.VMEM((ts * SUBLANES, LANES), F32),
                        pltpu.SemaphoreType.DMA(()), pltpu.SemaphoreType.DMA((2,))],
        compiler_params=_params("arbitrary", "arbitrary"),
        name="final",
    )(slot_rows, pick_rows, windows, windows, h, route, mod3, modf3, gf.reshape(1, d), ys)


def _plan(routet, cnt, before, ts, tmx, n_tiles):
    t = routet.shape[1]
    counts = cnt[0, :N_EXPERTS].astype(jnp.int32)
    tiles = (counts + (tmx - 1)) // tmx
    tile_end = jnp.cumsum(tiles)
    offs = ((tile_end - tiles) * tmx).astype(jnp.int32)
    n_used = tile_end[-1:]
    tile_ids = jnp.minimum(jnp.arange(n_tiles, dtype=jnp.int32), n_used[0] - 1)
    tile_expert = jnp.sum((tile_ids[:, None] >= tile_end[None, :]).astype(jnp.int32), axis=1)
    run_index = jnp.cumsum(jnp.concatenate([jnp.zeros((1,), jnp.int32),
                                            (tile_expert[1:] != tile_expert[:-1]).astype(jnp.int32)]))
    same_run = run_index[:, None] + 1 == run_index[None, :]
    has_next = jnp.any(same_run, axis=1)
    next_expert = jnp.where(has_next, jnp.max(jnp.where(same_run, tile_expert[None, :], 0), axis=1), tile_expert)

    def lookup(table):
        return jnp.sum(jnp.where(eid[None] == experts, table, 0), axis=0)

    def blocks(rows):
        return rows.reshape(TOP_K, t // ts, ts).transpose(1, 0, 2).reshape(t // ts, 1, TOP_K * ts)

    eid = routet[0:2].astype(jnp.int32)
    rank = routet[4:6].astype(jnp.int32)
    experts = jnp.arange(N_EXPERTS, dtype=jnp.int32)[:, None, None]
    slots = blocks(rank + lookup(offs[:, None, None]))
    ahead = before[:, :N_EXPERTS].astype(jnp.int32)
    in_tile = jnp.concatenate([ahead[1:], counts[None, :]], axis=0) - ahead
    overflow = jnp.any(in_tile > WINDOW_ROWS, axis=1).astype(jnp.int32)
    windows = jnp.concatenate([(offs[None, :] + ahead) * SUBLANES, overflow[:, None],
                               jnp.zeros((t // ts, LANES - N_EXPERTS - 1), jnp.int32)], axis=1)
    local = rank - lookup(jnp.repeat(ahead.T, ts, axis=1)[:, None, :])
    picks = blocks((eid * WINDOW_ROWS + local) * SUBLANES)
    fill_lo = jnp.concatenate([offs + counts, tile_end[-1:] * tmx]).astype(jnp.int32)
    fill_hi = jnp.concatenate([tile_end * tmx, jnp.full((1,), n_tiles * tmx, jnp.int32)]).astype(jnp.int32)
    src = _invert(fill_lo, fill_hi, slots, n_tiles * tmx)
    tables = [a.astype(jnp.int32) for a in (tile_expert, n_used, run_index, next_expert)]
    return (slots * SUBLANES, picks, windows.reshape(t // ts, 1, LANES)), src.reshape(n_tiles, 1, tmx), tables


def kernel(x, c, w_ada, b_ada, g_norm1, w_in, w_dw, b_dw, g_conv_ln, b_conv_ln, w_conv_pw, b_conv_pw,
           w_a2, b_a2, g_gla_norm, w_gla_o, w_out, g_norm2, w_router_g, b_router_g, w_router_e,
           b_router_e, w1, w3, w2, w_ada_f, b_ada_f, g_final):
    nb, s, d = x.shape
    assert w_ada.shape[0] == 1, "single-layer block"
    assert d == 2 * PACK_ROWS * LANES, "packed token rows assume D_MODEL = 1024"
    tm = min(512, s)
    tc = min(256, s)
    ts = min(256, s)
    tmx = 256
    t = nb * s
    n_tiles = (t * TOP_K) // tmx + N_EXPERTS
    mod3 = _ada(c, w_ada[0], b_ada[0]).reshape(nb, N_MOD, d)
    modf3 = _ada(c, w_ada_f, b_ada_f).reshape(nb, 2, d)
    z, q, k, v, rs, lg, gt = _proj(x, mod3, g_norm1[0], w_in[0], w_a2[0], b_a2[0], tm)
    og = _gla(q, k, lg, v, rs, g_gla_norm[0], tc)
    h, u2, logits = _merge(z, og, gt, x, mod3, w_dw[0], b_dw[0], g_conv_ln[0], b_conv_ln[0],
                           w_conv_pw[0], b_conv_pw[0], w_gla_o[0], w_out[0], g_norm2[0],
                           w_router_g[0], b_router_g[0], w_router_e[0], b_router_e[0], ts)
    route, routet, cnt, before = _route_call(logits.reshape(t, LANES), min(ROUTE_ROWS, t), ts)
    gather_tables, src_rows, tables = _plan(routet, cnt, before, ts, tmx, n_tiles)
    ys = _experts(*tables, src_rows, u2, w1[0], w3[0], w2[0])
    return _final(*gather_tables, h, route.reshape(nb, s, LANES), mod3, modf3, g_final, ys, ts)
```

```python
import functools

import jax
import jax.numpy as jnp
from jax import lax
from jax.experimental import pallas as pl
from jax.experimental.pallas import tpu as pltpu

F32 = jnp.float32
BF16 = jnp.bfloat16

EPS = 1e-6
CONV_DIM = 512
CONV_WIDTH = 31
GLA_HEADS = 4
GLA_DK = 128
GLA_DV = 256
GLA_LOWRANK = 16
GLA_TAU = 16.0
QK_DIM = GLA_HEADS * GLA_DK
V_DIM = GLA_HEADS * GLA_DV
N_GROUPS = 4
EXPERTS_PER_GROUP = 8
N_EXPERTS = N_GROUPS * EXPERTS_PER_GROUP
TOP_K = 2
N_MOD = 6

LANES = 128
SUBLANES = 8
PACK_ROWS = 4
CONV_ROWS = 32
WINDOW_ROWS = 32
ROUTE_ROWS = 2048
CLEAR_UNROLL = 16
CONV_HALO = 32
GLA_CHUNK = 128
GLA_FACTOR_RANGE = 60.0
VMEM_LIMIT = 56 * 1024 * 1024


def _bdot(a, b):
    return jnp.dot(a, b, preferred_element_type=F32)


def _split(a):
    hi = a.astype(BF16)
    lo = (a - hi.astype(F32)).astype(BF16)
    return hi, lo


def _dot3(a, b):
    ah, al = _split(a)
    bh, bl = _split(b)
    return _bdot(ah, bh) + (_bdot(ah, bl) + _bdot(al, bh))


def _sigmoid(x):
    return 1.0 / (1.0 + jnp.exp(-x))


def _rms(x, g):
    ms = jnp.mean(x * x, axis=-1, keepdims=True)
    return x * lax.rsqrt(ms + EPS) * g


def _rows_to_tiles(ref, val):
    n = val.shape[0]
    for j in range(val.shape[1] // LANES):
        ref[pl.ds(j, n, stride=SUBLANES), :] = val[:, j * LANES:(j + 1) * LANES]


def _tiles_to_rows(ref, n):
    return jnp.concatenate([ref[pl.ds(j, n, stride=SUBLANES), :] for j in range(SUBLANES)], axis=-1)


def _pack_rows(ref, val, tiles):
    _rows_to_tiles(tiles, val)
    ref[...] = pltpu.bitcast(tiles[...].astype(BF16), jnp.uint32)


def _unpack_rows(ref, n, tiles):
    tiles[...] = pltpu.bitcast(ref[...], BF16).astype(F32)
    return _tiles_to_rows(tiles, n).astype(BF16)


def _params(*sem):
    return pltpu.CompilerParams(dimension_semantics=sem, vmem_limit_bytes=VMEM_LIMIT)


def _ada_body(c_ref, w_ref, b_ref, o_ref):
    c = c_ref[...]
    o_ref[...] = _dot3(c * _sigmoid(c), w_ref[...]) + b_ref[...]


def _ada(c, w, b, tn=1024):
    nb, d = c.shape
    n = w.shape[1]
    return pl.pallas_call(
        _ada_body,
        grid=(n // tn,),
        in_specs=[pl.BlockSpec((nb, d), lambda j: (0, 0)),
                  pl.BlockSpec((d, tn), lambda j: (0, j)),
                  pl.BlockSpec((1, tn), lambda j: (0, j))],
        out_specs=pl.BlockSpec((nb, tn), lambda j: (0, j)),
        out_shape=jax.ShapeDtypeStruct((nb, n), F32),
        compiler_params=_params("arbitrary"),
        name="ada",
    )(c, w, b.reshape(1, n))


def _proj_body(x_ref, mod_ref, g1_ref, wm_ref, wa1_ref, wg_ref, wa2_ref, ba2_ref,
               z_ref, q_ref, k_ref, v_ref, rs_ref, lg_ref, gt_ref):
    x = x_ref[0]
    u = (_rms(x, g1_ref[...]) * (1.0 + mod_ref[0, 1:2, :]) + mod_ref[0, 0:1, :]).astype(BF16)
    c0 = 2 * CONV_DIM
    c1 = c0 + 2 * QK_DIM
    c2 = c1 + V_DIM
    c3 = c2 + V_DIM
    pc = _bdot(u, wm_ref[:, 0:c0])
    z_ref[0] = pc[:, :CONV_DIM] * _sigmoid(pc[:, CONV_DIM:])
    qk = _bdot(u, wm_ref[:, c0:c1])
    q_ref[0] = qk[:, :QK_DIM] * (GLA_DK ** -0.5)
    k_ref[0] = qk[:, QK_DIM:]
    v_ref[0] = _bdot(u, wm_ref[:, c1:c2]).astype(BF16)
    r = _bdot(u, wm_ref[:, c2:c3])
    rs_ref[0] = (r * _sigmoid(r)).astype(BF16)
    a1 = _bdot(u, wa1_ref[...])
    xg = _dot3(a1, wa2_ref[...]) + ba2_ref[...]
    lg_ref[0] = (jnp.minimum(xg, 0.0) - jnp.log1p(jnp.exp(-jnp.abs(xg)))) * (1.0 / GLA_TAU)
    gt_ref[0] = _sigmoid(_bdot(u, wg_ref[...])).astype(BF16)


def _wsplit_body(wt_ref, wm_ref, wa1_ref, wg_ref, buf, sem, *, c3):
    cw = buf.shape[0]
    n_main = c3 // cw
    j = pl.program_id(0)
    start = jnp.where(j <= n_main, j * cw, c3 + GLA_LOWRANK + (j - n_main - 1) * cw)
    chunk = pltpu.make_async_copy(wt_ref.at[pl.ds(pl.multiple_of(start, SUBLANES), cw), :], buf, sem)
    chunk.start()
    chunk.wait()

    @pl.when(j < n_main)
    def _():
        wm_ref[...] = buf[...].T.astype(BF16)

    @pl.when(j == n_main)
    def _():
        lane = lax.broadcasted_iota(jnp.int32, wa1_ref.shape, 1)
        wa1_ref[...] = jnp.where(lane < GLA_LOWRANK, buf[0:LANES, :].T, 0.0).astype(BF16)

    @pl.when(j > n_main)
    def _():
        wg_ref[...] = buf[...].T.astype(BF16)


def _wsplit(wt, c3, cw=512):
    n, d = wt.shape
    ng = n - c3 - GLA_LOWRANK
    n_main = c3 // cw
    return pl.pallas_call(
        functools.partial(_wsplit_body, c3=c3),
        grid=(n_main + 1 + ng // cw,),
        in_specs=[pl.BlockSpec(memory_space=pl.ANY)],
        out_specs=[pl.BlockSpec((d, cw), lambda j: (0, jnp.minimum(j, n_main - 1))),
                   pl.BlockSpec((d, LANES), lambda j: (0, 0)),
                   pl.BlockSpec((d, cw), lambda j: (0, jnp.clip(j - n_main - 1, 0, ng // cw - 1)))],
        out_shape=[jax.ShapeDtypeStruct((d, c3), BF16), jax.ShapeDtypeStruct((d, LANES), BF16),
                   jax.ShapeDtypeStruct((d, ng), BF16)],
        scratch_shapes=[pltpu.VMEM((cw, d), F32), pltpu.SemaphoreType.DMA(())],
        compiler_params=_params("arbitrary"),
        name="wsplit",
    )(wt)


def _proj(x, mod3, g1, w_in, w_a2, b_a2, tm):
    nb, s, d = x.shape
    c3 = 2 * CONV_DIM + 2 * QK_DIM + 2 * V_DIM
    wm, wa1, wg = _wsplit(jnp.swapaxes(w_in, 0, 1), c3)
    wa2 = jnp.pad(w_a2, ((0, LANES - GLA_LOWRANK), (0, 0)))
    ng = wg.shape[1]

    def tok(width):
        return pl.BlockSpec((1, tm, width), lambda b, i: (b, i, 0))

    def const(shape):
        return pl.BlockSpec(shape, lambda b, i: (0,) * len(shape))

    def out(width, dt):
        return jax.ShapeDtypeStruct((nb, s, width), dt)

    return pl.pallas_call(
        _proj_body,
        grid=(nb, s // tm),
        in_specs=[tok(d),
                  pl.BlockSpec((1, N_MOD, d), lambda b, i: (b, 0, 0)),
                  const((1, d)), const(wm.shape), const(wa1.shape), const(wg.shape),
                  const(wa2.shape), const((1, QK_DIM))],
        out_specs=[tok(CONV_DIM), tok(QK_DIM), tok(QK_DIM), tok(V_DIM), tok(V_DIM), tok(QK_DIM), tok(ng)],
        out_shape=[out(CONV_DIM, F32), out(QK_DIM, F32), out(QK_DIM, F32), out(V_DIM, BF16),
                   out(V_DIM, BF16), out(QK_DIM, F32), out(ng, BF16)],
        compiler_params=_params("arbitrary", "arbitrary"),
        name="proj",
    )(x, mod3, g1.reshape(1, d), wm, wa1, wg, wa2, b_a2.reshape(1, QK_DIM))


def _gla_body(q_ref, k_ref, lg_ref, v_ref, rs_ref, gn_ref, o_ref, st_ref, b_s, oi_s, *, n_chunks):
    cl = GLA_CHUNK

    @pl.when(pl.program_id(1) == 0)
    def _():
        st_ref[...] = jnp.zeros_like(st_ref)

    row = lax.broadcasted_iota(jnp.int32, (cl, cl), 0)
    col = lax.broadcasted_iota(jnp.int32, (cl, cl), 1)
    causal = col <= row
    tri = jnp.where(causal, 1.0, 0.0).astype(BF16)
    chunks = [slice(c * cl, (c + 1) * cl) for c in range(n_chunks)]
    heads = range(GLA_HEADS)
    ksl = [slice(h * GLA_DK, (h + 1) * GLA_DK) for h in heads]
    vsl = [slice(h * GLA_DV, (h + 1) * GLA_DV) for h in heads]

    lowest = None
    for rows in chunks:
        gh, gl = _split(lg_ref[0, rows, :])
        b = _bdot(tri, gh) + _bdot(tri, gl)
        b_s[rows, :] = b
        low = jnp.min(b[cl - 1:cl, :])
        lowest = low if lowest is None else jnp.minimum(lowest, low)

    def intra_pairwise(h, c):
        kf = k_ref[0, chunks[c], ksl[h]]
        vf = v_ref[0, chunks[c], vsl[h]].astype(F32)
        bh = b_s[chunks[c], ksl[h]]
        key = lax.broadcasted_iota(jnp.int32, (cl, 1), 0)

        def group(g, carry):
            r0 = pl.multiple_of(g * SUBLANES, SUBLANES)
            q8 = q_ref[0, pl.ds(c * cl + r0, SUBLANES), ksl[h]]
            b8 = b_s[pl.ds(c * cl + r0, SUBLANES), ksl[h]]
            out_rows = []
            for r in range(SUBLANES):
                diff = jnp.where(key <= r0 + r, b8[r:r + 1, :] - bh, -jnp.inf)
                att = jnp.sum(jnp.exp(diff) * kf * q8[r:r + 1, :], axis=-1, keepdims=True)
                out_rows.append(jnp.sum(att * vf, axis=0, keepdims=True))
            oi_s[h, pl.ds(c * cl + r0, SUBLANES), :] = jnp.concatenate(out_rows, axis=0)
            return carry

        lax.fori_loop(0, cl // SUBLANES, group, 0)
        return oi_s[h, chunks[c], :]

    def run(factored):
        cums = [b_s[rows, :] for rows in chunks]
        qes, kts, bts, klts, decays = [], [], [], [], []
        for rows, b in zip(chunks, cums):
            qes.append((q_ref[0, rows, :] * jnp.exp(b)).astype(BF16))
            kt = k_ref[0, rows, :].T
            bt = b.T
            bl = bt[:, cl - 1:cl]
            kts.append(kt)
            bts.append(bt)
            klts.append((kt * jnp.exp(bl - bt)).astype(BF16))
            decays.append(jnp.exp(bl))
        intra = {}
        if factored:
            kets = [(kt * jnp.exp(-bt)).astype(BF16) for kt, bt in zip(kts, bts)]
            atts = {}
            for c in range(n_chunks):
                for h in heads:
                    att = _bdot(qes[c][:, ksl[h]], kets[c][ksl[h], :])
                    atts[h, c] = jnp.where(causal, att, 0.0).astype(BF16)
            for c, rows in enumerate(chunks):
                for h in heads:
                    intra[h, c] = _bdot(atts[h, c], v_ref[0, rows, vsl[h]])
        else:
            for c in range(n_chunks):
                for h in heads:
                    intra[h, c] = intra_pairwise(h, c)
        updates = {}
        for c, rows in enumerate(chunks):
            for h in heads:
                updates[h, c] = _bdot(klts[c][ksl[h], :], v_ref[0, rows, vsl[h]])

        outs = {}
        states = []
        for h in heads:
            state = st_ref[h]
            for c, rows in enumerate(chunks):
                o = intra[h, c] + _bdot(qes[c][:, ksl[h]], state.astype(BF16))
                state = decays[c][ksl[h], :] * state + updates[h, c]
                outs[h, c] = (_rms(o, gn_ref[:, vsl[h]]) * rs_ref[0, rows, vsl[h]].astype(F32)).astype(BF16)
            states.append(state)
        for c, rows in enumerate(chunks):
            o_ref[0, rows, :] = jnp.concatenate([outs[h, c] for h in heads], axis=-1)
        for h in heads:
            st_ref[h] = states[h]

    in_range = lowest > -GLA_FACTOR_RANGE
    pl.when(in_range)(functools.partial(run, True))
    pl.when(jnp.logical_not(in_range))(functools.partial(run, False))


def _gla(q, k, lg, v, rs, gn, tc):
    nb, s, _ = q.shape

    def tok(width):
        return pl.BlockSpec((1, tc, width), lambda b, i: (b, i, 0))

    return pl.pallas_call(
        functools.partial(_gla_body, n_chunks=tc // GLA_CHUNK),
        grid=(nb, s // tc),
        in_specs=[tok(QK_DIM), tok(QK_DIM), tok(QK_DIM), tok(V_DIM), tok(V_DIM),
                  pl.BlockSpec((1, V_DIM), lambda b, i: (0, 0))],
        out_specs=tok(V_DIM),
        out_shape=jax.ShapeDtypeStruct((nb, s, V_DIM), BF16),
        scratch_shapes=[pltpu.VMEM((GLA_HEADS, GLA_DK, GLA_DV), F32),
                        pltpu.VMEM((tc, QK_DIM), F32),
                        pltpu.VMEM((GLA_HEADS, tc, GLA_DV), F32)],
        compiler_params=_params("arbitrary", "arbitrary"),
        name="gla",
    )(q, k, lg, v, rs, gn.reshape(1, V_DIM))


def _route(logits, cnt, sub):
    ts = logits.shape[0]
    lane = lax.broadcasted_iota(jnp.int32, (ts, LANES), 1).astype(F32)
    ninf = -jnp.inf
    lgm = jnp.where(lane < N_GROUPS, logits, ninf)
    gmax = jnp.max(lgm, axis=-1, keepdims=True)
    gsel = jnp.min(jnp.where(lgm == gmax, lane, float(LANES)), axis=-1, keepdims=True)
    wg = 1.0 / jnp.sum(jnp.exp(lgm - gmax), axis=-1, keepdims=True)
    base = N_GROUPS + EXPERTS_PER_GROUP * gsel
    le = jnp.where(lane >= base, jnp.where(lane < base + EXPERTS_PER_GROUP, logits, ninf), ninf)
    v1 = jnp.max(le, axis=-1, keepdims=True)
    i1 = jnp.min(jnp.where(le == v1, lane, float(LANES)), axis=-1, keepdims=True)
    le2 = jnp.where(lane == i1, ninf, le)
    v2 = jnp.max(le2, axis=-1, keepdims=True)
    i2 = jnp.min(jnp.where(le2 == v2, lane, float(LANES)), axis=-1, keepdims=True)
    e21 = jnp.exp(v2 - v1)
    w1 = wg / (1.0 + e21)
    w2 = w1 * e21
    eid1 = i1 - N_GROUPS
    eid2 = i2 - N_GROUPS
    oh1 = jnp.where(lane == eid1, 1.0, 0.0)
    oh2 = jnp.where(lane == eid2, 1.0, 0.0)
    ohs = oh1 + oh2
    row = lax.broadcasted_iota(jnp.int32, (sub, sub), 0)
    col = lax.broadcasted_iota(jnp.int32, (sub, sub), 1)
    before = jnp.where(col < row, 1.0, 0.0).astype(BF16)
    tots = []
    starts = []
    for lo in range(0, ts, sub):
        piece = ohs[lo:lo + sub, :]
        starts.append(cnt)
        tots.append(cnt + _bdot(before, piece.astype(BF16)))
        cnt = cnt + jnp.sum(piece, axis=0, keepdims=True)
    tot = jnp.concatenate(tots, axis=0)
    rank1 = jnp.sum(oh1 * tot, axis=-1, keepdims=True)
    rank2 = jnp.sum(oh2 * tot, axis=-1, keepdims=True)
    packed = jnp.where(lane == 0.0, eid1,
             jnp.where(lane == 1.0, eid2,
             jnp.where(lane == 2.0, w1,
             jnp.where(lane == 3.0, w2,
             jnp.where(lane == 4.0, rank1,
             jnp.where(lane == 5.0, rank2, 0.0))))))
    return packed, cnt, jnp.concatenate(starts, axis=0)


def _merge_body(z_ref, og_ref, gt_ref, x_ref, mod_ref, wdw_ref, bdw_ref, lng_ref, lnb_ref,
                wpw_ref, bpw_ref, wgo_ref, wout_ref, g2_ref, wr_ref, br_ref,
                h_ref, u2_ref, logit_ref, zbuf, zsh, u2t):
    ts = z_ref.shape[1]
    d = x_ref.shape[2]
    first_tile = pl.program_id(1) == 0

    @pl.when(first_tile)
    def _():
        zbuf[0:CONV_HALO, :] = jnp.zeros((CONV_HALO, CONV_DIM), F32)

    zbuf[CONV_HALO:CONV_HALO + ts, :] = z_ref[0]
    span = ts + CONV_HALO - SUBLANES
    for r in range(1, SUBLANES):
        zsh[r - 1] = zbuf[r:r + span, :]
    off = CONV_HALO - (CONV_WIDTH - 1)
    pieces = []
    for blk in range(ts // CONV_ROWS):
        acc = None
        for j in range(CONV_WIDTH):
            a, r = divmod(off + j, SUBLANES)
            lo = a * SUBLANES + blk * CONV_ROWS
            src = zbuf[lo:lo + CONV_ROWS, :] if r == 0 else zsh[r - 1, lo:lo + CONV_ROWS, :]
            term = src * jnp.concatenate([wdw_ref[j]] * (CONV_ROWS // SUBLANES), axis=0)
            acc = term if acc is None else acc + term
        conv = acc + bdw_ref[...]
        mu = jnp.mean(conv, axis=-1, keepdims=True)
        xc = conv - mu
        var = jnp.mean(xc * xc, axis=-1, keepdims=True)
        ln = xc * lax.rsqrt(var + EPS) * lng_ref[...] + lnb_ref[...]
        pieces.append((ln * _sigmoid(ln)).astype(BF16))
    zbuf[0:CONV_HALO, :] = zbuf[ts:ts + CONV_HALO, :]
    y_conv = _bdot(jnp.concatenate(pieces, axis=0), wpw_ref[...]) + bpw_ref[...]
    y_gla = _bdot(og_ref[0], wgo_ref[...])
    merged = gt_ref[0, :, 0:d].astype(F32) * y_conv + gt_ref[0, :, d:2 * d].astype(F32) * y_gla
    y = _bdot(merged.astype(BF16), wout_ref[...])
    h = x_ref[0] + mod_ref[0, 2:3, :] * y
    h_ref[0] = h
    u2 = _rms(h, g2_ref[...]) * (1.0 + mod_ref[0, 4:5, :]) + mod_ref[0, 3:4, :]
    _pack_rows(u2_ref, u2, u2t)
    logit_ref[0] = _dot3(u2, wr_ref[...]) + br_ref[...]


def _merge(z, og, gt, x, mod3, w_dw, b_dw, ln_g, ln_b, w_pw, b_pw, w_go, w_out, g2, w_rg, b_rg, w_re, b_re, ts):
    nb, s, d = x.shape
    npad = LANES - N_GROUPS - N_EXPERTS
    wr = jnp.pad(jnp.concatenate([w_rg, w_re], axis=1), ((0, 0), (0, npad)))
    br = jnp.pad(jnp.concatenate([b_rg, b_re]), (0, npad)).reshape(1, LANES)

    def tok(width):
        return pl.BlockSpec((1, ts, width), lambda b, i: (b, i, 0))

    def const(shape):
        return pl.BlockSpec(shape, lambda b, i: (0,) * len(shape))

    def row(v):
        return v.reshape(1, v.shape[-1])

    return pl.pallas_call(
        _merge_body,
        grid=(nb, s // ts),
        in_specs=[tok(CONV_DIM), tok(V_DIM), tok(2 * d), tok(d),
                  pl.BlockSpec((1, N_MOD, d), lambda b, i: (b, 0, 0)),
                  const((CONV_WIDTH, SUBLANES, CONV_DIM)), const((1, CONV_DIM)), const((1, CONV_DIM)), const((1, CONV_DIM)),
                  const((CONV_DIM, d)), const((1, d)), const((V_DIM, d)), const((d, d)), const((1, d)),
                  const((d, LANES)), const((1, LANES))],
        out_specs=[tok(d), pl.BlockSpec((ts * PACK_ROWS, LANES), lambda b, i: (b * (s // ts) + i, 0)), tok(LANES)],
        out_shape=[jax.ShapeDtypeStruct((nb, s, d), F32),
                   jax.ShapeDtypeStruct((nb * s * PACK_ROWS, LANES), jnp.uint32),
                   jax.ShapeDtypeStruct((nb, s, LANES), F32)],
        scratch_shapes=[pltpu.VMEM((CONV_HALO + ts, CONV_DIM), F32),
                        pltpu.VMEM((SUBLANES - 1, CONV_HALO + ts - SUBLANES, CONV_DIM), F32),
                        pltpu.VMEM((ts * SUBLANES, LANES), F32)],
        compiler_params=_params("arbitrary", "arbitrary"),
        name="merge",
    )(z, og, gt, x, mod3, jnp.broadcast_to(w_dw.reshape(CONV_WIDTH, 1, CONV_DIM), (CONV_WIDTH, SUBLANES, CONV_DIM)), row(b_dw), row(ln_g), row(ln_b),
      w_pw.astype(BF16), row(b_pw), w_go.astype(BF16), w_out.astype(BF16), row(g2), wr, br)


def _route_body(logit_ref, route_ref, routet_ref, cnt_ref, before_ref, cnt_sc, *, sub):
    @pl.when(pl.program_id(0) == 0)
    def _():
        cnt_sc[...] = jnp.zeros_like(cnt_sc)

    packed, cnt, before = _route(logit_ref[...], cnt_sc[...], sub)
    route_ref[...] = packed
    before_ref[...] = before
    routet_ref[...] = packed.T[0:SUBLANES, :]
    cnt_sc[...] = cnt
    cnt_ref[...] = jnp.broadcast_to(cnt, cnt_ref.shape)


def _route_call(logits, tr, sub):
    t = logits.shape[0]
    return pl.pallas_call(
        functools.partial(_route_body, sub=sub),
        grid=(t // tr,),
        in_specs=[pl.BlockSpec((tr, LANES), lambda i: (i, 0))],
        out_specs=[pl.BlockSpec((tr, LANES), lambda i: (i, 0)), pl.BlockSpec((SUBLANES, tr), lambda i: (0, i)),
                   pl.BlockSpec((SUBLANES, LANES), lambda i: (0, 0)),
                   pl.BlockSpec((tr // sub, LANES), lambda i: (i, 0))],
        out_shape=[jax.ShapeDtypeStruct((t, LANES), F32), jax.ShapeDtypeStruct((SUBLANES, t), F32),
                   jax.ShapeDtypeStruct((SUBLANES, LANES), F32), jax.ShapeDtypeStruct((t // sub, LANES), F32)],
        scratch_shapes=[pltpu.VMEM((1, LANES), F32)],
        compiler_params=_params("arbitrary"),
        name="route",
    )(logits)


def _row_copy(src, i, dst, j, sem):
    return pltpu.make_async_copy(src.at[pl.ds(pl.multiple_of(i, SUBLANES), SUBLANES), :],
                                 dst.at[pl.ds(pl.multiple_of(j, SUBLANES), SUBLANES), :], sem)


def _invert_body(lo_ref, hi_ref, slot_ref, src_ref):
    i = pl.program_id(0)
    ts = slot_ref.shape[-1] // TOP_K

    @pl.when(i == 0)
    def _():
        for e in range(lo_ref.shape[0]):
            lo = lo_ref[e]
            hi = hi_ref[e]

            def clear(p, carry, lo=lo, hi=hi):
                for j in range(CLEAR_UNROLL):
                    src_ref[jnp.minimum(lo + p * CLEAR_UNROLL + j, hi - 1)] = 0
                return carry

            trips = lax.shift_right_logical(hi - lo + (CLEAR_UNROLL - 1), CLEAR_UNROLL.bit_length() - 1)
            lax.fori_loop(0, trips, clear, 0)

    for r in range(ts):
        row = (i * ts + r) * PACK_ROWS
        for k in range(TOP_K):
            src_ref[slot_ref[0, 0, k * ts + r]] = row


def _invert(fill_lo, fill_hi, slots, n_slots):
    nt, _, width = slots.shape
    return pl.pallas_call(
        _invert_body,
        grid_spec=pltpu.PrefetchScalarGridSpec(
            num_scalar_prefetch=2,
            grid=(nt,),
            in_specs=[pl.BlockSpec((1, 1, width), lambda i, lo, hi: (i, 0, 0), memory_space=pltpu.SMEM)],
            out_specs=pl.BlockSpec(memory_space=pltpu.SMEM)),
        out_shape=jax.ShapeDtypeStruct((n_slots,), jnp.int32),
        compiler_params=_params("arbitrary"),
        name="invert",
    )(fill_lo, fill_hi, slots)


def _expert_body(te_ref, nu_ref, seg_ref, nxt_ref, src_ref, u2_ref, w1_ref, w3_ref, w2_ref, ys_ref,
                 u2v, xg, xt, w1f, w3f, w2f, w1b, w3b, w2b, sem, wsem):
    i = pl.program_id(0)
    tmx = src_ref.shape[-1]
    expert = te_ref[i]
    first = i == 0
    changed = jnp.logical_or(first, expert != te_ref[jnp.maximum(i - 1, 0)])
    slot = lax.rem(seg_ref[i], 2)

    def weight_copies(e, s):
        return [pltpu.make_async_copy(src.at[e], dst.at[s], wsem.at[s])
                for src, dst in ((w1_ref, w1f), (w3_ref, w3f), (w2_ref, w2f))]

    load = pltpu.make_async_copy(u2_ref, u2v, sem)

    @pl.when(first)
    def _():
        load.start()
        for cp in weight_copies(expert, 0):
            cp.start()

    @pl.when(changed)
    def _():
        for cp in weight_copies(expert, slot):
            cp.wait()
        w1b[...] = w1f[slot].astype(BF16)
        w3b[...] = w3f[slot].astype(BF16)
        w2b[...] = w2f[slot].astype(BF16)

    @pl.when(jnp.logical_and(changed, nxt_ref[i] != expert))
    def _():
        for cp in weight_copies(nxt_ref[i], 1 - slot):
            cp.start()

    @pl.when(first)
    def _():
        load.wait()

    @pl.when(i < nu_ref[0])
    def _():
        for r in range(tmx):
            row = pl.multiple_of(src_ref[0, 0, r], PACK_ROWS)
            xg[r * PACK_ROWS:(r + 1) * PACK_ROWS, :] = u2v[pl.ds(row, PACK_ROWS), :]
        x = _unpack_rows(xg, tmx, xt)
        h1 = _bdot(x, w1b[...])
        h3 = _bdot(x, w3b[...])
        hid = (h1 * _sigmoid(h1) * h3).astype(BF16)
        _rows_to_tiles(ys_ref, _bdot(hid, w2b[...]))

    @pl.when(i >= nu_ref[0])
    def _():
        ys_ref[...] = jnp.zeros_like(ys_ref)


def _experts(tile_expert, n_used, run_index, next_expert, src_rows, u2p, w1, w3, w2):
    n_tiles, _, tmx = src_rows.shape
    ne, d, f = w1.shape
    hbm = pl.BlockSpec(memory_space=pl.ANY)
    return pl.pallas_call(
        _expert_body,
        grid_spec=pltpu.PrefetchScalarGridSpec(
            num_scalar_prefetch=4,
            grid=(n_tiles,),
            in_specs=[pl.BlockSpec((1, 1, tmx), lambda i, *_: (i, 0, 0), memory_space=pltpu.SMEM),
                      hbm, hbm, hbm, hbm],
            out_specs=pl.BlockSpec((tmx * SUBLANES, LANES), lambda i, *_: (i, 0)),
            scratch_shapes=[pltpu.VMEM(u2p.shape, jnp.uint32),
                            pltpu.VMEM((tmx * PACK_ROWS, LANES), jnp.uint32),
                            pltpu.VMEM((tmx * SUBLANES, LANES), F32),
                            pltpu.VMEM((2, d, f), F32), pltpu.VMEM((2, d, f), F32), pltpu.VMEM((2, f, d), F32),
                            pltpu.VMEM((d, f), BF16), pltpu.VMEM((d, f), BF16), pltpu.VMEM((f, d), BF16),
                            pltpu.SemaphoreType.DMA(()), pltpu.SemaphoreType.DMA((2,))]),
        out_shape=jax.ShapeDtypeStruct((n_tiles * tmx * SUBLANES, LANES), F32),
        compiler_params=_params("arbitrary"),
        name="experts",
    )(tile_expert, n_used, run_index, next_expert, src_rows, u2p, w1, w3, w2)


def _final_body(p_ref, pick_ref, win_ref, nwin_ref, h_ref, route_ref, mod_ref, modf_ref, gf_ref, ys_ref, o_ref,
                wbuf, y1_buf, y2_buf, sem, wsem):
    ts = h_ref.shape[1]
    step = pl.program_id(0) * pl.num_programs(1) + pl.program_id(1)
    n_steps = pl.num_programs(0) * pl.num_programs(1)
    cur = lax.rem(step, 2)
    win_rows = WINDOW_ROWS * SUBLANES
    buf_rows = N_EXPERTS * win_rows

    def window_copies(ref, half):
        return [pltpu.make_async_copy(
            ys_ref.at[pl.ds(pl.multiple_of(ref[0, 0, e], SUBLANES), win_rows), :],
            wbuf.at[pl.ds(pl.multiple_of(half * buf_rows + e * win_rows, SUBLANES), win_rows), :],
            wsem.at[half]) for e in range(N_EXPERTS)]

    @pl.when(step == 0)
    def _():
        for e, cp in enumerate(window_copies(win_ref, 0)):
            cp.start(priority=e % 2)

    @pl.when(step + 1 < n_steps)
    def _():
        for e, cp in enumerate(window_copies(nwin_ref, 1 - cur)):
            cp.start(priority=e % 2)

    for cp in window_copies(win_ref, cur):
        cp.wait()
    overflow = win_ref[0, 0, N_EXPERTS]

    @pl.when(overflow == 0)
    def _():
        base = cur * buf_rows
        for k, buf in enumerate((y1_buf, y2_buf)):
            for r in range(ts):
                row = pl.multiple_of(base + pick_ref[0, 0, k * ts + r], SUBLANES)
                buf[r * SUBLANES:(r + 1) * SUBLANES, :] = wbuf[pl.ds(row, SUBLANES), :]

    @pl.when(overflow != 0)
    def _():
        for k, buf in enumerate((y1_buf, y2_buf)):
            for r in range(ts):
                _row_copy(ys_ref, p_ref[0, 0, k * ts + r], buf, r * SUBLANES, sem).start(priority=r % 2)
        pltpu.make_async_copy(ys_ref.at[pl.ds(0, ts * SUBLANES), :], y1_buf, sem).wait()
        pltpu.make_async_copy(ys_ref.at[pl.ds(0, ts * SUBLANES), :], y2_buf, sem).wait()

    route = route_ref[0]
    y2 = route[:, 2:3] * _tiles_to_rows(y1_buf, ts) + route[:, 3:4] * _tiles_to_rows(y2_buf, ts)
    h = h_ref[0] + mod_ref[0, 5:6, :] * y2
    o_ref[0] = _rms(h, gf_ref[...]) * (1.0 + modf_ref[0, 1:2, :]) + modf_ref[0, 0:1, :]


def _final(slot_rows, pick_rows, windows, h, route, mod3, modf3, gf, ys, ts):
    nb, s, d = h.shape
    nt = s // ts
    last = nb * nt - 1

    def tok(width):
        return pl.BlockSpec((1, ts, width), lambda b, i: (b, i, 0))

    def idx(width, ahead=0):
        return pl.BlockSpec((1, 1, width), lambda b, i: (jnp.minimum(b * nt + i + ahead, last), 0, 0),
                            memory_space=pltpu.SMEM)

    return pl.pallas_call(
        _final_body,
        grid=(nb, nt),
        in_specs=[idx(TOP_K * ts), idx(TOP_K * ts), idx(LANES), idx(LANES, ahead=1),
                  tok(d), tok(LANES),
                  pl.BlockSpec((1, N_MOD, d), lambda b, i: (b, 0, 0)),
                  pl.BlockSpec((1, 2, d), lambda b, i: (b, 0, 0)),
                  pl.BlockSpec((1, d), lambda b, i: (0, 0)),
                  pl.BlockSpec(memory_space=pl.ANY)],
        out_specs=tok(d),
        out_shape=jax.ShapeDtypeStruct((nb, s, d), F32),
        scratch_shapes=[pltpu.VMEM((2 * N_EXPERTS * WINDOW_ROWS * SUBLANES, LANES), F32),
                        pltpu.VMEM((ts * SUBLANES, LANES), F32), pltpu.VMEM((ts * SUBLANES, LANES), F32),
                        pltpu.SemaphoreType.DMA(()), pltpu.SemaphoreType.DMA((2,))],
        compiler_params=_params("arbitrary", "arbitrary"),
        name="final",
    )(slot_rows, pick_rows, windows, windows, h, route, mod3, modf3, gf.reshape(1, d), ys)


def _plan(routet, cnt, before, ts, tmx, n_tiles):
    t = routet.shape[1]
    counts = cnt[0, :N_EXPERTS].astype(jnp.int32)
    tiles = (counts + (tmx - 1)) // tmx
    tile_end = jnp.cumsum(tiles)
    offs = ((tile_end - tiles) * tmx).astype(jnp.int32)
    n_used = tile_end[-1:]
    tile_ids = jnp.minimum(jnp.arange(n_tiles, dtype=jnp.int32), n_used[0] - 1)
    tile_expert = jnp.sum((tile_ids[:, None] >= tile_end[None, :]).astype(jnp.int32), axis=1)
    run_index = jnp.cumsum(jnp.concatenate([jnp.zeros((1,), jnp.int32),
                                            (tile_expert[1:] != tile_expert[:-1]).astype(jnp.int32)]))
    same_run = run_index[:, None] + 1 == run_index[None, :]
    has_next = jnp.any(same_run, axis=1)
    next_expert = jnp.where(has_next, jnp.max(jnp.where(same_run, tile_expert[None, :], 0), axis=1), tile_expert)

    def lookup(table):
        return jnp.sum(jnp.where(eid[None] == experts, table, 0), axis=0)

    def blocks(rows):
        return rows.reshape(TOP_K, t // ts, ts).transpose(1, 0, 2).reshape(t // ts, 1, TOP_K * ts)

    eid = routet[0:2].astype(jnp.int32)
    rank = routet[4:6].astype(jnp.int32)
    experts = jnp.arange(N_EXPERTS, dtype=jnp.int32)[:, None, None]
    slots = blocks(rank + lookup(offs[:, None, None]))
    ahead = before[:, :N_EXPERTS].astype(jnp.int32)
    in_tile = jnp.concatenate([ahead[1:], counts[None, :]], axis=0) - ahead
    overflow = jnp.any(in_tile > WINDOW_ROWS, axis=1).astype(jnp.int32)
    windows = jnp.concatenate([(offs[None, :] + ahead) * SUBLANES, overflow[:, None],
                               jnp.zeros((t // ts, LANES - N_EXPERTS - 1), jnp.int32)], axis=1)
    local = rank - lookup(jnp.repeat(ahead.T, ts, axis=1)[:, None, :])
    picks = blocks((eid * WINDOW_ROWS + local) * SUBLANES)
    fill_lo = jnp.concatenate([offs + counts, tile_end[-1:] * tmx]).astype(jnp.int32)
    fill_hi = jnp.concatenate([tile_end * tmx, jnp.full((1,), n_tiles * tmx, jnp.int32)]).astype(jnp.int32)
    src = _invert(fill_lo, fill_hi, slots, n_tiles * tmx)
    tables = [a.astype(jnp.int32) for a in (tile_expert, n_used, run_index, next_expert)]
    return (slots * SUBLANES, picks, windows.reshape(t // ts, 1, LANES)), src.reshape(n_tiles, 1, tmx), tables


def kernel(x, c, w_ada, b_ada, g_norm1, w_in, w_dw, b_dw, g_conv_ln, b_conv_ln, w_conv_pw, b_conv_pw,
           w_a2, b_a2, g_gla_norm, w_gla_o, w_out, g_norm2, w_router_g, b_router_g, w_router_e,
           b_router_e, w1, w3, w2, w_ada_f, b_ada_f, g_final):
    nb, s, d = x.shape
    assert w_ada.shape[0] == 1, "single-layer block"
    assert d == 2 * PACK_ROWS * LANES, "packed token rows assume D_MODEL = 1024"
    tm = min(512, s)
    tc = min(256, s)
    ts = min(256, s)
    tmx = 256
    t = nb * s
    n_tiles = (t * TOP_K) // tmx + N_EXPERTS
    mod3 = _ada(c, w_ada[0], b_ada[0]).reshape(nb, N_MOD, d)
    modf3 = _ada(c, w_ada_f, b_ada_f).reshape(nb, 2, d)
    z, q, k, v, rs, lg, gt = _proj(x, mod3, g_norm1[0], w_in[0], w_a2[0], b_a2[0], tm)
    og = _gla(q, k, lg, v, rs, g_gla_norm[0], tc)
    h, u2, logits = _merge(z, og, gt, x, mod3, w_dw[0], b_dw[0], g_conv_ln[0], b_conv_ln[0],
                           w_conv_pw[0], b_conv_pw[0], w_gla_o[0], w_out[0], g_norm2[0],
                           w_router_g[0], b_router_g[0], w_router_e[0], b_router_e[0], ts)
    route, routet, cnt, before = _route_call(logits.reshape(t, LANES), min(ROUTE_ROWS, t), ts)
    gather_tables, src_rows, tables = _plan(routet, cnt, before, ts, tmx, n_tiles)
    ys = _experts(*tables, src_rows, u2, w1[0], w3[0], w2[0])
    return _final(*gather_tables, h, route.reshape(nb, s, LANES), mod3, modf3, g_final, ys, ts)
```

```python
import functools

import jax
import jax.numpy as jnp
from jax import lax
from jax.experimental import pallas as pl
from jax.experimental.pallas import tpu as pltpu

F32 = jnp.float32
BF16 = jnp.bfloat16

EPS = 1e-6
CONV_DIM = 512
CONV_WIDTH = 31
GLA_HEADS = 4
GLA_DK = 128
GLA_DV = 256
GLA_LOWRANK = 16
GLA_TAU = 16.0
QK_DIM = GLA_HEADS * GLA_DK
V_DIM = GLA_HEADS * GLA_DV
N_GROUPS = 4
EXPERTS_PER_GROUP = 8
N_EXPERTS = N_GROUPS * EXPERTS_PER_GROUP
TOP_K = 2
N_MOD = 6

LANES = 128
SUBLANES = 8
PACK_ROWS = 4
CONV_ROWS = 32
ROUTE_ROWS = 2048
CLEAR_UNROLL = 16
CONV_HALO = 32
GLA_CHUNK = 128
GLA_FACTOR_RANGE = 60.0
VMEM_LIMIT = 56 * 1024 * 1024


def _bdot(a, b):
    return jnp.dot(a, b, preferred_element_type=F32)


def _split(a):
    hi = a.astype(BF16)
    lo = (a - hi.astype(F32)).astype(BF16)
    return hi, lo


def _dot3(a, b):
    ah, al = _split(a)
    bh, bl = _split(b)
    return _bdot(ah, bh) + (_bdot(ah, bl) + _bdot(al, bh))


def _sigmoid(x):
    return 1.0 / (1.0 + jnp.exp(-x))


def _rms(x, g):
    ms = jnp.mean(x * x, axis=-1, keepdims=True)
    return x * lax.rsqrt(ms + EPS) * g


def _rows_to_tiles(ref, val):
    n = val.shape[0]
    for j in range(val.shape[1] // LANES):
        ref[pl.ds(j, n, stride=SUBLANES), :] = val[:, j * LANES:(j + 1) * LANES]


def _tiles_to_rows(ref, n):
    return jnp.concatenate([ref[pl.ds(j, n, stride=SUBLANES), :] for j in range(SUBLANES)], axis=-1)


def _pack_rows(ref, val, tiles):
    _rows_to_tiles(tiles, val)
    ref[...] = pltpu.bitcast(tiles[...].astype(BF16), jnp.uint32)


def _unpack_rows(ref, n, tiles):
    tiles[...] = pltpu.bitcast(ref[...], BF16).astype(F32)
    return _tiles_to_rows(tiles, n).astype(BF16)


def _params(*sem):
    return pltpu.CompilerParams(dimension_semantics=sem, vmem_limit_bytes=VMEM_LIMIT)


def _ada_body(c_ref, w_ref, b_ref, o_ref):
    c = c_ref[...]
    o_ref[...] = _dot3(c * _sigmoid(c), w_ref[...]) + b_ref[...]


def _ada(c, w, b, tn=1024):
    nb, d = c.shape
    n = w.shape[1]
    return pl.pallas_call(
        _ada_body,
        grid=(n // tn,),
        in_specs=[pl.BlockSpec((nb, d), lambda j: (0, 0)),
                  pl.BlockSpec((d, tn), lambda j: (0, j)),
                  pl.BlockSpec((1, tn), lambda j: (0, j))],
        out_specs=pl.BlockSpec((nb, tn), lambda j: (0, j)),
        out_shape=jax.ShapeDtypeStruct((nb, n), F32),
        compiler_params=_params("arbitrary"),
        name="ada",
    )(c, w, b.reshape(1, n))


def _proj_body(x_ref, mod_ref, g1_ref, wm_ref, wa1_ref, wg_ref, wa2_ref, ba2_ref,
               z_ref, q_ref, k_ref, v_ref, rs_ref, lg_ref, gt_ref, ls_ref):
    x = x_ref[0]
    u = (_rms(x, g1_ref[...]) * (1.0 + mod_ref[0, 1:2, :]) + mod_ref[0, 0:1, :]).astype(BF16)
    c0 = 2 * CONV_DIM
    c1 = c0 + 2 * QK_DIM
    c2 = c1 + V_DIM
    c3 = c2 + V_DIM
    pc = _bdot(u, wm_ref[:, 0:c0])
    z_ref[0] = pc[:, :CONV_DIM] * _sigmoid(pc[:, CONV_DIM:])
    qk = _bdot(u, wm_ref[:, c0:c1])
    q_ref[0] = qk[:, :QK_DIM] * (GLA_DK ** -0.5)
    k_ref[0] = qk[:, QK_DIM:]
    v_ref[0] = _bdot(u, wm_ref[:, c1:c2]).astype(BF16)
    r = _bdot(u, wm_ref[:, c2:c3])
    rs_ref[0] = (r * _sigmoid(r)).astype(BF16)
    a1 = _bdot(u, wa1_ref[...])
    xg = _dot3(a1, wa2_ref[...]) + ba2_ref[...]
    lg = (jnp.minimum(xg, 0.0) - jnp.log1p(jnp.exp(-jnp.abs(xg)))) * (1.0 / GLA_TAU)
    lg_ref[0] = lg
    ls_ref[0, 0] = jnp.concatenate([jnp.sum(lg[lo:lo + GLA_CHUNK, :], axis=0, keepdims=True)
                                    for lo in range(0, lg.shape[0], GLA_CHUNK)], axis=0)
    gt_ref[0] = _sigmoid(_bdot(u, wg_ref[...])).astype(BF16)


def _wsplit_body(wt_ref, wm_ref, wa1_ref, wg_ref, buf, sem, *, c3):
    cw = buf.shape[0]
    n_main = c3 // cw
    j = pl.program_id(0)
    start = jnp.where(j <= n_main, j * cw, c3 + GLA_LOWRANK + (j - n_main - 1) * cw)
    chunk = pltpu.make_async_copy(wt_ref.at[pl.ds(pl.multiple_of(start, SUBLANES), cw), :], buf, sem)
    chunk.start()
    chunk.wait()

    @pl.when(j < n_main)
    def _():
        wm_ref[...] = buf[...].T.astype(BF16)

    @pl.when(j == n_main)
    def _():
        lane = lax.broadcasted_iota(jnp.int32, wa1_ref.shape, 1)
        wa1_ref[...] = jnp.where(lane < GLA_LOWRANK, buf[0:LANES, :].T, 0.0).astype(BF16)

    @pl.when(j > n_main)
    def _():
        wg_ref[...] = buf[...].T.astype(BF16)


def _wsplit(wt, c3, cw=512):
    n, d = wt.shape
    ng = n - c3 - GLA_LOWRANK
    n_main = c3 // cw
    return pl.pallas_call(
        functools.partial(_wsplit_body, c3=c3),
        grid=(n_main + 1 + ng // cw,),
        in_specs=[pl.BlockSpec(memory_space=pl.ANY)],
        out_specs=[pl.BlockSpec((d, cw), lambda j: (0, jnp.minimum(j, n_main - 1))),
                   pl.BlockSpec((d, LANES), lambda j: (0, 0)),
                   pl.BlockSpec((d, cw), lambda j: (0, jnp.clip(j - n_main - 1, 0, ng // cw - 1)))],
        out_shape=[jax.ShapeDtypeStruct((d, c3), BF16), jax.ShapeDtypeStruct((d, LANES), BF16),
                   jax.ShapeDtypeStruct((d, ng), BF16)],
        scratch_shapes=[pltpu.VMEM((cw, d), F32), pltpu.SemaphoreType.DMA(())],
        compiler_params=_params("arbitrary"),
        name="wsplit",
    )(wt)


def _proj(x, mod3, g1, w_in, w_a2, b_a2, tm):
    nb, s, d = x.shape
    c3 = 2 * CONV_DIM + 2 * QK_DIM + 2 * V_DIM
    wm, wa1, wg = _wsplit(jnp.swapaxes(w_in, 0, 1), c3)
    wa2 = jnp.pad(w_a2, ((0, LANES - GLA_LOWRANK), (0, 0)))
    ng = wg.shape[1]

    def tok(width):
        return pl.BlockSpec((1, tm, width), lambda b, i: (b, i, 0))

    def const(shape):
        return pl.BlockSpec(shape, lambda b, i: (0,) * len(shape))

    def out(width, dt):
        return jax.ShapeDtypeStruct((nb, s, width), dt)

    return pl.pallas_call(
        _proj_body,
        grid=(nb, s // tm),
        in_specs=[tok(d),
                  pl.BlockSpec((1, N_MOD, d), lambda b, i: (b, 0, 0)),
                  const((1, d)), const(wm.shape), const(wa1.shape), const(wg.shape),
                  const(wa2.shape), const((1, QK_DIM))],
        out_specs=[tok(CONV_DIM), tok(QK_DIM), tok(QK_DIM), tok(V_DIM), tok(V_DIM), tok(QK_DIM), tok(ng),
                   pl.BlockSpec((1, 1, tm // GLA_CHUNK, QK_DIM), lambda b, i: (b, i, 0, 0))],
        out_shape=[out(CONV_DIM, F32), out(QK_DIM, F32), out(QK_DIM, F32), out(V_DIM, BF16),
                   out(V_DIM, BF16), out(QK_DIM, F32), out(ng, BF16),
                   jax.ShapeDtypeStruct((nb, s // tm, tm // GLA_CHUNK, QK_DIM), F32)],
        compiler_params=_params("arbitrary", "arbitrary"),
        name="proj",
    )(x, mod3, g1.reshape(1, d), wm, wa1, wg, wa2, b_a2.reshape(1, QK_DIM))


def _gla_body(ok_ref, q_ref, k_ref, lg_ref, v_ref, rs_ref, gn_ref, o_ref, st_ref, b_s, oi_s, *, n_chunks):
    cl = GLA_CHUNK

    @pl.when(pl.program_id(1) == 0)
    def _():
        st_ref[...] = jnp.zeros_like(st_ref)

    row = lax.broadcasted_iota(jnp.int32, (cl, cl), 0)
    col = lax.broadcasted_iota(jnp.int32, (cl, cl), 1)
    causal = col <= row
    tri = jnp.where(causal, 1.0, 0.0).astype(BF16)
    chunks = [slice(c * cl, (c + 1) * cl) for c in range(n_chunks)]
    heads = range(GLA_HEADS)
    ksl = [slice(h * GLA_DK, (h + 1) * GLA_DK) for h in heads]
    vsl = [slice(h * GLA_DV, (h + 1) * GLA_DV) for h in heads]

    def cumulative(rows):
        gh, gl = _split(lg_ref[0, rows, :])
        return _bdot(tri, gh) + _bdot(tri, gl)

    def intra_pairwise(h, c):
        kf = k_ref[0, chunks[c], ksl[h]]
        vf = v_ref[0, chunks[c], vsl[h]].astype(F32)
        bh = b_s[chunks[c], ksl[h]]
        key = lax.broadcasted_iota(jnp.int32, (cl, 1), 0)

        def group(g, carry):
            r0 = pl.multiple_of(g * SUBLANES, SUBLANES)
            q8 = q_ref[0, pl.ds(c * cl + r0, SUBLANES), ksl[h]]
            b8 = b_s[pl.ds(c * cl + r0, SUBLANES), ksl[h]]
            out_rows = []
            for r in range(SUBLANES):
                diff = jnp.where(key <= r0 + r, b8[r:r + 1, :] - bh, -jnp.inf)
                att = jnp.sum(jnp.exp(diff) * kf * q8[r:r + 1, :], axis=-1, keepdims=True)
                out_rows.append(jnp.sum(att * vf, axis=0, keepdims=True))
            oi_s[h, pl.ds(c * cl + r0, SUBLANES), :] = jnp.concatenate(out_rows, axis=0)
            return carry

        lax.fori_loop(0, cl // SUBLANES, group, 0)
        return oi_s[h, chunks[c], :]

    def run(factored):
        cums = [cumulative(rows) for rows in chunks]
        if not factored:
            for rows, b in zip(chunks, cums):
                b_s[rows, :] = b
        qes, kts, bts, klts, decays = [], [], [], [], []
        for rows, b in zip(chunks, cums):
            qes.append((q_ref[0, rows, :] * jnp.exp(b)).astype(BF16))
            kt = k_ref[0, rows, :].T
            bt = b.T
            bl = bt[:, cl - 1:cl]
            kts.append(kt)
            bts.append(bt)
            klts.append((kt * jnp.exp(bl - bt)).astype(BF16))
            decays.append(jnp.exp(bl))
        intra = {}
        if factored:
            kets = [(kt * jnp.exp(-bt)).astype(BF16) for kt, bt in zip(kts, bts)]
            atts = {}
            for c in range(n_chunks):
                for h in heads:
                    att = _bdot(qes[c][:, ksl[h]], kets[c][ksl[h], :])
                    atts[h, c] = jnp.where(causal, att, 0.0).astype(BF16)
            for c, rows in enumerate(chunks):
                for h in heads:
                    intra[h, c] = _bdot(atts[h, c], v_ref[0, rows, vsl[h]])
        else:
            for c in range(n_chunks):
                for h in heads:
                    intra[h, c] = intra_pairwise(h, c)
        updates = {}
        for c, rows in enumerate(chunks):
            for h in heads:
                updates[h, c] = _bdot(klts[c][ksl[h], :], v_ref[0, rows, vsl[h]])

        outs = {}
        states = []
        for h in heads:
            state = st_ref[h]
            for c, rows in enumerate(chunks):
                o = intra[h, c] + _bdot(qes[c][:, ksl[h]], state.astype(BF16))
                state = decays[c][ksl[h], :] * state + updates[h, c]
                outs[h, c] = (_rms(o, gn_ref[:, vsl[h]]) * rs_ref[0, rows, vsl[h]].astype(F32)).astype(BF16)
            states.append(state)
        for c, rows in enumerate(chunks):
            o_ref[0, rows, :] = jnp.concatenate([outs[h, c] for h in heads], axis=-1)
        for h in heads:
            st_ref[h] = states[h]

    in_range = ok_ref[pl.program_id(0) * pl.num_programs(1) + pl.program_id(1)] != 0
    pl.when(in_range)(functools.partial(run, True))
    pl.when(jnp.logical_not(in_range))(functools.partial(run, False))


def _gla(q, k, lg, lg_sums, v, rs, gn, tc):
    nb, s, _ = q.shape
    lowest = jnp.min(lg_sums.reshape(nb, s // tc, (tc // GLA_CHUNK) * QK_DIM), axis=-1)
    in_range = (lowest > -GLA_FACTOR_RANGE).astype(jnp.int32).reshape(-1)

    def tok(width):
        return pl.BlockSpec((1, tc, width), lambda b, i, ok: (b, i, 0))

    return pl.pallas_call(
        functools.partial(_gla_body, n_chunks=tc // GLA_CHUNK),
        grid_spec=pltpu.PrefetchScalarGridSpec(
            num_scalar_prefetch=1,
            grid=(nb, s // tc),
            in_specs=[tok(QK_DIM), tok(QK_DIM), tok(QK_DIM), tok(V_DIM), tok(V_DIM),
                      pl.BlockSpec((1, V_DIM), lambda b, i, ok: (0, 0))],
            out_specs=tok(V_DIM),
            scratch_shapes=[pltpu.VMEM((GLA_HEADS, GLA_DK, GLA_DV), F32),
                            pltpu.VMEM((tc, QK_DIM), F32),
                            pltpu.VMEM((GLA_HEADS, tc, GLA_DV), F32)]),
        out_shape=jax.ShapeDtypeStruct((nb, s, V_DIM), BF16),
        compiler_params=_params("arbitrary", "arbitrary"),
        name="gla",
    )(in_range, q, k, lg, v, rs, gn.reshape(1, V_DIM))


def _route(logits, cnt, sub):
    ts = logits.shape[0]
    lane = lax.broadcasted_iota(jnp.int32, (ts, LANES), 1).astype(F32)
    ninf = -jnp.inf
    lgm = jnp.where(lane < N_GROUPS, logits, ninf)
    gmax = jnp.max(lgm, axis=-1, keepdims=True)
    gsel = jnp.min(jnp.where(lgm == gmax, lane, float(LANES)), axis=-1, keepdims=True)
    wg = 1.0 / jnp.sum(jnp.exp(lgm - gmax), axis=-1, keepdims=True)
    base = N_GROUPS + EXPERTS_PER_GROUP * gsel
    le = jnp.where(lane >= base, jnp.where(lane < base + EXPERTS_PER_GROUP, logits, ninf), ninf)
    v1 = jnp.max(le, axis=-1, keepdims=True)
    i1 = jnp.min(jnp.where(le == v1, lane, float(LANES)), axis=-1, keepdims=True)
    le2 = jnp.where(lane == i1, ninf, le)
    v2 = jnp.max(le2, axis=-1, keepdims=True)
    i2 = jnp.min(jnp.where(le2 == v2, lane, float(LANES)), axis=-1, keepdims=True)
    e21 = jnp.exp(v2 - v1)
    w1 = wg / (1.0 + e21)
    w2 = w1 * e21
    eid1 = i1 - N_GROUPS
    eid2 = i2 - N_GROUPS
    oh1 = jnp.where(lane == eid1, 1.0, 0.0)
    oh2 = jnp.where(lane == eid2, 1.0, 0.0)
    ohs = oh1 + oh2
    row = lax.broadcasted_iota(jnp.int32, (sub, sub), 0)
    col = lax.broadcasted_iota(jnp.int32, (sub, sub), 1)
    before = jnp.where(col < row, 1.0, 0.0).astype(BF16)
    tots = []
    for lo in range(0, ts, sub):
        piece = ohs[lo:lo + sub, :]
        tots.append(cnt + _bdot(before, piece.astype(BF16)))
        cnt = cnt + jnp.sum(piece, axis=0, keepdims=True)
    tot = jnp.concatenate(tots, axis=0)
    rank1 = jnp.sum(oh1 * tot, axis=-1, keepdims=True)
    rank2 = jnp.sum(oh2 * tot, axis=-1, keepdims=True)
    packed = jnp.where(lane == 0.0, eid1,
             jnp.where(lane == 1.0, eid2,
             jnp.where(lane == 2.0, w1,
             jnp.where(lane == 3.0, w2,
             jnp.where(lane == 4.0, rank1,
             jnp.where(lane == 5.0, rank2, 0.0))))))
    return packed, cnt


def _merge_body(z_ref, og_ref, gt_ref, x_ref, mod_ref, wdw_ref, bdw_ref, lng_ref, lnb_ref,
                wpw_ref, bpw_ref, wgo_ref, wout_ref, g2_ref, wr_ref, br_ref,
                h_ref, u2_ref, logit_ref, zbuf, zsh, u2t):
    ts = z_ref.shape[1]
    d = x_ref.shape[2]
    first_tile = pl.program_id(1) == 0

    @pl.when(first_tile)
    def _():
        zbuf[0:CONV_HALO, :] = jnp.zeros((CONV_HALO, CONV_DIM), F32)

    zbuf[CONV_HALO:CONV_HALO + ts, :] = z_ref[0]
    span = ts + CONV_HALO - SUBLANES
    for r in range(1, SUBLANES):
        zsh[r - 1] = zbuf[r:r + span, :]
    off = CONV_HALO - (CONV_WIDTH - 1)
    pieces = []
    for blk in range(ts // CONV_ROWS):
        acc = None
        for j in range(CONV_WIDTH):
            a, r = divmod(off + j, SUBLANES)
            lo = a * SUBLANES + blk * CONV_ROWS
            src = zbuf[lo:lo + CONV_ROWS, :] if r == 0 else zsh[r - 1, lo:lo + CONV_ROWS, :]
            term = src * jnp.concatenate([wdw_ref[j]] * (CONV_ROWS // SUBLANES), axis=0)
            acc = term if acc is None else acc + term
        conv = acc + bdw_ref[...]
        mu = jnp.mean(conv, axis=-1, keepdims=True)
        xc = conv - mu
        var = jnp.mean(xc * xc, axis=-1, keepdims=True)
        ln = xc * lax.rsqrt(var + EPS) * lng_ref[...] + lnb_ref[...]
        pieces.append((ln * _sigmoid(ln)).astype(BF16))
    zbuf[0:CONV_HALO, :] = zbuf[ts:ts + CONV_HALO, :]
    y_conv = _bdot(jnp.concatenate(pieces, axis=0), wpw_ref[...]) + bpw_ref[...]
    y_gla = _bdot(og_ref[0], wgo_ref[...])
    merged = gt_ref[0, :, 0:d].astype(F32) * y_conv + gt_ref[0, :, d:2 * d].astype(F32) * y_gla
    y = _bdot(merged.astype(BF16), wout_ref[...])
    h = x_ref[0] + mod_ref[0, 2:3, :] * y
    h_ref[0] = h
    u2 = _rms(h, g2_ref[...]) * (1.0 + mod_ref[0, 4:5, :]) + mod_ref[0, 3:4, :]
    _pack_rows(u2_ref, u2, u2t)
    logit_ref[0] = _dot3(u2, wr_ref[...]) + br_ref[...]


def _merge(z, og, gt, x, mod3, w_dw, b_dw, ln_g, ln_b, w_pw, b_pw, w_go, w_out, g2, w_rg, b_rg, w_re, b_re, ts):
    nb, s, d = x.shape
    npad = LANES - N_GROUPS - N_EXPERTS
    wr = jnp.pad(jnp.concatenate([w_rg, w_re], axis=1), ((0, 0), (0, npad)))
    br = jnp.pad(jnp.concatenate([b_rg, b_re]), (0, npad)).reshape(1, LANES)

    def tok(width):
        return pl.BlockSpec((1, ts, width), lambda b, i: (b, i, 0))

    def const(shape):
        return pl.BlockSpec(shape, lambda b, i: (0,) * len(shape))

    def row(v):
        return v.reshape(1, v.shape[-1])

    return pl.pallas_call(
        _merge_body,
        grid=(nb, s // ts),
        in_specs=[tok(CONV_DIM), tok(V_DIM), tok(2 * d), tok(d),
                  pl.BlockSpec((1, N_MOD, d), lambda b, i: (b, 0, 0)),
                  const((CONV_WIDTH, SUBLANES, CONV_DIM)), const((1, CONV_DIM)), const((1, CONV_DIM)), const((1, CONV_DIM)),
                  const((CONV_DIM, d)), const((1, d)), const((V_DIM, d)), const((d, d)), const((1, d)),
                  const((d, LANES)), const((1, LANES))],
        out_specs=[tok(d), pl.BlockSpec((ts * PACK_ROWS, LANES), lambda b, i: (b * (s // ts) + i, 0)), tok(LANES)],
        out_shape=[jax.ShapeDtypeStruct((nb, s, d), F32),
                   jax.ShapeDtypeStruct((nb * s * PACK_ROWS, LANES), jnp.uint32),
                   jax.ShapeDtypeStruct((nb, s, LANES), F32)],
        scratch_shapes=[pltpu.VMEM((CONV_HALO + ts, CONV_DIM), F32),
                        pltpu.VMEM((SUBLANES - 1, CONV_HALO + ts - SUBLANES, CONV_DIM), F32),
                        pltpu.VMEM((ts * SUBLANES, LANES), F32)],
        compiler_params=_params("arbitrary", "arbitrary"),
        name="merge",
    )(z, og, gt, x, mod3, jnp.broadcast_to(w_dw.reshape(CONV_WIDTH, 1, CONV_DIM), (CONV_WIDTH, SUBLANES, CONV_DIM)), row(b_dw), row(ln_g), row(ln_b),
      w_pw.astype(BF16), row(b_pw), w_go.astype(BF16), w_out.astype(BF16), row(g2), wr, br)


def _route_body(logit_ref, route_ref, routet_ref, cnt_ref, cnt_sc, *, sub):
    @pl.when(pl.program_id(0) == 0)
    def _():
        cnt_sc[...] = jnp.zeros_like(cnt_sc)

    packed, cnt = _route(logit_ref[...], cnt_sc[...], sub)
    route_ref[...] = packed
    routet_ref[...] = packed.T[0:SUBLANES, :]
    cnt_sc[...] = cnt
    cnt_ref[...] = jnp.broadcast_to(cnt, cnt_ref.shape)


def _route_call(logits, tr, sub):
    t = logits.shape[0]
    return pl.pallas_call(
        functools.partial(_route_body, sub=sub),
        grid=(t // tr,),
        in_specs=[pl.BlockSpec((tr, LANES), lambda i: (i, 0))],
        out_specs=[pl.BlockSpec((tr, LANES), lambda i: (i, 0)), pl.BlockSpec((SUBLANES, tr), lambda i: (0, i)),
                   pl.BlockSpec((SUBLANES, LANES), lambda i: (0, 0))],
        out_shape=[jax.ShapeDtypeStruct((t, LANES), F32), jax.ShapeDtypeStruct((SUBLANES, t), F32),
                   jax.ShapeDtypeStruct((SUBLANES, LANES), F32)],
        scratch_shapes=[pltpu.VMEM((1, LANES), F32)],
        compiler_params=_params("arbitrary"),
        name="route",
    )(logits)


def _row_copy(src, i, dst, j, sem):
    return pltpu.make_async_copy(src.at[pl.ds(pl.multiple_of(i, SUBLANES), SUBLANES), :],
                                 dst.at[pl.ds(pl.multiple_of(j, SUBLANES), SUBLANES), :], sem)


def _invert_body(lo_ref, hi_ref, slot_ref, src_ref):
    i = pl.program_id(0)
    ts = slot_ref.shape[-1] // TOP_K

    @pl.when(i == 0)
    def _():
        for e in range(lo_ref.shape[0]):
            lo = lo_ref[e]
            hi = hi_ref[e]

            def clear(p, carry, lo=lo, hi=hi):
                for j in range(CLEAR_UNROLL):
                    src_ref[jnp.minimum(lo + p * CLEAR_UNROLL + j, hi - 1)] = 0
                return carry

            trips = lax.shift_right_logical(hi - lo + (CLEAR_UNROLL - 1), CLEAR_UNROLL.bit_length() - 1)
            lax.fori_loop(0, trips, clear, 0)

    for r in range(ts):
        row = (i * ts + r) * PACK_ROWS
        for k in range(TOP_K):
            src_ref[slot_ref[0, 0, k * ts + r]] = row


def _invert(fill_lo, fill_hi, slots, n_slots):
    nt, _, width = slots.shape
    return pl.pallas_call(
        _invert_body,
        grid_spec=pltpu.PrefetchScalarGridSpec(
            num_scalar_prefetch=2,
            grid=(nt,),
            in_specs=[pl.BlockSpec((1, 1, width), lambda i, lo, hi: (i, 0, 0), memory_space=pltpu.SMEM)],
            out_specs=pl.BlockSpec(memory_space=pltpu.SMEM)),
        out_shape=jax.ShapeDtypeStruct((n_slots,), jnp.int32),
        compiler_params=_params("arbitrary"),
        name="invert",
    )(fill_lo, fill_hi, slots)


def _expert_body(te_ref, nu_ref, seg_ref, nxt_ref, src_ref, u2_ref, w1_ref, w3_ref, w2_ref, ys_ref,
                 u2v, xg, xt, w1f, w3f, w2f, w1b, w3b, w2b, sem, wsem):
    i = pl.program_id(0)
    tmx = src_ref.shape[-1]
    expert = te_ref[i]
    first = i == 0
    changed = jnp.logical_or(first, expert != te_ref[jnp.maximum(i - 1, 0)])
    slot = lax.rem(seg_ref[i], 2)

    def weight_copies(e, s):
        return [pltpu.make_async_copy(src.at[e], dst.at[s], wsem.at[s])
                for src, dst in ((w1_ref, w1f), (w3_ref, w3f), (w2_ref, w2f))]

    load = pltpu.make_async_copy(u2_ref, u2v, sem)

    @pl.when(first)
    def _():
        load.start()
        for cp in weight_copies(expert, 0):
            cp.start()

    @pl.when(changed)
    def _():
        for cp in weight_copies(expert, slot):
            cp.wait()
        w1b[...] = w1f[slot].astype(BF16)
        w3b[...] = w3f[slot].astype(BF16)
        w2b[...] = w2f[slot].astype(BF16)

    @pl.when(jnp.logical_and(changed, nxt_ref[i] != expert))
    def _():
        for cp in weight_copies(nxt_ref[i], 1 - slot):
            cp.start()

    @pl.when(first)
    def _():
        load.wait()

    @pl.when(i < nu_ref[0])
    def _():
        for r in range(tmx):
            row = pl.multiple_of(src_ref[0, 0, r], PACK_ROWS)
            xg[r * PACK_ROWS:(r + 1) * PACK_ROWS, :] = u2v[pl.ds(row, PACK_ROWS), :]
        x = _unpack_rows(xg, tmx, xt)
        h1 = _bdot(x, w1b[...])
        h3 = _bdot(x, w3b[...])
        hid = (h1 * _sigmoid(h1) * h3).astype(BF16)
        _rows_to_tiles(ys_ref, _bdot(hid, w2b[...]))

    @pl.when(i >= nu_ref[0])
    def _():
        ys_ref[...] = jnp.zeros_like(ys_ref)


def _experts(tile_expert, n_used, run_index, next_expert, src_rows, u2p, w1, w3, w2):
    n_tiles, _, tmx = src_rows.shape
    ne, d, f = w1.shape
    hbm = pl.BlockSpec(memory_space=pl.ANY)
    return pl.pallas_call(
        _expert_body,
        grid_spec=pltpu.PrefetchScalarGridSpec(
            num_scalar_prefetch=4,
            grid=(n_tiles,),
            in_specs=[pl.BlockSpec((1, 1, tmx), lambda i, *_: (i, 0, 0), memory_space=pltpu.SMEM),
                      hbm, hbm, hbm, hbm],
            out_specs=pl.BlockSpec((tmx * SUBLANES, LANES), lambda i, *_: (i, 0)),
            scratch_shapes=[pltpu.VMEM(u2p.shape, jnp.uint32),
                            pltpu.VMEM((tmx * PACK_ROWS, LANES), jnp.uint32),
                            pltpu.VMEM((tmx * SUBLANES, LANES), F32),
                            pltpu.VMEM((2, d, f), F32), pltpu.VMEM((2, d, f), F32), pltpu.VMEM((2, f, d), F32),
                            pltpu.VMEM((d, f), BF16), pltpu.VMEM((d, f), BF16), pltpu.VMEM((f, d), BF16),
                            pltpu.SemaphoreType.DMA(()), pltpu.SemaphoreType.DMA((2,))]),
        out_shape=jax.ShapeDtypeStruct((n_tiles * tmx * SUBLANES, LANES), F32),
        compiler_params=_params("arbitrary"),
        name="experts",
    )(tile_expert, n_used, run_index, next_expert, src_rows, u2p, w1, w3, w2)


def _final_body(p_ref, pn_ref, h_ref, route_ref, mod_ref, modf_ref, gf_ref, ys_ref, o_ref, y_buf, sem):
    ts = h_ref.shape[1]
    step = pl.program_id(0) * pl.num_programs(1) + pl.program_id(1)
    n_steps = pl.num_programs(0) * pl.num_programs(1)
    cur = lax.rem(step, 2)

    def request_rows(ref, half):
        for k in range(TOP_K):
            for r in range(ts):
                _row_copy(ys_ref, ref[0, 0, k * ts + r], y_buf.at[half, k], r * SUBLANES,
                          sem.at[half]).start(priority=r % 2)

    @pl.when(step == 0)
    def _():
        request_rows(p_ref, 0)

    @pl.when(step + 1 < n_steps)
    def _():
        request_rows(pn_ref, 1 - cur)

    for k in range(TOP_K):
        pltpu.make_async_copy(ys_ref.at[pl.ds(0, ts * SUBLANES), :], y_buf.at[cur, k], sem.at[cur]).wait()

    route = route_ref[0]
    y2 = (route[:, 2:3] * _tiles_to_rows(y_buf.at[cur, 0], ts)
          + route[:, 3:4] * _tiles_to_rows(y_buf.at[cur, 1], ts))
    h = h_ref[0] + mod_ref[0, 5:6, :] * y2
    o_ref[0] = _rms(h, gf_ref[...]) * (1.0 + modf_ref[0, 1:2, :]) + modf_ref[0, 0:1, :]


def _final(slot_rows, h, route, mod3, modf3, gf, ys, ts):
    nb, s, d = h.shape
    nt = s // ts
    last = nb * nt - 1

    def tok(width):
        return pl.BlockSpec((1, ts, width), lambda b, i: (b, i, 0))

    def slots(ahead):
        return pl.BlockSpec((1, 1, TOP_K * ts), lambda b, i: (jnp.minimum(b * nt + i + ahead, last), 0, 0),
                            memory_space=pltpu.SMEM)

    return pl.pallas_call(
        _final_body,
        grid=(nb, nt),
        in_specs=[slots(0), slots(1), tok(d), tok(LANES),
                  pl.BlockSpec((1, N_MOD, d), lambda b, i: (b, 0, 0)),
                  pl.BlockSpec((1, 2, d), lambda b, i: (b, 0, 0)),
                  pl.BlockSpec((1, d), lambda b, i: (0, 0)),
                  pl.BlockSpec(memory_space=pl.ANY)],
        out_specs=tok(d),
        out_shape=jax.ShapeDtypeStruct((nb, s, d), F32),
        scratch_shapes=[pltpu.VMEM((2, TOP_K, ts * SUBLANES, LANES), F32), pltpu.SemaphoreType.DMA((2,))],
        compiler_params=_params("arbitrary", "arbitrary"),
        name="final",
    )(slot_rows, slot_rows, h, route, mod3, modf3, gf.reshape(1, d), ys)


def _plan(routet, cnt, ts, tmx, n_tiles):
    t = routet.shape[1]
    counts = cnt[0, :N_EXPERTS].astype(jnp.int32)
    tiles = (counts + (tmx - 1)) // tmx
    tile_end = jnp.cumsum(tiles)
    offs = ((tile_end - tiles) * tmx).astype(jnp.int32)
    n_used = tile_end[-1:]
    tile_ids = jnp.minimum(jnp.arange(n_tiles, dtype=jnp.int32), n_used[0] - 1)
    tile_expert = jnp.sum((tile_ids[:, None] >= tile_end[None, :]).astype(jnp.int32), axis=1)
    run_index = jnp.cumsum(jnp.concatenate([jnp.zeros((1,), jnp.int32),
                                            (tile_expert[1:] != tile_expert[:-1]).astype(jnp.int32)]))
    same_run = run_index[:, None] + 1 == run_index[None, :]
    has_next = jnp.any(same_run, axis=1)
    next_expert = jnp.where(has_next, jnp.max(jnp.where(same_run, tile_expert[None, :], 0), axis=1), tile_expert)

    eid = routet[0:2].astype(jnp.int32)
    experts = jnp.arange(N_EXPERTS, dtype=jnp.int32)[:, None, None]
    slot = routet[4:6].astype(jnp.int32) + jnp.sum(jnp.where(eid[None] == experts, offs[:, None, None], 0), axis=0)
    slots = slot.reshape(TOP_K, t // ts, ts).transpose(1, 0, 2).reshape(t // ts, 1, TOP_K * ts)
    fill_lo = jnp.concatenate([offs + counts, tile_end[-1:] * tmx]).astype(jnp.int32)
    fill_hi = jnp.concatenate([tile_end * tmx, jnp.full((1,), n_tiles * tmx, jnp.int32)]).astype(jnp.int32)
    src = _invert(fill_lo, fill_hi, slots, n_tiles * tmx)
    tables = [a.astype(jnp.int32) for a in (tile_expert, n_used, run_index, next_expert)]
    return slots * SUBLANES, src.reshape(n_tiles, 1, tmx), tables


def kernel(x, c, w_ada, b_ada, g_norm1, w_in, w_dw, b_dw, g_conv_ln, b_conv_ln, w_conv_pw, b_conv_pw,
           w_a2, b_a2, g_gla_norm, w_gla_o, w_out, g_norm2, w_router_g, b_router_g, w_router_e,
           b_router_e, w1, w3, w2, w_ada_f, b_ada_f, g_final):
    nb, s, d = x.shape
    assert w_ada.shape[0] == 1, "single-layer block"
    assert d == 2 * PACK_ROWS * LANES, "packed token rows assume D_MODEL = 1024"
    tm = min(512, s)
    tc = min(256, s)
    ts = min(256, s)
    tmx = 256
    t = nb * s
    n_tiles = (t * TOP_K) // tmx + N_EXPERTS
    mod3 = _ada(c, w_ada[0], b_ada[0]).reshape(nb, N_MOD, d)
    modf3 = _ada(c, w_ada_f, b_ada_f).reshape(nb, 2, d)
    z, q, k, v, rs, lg, gt, lg_sums = _proj(x, mod3, g_norm1[0], w_in[0], w_a2[0], b_a2[0], tm)
    og = _gla(q, k, lg, lg_sums, v, rs, g_gla_norm[0], tc)
    h, u2, logits = _merge(z, og, gt, x, mod3, w_dw[0], b_dw[0], g_conv_ln[0], b_conv_ln[0],
                           w_conv_pw[0], b_conv_pw[0], w_gla_o[0], w_out[0], g_norm2[0],
                           w_router_g[0], b_router_g[0], w_router_e[0], b_router_e[0], ts)
    route, routet, cnt = _route_call(logits.reshape(t, LANES), min(ROUTE_ROWS, t), ts)
    slot_rows, src_rows, tables = _plan(routet, cnt, ts, tmx, n_tiles)
    ys = _experts(*tables, src_rows, u2, w1[0], w3[0], w2[0])
    return _final(slot_rows, h, route.reshape(nb, s, LANES), mod3, modf3, g_final, ys, ts)
```

```python
import functools

import jax
import jax.numpy as jnp
from jax import lax
from jax.experimental import pallas as pl
from jax.experimental.pallas import tpu as pltpu

F32 = jnp.float32
BF16 = jnp.bfloat16

EPS = 1e-6
CONV_DIM = 512
CONV_WIDTH = 31
GLA_HEADS = 4
GLA_DK = 128
GLA_DV = 256
GLA_LOWRANK = 16
GLA_TAU = 16.0
QK_DIM = GLA_HEADS * GLA_DK
V_DIM = GLA_HEADS * GLA_DV
N_GROUPS = 4
EXPERTS_PER_GROUP = 8
N_EXPERTS = N_GROUPS * EXPERTS_PER_GROUP
TOP_K = 2
N_MOD = 6

LANES = 128
SUBLANES = 8
PACK_ROWS = 4
CONV_ROWS = 32
ROUTE_ROWS = 2048
CLEAR_UNROLL = 16
CONV_HALO = 32
GLA_CHUNK = 128
GLA_FACTOR_RANGE = 60.0
VMEM_LIMIT = 56 * 1024 * 1024


def _bdot(a, b):
    return jnp.dot(a, b, preferred_element_type=F32)


def _split(a):
    hi = a.astype(BF16)
    lo = (a - hi.astype(F32)).astype(BF16)
    return hi, lo


def _dot3(a, b):
    ah, al = _split(a)
    bh, bl = _split(b)
    return _bdot(ah, bh) + (_bdot(ah, bl) + _bdot(al, bh))


def _sigmoid(x):
    return 1.0 / (1.0 + jnp.exp(-x))


def _rms(x, g):
    ms = jnp.mean(x * x, axis=-1, keepdims=True)
    return x * lax.rsqrt(ms + EPS) * g


def _rows_to_tiles(ref, val):
    n = val.shape[0]
    for j in range(val.shape[1] // LANES):
        ref[pl.ds(j, n, stride=SUBLANES), :] = val[:, j * LANES:(j + 1) * LANES]


def _tiles_to_rows(ref, n):
    return jnp.concatenate([ref[pl.ds(j, n, stride=SUBLANES), :] for j in range(SUBLANES)], axis=-1)


def _pack_rows(ref, val, tiles):
    _rows_to_tiles(tiles, val)
    ref[...] = pltpu.bitcast(tiles[...].astype(BF16), jnp.uint32)


def _unpack_rows(ref, n, tiles):
    tiles[...] = pltpu.bitcast(ref[...], BF16).astype(F32)
    return _tiles_to_rows(tiles, n).astype(BF16)


def _params(*sem):
    return pltpu.CompilerParams(dimension_semantics=sem, vmem_limit_bytes=VMEM_LIMIT)


def _ada_body(c_ref, w_ref, b_ref, o_ref):
    c = c_ref[...]
    o_ref[...] = _dot3(c * _sigmoid(c), w_ref[...]) + b_ref[...]


def _ada(c, w, b, tn=1024):
    nb, d = c.shape
    n = w.shape[1]
    return pl.pallas_call(
        _ada_body,
        grid=(n // tn,),
        in_specs=[pl.BlockSpec((nb, d), lambda j: (0, 0)),
                  pl.BlockSpec((d, tn), lambda j: (0, j)),
                  pl.BlockSpec((1, tn), lambda j: (0, j))],
        out_specs=pl.BlockSpec((nb, tn), lambda j: (0, j)),
        out_shape=jax.ShapeDtypeStruct((nb, n), F32),
        compiler_params=_params("arbitrary"),
        name="ada",
    )(c, w, b.reshape(1, n))


def _proj_body(x_ref, mod_ref, g1_ref, wm_ref, wa1_ref, wg_ref, wa2_ref, ba2_ref,
               z_ref, q_ref, k_ref, v_ref, rs_ref, lg_ref, gt_ref, ls_ref):
    x = x_ref[0]
    u = (_rms(x, g1_ref[...]) * (1.0 + mod_ref[0, 1:2, :]) + mod_ref[0, 0:1, :]).astype(BF16)
    c0 = 2 * CONV_DIM
    c1 = c0 + 2 * QK_DIM
    c2 = c1 + V_DIM
    c3 = c2 + V_DIM
    pc = _bdot(u, wm_ref[:, 0:c0])
    z_ref[0] = pc[:, :CONV_DIM] * _sigmoid(pc[:, CONV_DIM:])
    qk = _bdot(u, wm_ref[:, c0:c1])
    q_ref[0] = qk[:, :QK_DIM] * (GLA_DK ** -0.5)
    k_ref[0] = qk[:, QK_DIM:]
    v_ref[0] = _bdot(u, wm_ref[:, c1:c2]).astype(BF16)
    r = _bdot(u, wm_ref[:, c2:c3])
    rs_ref[0] = (r * _sigmoid(r)).astype(BF16)
    a1 = _bdot(u, wa1_ref[...])
    xg = _dot3(a1, wa2_ref[...]) + ba2_ref[...]
    lg = (jnp.minimum(xg, 0.0) - jnp.log1p(jnp.exp(-jnp.abs(xg)))) * (1.0 / GLA_TAU)
    lg_ref[0] = lg
    ls_ref[0, 0] = jnp.concatenate([jnp.sum(lg[lo:lo + GLA_CHUNK, :], axis=0, keepdims=True)
                                    for lo in range(0, lg.shape[0], GLA_CHUNK)], axis=0)
    gt_ref[0] = _sigmoid(_bdot(u, wg_ref[...])).astype(BF16)


def _wsplit_body(wt_ref, wm_ref, wa1_ref, wg_ref, buf, sem, *, c3):
    cw = buf.shape[1]
    n_main = c3 // cw
    j = pl.program_id(0)
    cur = lax.rem(j, 2)

    def chunk(step, half):
        start = jnp.where(step <= n_main, step * cw, c3 + GLA_LOWRANK + (step - n_main - 1) * cw)
        return pltpu.make_async_copy(wt_ref.at[pl.ds(pl.multiple_of(start, SUBLANES), cw), :], buf.at[half],
                                     sem.at[half])

    @pl.when(j == 0)
    def _():
        chunk(j, 0).start()

    @pl.when(j + 1 < pl.num_programs(0))
    def _():
        chunk(j + 1, 1 - cur).start()

    chunk(j, cur).wait()

    @pl.when(j < n_main)
    def _():
        wm_ref[...] = buf[cur].T.astype(BF16)

    @pl.when(j == n_main)
    def _():
        lane = lax.broadcasted_iota(jnp.int32, wa1_ref.shape, 1)
        wa1_ref[...] = jnp.where(lane < GLA_LOWRANK, buf[cur, 0:LANES, :].T, 0.0).astype(BF16)

    @pl.when(j > n_main)
    def _():
        wg_ref[...] = buf[cur].T.astype(BF16)


def _wsplit(wt, c3, cw=512):
    n, d = wt.shape
    ng = n - c3 - GLA_LOWRANK
    n_main = c3 // cw
    return pl.pallas_call(
        functools.partial(_wsplit_body, c3=c3),
        grid=(n_main + 1 + ng // cw,),
        in_specs=[pl.BlockSpec(memory_space=pl.ANY)],
        out_specs=[pl.BlockSpec((d, cw), lambda j: (0, jnp.minimum(j, n_main - 1))),
                   pl.BlockSpec((d, LANES), lambda j: (0, 0)),
                   pl.BlockSpec((d, cw), lambda j: (0, jnp.clip(j - n_main - 1, 0, ng // cw - 1)))],
        out_shape=[jax.ShapeDtypeStruct((d, c3), BF16), jax.ShapeDtypeStruct((d, LANES), BF16),
                   jax.ShapeDtypeStruct((d, ng), BF16)],
        scratch_shapes=[pltpu.VMEM((2, cw, d), F32), pltpu.SemaphoreType.DMA((2,))],
        compiler_params=_params("arbitrary"),
        name="wsplit",
    )(wt)


def _proj(x, mod3, g1, w_in, w_a2, b_a2, tm):
    nb, s, d = x.shape
    c3 = 2 * CONV_DIM + 2 * QK_DIM + 2 * V_DIM
    wm, wa1, wg = _wsplit(jnp.swapaxes(w_in, 0, 1), c3)
    wa2 = jnp.pad(w_a2, ((0, LANES - GLA_LOWRANK), (0, 0)))
    ng = wg.shape[1]

    def tok(width):
        return pl.BlockSpec((1, tm, width), lambda b, i: (b, i, 0))

    def const(shape):
        return pl.BlockSpec(shape, lambda b, i: (0,) * len(shape))

    def out(width, dt):
        return jax.ShapeDtypeStruct((nb, s, width), dt)

    return pl.pallas_call(
        _proj_body,
        grid=(nb, s // tm),
        in_specs=[tok(d),
                  pl.BlockSpec((1, N_MOD, d), lambda b, i: (b, 0, 0)),
                  const((1, d)), const(wm.shape), const(wa1.shape), const(wg.shape),
                  const(wa2.shape), const((1, QK_DIM))],
        out_specs=[tok(CONV_DIM), tok(QK_DIM), tok(QK_DIM), tok(V_DIM), tok(V_DIM), tok(QK_DIM), tok(ng),
                   pl.BlockSpec((1, 1, tm // GLA_CHUNK, QK_DIM), lambda b, i: (b, i, 0, 0))],
        out_shape=[out(CONV_DIM, F32), out(QK_DIM, F32), out(QK_DIM, F32), out(V_DIM, BF16),
                   out(V_DIM, BF16), out(QK_DIM, F32), out(ng, BF16),
                   jax.ShapeDtypeStruct((nb, s // tm, tm // GLA_CHUNK, QK_DIM), F32)],
        compiler_params=_params("arbitrary", "arbitrary"),
        name="proj",
    )(x, mod3, g1.reshape(1, d), wm, wa1, wg, wa2, b_a2.reshape(1, QK_DIM))


def _gla_body(ok_ref, q_ref, k_ref, lg_ref, v_ref, rs_ref, gn_ref, o_ref, st_ref, b_s, oi_s, *, n_chunks):
    cl = GLA_CHUNK

    @pl.when(pl.program_id(1) == 0)
    def _():
        st_ref[...] = jnp.zeros_like(st_ref)

    row = lax.broadcasted_iota(jnp.int32, (cl, cl), 0)
    col = lax.broadcasted_iota(jnp.int32, (cl, cl), 1)
    causal = col <= row
    tri = jnp.where(causal, 1.0, 0.0).astype(BF16)
    chunks = [slice(c * cl, (c + 1) * cl) for c in range(n_chunks)]
    heads = range(GLA_HEADS)
    ksl = [slice(h * GLA_DK, (h + 1) * GLA_DK) for h in heads]
    vsl = [slice(h * GLA_DV, (h + 1) * GLA_DV) for h in heads]

    def cumulative(rows):
        gh, gl = _split(lg_ref[0, rows, :])
        return _bdot(tri, gh) + _bdot(tri, gl)

    def intra_pairwise(h, c):
        kf = k_ref[0, chunks[c], ksl[h]]
        vf = v_ref[0, chunks[c], vsl[h]].astype(F32)
        bh = b_s[chunks[c], ksl[h]]
        key = lax.broadcasted_iota(jnp.int32, (cl, 1), 0)

        def group(g, carry):
            r0 = pl.multiple_of(g * SUBLANES, SUBLANES)
            q8 = q_ref[0, pl.ds(c * cl + r0, SUBLANES), ksl[h]]
            b8 = b_s[pl.ds(c * cl + r0, SUBLANES), ksl[h]]
            out_rows = []
            for r in range(SUBLANES):
                diff = jnp.where(key <= r0 + r, b8[r:r + 1, :] - bh, -jnp.inf)
                att = jnp.sum(jnp.exp(diff) * kf * q8[r:r + 1, :], axis=-1, keepdims=True)
                out_rows.append(jnp.sum(att * vf, axis=0, keepdims=True))
            oi_s[h, pl.ds(c * cl + r0, SUBLANES), :] = jnp.concatenate(out_rows, axis=0)
            return carry

        lax.fori_loop(0, cl // SUBLANES, group, 0)
        return oi_s[h, chunks[c], :]

    def run(factored):
        cums = [cumulative(rows) for rows in chunks]
        if not factored:
            for rows, b in zip(chunks, cums):
                b_s[rows, :] = b
        qes, kts, bts, klts, decays = [], [], [], [], []
        for rows, b in zip(chunks, cums):
            qes.append((q_ref[0, rows, :] * jnp.exp(b)).astype(BF16))
            kt = k_ref[0, rows, :].T
            bt = b.T
            bl = bt[:, cl - 1:cl]
            kts.append(kt)
            bts.append(bt)
            klts.append((kt * jnp.exp(bl - bt)).astype(BF16))
            decays.append(jnp.exp(bl))
        intra = {}
        if factored:
            kets = [(kt * jnp.exp(-bt)).astype(BF16) for kt, bt in zip(kts, bts)]
            atts = {}
            for c in range(n_chunks):
                for h in heads:
                    att = _bdot(qes[c][:, ksl[h]], kets[c][ksl[h], :])
                    atts[h, c] = jnp.where(causal, att, 0.0).astype(BF16)
            for c, rows in enumerate(chunks):
                for h in heads:
                    intra[h, c] = _bdot(atts[h, c], v_ref[0, rows, vsl[h]])
        else:
            for c in range(n_chunks):
                for h in heads:
                    intra[h, c] = intra_pairwise(h, c)
        updates = {}
        for c, rows in enumerate(chunks):
            for h in heads:
                updates[h, c] = _bdot(klts[c][ksl[h], :], v_ref[0, rows, vsl[h]])

        outs = {}
        states = []
        for h in heads:
            state = st_ref[h]
            for c, rows in enumerate(chunks):
                o = intra[h, c] + _bdot(qes[c][:, ksl[h]], state.astype(BF16))
                state = decays[c][ksl[h], :] * state + updates[h, c]
                outs[h, c] = (_rms(o, gn_ref[:, vsl[h]]) * rs_ref[0, rows, vsl[h]].astype(F32)).astype(BF16)
            states.append(state)
        for c, rows in enumerate(chunks):
            o_ref[0, rows, :] = jnp.concatenate([outs[h, c] for h in heads], axis=-1)
        for h in heads:
            st_ref[h] = states[h]

    in_range = ok_ref[pl.program_id(0) * pl.num_programs(1) + pl.program_id(1)] != 0
    pl.when(in_range)(functools.partial(run, True))
    pl.when(jnp.logical_not(in_range))(functools.partial(run, False))


def _gla(q, k, lg, lg_sums, v, rs, gn, tc):
    nb, s, _ = q.shape
    lowest = jnp.min(lg_sums.reshape(nb, s // tc, (tc // GLA_CHUNK) * QK_DIM), axis=-1)
    in_range = (lowest > -GLA_FACTOR_RANGE).astype(jnp.int32).reshape(-1)

    def tok(width):
        return pl.BlockSpec((1, tc, width), lambda b, i, ok: (b, i, 0))

    return pl.pallas_call(
        functools.partial(_gla_body, n_chunks=tc // GLA_CHUNK),
        grid_spec=pltpu.PrefetchScalarGridSpec(
            num_scalar_prefetch=1,
            grid=(nb, s // tc),
            in_specs=[tok(QK_DIM), tok(QK_DIM), tok(QK_DIM), tok(V_DIM), tok(V_DIM),
                      pl.BlockSpec((1, V_DIM), lambda b, i, ok: (0, 0))],
            out_specs=tok(V_DIM),
            scratch_shapes=[pltpu.VMEM((GLA_HEADS, GLA_DK, GLA_DV), F32),
                            pltpu.VMEM((tc, QK_DIM), F32),
                            pltpu.VMEM((GLA_HEADS, tc, GLA_DV), F32)]),
        out_shape=jax.ShapeDtypeStruct((nb, s, V_DIM), BF16),
        compiler_params=_params("arbitrary", "arbitrary"),
        name="gla",
    )(in_range, q, k, lg, v, rs, gn.reshape(1, V_DIM))


def _route(logits, cnt, sub):
    ts = logits.shape[0]
    lane = lax.broadcasted_iota(jnp.int32, (ts, LANES), 1).astype(F32)
    ninf = -jnp.inf
    lgm = jnp.where(lane < N_GROUPS, logits, ninf)
    gmax = jnp.max(lgm, axis=-1, keepdims=True)
    gsel = jnp.min(jnp.where(lgm == gmax, lane, float(LANES)), axis=-1, keepdims=True)
    wg = 1.0 / jnp.sum(jnp.exp(lgm - gmax), axis=-1, keepdims=True)
    base = N_GROUPS + EXPERTS_PER_GROUP * gsel
    le = jnp.where(lane >= base, jnp.where(lane < base + EXPERTS_PER_GROUP, logits, ninf), ninf)
    v1 = jnp.max(le, axis=-1, keepdims=True)
    i1 = jnp.min(jnp.where(le == v1, lane, float(LANES)), axis=-1, keepdims=True)
    le2 = jnp.where(lane == i1, ninf, le)
    v2 = jnp.max(le2, axis=-1, keepdims=True)
    i2 = jnp.min(jnp.where(le2 == v2, lane, float(LANES)), axis=-1, keepdims=True)
    e21 = jnp.exp(v2 - v1)
    w1 = wg / (1.0 + e21)
    w2 = w1 * e21
    eid1 = i1 - N_GROUPS
    eid2 = i2 - N_GROUPS
    oh1 = jnp.where(lane == eid1, 1.0, 0.0)
    oh2 = jnp.where(lane == eid2, 1.0, 0.0)
    ohs = oh1 + oh2
    row = lax.broadcasted_iota(jnp.int32, (sub, sub), 0)
    col = lax.broadcasted_iota(jnp.int32, (sub, sub), 1)
    before = jnp.where(col < row, 1.0, 0.0).astype(BF16)
    tots = []
    for lo in range(0, ts, sub):
        piece = ohs[lo:lo + sub, :]
        tots.append(cnt + _bdot(before, piece.astype(BF16)))
        cnt = cnt + jnp.sum(piece, axis=0, keepdims=True)
    tot = jnp.concatenate(tots, axis=0)
    rank1 = jnp.sum(oh1 * tot, axis=-1, keepdims=True)
    rank2 = jnp.sum(oh2 * tot, axis=-1, keepdims=True)
    packed = jnp.where(lane == 0.0, eid1,
             jnp.where(lane == 1.0, eid2,
             jnp.where(lane == 2.0, w1,
             jnp.where(lane == 3.0, w2,
             jnp.where(lane == 4.0, rank1,
             jnp.where(lane == 5.0, rank2, 0.0))))))
    return packed, cnt


def _merge_body(z_ref, og_ref, gt_ref, x_ref, mod_ref, wdw_ref, bdw_ref, lng_ref, lnb_ref,
                wpw_ref, bpw_ref, wgo_ref, wout_ref, g2_ref, wr_ref, br_ref,
                h_ref, u2_ref, logit_ref, zbuf, zsh, u2t):
    ts = z_ref.shape[1]
    d = x_ref.shape[2]
    first_tile = pl.program_id(1) == 0

    @pl.when(first_tile)
    def _():
        zbuf[0:CONV_HALO, :] = jnp.zeros((CONV_HALO, CONV_DIM), F32)

    zbuf[CONV_HALO:CONV_HALO + ts, :] = z_ref[0]
    span = ts + CONV_HALO - SUBLANES
    for r in range(1, SUBLANES):
        zsh[r - 1] = zbuf[r:r + span, :]
    off = CONV_HALO - (CONV_WIDTH - 1)
    pieces = []
    for blk in range(ts // CONV_ROWS):
        acc = None
        for j in range(CONV_WIDTH):
            a, r = divmod(off + j, SUBLANES)
            lo = a * SUBLANES + blk * CONV_ROWS
            src = zbuf[lo:lo + CONV_ROWS, :] if r == 0 else zsh[r - 1, lo:lo + CONV_ROWS, :]
            term = src * jnp.concatenate([wdw_ref[j]] * (CONV_ROWS // SUBLANES), axis=0)
            acc = term if acc is None else acc + term
        conv = acc + bdw_ref[...]
        mu = jnp.mean(conv, axis=-1, keepdims=True)
        xc = conv - mu
        var = jnp.mean(xc * xc, axis=-1, keepdims=True)
        ln = xc * lax.rsqrt(var + EPS) * lng_ref[...] + lnb_ref[...]
        pieces.append((ln * _sigmoid(ln)).astype(BF16))
    zbuf[0:CONV_HALO, :] = zbuf[ts:ts + CONV_HALO, :]
    y_conv = _bdot(jnp.concatenate(pieces, axis=0), wpw_ref[...]) + bpw_ref[...]
    y_gla = _bdot(og_ref[0], wgo_ref[...])
    merged = gt_ref[0, :, 0:d].astype(F32) * y_conv + gt_ref[0, :, d:2 * d].astype(F32) * y_gla
    y = _bdot(merged.astype(BF16), wout_ref[...])
    h = x_ref[0] + mod_ref[0, 2:3, :] * y
    h_ref[0] = h
    u2 = _rms(h, g2_ref[...]) * (1.0 + mod_ref[0, 4:5, :]) + mod_ref[0, 3:4, :]
    _pack_rows(u2_ref, u2, u2t)
    logit_ref[0] = _dot3(u2, wr_ref[...]) + br_ref[...]


def _merge(z, og, gt, x, mod3, w_dw, b_dw, ln_g, ln_b, w_pw, b_pw, w_go, w_out, g2, w_rg, b_rg, w_re, b_re, ts):
    nb, s, d = x.shape
    npad = LANES - N_GROUPS - N_EXPERTS
    wr = jnp.pad(jnp.concatenate([w_rg, w_re], axis=1), ((0, 0), (0, npad)))
    br = jnp.pad(jnp.concatenate([b_rg, b_re]), (0, npad)).reshape(1, LANES)

    def tok(width):
        return pl.BlockSpec((1, ts, width), lambda b, i: (b, i, 0))

    def const(shape):
        return pl.BlockSpec(shape, lambda b, i: (0,) * len(shape))

    def row(v):
        return v.reshape(1, v.shape[-1])

    return pl.pallas_call(
        _merge_body,
        grid=(nb, s // ts),
        in_specs=[tok(CONV_DIM), tok(V_DIM), tok(2 * d), tok(d),
                  pl.BlockSpec((1, N_MOD, d), lambda b, i: (b, 0, 0)),
                  const((CONV_WIDTH, SUBLANES, CONV_DIM)), const((1, CONV_DIM)), const((1, CONV_DIM)), const((1, CONV_DIM)),
                  const((CONV_DIM, d)), const((1, d)), const((V_DIM, d)), const((d, d)), const((1, d)),
                  const((d, LANES)), const((1, LANES))],
        out_specs=[tok(d), pl.BlockSpec((ts * PACK_ROWS, LANES), lambda b, i: (b * (s // ts) + i, 0)), tok(LANES)],
        out_shape=[jax.ShapeDtypeStruct((nb, s, d), F32),
                   jax.ShapeDtypeStruct((nb * s * PACK_ROWS, LANES), jnp.uint32),
                   jax.ShapeDtypeStruct((nb, s, LANES), F32)],
        scratch_shapes=[pltpu.VMEM((CONV_HALO + ts, CONV_DIM), F32),
                        pltpu.VMEM((SUBLANES - 1, CONV_HALO + ts - SUBLANES, CONV_DIM), F32),
                        pltpu.VMEM((ts * SUBLANES, LANES), F32)],
        compiler_params=_params("arbitrary", "arbitrary"),
        name="merge",
    )(z, og, gt, x, mod3, jnp.broadcast_to(w_dw.reshape(CONV_WIDTH, 1, CONV_DIM), (CONV_WIDTH, SUBLANES, CONV_DIM)), row(b_dw), row(ln_g), row(ln_b),
      w_pw.astype(BF16), row(b_pw), w_go.astype(BF16), w_out.astype(BF16), row(g2), wr, br)


def _route_body(logit_ref, route_ref, routet_ref, cnt_ref, cnt_sc, *, sub):
    @pl.when(pl.program_id(0) == 0)
    def _():
        cnt_sc[...] = jnp.zeros_like(cnt_sc)

    packed, cnt = _route(logit_ref[...], cnt_sc[...], sub)
    route_ref[...] = packed
    routet_ref[...] = packed.T[0:SUBLANES, :]
    cnt_sc[...] = cnt
    cnt_ref[...] = jnp.broadcast_to(cnt, cnt_ref.shape)


def _route_call(logits, tr, sub):
    t = logits.shape[0]
    return pl.pallas_call(
        functools.partial(_route_body, sub=sub),
        grid=(t // tr,),
        in_specs=[pl.BlockSpec((tr, LANES), lambda i: (i, 0))],
        out_specs=[pl.BlockSpec((tr, LANES), lambda i: (i, 0)), pl.BlockSpec((SUBLANES, tr), lambda i: (0, i)),
                   pl.BlockSpec((SUBLANES, LANES), lambda i: (0, 0))],
        out_shape=[jax.ShapeDtypeStruct((t, LANES), F32), jax.ShapeDtypeStruct((SUBLANES, t), F32),
                   jax.ShapeDtypeStruct((SUBLANES, LANES), F32)],
        scratch_shapes=[pltpu.VMEM((1, LANES), F32)],
        compiler_params=_params("arbitrary"),
        name="route",
    )(logits)


def _row_copy(src, i, dst, j, sem):
    return pltpu.make_async_copy(src.at[pl.ds(pl.multiple_of(i, SUBLANES), SUBLANES), :],
                                 dst.at[pl.ds(pl.multiple_of(j, SUBLANES), SUBLANES), :], sem)


def _invert_body(lo_ref, hi_ref, slot_ref, src_ref):
    i = pl.program_id(0)
    ts = slot_ref.shape[-1] // TOP_K

    @pl.when(i == 0)
    def _():
        for e in range(lo_ref.shape[0]):
            lo = lo_ref[e]
            hi = hi_ref[e]

            def clear(p, carry, lo=lo, hi=hi):
                for j in range(CLEAR_UNROLL):
                    src_ref[jnp.minimum(lo + p * CLEAR_UNROLL + j, hi - 1)] = 0
                return carry

            trips = lax.shift_right_logical(hi - lo + (CLEAR_UNROLL - 1), CLEAR_UNROLL.bit_length() - 1)
            lax.fori_loop(0, trips, clear, 0)

    for r in range(ts):
        row = (i * ts + r) * PACK_ROWS
        for k in range(TOP_K):
            src_ref[slot_ref[0, 0, k * ts + r]] = row


def _invert(fill_lo, fill_hi, slots, n_slots):
    nt, _, width = slots.shape
    return pl.pallas_call(
        _invert_body,
        grid_spec=pltpu.PrefetchScalarGridSpec(
            num_scalar_prefetch=2,
            grid=(nt,),
            in_specs=[pl.BlockSpec((1, 1, width), lambda i, lo, hi: (i, 0, 0), memory_space=pltpu.SMEM)],
            out_specs=pl.BlockSpec(memory_space=pltpu.SMEM)),
        out_shape=jax.ShapeDtypeStruct((n_slots,), jnp.int32),
        compiler_params=_params("arbitrary"),
        name="invert",
    )(fill_lo, fill_hi, slots)


def _expert_body(te_ref, nu_ref, seg_ref, nxt_ref, src_ref, u2_ref, w1_ref, w3_ref, w2_ref, ys_ref,
                 u2v, xg, xt, w1f, w3f, w2f, w1b, w3b, w2b, sem, wsem):
    i = pl.program_id(0)
    tmx = src_ref.shape[-1]
    expert = te_ref[i]
    first = i == 0
    changed = jnp.logical_or(first, expert != te_ref[jnp.maximum(i - 1, 0)])
    slot = lax.rem(seg_ref[i], 2)

    def weight_copies(e, s):
        return [pltpu.make_async_copy(src.at[e], dst.at[s], wsem.at[s])
                for src, dst in ((w1_ref, w1f), (w3_ref, w3f), (w2_ref, w2f))]

    load = pltpu.make_async_copy(u2_ref, u2v, sem)

    @pl.when(first)
    def _():
        load.start()
        for cp in weight_copies(expert, 0):
            cp.start()

    @pl.when(changed)
    def _():
        for cp in weight_copies(expert, slot):
            cp.wait()
        w1b[...] = w1f[slot].astype(BF16)
        w3b[...] = w3f[slot].astype(BF16)
        w2b[...] = w2f[slot].astype(BF16)

    @pl.when(jnp.logical_and(changed, nxt_ref[i] != expert))
    def _():
        for cp in weight_copies(nxt_ref[i], 1 - slot):
            cp.start()

    @pl.when(first)
    def _():
        load.wait()

    @pl.when(i < nu_ref[0])
    def _():
        for r in range(tmx):
            row = pl.multiple_of(src_ref[0, 0, r], PACK_ROWS)
            xg[r * PACK_ROWS:(r + 1) * PACK_ROWS, :] = u2v[pl.ds(row, PACK_ROWS), :]
        x = _unpack_rows(xg, tmx, xt)
        h1 = _bdot(x, w1b[...])
        h3 = _bdot(x, w3b[...])
        hid = (h1 * _sigmoid(h1) * h3).astype(BF16)
        _rows_to_tiles(ys_ref, _bdot(hid, w2b[...]))

    @pl.when(i >= nu_ref[0])
    def _():
        ys_ref[...] = jnp.zeros_like(ys_ref)


def _experts(tile_expert, n_used, run_index, next_expert, src_rows, u2p, w1, w3, w2):
    n_tiles, _, tmx = src_rows.shape
    ne, d, f = w1.shape
    hbm = pl.BlockSpec(memory_space=pl.ANY)
    return pl.pallas_call(
        _expert_body,
        grid_spec=pltpu.PrefetchScalarGridSpec(
            num_scalar_prefetch=4,
            grid=(n_tiles,),
            in_specs=[pl.BlockSpec((1, 1, tmx), lambda i, *_: (i, 0, 0), memory_space=pltpu.SMEM),
                      hbm, hbm, hbm, hbm],
            out_specs=pl.BlockSpec((tmx * SUBLANES, LANES), lambda i, *_: (i, 0)),
            scratch_shapes=[pltpu.VMEM(u2p.shape, jnp.uint32),
                            pltpu.VMEM((tmx * PACK_ROWS, LANES), jnp.uint32),
                            pltpu.VMEM((tmx * SUBLANES, LANES), F32),
                            pltpu.VMEM((2, d, f), F32), pltpu.VMEM((2, d, f), F32), pltpu.VMEM((2, f, d), F32),
                            pltpu.VMEM((d, f), BF16), pltpu.VMEM((d, f), BF16), pltpu.VMEM((f, d), BF16),
                            pltpu.SemaphoreType.DMA(()), pltpu.SemaphoreType.DMA((2,))]),
        out_shape=jax.ShapeDtypeStruct((n_tiles * tmx * SUBLANES, LANES), F32),
        compiler_params=_params("arbitrary"),
        name="experts",
    )(tile_expert, n_used, run_index, next_expert, src_rows, u2p, w1, w3, w2)


def _final_body(p_ref, pn_ref, h_ref, route_ref, mod_ref, modf_ref, gf_ref, ys_ref, o_ref, y_buf, sem):
    ts = h_ref.shape[1]
    step = pl.program_id(0) * pl.num_programs(1) + pl.program_id(1)
    n_steps = pl.num_programs(0) * pl.num_programs(1)
    cur = lax.rem(step, 2)

    def request_rows(ref, half):
        for k in range(TOP_K):
            for r in range(ts):
                _row_copy(ys_ref, ref[0, 0, k * ts + r], y_buf.at[half, k], r * SUBLANES,
                          sem.at[half]).start(priority=r % 2)

    @pl.when(step == 0)
    def _():
        request_rows(p_ref, 0)

    @pl.when(step + 1 < n_steps)
    def _():
        request_rows(pn_ref, 1 - cur)

    for k in range(TOP_K):
        pltpu.make_async_copy(ys_ref.at[pl.ds(0, ts * SUBLANES), :], y_buf.at[cur, k], sem.at[cur]).wait()

    route = route_ref[0]
    y2 = (route[:, 2:3] * _tiles_to_rows(y_buf.at[cur, 0], ts)
          + route[:, 3:4] * _tiles_to_rows(y_buf.at[cur, 1], ts))
    h = h_ref[0] + mod_ref[0, 5:6, :] * y2
    o_ref[0] = _rms(h, gf_ref[...]) * (1.0 + modf_ref[0, 1:2, :]) + modf_ref[0, 0:1, :]


def _final(slot_rows, h, route, mod3, modf3, gf, ys, ts):
    nb, s, d = h.shape
    nt = s // ts
    last = nb * nt - 1

    def tok(width):
        return pl.BlockSpec((1, ts, width), lambda b, i: (b, i, 0))

    def slots(ahead):
        return pl.BlockSpec((1, 1, TOP_K * ts), lambda b, i: (jnp.minimum(b * nt + i + ahead, last), 0, 0),
                            memory_space=pltpu.SMEM)

    return pl.pallas_call(
        _final_body,
        grid=(nb, nt),
        in_specs=[slots(0), slots(1), tok(d), tok(LANES),
                  pl.BlockSpec((1, N_MOD, d), lambda b, i: (b, 0, 0)),
                  pl.BlockSpec((1, 2, d), lambda b, i: (b, 0, 0)),
                  pl.BlockSpec((1, d), lambda b, i: (0, 0)),
                  pl.BlockSpec(memory_space=pl.ANY)],
        out_specs=tok(d),
        out_shape=jax.ShapeDtypeStruct((nb, s, d), F32),
        scratch_shapes=[pltpu.VMEM((2, TOP_K, ts * SUBLANES, LANES), F32), pltpu.SemaphoreType.DMA((2,))],
        compiler_params=_params("arbitrary", "arbitrary"),
        name="final",
    )(slot_rows, slot_rows, h, route, mod3, modf3, gf.reshape(1, d), ys)


def _plan(routet, cnt, ts, tmx, n_tiles):
    t = routet.shape[1]
    counts = cnt[0, :N_EXPERTS].astype(jnp.int32)
    tiles = (counts + (tmx - 1)) // tmx
    tile_end = jnp.cumsum(tiles)
    offs = ((tile_end - tiles) * tmx).astype(jnp.int32)
    n_used = tile_end[-1:]
    tile_ids = jnp.minimum(jnp.arange(n_tiles, dtype=jnp.int32), n_used[0] - 1)
    tile_expert = jnp.sum((tile_ids[:, None] >= tile_end[None, :]).astype(jnp.int32), axis=1)
    run_index = jnp.cumsum(jnp.concatenate([jnp.zeros((1,), jnp.int32),
                                            (tile_expert[1:] != tile_expert[:-1]).astype(jnp.int32)]))
    same_run = run_index[:, None] + 1 == run_index[None, :]
    has_next = jnp.any(same_run, axis=1)
    next_expert = jnp.where(has_next, jnp.max(jnp.where(same_run, tile_expert[None, :], 0), axis=1), tile_expert)

    eid = routet[0:2].astype(jnp.int32)
    experts = jnp.arange(N_EXPERTS, dtype=jnp.int32)[:, None, None]
    slot = routet[4:6].astype(jnp.int32) + jnp.sum(jnp.where(eid[None] == experts, offs[:, None, None], 0), axis=0)
    slots = slot.reshape(TOP_K, t // ts, ts).transpose(1, 0, 2).reshape(t // ts, 1, TOP_K * ts)
    fill_lo = jnp.concatenate([offs + counts, tile_end[-1:] * tmx]).astype(jnp.int32)
    fill_hi = jnp.concatenate([tile_end * tmx, jnp.full((1,), n_tiles * tmx, jnp.int32)]).astype(jnp.int32)
    src = _invert(fill_lo, fill_hi, slots, n_tiles * tmx)
    tables = [a.astype(jnp.int32) for a in (tile_expert, n_used, run_index, next_expert)]
    return slots * SUBLANES, src.reshape(n_tiles, 1, tmx), tables


def kernel(x, c, w_ada, b_ada, g_norm1, w_in, w_dw, b_dw, g_conv_ln, b_conv_ln, w_conv_pw, b_conv_pw,
           w_a2, b_a2, g_gla_norm, w_gla_o, w_out, g_norm2, w_router_g, b_router_g, w_router_e,
           b_router_e, w1, w3, w2, w_ada_f, b_ada_f, g_final):
    nb, s, d = x.shape
    assert w_ada.shape[0] == 1, "single-layer block"
    assert d == 2 * PACK_ROWS * LANES, "packed token rows assume D_MODEL = 1024"
    tm = min(512, s)
    tc = min(256, s)
    ts = min(256, s)
    tf = min(512, s)
    tmx = 256
    t = nb * s
    n_tiles = (t * TOP_K) // tmx + N_EXPERTS
    mod3 = _ada(c, w_ada[0], b_ada[0]).reshape(nb, N_MOD, d)
    modf3 = _ada(c, w_ada_f, b_ada_f).reshape(nb, 2, d)
    z, q, k, v, rs, lg, gt, lg_sums = _proj(x, mod3, g_norm1[0], w_in[0], w_a2[0], b_a2[0], tm)
    og = _gla(q, k, lg, lg_sums, v, rs, g_gla_norm[0], tc)
    h, u2, logits = _merge(z, og, gt, x, mod3, w_dw[0], b_dw[0], g_conv_ln[0], b_conv_ln[0],
                           w_conv_pw[0], b_conv_pw[0], w_gla_o[0], w_out[0], g_norm2[0],
                           w_router_g[0], b_router_g[0], w_router_e[0], b_router_e[0], ts)
    route, routet, cnt = _route_call(logits.reshape(t, LANES), min(ROUTE_ROWS, t), ts)
    slot_rows, src_rows, tables = _plan(routet, cnt, tf, tmx, n_tiles)
    ys = _experts(*tables, src_rows, u2, w1[0], w3[0], w2[0])
    return _final(slot_rows, h, route.reshape(nb, s, LANES), mod3, modf3, g_final, ys, tf)
```

```python
import functools

import jax
import jax.numpy as jnp
from jax import lax
from jax.experimental import pallas as pl
from jax.experimental.pallas import tpu as pltpu

F32 = jnp.float32
BF16 = jnp.bfloat16

EPS = 1e-6
CONV_DIM = 512
CONV_WIDTH = 31
GLA_HEADS = 4
GLA_DK = 128
GLA_DV = 256
GLA_LOWRANK = 16
GLA_TAU = 16.0
QK_DIM = GLA_HEADS * GLA_DK
V_DIM = GLA_HEADS * GLA_DV
N_GROUPS = 4
EXPERTS_PER_GROUP = 8
N_EXPERTS = N_GROUPS * EXPERTS_PER_GROUP
TOP_K = 2
N_MOD = 6

LANES = 128
SUBLANES = 8
PACK_ROWS = 4
CONV_ROWS = 32
ROUTE_ROWS = 2048
CLEAR_UNROLL = 16
CONV_HALO = 32
GLA_CHUNK = 128
GLA_FACTOR_RANGE = 60.0
VMEM_LIMIT = 56 * 1024 * 1024


def _bdot(a, b):
    return jnp.dot(a, b, preferred_element_type=F32)


def _split(a):
    hi = a.astype(BF16)
    lo = (a - hi.astype(F32)).astype(BF16)
    return hi, lo


def _dot3(a, b):
    ah, al = _split(a)
    bh, bl = _split(b)
    return _bdot(ah, bh) + (_bdot(ah, bl) + _bdot(al, bh))


def _sigmoid(x):
    return 1.0 / (1.0 + jnp.exp(-x))


def _rms(x, g):
    ms = jnp.mean(x * x, axis=-1, keepdims=True)
    return x * lax.rsqrt(ms + EPS) * g


def _rows_to_tiles(ref, val):
    n = val.shape[0]
    for j in range(val.shape[1] // LANES):
        ref[pl.ds(j, n, stride=SUBLANES), :] = val[:, j * LANES:(j + 1) * LANES]


def _tiles_to_rows(ref, n):
    return jnp.concatenate([ref[pl.ds(j, n, stride=SUBLANES), :] for j in range(SUBLANES)], axis=-1)


def _pack_rows(ref, val, tiles):
    _rows_to_tiles(tiles, val)
    ref[...] = pltpu.bitcast(tiles[...].astype(BF16), jnp.uint32)


def _unpack_rows(ref, n, tiles):
    tiles[...] = pltpu.bitcast(ref[...], BF16).astype(F32)
    return _tiles_to_rows(tiles, n).astype(BF16)


def _params(*sem):
    return pltpu.CompilerParams(dimension_semantics=sem, vmem_limit_bytes=VMEM_LIMIT)


def _ada_body(c_ref, w_ref, b_ref, o_ref):
    c = c_ref[...]
    o_ref[...] = _dot3(c * _sigmoid(c), w_ref[...]) + b_ref[...]


def _ada(c, w, b, tn=1024):
    nb, d = c.shape
    n = w.shape[1]
    return pl.pallas_call(
        _ada_body,
        grid=(n // tn,),
        in_specs=[pl.BlockSpec((nb, d), lambda j: (0, 0)),
                  pl.BlockSpec((d, tn), lambda j: (0, j)),
                  pl.BlockSpec((1, tn), lambda j: (0, j))],
        out_specs=pl.BlockSpec((nb, tn), lambda j: (0, j)),
        out_shape=jax.ShapeDtypeStruct((nb, n), F32),
        compiler_params=_params("arbitrary"),
        name="ada",
    )(c, w, b.reshape(1, n))


def _proj_body(x_ref, mod_ref, g1_ref, wm_ref, wa1_ref, wg_ref, wa2_ref, ba2_ref,
               z_ref, q_ref, k_ref, v_ref, rs_ref, lg_ref, gt_ref, ls_ref):
    x = x_ref[0]
    u = (_rms(x, g1_ref[...]) * (1.0 + mod_ref[0, 1:2, :]) + mod_ref[0, 0:1, :]).astype(BF16)
    c0 = 2 * CONV_DIM
    c1 = c0 + 2 * QK_DIM
    c2 = c1 + V_DIM
    c3 = c2 + V_DIM
    pc = _bdot(u, wm_ref[:, 0:c0])
    z_ref[0] = pc[:, :CONV_DIM] * _sigmoid(pc[:, CONV_DIM:])
    qk = _bdot(u, wm_ref[:, c0:c1])
    q_ref[0] = qk[:, :QK_DIM] * (GLA_DK ** -0.5)
    k_ref[0] = qk[:, QK_DIM:]
    v_ref[0] = _bdot(u, wm_ref[:, c1:c2]).astype(BF16)
    r = _bdot(u, wm_ref[:, c2:c3])
    rs_ref[0] = (r * _sigmoid(r)).astype(BF16)
    a1 = _bdot(u, wa1_ref[...])
    xg = _dot3(a1, wa2_ref[...]) + ba2_ref[...]
    lg = (jnp.minimum(xg, 0.0) - jnp.log1p(jnp.exp(-jnp.abs(xg)))) * (1.0 / GLA_TAU)
    lg_ref[0] = lg
    ls_ref[0, 0] = jnp.concatenate([jnp.sum(lg[lo:lo + GLA_CHUNK, :], axis=0, keepdims=True)
                                    for lo in range(0, lg.shape[0], GLA_CHUNK)], axis=0)
    gt_ref[0] = _sigmoid(_bdot(u, wg_ref[...])).astype(BF16)


def _wsplit_body(wt_ref, wm_ref, wa1_ref, wg_ref, buf, sem, *, c3):
    cw = buf.shape[1]
    n_main = c3 // cw
    j = pl.program_id(0)
    cur = lax.rem(j, 2)

    def chunk(step, half):
        start = jnp.where(step <= n_main, step * cw, c3 + GLA_LOWRANK + (step - n_main - 1) * cw)
        return pltpu.make_async_copy(wt_ref.at[pl.ds(pl.multiple_of(start, SUBLANES), cw), :], buf.at[half],
                                     sem.at[half])

    @pl.when(j == 0)
    def _():
        chunk(j, 0).start()

    @pl.when(j + 1 < pl.num_programs(0))
    def _():
        chunk(j + 1, 1 - cur).start()

    chunk(j, cur).wait()

    @pl.when(j < n_main)
    def _():
        wm_ref[...] = buf[cur].T.astype(BF16)

    @pl.when(j == n_main)
    def _():
        lane = lax.broadcasted_iota(jnp.int32, wa1_ref.shape, 1)
        wa1_ref[...] = jnp.where(lane < GLA_LOWRANK, buf[cur, 0:LANES, :].T, 0.0).astype(BF16)

    @pl.when(j > n_main)
    def _():
        wg_ref[...] = buf[cur].T.astype(BF16)


def _wsplit(wt, c3, cw=512):
    n, d = wt.shape
    ng = n - c3 - GLA_LOWRANK
    n_main = c3 // cw
    return pl.pallas_call(
        functools.partial(_wsplit_body, c3=c3),
        grid=(n_main + 1 + ng // cw,),
        in_specs=[pl.BlockSpec(memory_space=pl.ANY)],
        out_specs=[pl.BlockSpec((d, cw), lambda j: (0, jnp.minimum(j, n_main - 1))),
                   pl.BlockSpec((d, LANES), lambda j: (0, 0)),
                   pl.BlockSpec((d, cw), lambda j: (0, jnp.clip(j - n_main - 1, 0, ng // cw - 1)))],
        out_shape=[jax.ShapeDtypeStruct((d, c3), BF16), jax.ShapeDtypeStruct((d, LANES), BF16),
                   jax.ShapeDtypeStruct((d, ng), BF16)],
        scratch_shapes=[pltpu.VMEM((2, cw, d), F32), pltpu.SemaphoreType.DMA((2,))],
        compiler_params=_params("arbitrary"),
        name="wsplit",
    )(wt)


def _proj(x, mod3, g1, w_in, w_a2, b_a2, tm):
    nb, s, d = x.shape
    c3 = 2 * CONV_DIM + 2 * QK_DIM + 2 * V_DIM
    wm, wa1, wg = _wsplit(jnp.swapaxes(w_in, 0, 1), c3)
    wa2 = jnp.pad(w_a2, ((0, LANES - GLA_LOWRANK), (0, 0)))
    ng = wg.shape[1]

    def tok(width):
        return pl.BlockSpec((1, tm, width), lambda b, i: (b, i, 0))

    def const(shape):
        return pl.BlockSpec(shape, lambda b, i: (0,) * len(shape))

    def out(width, dt):
        return jax.ShapeDtypeStruct((nb, s, width), dt)

    return pl.pallas_call(
        _proj_body,
        grid=(nb, s // tm),
        in_specs=[tok(d),
                  pl.BlockSpec((1, N_MOD, d), lambda b, i: (b, 0, 0)),
                  const((1, d)), const(wm.shape), const(wa1.shape), const(wg.shape),
                  const(wa2.shape), const((1, QK_DIM))],
        out_specs=[tok(CONV_DIM), tok(QK_DIM), tok(QK_DIM), tok(V_DIM), tok(V_DIM), tok(QK_DIM), tok(ng),
                   pl.BlockSpec((1, 1, tm // GLA_CHUNK, QK_DIM), lambda b, i: (b, i, 0, 0))],
        out_shape=[out(CONV_DIM, F32), out(QK_DIM, F32), out(QK_DIM, F32), out(V_DIM, BF16),
                   out(V_DIM, BF16), out(QK_DIM, F32), out(ng, BF16),
                   jax.ShapeDtypeStruct((nb, s // tm, tm // GLA_CHUNK, QK_DIM), F32)],
        compiler_params=_params("arbitrary", "arbitrary"),
        name="proj",
    )(x, mod3, g1.reshape(1, d), wm, wa1, wg, wa2, b_a2.reshape(1, QK_DIM))


def _gla_body(ok_ref, q_ref, k_ref, lg_ref, v_ref, rs_ref, gn_ref, o_ref, st_ref, b_s, oi_s, *, n_chunks):
    cl = GLA_CHUNK

    @pl.when(pl.program_id(1) == 0)
    def _():
        st_ref[...] = jnp.zeros_like(st_ref)

    row = lax.broadcasted_iota(jnp.int32, (cl, cl), 0)
    col = lax.broadcasted_iota(jnp.int32, (cl, cl), 1)
    causal = col <= row
    tri = jnp.where(causal, 1.0, 0.0).astype(BF16)
    chunks = [slice(c * cl, (c + 1) * cl) for c in range(n_chunks)]
    heads = range(GLA_HEADS)
    ksl = [slice(h * GLA_DK, (h + 1) * GLA_DK) for h in heads]
    vsl = [slice(h * GLA_DV, (h + 1) * GLA_DV) for h in heads]

    def cumulative(rows):
        gh, gl = _split(lg_ref[0, rows, :])
        return _bdot(tri, gh) + _bdot(tri, gl)

    def intra_pairwise(h, c):
        kf = k_ref[0, chunks[c], ksl[h]]
        vf = v_ref[0, chunks[c], vsl[h]].astype(F32)
        bh = b_s[chunks[c], ksl[h]]
        key = lax.broadcasted_iota(jnp.int32, (cl, 1), 0)

        def group(g, carry):
            r0 = pl.multiple_of(g * SUBLANES, SUBLANES)
            q8 = q_ref[0, pl.ds(c * cl + r0, SUBLANES), ksl[h]]
            b8 = b_s[pl.ds(c * cl + r0, SUBLANES), ksl[h]]
            out_rows = []
            for r in range(SUBLANES):
                diff = jnp.where(key <= r0 + r, b8[r:r + 1, :] - bh, -jnp.inf)
                att = jnp.sum(jnp.exp(diff) * kf * q8[r:r + 1, :], axis=-1, keepdims=True)
                out_rows.append(jnp.sum(att * vf, axis=0, keepdims=True))
            oi_s[h, pl.ds(c * cl + r0, SUBLANES), :] = jnp.concatenate(out_rows, axis=0)
            return carry

        lax.fori_loop(0, cl // SUBLANES, group, 0)
        return oi_s[h, chunks[c], :]

    def run(factored):
        cums = [cumulative(rows) for rows in chunks]
        if not factored:
            for rows, b in zip(chunks, cums):
                b_s[rows, :] = b
        qes, kts, bts, klts, decays = [], [], [], [], []
        for rows, b in zip(chunks, cums):
            qes.append((q_ref[0, rows, :] * jnp.exp(b)).astype(BF16))
            kt = k_ref[0, rows, :].T
            bt = b.T
            bl = bt[:, cl - 1:cl]
            kts.append(kt)
            bts.append(bt)
            klts.append((kt * jnp.exp(bl - bt)).astype(BF16))
            decays.append(jnp.exp(bl))
        intra = {}
        if factored:
            kets = [(kt * jnp.exp(-bt)).astype(BF16) for kt, bt in zip(kts, bts)]
            atts = {}
            for c in range(n_chunks):
                for h in heads:
                    att = _bdot(qes[c][:, ksl[h]], kets[c][ksl[h], :])
                    atts[h, c] = jnp.where(causal, att, 0.0).astype(BF16)
            for c, rows in enumerate(chunks):
                for h in heads:
                    intra[h, c] = _bdot(atts[h, c], v_ref[0, rows, vsl[h]])
        else:
            for c in range(n_chunks):
                for h in heads:
                    intra[h, c] = intra_pairwise(h, c)
        updates = {}
        for c, rows in enumerate(chunks):
            for h in heads:
                updates[h, c] = _bdot(klts[c][ksl[h], :], v_ref[0, rows, vsl[h]])

        outs = {}
        states = []
        for h in heads:
            state = st_ref[h]
            for c, rows in enumerate(chunks):
                o = intra[h, c] + _bdot(qes[c][:, ksl[h]], state.astype(BF16))
                state = decays[c][ksl[h], :] * state + updates[h, c]
                outs[h, c] = (_rms(o, gn_ref[:, vsl[h]]) * rs_ref[0, rows, vsl[h]].astype(F32)).astype(BF16)
            states.append(state)
        for c, rows in enumerate(chunks):
            o_ref[0, rows, :] = jnp.concatenate([outs[h, c] for h in heads], axis=-1)
        for h in heads:
            st_ref[h] = states[h]

    in_range = ok_ref[pl.program_id(0) * pl.num_programs(1) + pl.program_id(1)] != 0
    pl.when(in_range)(functools.partial(run, True))
    pl.when(jnp.logical_not(in_range))(functools.partial(run, False))


def _gla(q, k, lg, lg_sums, v, rs, gn, tc):
    nb, s, _ = q.shape
    lowest = jnp.min(lg_sums.reshape(nb, s // tc, (tc // GLA_CHUNK) * QK_DIM), axis=-1)
    in_range = (lowest > -GLA_FACTOR_RANGE).astype(jnp.int32).reshape(-1)

    def tok(width):
        return pl.BlockSpec((1, tc, width), lambda b, i, ok: (b, i, 0))

    return pl.pallas_call(
        functools.partial(_gla_body, n_chunks=tc // GLA_CHUNK),
        grid_spec=pltpu.PrefetchScalarGridSpec(
            num_scalar_prefetch=1,
            grid=(nb, s // tc),
            in_specs=[tok(QK_DIM), tok(QK_DIM), tok(QK_DIM), tok(V_DIM), tok(V_DIM),
                      pl.BlockSpec((1, V_DIM), lambda b, i, ok: (0, 0))],
            out_specs=tok(V_DIM),
            scratch_shapes=[pltpu.VMEM((GLA_HEADS, GLA_DK, GLA_DV), F32),
                            pltpu.VMEM((tc, QK_DIM), F32),
                            pltpu.VMEM((GLA_HEADS, tc, GLA_DV), F32)]),
        out_shape=jax.ShapeDtypeStruct((nb, s, V_DIM), BF16),
        compiler_params=_params("arbitrary", "arbitrary"),
        name="gla",
    )(in_range, q, k, lg, v, rs, gn.reshape(1, V_DIM))


def _route(logits, cnt, sub):
    ts = logits.shape[0]
    lane = lax.broadcasted_iota(jnp.int32, (ts, LANES), 1).astype(F32)
    ninf = -jnp.inf
    lgm = jnp.where(lane < N_GROUPS, logits, ninf)
    gmax = jnp.max(lgm, axis=-1, keepdims=True)
    gsel = jnp.min(jnp.where(lgm == gmax, lane, float(LANES)), axis=-1, keepdims=True)
    wg = 1.0 / jnp.sum(jnp.exp(lgm - gmax), axis=-1, keepdims=True)
    base = N_GROUPS + EXPERTS_PER_GROUP * gsel
    le = jnp.where(lane >= base, jnp.where(lane < base + EXPERTS_PER_GROUP, logits, ninf), ninf)
    v1 = jnp.max(le, axis=-1, keepdims=True)
    i1 = jnp.min(jnp.where(le == v1, lane, float(LANES)), axis=-1, keepdims=True)
    le2 = jnp.where(lane == i1, ninf, le)
    v2 = jnp.max(le2, axis=-1, keepdims=True)
    i2 = jnp.min(jnp.where(le2 == v2, lane, float(LANES)), axis=-1, keepdims=True)
    e21 = jnp.exp(v2 - v1)
    w1 = wg / (1.0 + e21)
    w2 = w1 * e21
    eid1 = i1 - N_GROUPS
    eid2 = i2 - N_GROUPS
    oh1 = jnp.where(lane == eid1, 1.0, 0.0)
    oh2 = jnp.where(lane == eid2, 1.0, 0.0)
    ohs = oh1 + oh2
    row = lax.broadcasted_iota(jnp.int32, (sub, sub), 0)
    col = lax.broadcasted_iota(jnp.int32, (sub, sub), 1)
    before = jnp.where(col < row, 1.0, 0.0).astype(BF16)
    tots = []
    for lo in range(0, ts, sub):
        piece = ohs[lo:lo + sub, :]
        tots.append(cnt + _bdot(before, piece.astype(BF16)))
        cnt = cnt + jnp.sum(piece, axis=0, keepdims=True)
    tot = jnp.concatenate(tots, axis=0)
    rank1 = jnp.sum(oh1 * tot, axis=-1, keepdims=True)
    rank2 = jnp.sum(oh2 * tot, axis=-1, keepdims=True)
    packed = jnp.where(lane == 0.0, eid1,
             jnp.where(lane == 1.0, eid2,
             jnp.where(lane == 2.0, w1,
             jnp.where(lane == 3.0, w2,
             jnp.where(lane == 4.0, rank1,
             jnp.where(lane == 5.0, rank2, 0.0))))))
    return packed, cnt


def _merge_body(z_ref, og_ref, gt_ref, x_ref, mod_ref, wdw_ref, bdw_ref, lng_ref, lnb_ref,
                wpw_ref, bpw_ref, wgo_ref, wout_ref, g2_ref, wr_ref, br_ref,
                h_ref, u2_ref, logit_ref, zbuf, zsh, u2t):
    ts = z_ref.shape[1]
    d = x_ref.shape[2]
    first_tile = pl.program_id(1) == 0

    @pl.when(first_tile)
    def _():
        zbuf[0:CONV_HALO, :] = jnp.zeros((CONV_HALO, CONV_DIM), F32)

    zbuf[CONV_HALO:CONV_HALO + ts, :] = z_ref[0]
    span = ts + CONV_HALO - SUBLANES
    for r in range(1, SUBLANES):
        zsh[r - 1] = zbuf[r:r + span, :]
    off = CONV_HALO - (CONV_WIDTH - 1)
    pieces = []
    for blk in range(ts // CONV_ROWS):
        acc = None
        for j in range(CONV_WIDTH):
            a, r = divmod(off + j, SUBLANES)
            lo = a * SUBLANES + blk * CONV_ROWS
            src = zbuf[lo:lo + CONV_ROWS, :] if r == 0 else zsh[r - 1, lo:lo + CONV_ROWS, :]
            term = src * jnp.concatenate([wdw_ref[j]] * (CONV_ROWS // SUBLANES), axis=0)
            acc = term if acc is None else acc + term
        conv = acc + bdw_ref[...]
        mu = jnp.mean(conv, axis=-1, keepdims=True)
        xc = conv - mu
        var = jnp.mean(xc * xc, axis=-1, keepdims=True)
        ln = xc * lax.rsqrt(var + EPS) * lng_ref[...] + lnb_ref[...]
        pieces.append((ln * _sigmoid(ln)).astype(BF16))
    zbuf[0:CONV_HALO, :] = zbuf[ts:ts + CONV_HALO, :]
    y_conv = _bdot(jnp.concatenate(pieces, axis=0), wpw_ref[...]) + bpw_ref[...]
    y_gla = _bdot(og_ref[0], wgo_ref[...])
    merged = gt_ref[0, :, 0:d].astype(F32) * y_conv + gt_ref[0, :, d:2 * d].astype(F32) * y_gla
    y = _bdot(merged.astype(BF16), wout_ref[...])
    h = x_ref[0] + mod_ref[0, 2:3, :] * y
    h_ref[0] = h
    u2 = _rms(h, g2_ref[...]) * (1.0 + mod_ref[0, 4:5, :]) + mod_ref[0, 3:4, :]
    _pack_rows(u2_ref, u2, u2t)
    logit_ref[0] = _dot3(u2, wr_ref[...]) + br_ref[...]


def _merge(z, og, gt, x, mod3, w_dw, b_dw, ln_g, ln_b, w_pw, b_pw, w_go, w_out, g2, w_rg, b_rg, w_re, b_re, ts):
    nb, s, d = x.shape
    npad = LANES - N_GROUPS - N_EXPERTS
    wr = jnp.pad(jnp.concatenate([w_rg, w_re], axis=1), ((0, 0), (0, npad)))
    br = jnp.pad(jnp.concatenate([b_rg, b_re]), (0, npad)).reshape(1, LANES)

    def tok(width):
        return pl.BlockSpec((1, ts, width), lambda b, i: (b, i, 0))

    def const(shape):
        return pl.BlockSpec(shape, lambda b, i: (0,) * len(shape))

    def row(v):
        return v.reshape(1, v.shape[-1])

    return pl.pallas_call(
        _merge_body,
        grid=(nb, s // ts),
        in_specs=[tok(CONV_DIM), tok(V_DIM), tok(2 * d), tok(d),
                  pl.BlockSpec((1, N_MOD, d), lambda b, i: (b, 0, 0)),
                  const((CONV_WIDTH, SUBLANES, CONV_DIM)), const((1, CONV_DIM)), const((1, CONV_DIM)), const((1, CONV_DIM)),
                  const((CONV_DIM, d)), const((1, d)), const((V_DIM, d)), const((d, d)), const((1, d)),
                  const((d, LANES)), const((1, LANES))],
        out_specs=[tok(d), pl.BlockSpec((ts * PACK_ROWS, LANES), lambda b, i: (b * (s // ts) + i, 0)), tok(LANES)],
        out_shape=[jax.ShapeDtypeStruct((nb, s, d), F32),
                   jax.ShapeDtypeStruct((nb * s * PACK_ROWS, LANES), jnp.uint32),
                   jax.ShapeDtypeStruct((nb, s, LANES), F32)],
        scratch_shapes=[pltpu.VMEM((CONV_HALO + ts, CONV_DIM), F32),
                        pltpu.VMEM((SUBLANES - 1, CONV_HALO + ts - SUBLANES, CONV_DIM), F32),
                        pltpu.VMEM((ts * SUBLANES, LANES), F32)],
        compiler_params=_params("arbitrary", "arbitrary"),
        name="merge",
    )(z, og, gt, x, mod3, jnp.broadcast_to(w_dw.reshape(CONV_WIDTH, 1, CONV_DIM), (CONV_WIDTH, SUBLANES, CONV_DIM)), row(b_dw), row(ln_g), row(ln_b),
      w_pw.astype(BF16), row(b_pw), w_go.astype(BF16), w_out.astype(BF16), row(g2), wr, br)


def _route_body(logit_ref, route_ref, routet_ref, cnt_ref, cnt_sc, *, sub):
    @pl.when(pl.program_id(0) == 0)
    def _():
        cnt_sc[...] = jnp.zeros_like(cnt_sc)

    packed, cnt = _route(logit_ref[...], cnt_sc[...], sub)
    route_ref[...] = packed
    routet_ref[...] = packed.T[0:SUBLANES, :]
    cnt_sc[...] = cnt
    cnt_ref[...] = jnp.broadcast_to(cnt, cnt_ref.shape)


def _route_call(logits, tr, sub):
    t = logits.shape[0]
    return pl.pallas_call(
        functools.partial(_route_body, sub=sub),
        grid=(t // tr,),
        in_specs=[pl.BlockSpec((tr, LANES), lambda i: (i, 0))],
        out_specs=[pl.BlockSpec((tr, LANES), lambda i: (i, 0)), pl.BlockSpec((SUBLANES, tr), lambda i: (0, i)),
                   pl.BlockSpec((SUBLANES, LANES), lambda i: (0, 0))],
        out_shape=[jax.ShapeDtypeStruct((t, LANES), F32), jax.ShapeDtypeStruct((SUBLANES, t), F32),
                   jax.ShapeDtypeStruct((SUBLANES, LANES), F32)],
        scratch_shapes=[pltpu.VMEM((1, LANES), F32)],
        compiler_params=_params("arbitrary"),
        name="route",
    )(logits)


def _row_copy(src, i, dst, j, sem):
    return pltpu.make_async_copy(src.at[pl.ds(pl.multiple_of(i, SUBLANES), SUBLANES), :],
                                 dst.at[pl.ds(pl.multiple_of(j, SUBLANES), SUBLANES), :], sem)


def _invert_body(lo_ref, hi_ref, slot_ref, src_ref):
    i = pl.program_id(0)
    ts = slot_ref.shape[-1] // TOP_K

    @pl.when(i == 0)
    def _():
        for e in range(lo_ref.shape[0]):
            lo = lo_ref[e]
            hi = hi_ref[e]

            def clear(p, carry, lo=lo, hi=hi):
                for j in range(CLEAR_UNROLL):
                    src_ref[jnp.minimum(lo + p * CLEAR_UNROLL + j, hi - 1)] = 0
                return carry

            trips = lax.shift_right_logical(hi - lo + (CLEAR_UNROLL - 1), CLEAR_UNROLL.bit_length() - 1)
            lax.fori_loop(0, trips, clear, 0)

    for r in range(ts):
        row = (i * ts + r) * PACK_ROWS
        for k in range(TOP_K):
            src_ref[slot_ref[0, 0, k * ts + r]] = row


def _invert(fill_lo, fill_hi, slots, n_slots):
    nt, _, width = slots.shape
    return pl.pallas_call(
        _invert_body,
        grid_spec=pltpu.PrefetchScalarGridSpec(
            num_scalar_prefetch=2,
            grid=(nt,),
            in_specs=[pl.BlockSpec((1, 1, width), lambda i, lo, hi: (i, 0, 0), memory_space=pltpu.SMEM)],
            out_specs=pl.BlockSpec(memory_space=pltpu.SMEM)),
        out_shape=jax.ShapeDtypeStruct((n_slots,), jnp.int32),
        compiler_params=_params("arbitrary"),
        name="invert",
    )(fill_lo, fill_hi, slots)


def _expert_body(te_ref, nu_ref, seg_ref, nxt_ref, src_ref, u2_ref, w1_ref, w3_ref, w2_ref, ys_ref,
                 u2v, xg, xt, w1f, w3f, w2f, w1b, w3b, w2b, sem, wsem):
    i = pl.program_id(0)
    tmx = src_ref.shape[-1]
    expert = te_ref[i]
    first = i == 0
    changed = jnp.logical_or(first, expert != te_ref[jnp.maximum(i - 1, 0)])
    slot = lax.rem(seg_ref[i], 2)

    def weight_copies(e, s):
        return [pltpu.make_async_copy(src.at[e], dst.at[s], wsem.at[s])
                for src, dst in ((w1_ref, w1f), (w3_ref, w3f), (w2_ref, w2f))]

    load = pltpu.make_async_copy(u2_ref, u2v, sem)

    @pl.when(first)
    def _():
        load.start()
        for cp in weight_copies(expert, 0):
            cp.start()

    @pl.when(changed)
    def _():
        for cp in weight_copies(expert, slot):
            cp.wait()
        w1b[...] = w1f[slot].astype(BF16)
        w3b[...] = w3f[slot].astype(BF16)
        w2b[...] = w2f[slot].astype(BF16)

    @pl.when(jnp.logical_and(changed, nxt_ref[i] != expert))
    def _():
        for cp in weight_copies(nxt_ref[i], 1 - slot):
            cp.start()

    @pl.when(first)
    def _():
        load.wait()

    @pl.when(i < nu_ref[0])
    def _():
        for r in range(tmx):
            row = pl.multiple_of(src_ref[0, 0, r], PACK_ROWS)
            xg[r * PACK_ROWS:(r + 1) * PACK_ROWS, :] = u2v[pl.ds(row, PACK_ROWS), :]
        x = _unpack_rows(xg, tmx, xt)
        h1 = _bdot(x, w1b[...])
        h3 = _bdot(x, w3b[...])
        hid = (h1 * _sigmoid(h1) * h3).astype(BF16)
        _rows_to_tiles(ys_ref, _bdot(hid, w2b[...]))

    @pl.when(i >= nu_ref[0])
    def _():
        ys_ref[...] = jnp.zeros_like(ys_ref)


def _experts(tile_expert, n_used, run_index, next_expert, src_rows, u2p, w1, w3, w2):
    n_tiles, _, tmx = src_rows.shape
    ne, d, f = w1.shape
    hbm = pl.BlockSpec(memory_space=pl.ANY)
    return pl.pallas_call(
        _expert_body,
        grid_spec=pltpu.PrefetchScalarGridSpec(
            num_scalar_prefetch=4,
            grid=(n_tiles,),
            in_specs=[pl.BlockSpec((1, 1, tmx), lambda i, *_: (i, 0, 0), memory_space=pltpu.SMEM),
                      hbm, hbm, hbm, hbm],
            out_specs=pl.BlockSpec((tmx * SUBLANES, LANES), lambda i, *_: (i, 0)),
            scratch_shapes=[pltpu.VMEM(u2p.shape, jnp.uint32),
                            pltpu.VMEM((tmx * PACK_ROWS, LANES), jnp.uint32),
                            pltpu.VMEM((tmx * SUBLANES, LANES), F32),
                            pltpu.VMEM((2, d, f), F32), pltpu.VMEM((2, d, f), F32), pltpu.VMEM((2, f, d), F32),
                            pltpu.VMEM((d, f), BF16), pltpu.VMEM((d, f), BF16), pltpu.VMEM((f, d), BF16),
                            pltpu.SemaphoreType.DMA(()), pltpu.SemaphoreType.DMA((2,))]),
        out_shape=jax.ShapeDtypeStruct((n_tiles * tmx * SUBLANES, LANES), F32),
        compiler_params=_params("arbitrary"),
        name="experts",
    )(tile_expert, n_used, run_index, next_expert, src_rows, u2p, w1, w3, w2)


def _final_body(p_ref, pn_ref, h_ref, route_ref, mod_ref, modf_ref, gf_ref, ys_ref, o_ref, y_buf, sem):
    ts = h_ref.shape[1]
    step = pl.program_id(0) * pl.num_programs(1) + pl.program_id(1)
    n_steps = pl.num_programs(0) * pl.num_programs(1)
    cur = lax.rem(step, 2)

    def request_rows(ref, half):
        for k in range(TOP_K):
            for r in range(ts):
                _row_copy(ys_ref, ref[0, 0, k * ts + r], y_buf.at[half, k], r * SUBLANES,
                          sem.at[half]).start(priority=r % 2)

    @pl.when(step == 0)
    def _():
        request_rows(p_ref, 0)

    @pl.when(step + 1 < n_steps)
    def _():
        request_rows(pn_ref, 1 - cur)

    for k in range(TOP_K):
        pltpu.make_async_copy(ys_ref.at[pl.ds(0, ts * SUBLANES), :], y_buf.at[cur, k], sem.at[cur]).wait()

    route = route_ref[0]
    y2 = (route[:, 2:3] * _tiles_to_rows(y_buf.at[cur, 0], ts)
          + route[:, 3:4] * _tiles_to_rows(y_buf.at[cur, 1], ts))
    h = h_ref[0] + mod_ref[0, 5:6, :] * y2
    o_ref[0] = _rms(h, gf_ref[...]) * (1.0 + modf_ref[0, 1:2, :]) + modf_ref[0, 0:1, :]


def _final(slot_rows, h, route, mod3, modf3, gf, ys, ts):
    nb, s, d = h.shape
    nt = s // ts
    last = nb * nt - 1

    def tok(width):
        return pl.BlockSpec((1, ts, width), lambda b, i: (b, i, 0))

    def slots(ahead):
        return pl.BlockSpec((1, 1, TOP_K * ts), lambda b, i: (jnp.minimum(b * nt + i + ahead, last), 0, 0),
                            memory_space=pltpu.SMEM)

    return pl.pallas_call(
        _final_body,
        grid=(nb, nt),
        in_specs=[slots(0), slots(1), tok(d), tok(LANES),
                  pl.BlockSpec((1, N_MOD, d), lambda b, i: (b, 0, 0)),
                  pl.BlockSpec((1, 2, d), lambda b, i: (b, 0, 0)),
                  pl.BlockSpec((1, d), lambda b, i: (0, 0)),
                  pl.BlockSpec(memory_space=pl.ANY)],
        out_specs=tok(d),
        out_shape=jax.ShapeDtypeStruct((nb, s, d), F32),
        scratch_shapes=[pltpu.VMEM((2, TOP_K, ts * SUBLANES, LANES), F32), pltpu.SemaphoreType.DMA((2,))],
        compiler_params=_params("arbitrary", "arbitrary"),
        name="final",
    )(slot_rows, slot_rows, h, route, mod3, modf3, gf.reshape(1, d), ys)


def _plan(routet, cnt, ts, ti, tmx, n_tiles):
    t = routet.shape[1]
    counts = cnt[0, :N_EXPERTS].astype(jnp.int32)
    tiles = (counts + (tmx - 1)) // tmx
    tile_end = jnp.cumsum(tiles)
    offs = ((tile_end - tiles) * tmx).astype(jnp.int32)
    n_used = tile_end[-1:]
    tile_ids = jnp.minimum(jnp.arange(n_tiles, dtype=jnp.int32), n_used[0] - 1)
    tile_expert = jnp.sum((tile_ids[:, None] >= tile_end[None, :]).astype(jnp.int32), axis=1)
    run_index = jnp.cumsum(jnp.concatenate([jnp.zeros((1,), jnp.int32),
                                            (tile_expert[1:] != tile_expert[:-1]).astype(jnp.int32)]))
    same_run = run_index[:, None] + 1 == run_index[None, :]
    has_next = jnp.any(same_run, axis=1)
    next_expert = jnp.where(has_next, jnp.max(jnp.where(same_run, tile_expert[None, :], 0), axis=1), tile_expert)

    eid = routet[0:2].astype(jnp.int32)
    experts = jnp.arange(N_EXPERTS, dtype=jnp.int32)[:, None, None]
    slot = routet[4:6].astype(jnp.int32) + jnp.sum(jnp.where(eid[None] == experts, offs[:, None, None], 0), axis=0)
    def tiled(tile):
        return slot.reshape(TOP_K, t // tile, tile).transpose(1, 0, 2).reshape(t // tile, 1, TOP_K * tile)

    fill_lo = jnp.concatenate([offs + counts, tile_end[-1:] * tmx]).astype(jnp.int32)
    fill_hi = jnp.concatenate([tile_end * tmx, jnp.full((1,), n_tiles * tmx, jnp.int32)]).astype(jnp.int32)
    src = _invert(fill_lo, fill_hi, tiled(ti), n_tiles * tmx)
    tables = [a.astype(jnp.int32) for a in (tile_expert, n_used, run_index, next_expert)]
    return tiled(ts) * SUBLANES, src.reshape(n_tiles, 1, tmx), tables


def kernel(x, c, w_ada, b_ada, g_norm1, w_in, w_dw, b_dw, g_conv_ln, b_conv_ln, w_conv_pw, b_conv_pw,
           w_a2, b_a2, g_gla_norm, w_gla_o, w_out, g_norm2, w_router_g, b_router_g, w_router_e,
           b_router_e, w1, w3, w2, w_ada_f, b_ada_f, g_final):
    nb, s, d = x.shape
    assert w_ada.shape[0] == 1, "single-layer block"
    assert d == 2 * PACK_ROWS * LANES, "packed token rows assume D_MODEL = 1024"
    tm = min(512, s)
    tc = min(256, s)
    ts = min(256, s)
    ti = min(1024, s)
    tmx = 256
    t = nb * s
    n_tiles = (t * TOP_K) // tmx + N_EXPERTS
    mod3 = _ada(c, w_ada[0], b_ada[0]).reshape(nb, N_MOD, d)
    modf3 = _ada(c, w_ada_f, b_ada_f).reshape(nb, 2, d)
    z, q, k, v, rs, lg, gt, lg_sums = _proj(x, mod3, g_norm1[0], w_in[0], w_a2[0], b_a2[0], tm)
    og = _gla(q, k, lg, lg_sums, v, rs, g_gla_norm[0], tc)
    h, u2, logits = _merge(z, og, gt, x, mod3, w_dw[0], b_dw[0], g_conv_ln[0], b_conv_ln[0],
                           w_conv_pw[0], b_conv_pw[0], w_gla_o[0], w_out[0], g_norm2[0],
                           w_router_g[0], b_router_g[0], w_router_e[0], b_router_e[0], ts)
    route, routet, cnt = _route_call(logits.reshape(t, LANES), min(ROUTE_ROWS, t), ts)
    slot_rows, src_rows, tables = _plan(routet, cnt, ts, ti, tmx, n_tiles)
    ys = _experts(*tables, src_rows, u2, w1[0], w3[0], w2[0])
    return _final(slot_rows, h, route.reshape(nb, s, LANES), mod3, modf3, g_final, ys, ts)
```

```python
import functools

import jax
import jax.numpy as jnp
from jax import lax
from jax.experimental import pallas as pl
from jax.experimental.pallas import tpu as pltpu

F32 = jnp.float32
BF16 = jnp.bfloat16

EPS = 1e-6
CONV_DIM = 512
CONV_WIDTH = 31
GLA_HEADS = 4
GLA_DK = 128
GLA_DV = 256
GLA_LOWRANK = 16
GLA_TAU = 16.0
QK_DIM = GLA_HEADS * GLA_DK
V_DIM = GLA_HEADS * GLA_DV
N_GROUPS = 4
EXPERTS_PER_GROUP = 8
N_EXPERTS = N_GROUPS * EXPERTS_PER_GROUP
TOP_K = 2
N_MOD = 6

LANES = 128
SUBLANES = 8
PACK_ROWS = 4
CONV_ROWS = 32
ROUTE_ROWS = 2048
CLEAR_UNROLL = 16
CONV_HALO = 32
GLA_CHUNK = 128
GLA_FACTOR_RANGE = 60.0
VMEM_LIMIT = 56 * 1024 * 1024


def _bdot(a, b):
    return jnp.dot(a, b, preferred_element_type=F32)


def _split(a):
    hi = a.astype(BF16)
    lo = (a - hi.astype(F32)).astype(BF16)
    return hi, lo


def _dot3(a, b):
    ah, al = _split(a)
    bh, bl = _split(b)
    return _bdot(ah, bh) + (_bdot(ah, bl) + _bdot(al, bh))


def _sigmoid(x):
    return 1.0 / (1.0 + jnp.exp(-x))


def _rms(x, g):
    ms = jnp.mean(x * x, axis=-1, keepdims=True)
    return x * lax.rsqrt(ms + EPS) * g


def _rows_to_tiles(ref, val):
    n = val.shape[0]
    for j in range(val.shape[1] // LANES):
        ref[pl.ds(j, n, stride=SUBLANES), :] = val[:, j * LANES:(j + 1) * LANES]


def _tiles_to_rows(ref, n):
    return jnp.concatenate([ref[pl.ds(j, n, stride=SUBLANES), :] for j in range(SUBLANES)], axis=-1)


def _pack_rows(ref, val, tiles):
    _rows_to_tiles(tiles, val)
    ref[...] = pltpu.bitcast(tiles[...].astype(BF16), jnp.uint32)


def _unpack_rows(ref, n, tiles):
    tiles[...] = pltpu.bitcast(ref[...], BF16).astype(F32)
    return _tiles_to_rows(tiles, n).astype(BF16)


def _params(*sem):
    return pltpu.CompilerParams(dimension_semantics=sem, vmem_limit_bytes=VMEM_LIMIT)


def _ada_body(c_ref, w_ref, b_ref, o_ref):
    c = c_ref[...]
    o_ref[...] = _dot3(c * _sigmoid(c), w_ref[...]) + b_ref[...]


def _ada(c, w, b, tn=1024):
    nb, d = c.shape
    n = w.shape[1]
    return pl.pallas_call(
        _ada_body,
        grid=(n // tn,),
        in_specs=[pl.BlockSpec((nb, d), lambda j: (0, 0)),
                  pl.BlockSpec((d, tn), lambda j: (0, j)),
                  pl.BlockSpec((1, tn), lambda j: (0, j))],
        out_specs=pl.BlockSpec((nb, tn), lambda j: (0, j)),
        out_shape=jax.ShapeDtypeStruct((nb, n), F32),
        compiler_params=_params("arbitrary"),
        name="ada",
    )(c, w, b.reshape(1, n))


def _proj_body(x_ref, mod_ref, g1_ref, wm_ref, wa1_ref, wg_ref, wa2_ref, ba2_ref,
               z_ref, q_ref, k_ref, v_ref, rs_ref, lg_ref, gt_ref, ls_ref):
    x = x_ref[0]
    u = (_rms(x, g1_ref[...]) * (1.0 + mod_ref[0, 1:2, :]) + mod_ref[0, 0:1, :]).astype(BF16)
    c0 = 2 * CONV_DIM
    c1 = c0 + 2 * QK_DIM
    c2 = c1 + V_DIM
    c3 = c2 + V_DIM
    pc = _bdot(u, wm_ref[:, 0:c0])
    z_ref[0] = pc[:, :CONV_DIM] * _sigmoid(pc[:, CONV_DIM:])
    qk = _bdot(u, wm_ref[:, c0:c1])
    q_ref[0] = qk[:, :QK_DIM] * (GLA_DK ** -0.5)
    k_ref[0] = qk[:, QK_DIM:]
    v_ref[0] = _bdot(u, wm_ref[:, c1:c2]).astype(BF16)
    r = _bdot(u, wm_ref[:, c2:c3])
    rs_ref[0] = (r * _sigmoid(r)).astype(BF16)
    a1 = _bdot(u, wa1_ref[...])
    xg = _dot3(a1, wa2_ref[...]) + ba2_ref[...]
    lg = (jnp.minimum(xg, 0.0) - jnp.log1p(jnp.exp(-jnp.abs(xg)))) * (1.0 / GLA_TAU)
    lg_ref[0] = lg
    ls_ref[0, 0] = jnp.concatenate([jnp.sum(lg[lo:lo + GLA_CHUNK, :], axis=0, keepdims=True)
                                    for lo in range(0, lg.shape[0], GLA_CHUNK)], axis=0)
    gt_ref[0] = _sigmoid(_bdot(u, wg_ref[...])).astype(BF16)


def _wsplit_body(wt_ref, wm_ref, wa1_ref, wg_ref, buf, sem, *, c3):
    cw = buf.shape[1]
    n_main = c3 // cw
    j = pl.program_id(0)
    cur = lax.rem(j, 2)

    def chunk(step, half):
        start = jnp.where(step <= n_main, step * cw, c3 + GLA_LOWRANK + (step - n_main - 1) * cw)
        return pltpu.make_async_copy(wt_ref.at[pl.ds(pl.multiple_of(start, SUBLANES), cw), :], buf.at[half],
                                     sem.at[half])

    @pl.when(j == 0)
    def _():
        chunk(j, 0).start()

    @pl.when(j + 1 < pl.num_programs(0))
    def _():
        chunk(j + 1, 1 - cur).start()

    chunk(j, cur).wait()

    @pl.when(j < n_main)
    def _():
        wm_ref[...] = buf[cur].T.astype(BF16)

    @pl.when(j == n_main)
    def _():
        lane = lax.broadcasted_iota(jnp.int32, wa1_ref.shape, 1)
        wa1_ref[...] = jnp.where(lane < GLA_LOWRANK, buf[cur, 0:LANES, :].T, 0.0).astype(BF16)

    @pl.when(j > n_main)
    def _():
        wg_ref[...] = buf[cur].T.astype(BF16)


def _wsplit(wt, c3, cw=512):
    n, d = wt.shape
    ng = n - c3 - GLA_LOWRANK
    n_main = c3 // cw
    return pl.pallas_call(
        functools.partial(_wsplit_body, c3=c3),
        grid=(n_main + 1 + ng // cw,),
        in_specs=[pl.BlockSpec(memory_space=pl.ANY)],
        out_specs=[pl.BlockSpec((d, cw), lambda j: (0, jnp.minimum(j, n_main - 1))),
                   pl.BlockSpec((d, LANES), lambda j: (0, 0)),
                   pl.BlockSpec((d, cw), lambda j: (0, jnp.clip(j - n_main - 1, 0, ng // cw - 1)))],
        out_shape=[jax.ShapeDtypeStruct((d, c3), BF16), jax.ShapeDtypeStruct((d, LANES), BF16),
                   jax.ShapeDtypeStruct((d, ng), BF16)],
        scratch_shapes=[pltpu.VMEM((2, cw, d), F32), pltpu.SemaphoreType.DMA((2,))],
        compiler_params=_params("arbitrary"),
        name="wsplit",
    )(wt)


def _proj(x, mod3, g1, w_in, w_a2, b_a2, tm):
    nb, s, d = x.shape
    c3 = 2 * CONV_DIM + 2 * QK_DIM + 2 * V_DIM
    wm, wa1, wg = _wsplit(jnp.swapaxes(w_in, 0, 1), c3)
    wa2 = jnp.pad(w_a2, ((0, LANES - GLA_LOWRANK), (0, 0)))
    ng = wg.shape[1]

    def tok(width):
        return pl.BlockSpec((1, tm, width), lambda b, i: (b, i, 0))

    def const(shape):
        return pl.BlockSpec(shape, lambda b, i: (0,) * len(shape))

    def out(width, dt):
        return jax.ShapeDtypeStruct((nb, s, width), dt)

    return pl.pallas_call(
        _proj_body,
        grid=(nb, s // tm),
        in_specs=[tok(d),
                  pl.BlockSpec((1, N_MOD, d), lambda b, i: (b, 0, 0)),
                  const((1, d)), const(wm.shape), const(wa1.shape), const(wg.shape),
                  const(wa2.shape), const((1, QK_DIM))],
        out_specs=[tok(CONV_DIM), tok(QK_DIM), tok(QK_DIM), tok(V_DIM), tok(V_DIM), tok(QK_DIM), tok(ng),
                   pl.BlockSpec((1, 1, tm // GLA_CHUNK, QK_DIM), lambda b, i: (b, i, 0, 0))],
        out_shape=[out(CONV_DIM, F32), out(QK_DIM, F32), out(QK_DIM, F32), out(V_DIM, BF16),
                   out(V_DIM, BF16), out(QK_DIM, F32), out(ng, BF16),
                   jax.ShapeDtypeStruct((nb, s // tm, tm // GLA_CHUNK, QK_DIM), F32)],
        compiler_params=_params("arbitrary", "arbitrary"),
        name="proj",
    )(x, mod3, g1.reshape(1, d), wm, wa1, wg, wa2, b_a2.reshape(1, QK_DIM))


def _gla_body(ok_ref, q_ref, k_ref, lg_ref, v_ref, rs_ref, gn_ref, o_ref, st_ref, b_s, oi_s, *, n_chunks):
    cl = GLA_CHUNK

    @pl.when(pl.program_id(1) == 0)
    def _():
        st_ref[...] = jnp.zeros_like(st_ref)

    row = lax.broadcasted_iota(jnp.int32, (cl, cl), 0)
    col = lax.broadcasted_iota(jnp.int32, (cl, cl), 1)
    causal = col <= row
    tri = jnp.where(causal, 1.0, 0.0).astype(BF16)
    chunks = [slice(c * cl, (c + 1) * cl) for c in range(n_chunks)]
    heads = range(GLA_HEADS)
    ksl = [slice(h * GLA_DK, (h + 1) * GLA_DK) for h in heads]
    vsl = [slice(h * GLA_DV, (h + 1) * GLA_DV) for h in heads]

    def cumulative(rows):
        gh, gl = _split(lg_ref[0, rows, :])
        return _bdot(tri, gh) + _bdot(tri, gl)

    def intra_pairwise(h, c):
        kf = k_ref[0, chunks[c], ksl[h]]
        vf = v_ref[0, chunks[c], vsl[h]].astype(F32)
        bh = b_s[chunks[c], ksl[h]]
        key = lax.broadcasted_iota(jnp.int32, (cl, 1), 0)

        def group(g, carry):
            r0 = pl.multiple_of(g * SUBLANES, SUBLANES)
            q8 = q_ref[0, pl.ds(c * cl + r0, SUBLANES), ksl[h]]
            b8 = b_s[pl.ds(c * cl + r0, SUBLANES), ksl[h]]
            out_rows = []
            for r in range(SUBLANES):
                diff = jnp.where(key <= r0 + r, b8[r:r + 1, :] - bh, -jnp.inf)
                att = jnp.sum(jnp.exp(diff) * kf * q8[r:r + 1, :], axis=-1, keepdims=True)
                out_rows.append(jnp.sum(att * vf, axis=0, keepdims=True))
            oi_s[h, pl.ds(c * cl + r0, SUBLANES), :] = jnp.concatenate(out_rows, axis=0)
            return carry

        lax.fori_loop(0, cl // SUBLANES, group, 0)
        return oi_s[h, chunks[c], :]

    def run(factored):
        cums = [cumulative(rows) for rows in chunks]
        if not factored:
            for rows, b in zip(chunks, cums):
                b_s[rows, :] = b
        qes, kts, bts, klts, decays = [], [], [], [], []
        for rows, b in zip(chunks, cums):
            qes.append((q_ref[0, rows, :] * jnp.exp(b)).astype(BF16))
            kt = k_ref[0, rows, :].T
            bt = b.T
            bl = bt[:, cl - 1:cl]
            kts.append(kt)
            bts.append(bt)
            klts.append((kt * jnp.exp(bl - bt)).astype(BF16))
            decays.append(jnp.exp(bl))
        intra = {}
        if factored:
            kets = [(kt * jnp.exp(-bt)).astype(BF16) for kt, bt in zip(kts, bts)]
            atts = {}
            for c in range(n_chunks):
                for h in heads:
                    att = _bdot(qes[c][:, ksl[h]], kets[c][ksl[h], :])
                    atts[h, c] = jnp.where(causal, att, 0.0).astype(BF16)
            for c, rows in enumerate(chunks):
                for h in heads:
                    intra[h, c] = _bdot(atts[h, c], v_ref[0, rows, vsl[h]])
        else:
            for c in range(n_chunks):
                for h in heads:
                    intra[h, c] = intra_pairwise(h, c)
        updates = {}
        for c, rows in enumerate(chunks):
            for h in heads:
                updates[h, c] = _bdot(klts[c][ksl[h], :], v_ref[0, rows, vsl[h]])

        outs = {}
        states = []
        for h in heads:
            state = st_ref[h]
            for c, rows in enumerate(chunks):
                o = intra[h, c] + _bdot(qes[c][:, ksl[h]], state.astype(BF16))
                state = decays[c][ksl[h], :] * state + updates[h, c]
                outs[h, c] = (_rms(o, gn_ref[:, vsl[h]]) * rs_ref[0, rows, vsl[h]].astype(F32)).astype(BF16)
            states.append(state)
        for c, rows in enumerate(chunks):
            o_ref[0, rows, :] = jnp.concatenate([outs[h, c] for h in heads], axis=-1)
        for h in heads:
            st_ref[h] = states[h]

    in_range = ok_ref[pl.program_id(0) * pl.num_programs(1) + pl.program_id(1)] != 0
    pl.when(in_range)(functools.partial(run, True))
    pl.when(jnp.logical_not(in_range))(functools.partial(run, False))


def _gla(q, k, lg, lg_sums, v, rs, gn, tc):
    nb, s, _ = q.shape
    lowest = jnp.min(lg_sums.reshape(nb, s // tc, (tc // GLA_CHUNK) * QK_DIM), axis=-1)
    in_range = (lowest > -GLA_FACTOR_RANGE).astype(jnp.int32).reshape(-1)

    def tok(width):
        return pl.BlockSpec((1, tc, width), lambda b, i, ok: (b, i, 0))

    return pl.pallas_call(
        functools.partial(_gla_body, n_chunks=tc // GLA_CHUNK),
        grid_spec=pltpu.PrefetchScalarGridSpec(
            num_scalar_prefetch=1,
            grid=(nb, s // tc),
            in_specs=[tok(QK_DIM), tok(QK_DIM), tok(QK_DIM), tok(V_DIM), tok(V_DIM),
                      pl.BlockSpec((1, V_DIM), lambda b, i, ok: (0, 0))],
            out_specs=tok(V_DIM),
            scratch_shapes=[pltpu.VMEM((GLA_HEADS, GLA_DK, GLA_DV), F32),
                            pltpu.VMEM((tc, QK_DIM), F32),
                            pltpu.VMEM((GLA_HEADS, tc, GLA_DV), F32)]),
        out_shape=jax.ShapeDtypeStruct((nb, s, V_DIM), BF16),
        compiler_params=_params("arbitrary", "arbitrary"),
        name="gla",
    )(in_range, q, k, lg, v, rs, gn.reshape(1, V_DIM))


def _route(logits, cnt, sub):
    ts = logits.shape[0]
    lane = lax.broadcasted_iota(jnp.int32, (ts, LANES), 1).astype(F32)
    ninf = -jnp.inf
    lgm = jnp.where(lane < N_GROUPS, logits, ninf)
    gmax = jnp.max(lgm, axis=-1, keepdims=True)
    gsel = jnp.min(jnp.where(lgm == gmax, lane, float(LANES)), axis=-1, keepdims=True)
    wg = 1.0 / jnp.sum(jnp.exp(lgm - gmax), axis=-1, keepdims=True)
    base = N_GROUPS + EXPERTS_PER_GROUP * gsel
    le = jnp.where(lane >= base, jnp.where(lane < base + EXPERTS_PER_GROUP, logits, ninf), ninf)
    v1 = jnp.max(le, axis=-1, keepdims=True)
    i1 = jnp.min(jnp.where(le == v1, lane, float(LANES)), axis=-1, keepdims=True)
    le2 = jnp.where(lane == i1, ninf, le)
    v2 = jnp.max(le2, axis=-1, keepdims=True)
    i2 = jnp.min(jnp.where(le2 == v2, lane, float(LANES)), axis=-1, keepdims=True)
    e21 = jnp.exp(v2 - v1)
    w1 = wg / (1.0 + e21)
    w2 = w1 * e21
    eid1 = i1 - N_GROUPS
    eid2 = i2 - N_GROUPS
    oh1 = jnp.where(lane == eid1, 1.0, 0.0)
    oh2 = jnp.where(lane == eid2, 1.0, 0.0)
    ohs = oh1 + oh2
    row = lax.broadcasted_iota(jnp.int32, (sub, sub), 0)
    col = lax.broadcasted_iota(jnp.int32, (sub, sub), 1)
    before = jnp.where(col < row, 1.0, 0.0).astype(BF16)
    tots = []
    for lo in range(0, ts, sub):
        piece = ohs[lo:lo + sub, :]
        tots.append(cnt + _bdot(before, piece.astype(BF16)))
        cnt = cnt + jnp.sum(piece, axis=0, keepdims=True)
    tot = jnp.concatenate(tots, axis=0)
    rank1 = jnp.sum(oh1 * tot, axis=-1, keepdims=True)
    rank2 = jnp.sum(oh2 * tot, axis=-1, keepdims=True)
    packed = jnp.where(lane == 0.0, eid1,
             jnp.where(lane == 1.0, eid2,
             jnp.where(lane == 2.0, w1,
             jnp.where(lane == 3.0, w2,
             jnp.where(lane == 4.0, rank1,
             jnp.where(lane == 5.0, rank2, 0.0))))))
    return packed, cnt


def _merge_body(z_ref, og_ref, gt_ref, x_ref, mod_ref, wdw_ref, bdw_ref, lng_ref, lnb_ref,
                wpw_ref, bpw_ref, wgo_ref, wout_ref, g2_ref, wr_ref, br_ref,
                h_ref, u2_ref, logit_ref, zbuf, zsh, u2t):
    ts = z_ref.shape[1]
    d = x_ref.shape[2]
    first_tile = pl.program_id(1) == 0

    @pl.when(first_tile)
    def _():
        zbuf[0:CONV_HALO, :] = jnp.zeros((CONV_HALO, CONV_DIM), F32)

    zbuf[CONV_HALO:CONV_HALO + ts, :] = z_ref[0]
    span = ts + CONV_HALO - SUBLANES
    for r in range(1, SUBLANES):
        zsh[r - 1] = zbuf[r:r + span, :]
    off = CONV_HALO - (CONV_WIDTH - 1)
    pieces = []
    for blk in range(ts // CONV_ROWS):
        acc = None
        for j in range(CONV_WIDTH):
            a, r = divmod(off + j, SUBLANES)
            lo = a * SUBLANES + blk * CONV_ROWS
            src = zbuf[lo:lo + CONV_ROWS, :] if r == 0 else zsh[r - 1, lo:lo + CONV_ROWS, :]
            term = src * jnp.concatenate([wdw_ref[j]] * (CONV_ROWS // SUBLANES), axis=0)
            acc = term if acc is None else acc + term
        conv = acc + bdw_ref[...]
        mu = jnp.mean(conv, axis=-1, keepdims=True)
        xc = conv - mu
        var = jnp.mean(xc * xc, axis=-1, keepdims=True)
        ln = xc * lax.rsqrt(var + EPS) * lng_ref[...] + lnb_ref[...]
        pieces.append((ln * _sigmoid(ln)).astype(BF16))
    zbuf[0:CONV_HALO, :] = zbuf[ts:ts + CONV_HALO, :]
    y_conv = _bdot(jnp.concatenate(pieces, axis=0), wpw_ref[...]) + bpw_ref[...]
    y_gla = _bdot(og_ref[0], wgo_ref[...])
    merged = gt_ref[0, :, 0:d].astype(F32) * y_conv + gt_ref[0, :, d:2 * d].astype(F32) * y_gla
    y = _bdot(merged.astype(BF16), wout_ref[...])
    h = x_ref[0] + mod_ref[0, 2:3, :] * y
    h_ref[0] = h
    u2 = _rms(h, g2_ref[...]) * (1.0 + mod_ref[0, 4:5, :]) + mod_ref[0, 3:4, :]
    _pack_rows(u2_ref, u2, u2t)
    logit_ref[0] = _dot3(u2, wr_ref[...]) + br_ref[...]


def _merge(z, og, gt, x, mod3, w_dw, b_dw, ln_g, ln_b, w_pw, b_pw, w_go, w_out, g2, w_rg, b_rg, w_re, b_re, ts):
    nb, s, d = x.shape
    npad = LANES - N_GROUPS - N_EXPERTS
    wr = jnp.pad(jnp.concatenate([w_rg, w_re], axis=1), ((0, 0), (0, npad)))
    br = jnp.pad(jnp.concatenate([b_rg, b_re]), (0, npad)).reshape(1, LANES)

    def tok(width):
        return pl.BlockSpec((1, ts, width), lambda b, i: (b, i, 0))

    def const(shape):
        return pl.BlockSpec(shape, lambda b, i: (0,) * len(shape))

    def row(v):
        return v.reshape(1, v.shape[-1])

    return pl.pallas_call(
        _merge_body,
        grid=(nb, s // ts),
        in_specs=[tok(CONV_DIM), tok(V_DIM), tok(2 * d), tok(d),
                  pl.BlockSpec((1, N_MOD, d), lambda b, i: (b, 0, 0)),
                  const((CONV_WIDTH, SUBLANES, CONV_DIM)), const((1, CONV_DIM)), const((1, CONV_DIM)), const((1, CONV_DIM)),
                  const((CONV_DIM, d)), const((1, d)), const((V_DIM, d)), const((d, d)), const((1, d)),
                  const((d, LANES)), const((1, LANES))],
        out_specs=[tok(d), pl.BlockSpec((ts * PACK_ROWS, LANES), lambda b, i: (b * (s // ts) + i, 0)), tok(LANES)],
        out_shape=[jax.ShapeDtypeStruct((nb, s, d), F32),
                   jax.ShapeDtypeStruct((nb * s * PACK_ROWS, LANES), jnp.uint32),
                   jax.ShapeDtypeStruct((nb, s, LANES), F32)],
        scratch_shapes=[pltpu.VMEM((CONV_HALO + ts, CONV_DIM), F32),
                        pltpu.VMEM((SUBLANES - 1, CONV_HALO + ts - SUBLANES, CONV_DIM), F32),
                        pltpu.VMEM((ts * SUBLANES, LANES), F32)],
        compiler_params=_params("arbitrary", "arbitrary"),
        name="merge",
    )(z, og, gt, x, mod3, jnp.broadcast_to(w_dw.reshape(CONV_WIDTH, 1, CONV_DIM), (CONV_WIDTH, SUBLANES, CONV_DIM)), row(b_dw), row(ln_g), row(ln_b),
      w_pw.astype(BF16), row(b_pw), w_go.astype(BF16), w_out.astype(BF16), row(g2), wr, br)


def _route_body(logit_ref, route_ref, routet_ref, cnt_ref, cnt_sc, *, sub):
    @pl.when(pl.program_id(0) == 0)
    def _():
        cnt_sc[...] = jnp.zeros_like(cnt_sc)

    packed, cnt = _route(logit_ref[...], cnt_sc[...], sub)
    route_ref[...] = packed
    routet_ref[...] = packed.T[0:SUBLANES, :]
    cnt_sc[...] = cnt
    cnt_ref[...] = jnp.broadcast_to(cnt, cnt_ref.shape)


def _route_call(logits, tr, sub):
    t = logits.shape[0]
    return pl.pallas_call(
        functools.partial(_route_body, sub=sub),
        grid=(t // tr,),
        in_specs=[pl.BlockSpec((tr, LANES), lambda i: (i, 0))],
        out_specs=[pl.BlockSpec((tr, LANES), lambda i: (i, 0)), pl.BlockSpec((SUBLANES, tr), lambda i: (0, i)),
                   pl.BlockSpec((SUBLANES, LANES), lambda i: (0, 0))],
        out_shape=[jax.ShapeDtypeStruct((t, LANES), F32), jax.ShapeDtypeStruct((SUBLANES, t), F32),
                   jax.ShapeDtypeStruct((SUBLANES, LANES), F32)],
        scratch_shapes=[pltpu.VMEM((1, LANES), F32)],
        compiler_params=_params("arbitrary"),
        name="route",
    )(logits)


def _row_copy(src, i, dst, j, sem):
    return pltpu.make_async_copy(src.at[pl.ds(pl.multiple_of(i, SUBLANES), SUBLANES), :],
                                 dst.at[pl.ds(pl.multiple_of(j, SUBLANES), SUBLANES), :], sem)


def _invert_body(lo_ref, hi_ref, slot_ref, src_ref):
    i = pl.program_id(0)
    ts = slot_ref.shape[-1] // TOP_K

    @pl.when(i == 0)
    def _():
        for e in range(lo_ref.shape[0]):
            lo = lo_ref[e]
            hi = hi_ref[e]

            def clear(p, carry, lo=lo, hi=hi):
                for j in range(CLEAR_UNROLL):
                    src_ref[jnp.minimum(lo + p * CLEAR_UNROLL + j, hi - 1)] = 0
                return carry

            trips = lax.shift_right_logical(hi - lo + (CLEAR_UNROLL - 1), CLEAR_UNROLL.bit_length() - 1)
            lax.fori_loop(0, trips, clear, 0)

    for r in range(ts):
        row = (i * ts + r) * PACK_ROWS
        for k in range(TOP_K):
            src_ref[slot_ref[0, 0, k * ts + r]] = row


def _invert(fill_lo, fill_hi, slots, n_slots):
    nt, _, width = slots.shape
    return pl.pallas_call(
        _invert_body,
        grid_spec=pltpu.PrefetchScalarGridSpec(
            num_scalar_prefetch=2,
            grid=(nt,),
            in_specs=[pl.BlockSpec((1, 1, width), lambda i, lo, hi: (i, 0, 0), memory_space=pltpu.SMEM)],
            out_specs=pl.BlockSpec(memory_space=pltpu.SMEM)),
        out_shape=jax.ShapeDtypeStruct((n_slots,), jnp.int32),
        compiler_params=_params("arbitrary"),
        name="invert",
    )(fill_lo, fill_hi, slots)


def _expert_body(te_ref, nu_ref, seg_ref, nxt_ref, src_ref, u2_ref, w1_ref, w3_ref, w2_ref, ys_ref,
                 u2v, xg, xt, w1f, w3f, w2f, w1b, w3b, w2b, sem, wsem):
    i = pl.program_id(0)
    tmx = src_ref.shape[-1]
    expert = te_ref[i]
    first = i == 0
    changed = jnp.logical_or(first, expert != te_ref[jnp.maximum(i - 1, 0)])
    slot = lax.rem(seg_ref[i], 2)

    def weight_copies(e, s):
        return [pltpu.make_async_copy(src.at[e], dst.at[s], wsem.at[s])
                for src, dst in ((w1_ref, w1f), (w3_ref, w3f), (w2_ref, w2f))]

    load = pltpu.make_async_copy(u2_ref, u2v, sem)

    @pl.when(first)
    def _():
        load.start()
        for cp in weight_copies(expert, 0):
            cp.start()

    @pl.when(changed)
    def _():
        for cp in weight_copies(expert, slot):
            cp.wait()
        w1b[...] = w1f[slot].astype(BF16)
        w3b[...] = w3f[slot].astype(BF16)
        w2b[...] = w2f[slot].astype(BF16)

    @pl.when(jnp.logical_and(changed, nxt_ref[i] != expert))
    def _():
        for cp in weight_copies(nxt_ref[i], 1 - slot):
            cp.start()

    @pl.when(first)
    def _():
        load.wait()

    @pl.when(i < nu_ref[0])
    def _():
        for r in range(tmx):
            row = pl.multiple_of(src_ref[0, 0, r], PACK_ROWS)
            xg[r * PACK_ROWS:(r + 1) * PACK_ROWS, :] = u2v[pl.ds(row, PACK_ROWS), :]
        x = _unpack_rows(xg, tmx, xt)
        h1 = _bdot(x, w1b[...])
        h3 = _bdot(x, w3b[...])
        hid = (h1 * _sigmoid(h1) * h3).astype(BF16)
        _rows_to_tiles(ys_ref, _bdot(hid, w2b[...]))

    @pl.when(i >= nu_ref[0])
    def _():
        ys_ref[...] = jnp.zeros_like(ys_ref)


def _experts(tile_expert, n_used, run_index, next_expert, src_rows, u2p, w1, w3, w2):
    n_tiles, _, tmx = src_rows.shape
    ne, d, f = w1.shape
    hbm = pl.BlockSpec(memory_space=pl.ANY)
    return pl.pallas_call(
        _expert_body,
        grid_spec=pltpu.PrefetchScalarGridSpec(
            num_scalar_prefetch=4,
            grid=(n_tiles,),
            in_specs=[pl.BlockSpec((1, 1, tmx), lambda i, *_: (i, 0, 0), memory_space=pltpu.SMEM),
                      hbm, hbm, hbm, hbm],
            out_specs=pl.BlockSpec((tmx * SUBLANES, LANES), lambda i, *_: (i, 0)),
            scratch_shapes=[pltpu.VMEM(u2p.shape, jnp.uint32),
                            pltpu.VMEM((tmx * PACK_ROWS, LANES), jnp.uint32),
                            pltpu.VMEM((tmx * SUBLANES, LANES), F32),
                            pltpu.VMEM((2, d, f), F32), pltpu.VMEM((2, d, f), F32), pltpu.VMEM((2, f, d), F32),
                            pltpu.VMEM((d, f), BF16), pltpu.VMEM((d, f), BF16), pltpu.VMEM((f, d), BF16),
                            pltpu.SemaphoreType.DMA(()), pltpu.SemaphoreType.DMA((2,))]),
        out_shape=jax.ShapeDtypeStruct((n_tiles * tmx * SUBLANES, LANES), F32),
        compiler_params=_params("arbitrary"),
        name="experts",
    )(tile_expert, n_used, run_index, next_expert, src_rows, u2p, w1, w3, w2)


def _final_body(p_ref, pn_ref, h_ref, route_ref, mod_ref, modf_ref, gf_ref, ys_ref, o_ref, y_buf, sem):
    ts = h_ref.shape[1]
    step = pl.program_id(0) * pl.num_programs(1) + pl.program_id(1)
    n_steps = pl.num_programs(0) * pl.num_programs(1)
    cur = lax.rem(step, 2)

    def request_rows(ref, half):
        for k in range(TOP_K):
            for r in range(ts):
                _row_copy(ys_ref, ref[0, 0, k * ts + r], y_buf.at[half, k], r * SUBLANES,
                          sem.at[half]).start(priority=r % 2)

    @pl.when(step == 0)
    def _():
        request_rows(p_ref, 0)

    @pl.when(step + 1 < n_steps)
    def _():
        request_rows(pn_ref, 1 - cur)

    for k in range(TOP_K):
        pltpu.make_async_copy(ys_ref.at[pl.ds(0, ts * SUBLANES), :], y_buf.at[cur, k], sem.at[cur]).wait()

    route = route_ref[0]
    y2 = (route[:, 2:3] * _tiles_to_rows(y_buf.at[cur, 0], ts)
          + route[:, 3:4] * _tiles_to_rows(y_buf.at[cur, 1], ts))
    h = h_ref[0] + mod_ref[0, 5:6, :] * y2
    o_ref[0] = _rms(h, gf_ref[...]) * (1.0 + modf_ref[0, 1:2, :]) + modf_ref[0, 0:1, :]


def _final(slot_rows, h, route, mod3, modf3, gf, ys, ts):
    nb, s, d = h.shape
    nt = s // ts
    last = nb * nt - 1

    def tok(width):
        return pl.BlockSpec((1, ts, width), lambda b, i: (b, i, 0))

    def slots(ahead):
        return pl.BlockSpec((1, 1, TOP_K * ts), lambda b, i: (jnp.minimum(b * nt + i + ahead, last), 0, 0),
                            memory_space=pltpu.SMEM)

    return pl.pallas_call(
        _final_body,
        grid=(nb, nt),
        in_specs=[slots(0), slots(1), tok(d), tok(LANES),
                  pl.BlockSpec((1, N_MOD, d), lambda b, i: (b, 0, 0)),
                  pl.BlockSpec((1, 2, d), lambda b, i: (b, 0, 0)),
                  pl.BlockSpec((1, d), lambda b, i: (0, 0)),
                  pl.BlockSpec(memory_space=pl.ANY)],
        out_specs=tok(d),
        out_shape=jax.ShapeDtypeStruct((nb, s, d), F32),
        scratch_shapes=[pltpu.VMEM((2, TOP_K, ts * SUBLANES, LANES), F32), pltpu.SemaphoreType.DMA((2,))],
        compiler_params=_params("arbitrary", "arbitrary"),
        name="final",
    )(slot_rows, slot_rows, h, route, mod3, modf3, gf.reshape(1, d), ys)


def _plan(routet, cnt, ts, ti, tmx, n_tiles):
    t = routet.shape[1]
    counts = cnt[0, :N_EXPERTS].astype(jnp.int32)
    tiles = (counts + (tmx - 1)) // tmx
    tile_end = jnp.cumsum(tiles)
    offs = ((tile_end - tiles) * tmx).astype(jnp.int32)
    n_used = tile_end[-1:]
    tile_ids = jnp.minimum(jnp.arange(n_tiles, dtype=jnp.int32), n_used[0] - 1)
    tile_expert = jnp.sum((tile_ids[:, None] >= tile_end[None, :]).astype(jnp.int32), axis=1)
    run_index = jnp.cumsum(jnp.concatenate([jnp.zeros((1,), jnp.int32),
                                            (tile_expert[1:] != tile_expert[:-1]).astype(jnp.int32)]))
    same_run = run_index[:, None] + 1 == run_index[None, :]
    has_next = jnp.any(same_run, axis=1)
    next_expert = jnp.where(has_next, jnp.max(jnp.where(same_run, tile_expert[None, :], 0), axis=1), tile_expert)

    eid = routet[0:2].astype(jnp.int32)
    experts = jnp.arange(N_EXPERTS, dtype=jnp.int32)[:, None, None]
    slot = routet[4:6].astype(jnp.int32) + jnp.sum(jnp.where(eid[None] == experts, offs[:, None, None], 0), axis=0)
    def tiled(tile):
        return slot.reshape(TOP_K, t // tile, tile).transpose(1, 0, 2).reshape(t // tile, 1, TOP_K * tile)

    fill_lo = jnp.concatenate([offs + counts, tile_end[-1:] * tmx]).astype(jnp.int32)
    fill_hi = jnp.concatenate([tile_end * tmx, jnp.full((1,), n_tiles * tmx, jnp.int32)]).astype(jnp.int32)
    src = _invert(fill_lo, fill_hi, tiled(ti), n_tiles * tmx)
    tables = [a.astype(jnp.int32) for a in (tile_expert, n_used, run_index, next_expert)]
    return tiled(ts) * SUBLANES, src.reshape(n_tiles, 1, tmx), tables


def kernel(x, c, w_ada, b_ada, g_norm1, w_in, w_dw, b_dw, g_conv_ln, b_conv_ln, w_conv_pw, b_conv_pw,
           w_a2, b_a2, g_gla_norm, w_gla_o, w_out, g_norm2, w_router_g, b_router_g, w_router_e,
           b_router_e, w1, w3, w2, w_ada_f, b_ada_f, g_final):
    nb, s, d = x.shape
    assert w_ada.shape[0] == 1, "single-layer block"
    assert d == 2 * PACK_ROWS * LANES, "packed token rows assume D_MODEL = 1024"
    tm = min(512, s)
    tc = min(256, s)
    tg = min(512, s)
    ts = min(256, s)
    ti = min(1024, s)
    tmx = 256
    t = nb * s
    n_tiles = (t * TOP_K) // tmx + N_EXPERTS
    mod3 = _ada(c, w_ada[0], b_ada[0]).reshape(nb, N_MOD, d)
    modf3 = _ada(c, w_ada_f, b_ada_f).reshape(nb, 2, d)
    z, q, k, v, rs, lg, gt, lg_sums = _proj(x, mod3, g_norm1[0], w_in[0], w_a2[0], b_a2[0], tm)
    og = _gla(q, k, lg, lg_sums, v, rs, g_gla_norm[0], tc)
    h, u2, logits = _merge(z, og, gt, x, mod3, w_dw[0], b_dw[0], g_conv_ln[0], b_conv_ln[0],
                           w_conv_pw[0], b_conv_pw[0], w_gla_o[0], w_out[0], g_norm2[0],
                           w_router_g[0], b_router_g[0], w_router_e[0], b_router_e[0], tg)
    route, routet, cnt = _route_call(logits.reshape(t, LANES), min(ROUTE_ROWS, t), ts)
    slot_rows, src_rows, tables = _plan(routet, cnt, ts, ti, tmx, n_tiles)
    ys = _experts(*tables, src_rows, u2, w1[0], w3[0], w2[0])
    return _final(slot_rows, h, route.reshape(nb, s, LANES), mod3, modf3, g_final, ys, ts)
```

```python
import functools

import jax
import jax.numpy as jnp
from jax import lax
from jax.experimental import pallas as pl
from jax.experimental.pallas import tpu as pltpu

F32 = jnp.float32
BF16 = jnp.bfloat16

EPS = 1e-6
CONV_DIM = 512
CONV_WIDTH = 31
GLA_HEADS = 4
GLA_DK = 128
GLA_DV = 256
GLA_LOWRANK = 16
GLA_TAU = 16.0
QK_DIM = GLA_HEADS * GLA_DK
V_DIM = GLA_HEADS * GLA_DV
N_GROUPS = 4
EXPERTS_PER_GROUP = 8
N_EXPERTS = N_GROUPS * EXPERTS_PER_GROUP
TOP_K = 2
N_MOD = 6

LANES = 128
SUBLANES = 8
PACK_ROWS = 4
CONV_ROWS = 32
FINAL_ROWS = 32
ROUTE_ROWS = 2048
CLEAR_UNROLL = 16
CONV_HALO = 32
GLA_CHUNK = 128
GLA_FACTOR_RANGE = 60.0
VMEM_LIMIT = 56 * 1024 * 1024


def _bdot(a, b):
    return jnp.dot(a, b, preferred_element_type=F32)


def _split(a):
    hi = a.astype(BF16)
    lo = (a - hi.astype(F32)).astype(BF16)
    return hi, lo


def _dot3(a, b):
    ah, al = _split(a)
    bh, bl = _split(b)
    return _bdot(ah, bh) + (_bdot(ah, bl) + _bdot(al, bh))


def _sigmoid(x):
    return 1.0 / (1.0 + jnp.exp(-x))


def _rms(x, g):
    ms = jnp.mean(x * x, axis=-1, keepdims=True)
    return x * lax.rsqrt(ms + EPS) * g


def _rows_to_tiles(ref, val):
    n = val.shape[0]
    for j in range(val.shape[1] // LANES):
        ref[pl.ds(j, n, stride=SUBLANES), :] = val[:, j * LANES:(j + 1) * LANES]


def _tiles_to_rows(ref, n):
    return jnp.concatenate([ref[pl.ds(j, n, stride=SUBLANES), :] for j in range(SUBLANES)], axis=-1)


def _pack_rows(ref, val, tiles):
    _rows_to_tiles(tiles, val)
    ref[...] = pltpu.bitcast(tiles[...].astype(BF16), jnp.uint32)


def _unpack_rows(ref, n, tiles):
    tiles[...] = pltpu.bitcast(ref[...], BF16).astype(F32)
    return _tiles_to_rows(tiles, n).astype(BF16)


def _params(*sem):
    return pltpu.CompilerParams(dimension_semantics=sem, vmem_limit_bytes=VMEM_LIMIT)


def _ada_body(c_ref, w_ref, b_ref, o_ref):
    c = c_ref[...]
    o_ref[...] = _dot3(c * _sigmoid(c), w_ref[...]) + b_ref[...]


def _ada(c, w, b, tn=1024):
    nb, d = c.shape
    n = w.shape[1]
    return pl.pallas_call(
        _ada_body,
        grid=(n // tn,),
        in_specs=[pl.BlockSpec((nb, d), lambda j: (0, 0)),
                  pl.BlockSpec((d, tn), lambda j: (0, j)),
                  pl.BlockSpec((1, tn), lambda j: (0, j))],
        out_specs=pl.BlockSpec((nb, tn), lambda j: (0, j)),
        out_shape=jax.ShapeDtypeStruct((nb, n), F32),
        compiler_params=_params("arbitrary"),
        name="ada",
    )(c, w, b.reshape(1, n))


def _proj_body(x_ref, mod_ref, g1_ref, wm_ref, wa1_ref, wg_ref, wa2_ref, ba2_ref,
               z_ref, q_ref, k_ref, v_ref, rs_ref, lg_ref, gt_ref, ls_ref):
    x = x_ref[0]
    u = (_rms(x, g1_ref[...]) * (1.0 + mod_ref[0, 1:2, :]) + mod_ref[0, 0:1, :]).astype(BF16)
    c0 = 2 * CONV_DIM
    c1 = c0 + 2 * QK_DIM
    c2 = c1 + V_DIM
    c3 = c2 + V_DIM
    pc = _bdot(u, wm_ref[:, 0:c0])
    z_ref[0] = pc[:, :CONV_DIM] * _sigmoid(pc[:, CONV_DIM:])
    qk = _bdot(u, wm_ref[:, c0:c1])
    q_ref[0] = qk[:, :QK_DIM] * (GLA_DK ** -0.5)
    k_ref[0] = qk[:, QK_DIM:]
    v_ref[0] = _bdot(u, wm_ref[:, c1:c2]).astype(BF16)
    r = _bdot(u, wm_ref[:, c2:c3])
    rs_ref[0] = (r * _sigmoid(r)).astype(BF16)
    a1 = _bdot(u, wa1_ref[...])
    xg = _dot3(a1, wa2_ref[...]) + ba2_ref[...]
    lg = (jnp.minimum(xg, 0.0) - jnp.log1p(jnp.exp(-jnp.abs(xg)))) * (1.0 / GLA_TAU)
    lg_ref[0] = lg
    ls_ref[0, 0] = jnp.concatenate([jnp.sum(lg[lo:lo + GLA_CHUNK, :], axis=0, keepdims=True)
                                    for lo in range(0, lg.shape[0], GLA_CHUNK)], axis=0)
    gt_ref[0] = _sigmoid(_bdot(u, wg_ref[...])).astype(BF16)


def _wsplit_body(wt_ref, wm_ref, wa1_ref, wg_ref, buf, sem, *, c3):
    cw = buf.shape[1]
    n_main = c3 // cw
    j = pl.program_id(0)
    cur = lax.rem(j, 2)

    def chunk(step, half):
        start = jnp.where(step <= n_main, step * cw, c3 + GLA_LOWRANK + (step - n_main - 1) * cw)
        return pltpu.make_async_copy(wt_ref.at[pl.ds(pl.multiple_of(start, SUBLANES), cw), :], buf.at[half],
                                     sem.at[half])

    @pl.when(j == 0)
    def _():
        chunk(j, 0).start()

    @pl.when(j + 1 < pl.num_programs(0))
    def _():
        chunk(j + 1, 1 - cur).start()

    chunk(j, cur).wait()

    @pl.when(j < n_main)
    def _():
        wm_ref[...] = buf[cur].T.astype(BF16)

    @pl.when(j == n_main)
    def _():
        lane = lax.broadcasted_iota(jnp.int32, wa1_ref.shape, 1)
        wa1_ref[...] = jnp.where(lane < GLA_LOWRANK, buf[cur, 0:LANES, :].T, 0.0).astype(BF16)

    @pl.when(j > n_main)
    def _():
        wg_ref[...] = buf[cur].T.astype(BF16)


def _wsplit(wt, c3, cw=512):
    n, d = wt.shape
    ng = n - c3 - GLA_LOWRANK
    n_main = c3 // cw
    return pl.pallas_call(
        functools.partial(_wsplit_body, c3=c3),
        grid=(n_main + 1 + ng // cw,),
        in_specs=[pl.BlockSpec(memory_space=pl.ANY)],
        out_specs=[pl.BlockSpec((d, cw), lambda j: (0, jnp.minimum(j, n_main - 1))),
                   pl.BlockSpec((d, LANES), lambda j: (0, 0)),
                   pl.BlockSpec((d, cw), lambda j: (0, jnp.clip(j - n_main - 1, 0, ng // cw - 1)))],
        out_shape=[jax.ShapeDtypeStruct((d, c3), BF16), jax.ShapeDtypeStruct((d, LANES), BF16),
                   jax.ShapeDtypeStruct((d, ng), BF16)],
        scratch_shapes=[pltpu.VMEM((2, cw, d), F32), pltpu.SemaphoreType.DMA((2,))],
        compiler_params=_params("arbitrary"),
        name="wsplit",
    )(wt)


def _proj(x, mod3, g1, w_in, w_a2, b_a2, tm):
    nb, s, d = x.shape
    c3 = 2 * CONV_DIM + 2 * QK_DIM + 2 * V_DIM
    wm, wa1, wg = _wsplit(jnp.swapaxes(w_in, 0, 1), c3)
    wa2 = jnp.pad(w_a2, ((0, LANES - GLA_LOWRANK), (0, 0)))
    ng = wg.shape[1]

    def tok(width):
        return pl.BlockSpec((1, tm, width), lambda b, i: (b, i, 0))

    def const(shape):
        return pl.BlockSpec(shape, lambda b, i: (0,) * len(shape))

    def out(width, dt):
        return jax.ShapeDtypeStruct((nb, s, width), dt)

    return pl.pallas_call(
        _proj_body,
        grid=(nb, s // tm),
        in_specs=[tok(d),
                  pl.BlockSpec((1, N_MOD, d), lambda b, i: (b, 0, 0)),
                  const((1, d)), const(wm.shape), const(wa1.shape), const(wg.shape),
                  const(wa2.shape), const((1, QK_DIM))],
        out_specs=[tok(CONV_DIM), tok(QK_DIM), tok(QK_DIM), tok(V_DIM), tok(V_DIM), tok(QK_DIM), tok(ng),
                   pl.BlockSpec((1, 1, tm // GLA_CHUNK, QK_DIM), lambda b, i: (b, i, 0, 0))],
        out_shape=[out(CONV_DIM, F32), out(QK_DIM, F32), out(QK_DIM, F32), out(V_DIM, BF16),
                   out(V_DIM, BF16), out(QK_DIM, F32), out(ng, BF16),
                   jax.ShapeDtypeStruct((nb, s // tm, tm // GLA_CHUNK, QK_DIM), F32)],
        compiler_params=_params("arbitrary", "arbitrary"),
        name="proj",
    )(x, mod3, g1.reshape(1, d), wm, wa1, wg, wa2, b_a2.reshape(1, QK_DIM))


def _gla_body(ok_ref, q_ref, k_ref, lg_ref, v_ref, rs_ref, gn_ref, o_ref, st_ref, b_s, oi_s, *, n_chunks):
    cl = GLA_CHUNK

    @pl.when(pl.program_id(1) == 0)
    def _():
        st_ref[...] = jnp.zeros_like(st_ref)

    row = lax.broadcasted_iota(jnp.int32, (cl, cl), 0)
    col = lax.broadcasted_iota(jnp.int32, (cl, cl), 1)
    causal = col <= row
    tri = jnp.where(causal, 1.0, 0.0).astype(BF16)
    chunks = [slice(c * cl, (c + 1) * cl) for c in range(n_chunks)]
    heads = range(GLA_HEADS)
    ksl = [slice(h * GLA_DK, (h + 1) * GLA_DK) for h in heads]
    vsl = [slice(h * GLA_DV, (h + 1) * GLA_DV) for h in heads]

    def cumulative(rows):
        gh, gl = _split(lg_ref[0, rows, :])
        return _bdot(tri, gh) + _bdot(tri, gl)

    def intra_pairwise(h, c):
        kf = k_ref[0, chunks[c], ksl[h]]
        vf = v_ref[0, chunks[c], vsl[h]].astype(F32)
        bh = b_s[chunks[c], ksl[h]]
        key = lax.broadcasted_iota(jnp.int32, (cl, 1), 0)

        def group(g, carry):
            r0 = pl.multiple_of(g * SUBLANES, SUBLANES)
            q8 = q_ref[0, pl.ds(c * cl + r0, SUBLANES), ksl[h]]
            b8 = b_s[pl.ds(c * cl + r0, SUBLANES), ksl[h]]
            out_rows = []
            for r in range(SUBLANES):
                diff = jnp.where(key <= r0 + r, b8[r:r + 1, :] - bh, -jnp.inf)
                att = jnp.sum(jnp.exp(diff) * kf * q8[r:r + 1, :], axis=-1, keepdims=True)
                out_rows.append(jnp.sum(att * vf, axis=0, keepdims=True))
            oi_s[h, pl.ds(c * cl + r0, SUBLANES), :] = jnp.concatenate(out_rows, axis=0)
            return carry

        lax.fori_loop(0, cl // SUBLANES, group, 0)
        return oi_s[h, chunks[c], :]

    def run(factored):
        cums = [cumulative(rows) for rows in chunks]
        if not factored:
            for rows, b in zip(chunks, cums):
                b_s[rows, :] = b
        qes, kts, bts, klts, decays = [], [], [], [], []
        for rows, b in zip(chunks, cums):
            qes.append((q_ref[0, rows, :] * jnp.exp(b)).astype(BF16))
            kt = k_ref[0, rows, :].T
            bt = b.T
            bl = bt[:, cl - 1:cl]
            kts.append(kt)
            bts.append(bt)
            klts.append((kt * jnp.exp(bl - bt)).astype(BF16))
            decays.append(jnp.exp(bl))
        intra = {}
        if factored:
            kets = [(kt * jnp.exp(-bt)).astype(BF16) for kt, bt in zip(kts, bts)]
            atts = {}
            for c in range(n_chunks):
                for h in heads:
                    att = _bdot(qes[c][:, ksl[h]], kets[c][ksl[h], :])
                    atts[h, c] = jnp.where(causal, att, 0.0).astype(BF16)
            for c, rows in enumerate(chunks):
                for h in heads:
                    intra[h, c] = _bdot(atts[h, c], v_ref[0, rows, vsl[h]])
        else:
            for c in range(n_chunks):
                for h in heads:
                    intra[h, c] = intra_pairwise(h, c)
        updates = {}
        for c, rows in enumerate(chunks):
            for h in heads:
                updates[h, c] = _bdot(klts[c][ksl[h], :], v_ref[0, rows, vsl[h]])

        outs = {}
        states = []
        for h in heads:
            state = st_ref[h]
            for c, rows in enumerate(chunks):
                o = intra[h, c] + _bdot(qes[c][:, ksl[h]], state.astype(BF16))
                state = decays[c][ksl[h], :] * state + updates[h, c]
                outs[h, c] = (_rms(o, gn_ref[:, vsl[h]]) * rs_ref[0, rows, vsl[h]].astype(F32)).astype(BF16)
            states.append(state)
        for c, rows in enumerate(chunks):
            o_ref[0, rows, :] = jnp.concatenate([outs[h, c] for h in heads], axis=-1)
        for h in heads:
            st_ref[h] = states[h]

    in_range = ok_ref[pl.program_id(0) * pl.num_programs(1) + pl.program_id(1)] != 0
    pl.when(in_range)(functools.partial(run, True))
    pl.when(jnp.logical_not(in_range))(functools.partial(run, False))


def _gla(q, k, lg, lg_sums, v, rs, gn, tc):
    nb, s, _ = q.shape
    lowest = jnp.min(lg_sums.reshape(nb, s // tc, (tc // GLA_CHUNK) * QK_DIM), axis=-1)
    in_range = (lowest > -GLA_FACTOR_RANGE).astype(jnp.int32).reshape(-1)

    def tok(width):
        return pl.BlockSpec((1, tc, width), lambda b, i, ok: (b, i, 0))

    return pl.pallas_call(
        functools.partial(_gla_body, n_chunks=tc // GLA_CHUNK),
        grid_spec=pltpu.PrefetchScalarGridSpec(
            num_scalar_prefetch=1,
            grid=(nb, s // tc),
            in_specs=[tok(QK_DIM), tok(QK_DIM), tok(QK_DIM), tok(V_DIM), tok(V_DIM),
                      pl.BlockSpec((1, V_DIM), lambda b, i, ok: (0, 0))],
            out_specs=tok(V_DIM),
            scratch_shapes=[pltpu.VMEM((GLA_HEADS, GLA_DK, GLA_DV), F32),
                            pltpu.VMEM((tc, QK_DIM), F32),
                            pltpu.VMEM((GLA_HEADS, tc, GLA_DV), F32)]),
        out_shape=jax.ShapeDtypeStruct((nb, s, V_DIM), BF16),
        compiler_params=_params("arbitrary", "arbitrary"),
        name="gla",
    )(in_range, q, k, lg, v, rs, gn.reshape(1, V_DIM))


def _route(logits, cnt, sub):
    ts = logits.shape[0]
    lane = lax.broadcasted_iota(jnp.int32, (ts, LANES), 1).astype(F32)
    ninf = -jnp.inf
    lgm = jnp.where(lane < N_GROUPS, logits, ninf)
    gmax = jnp.max(lgm, axis=-1, keepdims=True)
    gsel = jnp.min(jnp.where(lgm == gmax, lane, float(LANES)), axis=-1, keepdims=True)
    wg = 1.0 / jnp.sum(jnp.exp(lgm - gmax), axis=-1, keepdims=True)
    base = N_GROUPS + EXPERTS_PER_GROUP * gsel
    le = jnp.where(lane >= base, jnp.where(lane < base + EXPERTS_PER_GROUP, logits, ninf), ninf)
    v1 = jnp.max(le, axis=-1, keepdims=True)
    i1 = jnp.min(jnp.where(le == v1, lane, float(LANES)), axis=-1, keepdims=True)
    le2 = jnp.where(lane == i1, ninf, le)
    v2 = jnp.max(le2, axis=-1, keepdims=True)
    i2 = jnp.min(jnp.where(le2 == v2, lane, float(LANES)), axis=-1, keepdims=True)
    e21 = jnp.exp(v2 - v1)
    w1 = wg / (1.0 + e21)
    w2 = w1 * e21
    eid1 = i1 - N_GROUPS
    eid2 = i2 - N_GROUPS
    oh1 = jnp.where(lane == eid1, 1.0, 0.0)
    oh2 = jnp.where(lane == eid2, 1.0, 0.0)
    ohs = oh1 + oh2
    row = lax.broadcasted_iota(jnp.int32, (sub, sub), 0)
    col = lax.broadcasted_iota(jnp.int32, (sub, sub), 1)
    before = jnp.where(col < row, 1.0, 0.0).astype(BF16)
    tots = []
    for lo in range(0, ts, sub):
        piece = ohs[lo:lo + sub, :]
        tots.append(cnt + _bdot(before, piece.astype(BF16)))
        cnt = cnt + jnp.sum(piece, axis=0, keepdims=True)
    tot = jnp.concatenate(tots, axis=0)
    rank1 = jnp.sum(oh1 * tot, axis=-1, keepdims=True)
    rank2 = jnp.sum(oh2 * tot, axis=-1, keepdims=True)
    packed = jnp.where(lane == 0.0, eid1,
             jnp.where(lane == 1.0, eid2,
             jnp.where(lane == 2.0, w1,
             jnp.where(lane == 3.0, w2,
             jnp.where(lane == 4.0, rank1,
             jnp.where(lane == 5.0, rank2, 0.0))))))
    return packed, cnt


def _merge_body(z_ref, og_ref, gt_ref, x_ref, mod_ref, wdw_ref, bdw_ref, lng_ref, lnb_ref,
                wpw_ref, bpw_ref, wgo_ref, wout_ref, g2_ref, wr_ref, br_ref,
                h_ref, u2_ref, logit_ref, zbuf, zsh, u2t):
    ts = z_ref.shape[1]
    d = x_ref.shape[2]
    first_tile = pl.program_id(1) == 0

    @pl.when(first_tile)
    def _():
        zbuf[0:CONV_HALO, :] = jnp.zeros((CONV_HALO, CONV_DIM), F32)

    zbuf[CONV_HALO:CONV_HALO + ts, :] = z_ref[0]
    span = ts + CONV_HALO - SUBLANES
    for r in range(1, SUBLANES):
        zsh[r - 1] = zbuf[r:r + span, :]
    off = CONV_HALO - (CONV_WIDTH - 1)
    pieces = []
    for blk in range(ts // CONV_ROWS):
        acc = None
        for j in range(CONV_WIDTH):
            a, r = divmod(off + j, SUBLANES)
            lo = a * SUBLANES + blk * CONV_ROWS
            src = zbuf[lo:lo + CONV_ROWS, :] if r == 0 else zsh[r - 1, lo:lo + CONV_ROWS, :]
            term = src * jnp.concatenate([wdw_ref[j]] * (CONV_ROWS // SUBLANES), axis=0)
            acc = term if acc is None else acc + term
        conv = acc + bdw_ref[...]
        mu = jnp.mean(conv, axis=-1, keepdims=True)
        xc = conv - mu
        var = jnp.mean(xc * xc, axis=-1, keepdims=True)
        ln = xc * lax.rsqrt(var + EPS) * lng_ref[...] + lnb_ref[...]
        pieces.append((ln * _sigmoid(ln)).astype(BF16))
    zbuf[0:CONV_HALO, :] = zbuf[ts:ts + CONV_HALO, :]
    y_conv = _bdot(jnp.concatenate(pieces, axis=0), wpw_ref[...]) + bpw_ref[...]
    y_gla = _bdot(og_ref[0], wgo_ref[...])
    merged = gt_ref[0, :, 0:d].astype(F32) * y_conv + gt_ref[0, :, d:2 * d].astype(F32) * y_gla
    y = _bdot(merged.astype(BF16), wout_ref[...])
    h = x_ref[0] + mod_ref[0, 2:3, :] * y
    h_ref[0] = h
    u2 = _rms(h, g2_ref[...]) * (1.0 + mod_ref[0, 4:5, :]) + mod_ref[0, 3:4, :]
    _pack_rows(u2_ref, u2, u2t)
    logit_ref[0] = _dot3(u2, wr_ref[...]) + br_ref[...]


def _merge(z, og, gt, x, mod3, w_dw, b_dw, ln_g, ln_b, w_pw, b_pw, w_go, w_out, g2, w_rg, b_rg, w_re, b_re, ts):
    nb, s, d = x.shape
    npad = LANES - N_GROUPS - N_EXPERTS
    wr = jnp.pad(jnp.concatenate([w_rg, w_re], axis=1), ((0, 0), (0, npad)))
    br = jnp.pad(jnp.concatenate([b_rg, b_re]), (0, npad)).reshape(1, LANES)

    def tok(width):
        return pl.BlockSpec((1, ts, width), lambda b, i: (b, i, 0))

    def const(shape):
        return pl.BlockSpec(shape, lambda b, i: (0,) * len(shape))

    def row(v):
        return v.reshape(1, v.shape[-1])

    return pl.pallas_call(
        _merge_body,
        grid=(nb, s // ts),
        in_specs=[tok(CONV_DIM), tok(V_DIM), tok(2 * d), tok(d),
                  pl.BlockSpec((1, N_MOD, d), lambda b, i: (b, 0, 0)),
                  const((CONV_WIDTH, SUBLANES, CONV_DIM)), const((1, CONV_DIM)), const((1, CONV_DIM)), const((1, CONV_DIM)),
                  const((CONV_DIM, d)), const((1, d)), const((V_DIM, d)), const((d, d)), const((1, d)),
                  const((d, LANES)), const((1, LANES))],
        out_specs=[tok(d), pl.BlockSpec((ts * PACK_ROWS, LANES), lambda b, i: (b * (s // ts) + i, 0)), tok(LANES)],
        out_shape=[jax.ShapeDtypeStruct((nb, s, d), F32),
                   jax.ShapeDtypeStruct((nb * s * PACK_ROWS, LANES), jnp.uint32),
                   jax.ShapeDtypeStruct((nb, s, LANES), F32)],
        scratch_shapes=[pltpu.VMEM((CONV_HALO + ts, CONV_DIM), F32),
                        pltpu.VMEM((SUBLANES - 1, CONV_HALO + ts - SUBLANES, CONV_DIM), F32),
                        pltpu.VMEM((ts * SUBLANES, LANES), F32)],
        compiler_params=_params("arbitrary", "arbitrary"),
        name="merge",
    )(z, og, gt, x, mod3, jnp.broadcast_to(w_dw.reshape(CONV_WIDTH, 1, CONV_DIM), (CONV_WIDTH, SUBLANES, CONV_DIM)), row(b_dw), row(ln_g), row(ln_b),
      w_pw.astype(BF16), row(b_pw), w_go.astype(BF16), w_out.astype(BF16), row(g2), wr, br)


def _route_body(logit_ref, route_ref, routet_ref, cnt_ref, cnt_sc, *, sub):
    @pl.when(pl.program_id(0) == 0)
    def _():
        cnt_sc[...] = jnp.zeros_like(cnt_sc)

    packed, cnt = _route(logit_ref[...], cnt_sc[...], sub)
    route_ref[...] = packed
    routet_ref[...] = packed.T[0:SUBLANES, :]
    cnt_sc[...] = cnt
    cnt_ref[...] = jnp.broadcast_to(cnt, cnt_ref.shape)


def _route_call(logits, tr, sub):
    t = logits.shape[0]
    return pl.pallas_call(
        functools.partial(_route_body, sub=sub),
        grid=(t // tr,),
        in_specs=[pl.BlockSpec((tr, LANES), lambda i: (i, 0))],
        out_specs=[pl.BlockSpec((tr, LANES), lambda i: (i, 0)), pl.BlockSpec((SUBLANES, tr), lambda i: (0, i)),
                   pl.BlockSpec((SUBLANES, LANES), lambda i: (0, 0))],
        out_shape=[jax.ShapeDtypeStruct((t, LANES), F32), jax.ShapeDtypeStruct((SUBLANES, t), F32),
                   jax.ShapeDtypeStruct((SUBLANES, LANES), F32)],
        scratch_shapes=[pltpu.VMEM((1, LANES), F32)],
        compiler_params=_params("arbitrary"),
        name="route",
    )(logits)


def _row_copy(src, i, dst, j, sem):
    return pltpu.make_async_copy(src.at[pl.ds(pl.multiple_of(i, SUBLANES), SUBLANES), :],
                                 dst.at[pl.ds(pl.multiple_of(j, SUBLANES), SUBLANES), :], sem)


def _invert_body(lo_ref, hi_ref, slot_ref, src_ref):
    i = pl.program_id(0)
    ts = slot_ref.shape[-1] // TOP_K

    @pl.when(i == 0)
    def _():
        for e in range(lo_ref.shape[0]):
            lo = lo_ref[e]
            hi = hi_ref[e]

            def clear(p, carry, lo=lo, hi=hi):
                for j in range(CLEAR_UNROLL):
                    src_ref[jnp.minimum(lo + p * CLEAR_UNROLL + j, hi - 1)] = 0
                return carry

            trips = lax.shift_right_logical(hi - lo + (CLEAR_UNROLL - 1), CLEAR_UNROLL.bit_length() - 1)
            lax.fori_loop(0, trips, clear, 0)

    for r in range(ts):
        row = (i * ts + r) * PACK_ROWS
        for k in range(TOP_K):
            src_ref[slot_ref[0, 0, k * ts + r]] = row


def _invert(fill_lo, fill_hi, slots, n_slots):
    nt, _, width = slots.shape
    return pl.pallas_call(
        _invert_body,
        grid_spec=pltpu.PrefetchScalarGridSpec(
            num_scalar_prefetch=2,
            grid=(nt,),
            in_specs=[pl.BlockSpec((1, 1, width), lambda i, lo, hi: (i, 0, 0), memory_space=pltpu.SMEM)],
            out_specs=pl.BlockSpec(memory_space=pltpu.SMEM)),
        out_shape=jax.ShapeDtypeStruct((n_slots,), jnp.int32),
        compiler_params=_params("arbitrary"),
        name="invert",
    )(fill_lo, fill_hi, slots)


def _expert_body(te_ref, nu_ref, seg_ref, nxt_ref, src_ref, u2_ref, w1_ref, w3_ref, w2_ref, ys_ref,
                 u2v, xg, xt, w1f, w3f, w2f, w1b, w3b, w2b, sem, wsem):
    i = pl.program_id(0)
    tmx = src_ref.shape[-1]
    expert = te_ref[i]
    first = i == 0
    changed = jnp.logical_or(first, expert != te_ref[jnp.maximum(i - 1, 0)])
    slot = lax.rem(seg_ref[i], 2)

    def weight_copies(e, s):
        return [pltpu.make_async_copy(src.at[e], dst.at[s], wsem.at[s])
                for src, dst in ((w1_ref, w1f), (w3_ref, w3f), (w2_ref, w2f))]

    load = pltpu.make_async_copy(u2_ref, u2v, sem)

    @pl.when(first)
    def _():
        load.start()
        for cp in weight_copies(expert, 0):
            cp.start()

    @pl.when(changed)
    def _():
        for cp in weight_copies(expert, slot):
            cp.wait()
        w1b[...] = w1f[slot].astype(BF16)
        w3b[...] = w3f[slot].astype(BF16)
        w2b[...] = w2f[slot].astype(BF16)

    @pl.when(jnp.logical_and(changed, nxt_ref[i] != expert))
    def _():
        for cp in weight_copies(nxt_ref[i], 1 - slot):
            cp.start()

    @pl.when(first)
    def _():
        load.wait()

    @pl.when(i < nu_ref[0])
    def _():
        for r in range(tmx):
            row = pl.multiple_of(src_ref[0, 0, r], PACK_ROWS)
            xg[r * PACK_ROWS:(r + 1) * PACK_ROWS, :] = u2v[pl.ds(row, PACK_ROWS), :]
        x = _unpack_rows(xg, tmx, xt)
        h1 = _bdot(x, w1b[...])
        h3 = _bdot(x, w3b[...])
        hid = (h1 * _sigmoid(h1) * h3).astype(BF16)
        _rows_to_tiles(ys_ref, _bdot(hid, w2b[...]))

    @pl.when(i >= nu_ref[0])
    def _():
        ys_ref[...] = jnp.zeros_like(ys_ref)


def _experts(tile_expert, n_used, run_index, next_expert, src_rows, u2p, w1, w3, w2):
    n_tiles, _, tmx = src_rows.shape
    ne, d, f = w1.shape
    hbm = pl.BlockSpec(memory_space=pl.ANY)
    return pl.pallas_call(
        _expert_body,
        grid_spec=pltpu.PrefetchScalarGridSpec(
            num_scalar_prefetch=4,
            grid=(n_tiles,),
            in_specs=[pl.BlockSpec((1, 1, tmx), lambda i, *_: (i, 0, 0), memory_space=pltpu.SMEM),
                      hbm, hbm, hbm, hbm],
            out_specs=pl.BlockSpec((tmx * SUBLANES, LANES), lambda i, *_: (i, 0)),
            scratch_shapes=[pltpu.VMEM(u2p.shape, jnp.uint32),
                            pltpu.VMEM((tmx * PACK_ROWS, LANES), jnp.uint32),
                            pltpu.VMEM((tmx * SUBLANES, LANES), F32),
                            pltpu.VMEM((2, d, f), F32), pltpu.VMEM((2, d, f), F32), pltpu.VMEM((2, f, d), F32),
                            pltpu.VMEM((d, f), BF16), pltpu.VMEM((d, f), BF16), pltpu.VMEM((f, d), BF16),
                            pltpu.SemaphoreType.DMA(()), pltpu.SemaphoreType.DMA((2,))]),
        out_shape=jax.ShapeDtypeStruct((n_tiles * tmx * SUBLANES, LANES), F32),
        compiler_params=_params("arbitrary"),
        name="experts",
    )(tile_expert, n_used, run_index, next_expert, src_rows, u2p, w1, w3, w2)


def _final_body(p_ref, pn_ref, h_ref, route_ref, mod_ref, modf_ref, gf_ref, ys_ref, o_ref, y_buf, sem):
    ts = h_ref.shape[1]
    step = pl.program_id(0) * pl.num_programs(1) + pl.program_id(1)
    n_steps = pl.num_programs(0) * pl.num_programs(1)
    cur = lax.rem(step, 2)

    def request_rows(ref, half, lo, n):
        for k in range(TOP_K):
            for r in range(lo, lo + n):
                _row_copy(ys_ref, ref[0, 0, k * ts + r], y_buf.at[half, k], r * SUBLANES,
                          sem.at[half]).start(priority=r % 2)

    def drain(half):
        for k in range(TOP_K):
            pltpu.make_async_copy(ys_ref.at[pl.ds(0, ts * SUBLANES), :], y_buf.at[half, k], sem.at[half]).wait()

    @pl.when(step == 0)
    def _():
        request_rows(p_ref, 0, 0, ts)

    drain(cur)
    for lo in range(0, ts, FINAL_ROWS):
        request_rows(pn_ref, 1 - cur, lo, FINAL_ROWS)
        rows = slice(lo, lo + FINAL_ROWS)
        route = route_ref[0, rows, :]
        picked = [jnp.concatenate([y_buf.at[cur, k][pl.ds(lo * SUBLANES + j, FINAL_ROWS, stride=SUBLANES), :]
                                   for j in range(SUBLANES)], axis=-1) for k in range(TOP_K)]
        h = h_ref[0, rows, :] + mod_ref[0, 5:6, :] * (route[:, 2:3] * picked[0] + route[:, 3:4] * picked[1])
        o_ref[0, rows, :] = _rms(h, gf_ref[...]) * (1.0 + modf_ref[0, 1:2, :]) + modf_ref[0, 0:1, :]

    @pl.when(step == n_steps - 1)
    def _():
        drain(1 - cur)


def _final(slot_rows, h, route, mod3, modf3, gf, ys, ts):
    nb, s, d = h.shape
    nt = s // ts
    last = nb * nt - 1

    def tok(width):
        return pl.BlockSpec((1, ts, width), lambda b, i: (b, i, 0))

    def slots(ahead):
        return pl.BlockSpec((1, 1, TOP_K * ts), lambda b, i: (jnp.minimum(b * nt + i + ahead, last), 0, 0),
                            memory_space=pltpu.SMEM)

    return pl.pallas_call(
        _final_body,
        grid=(nb, nt),
        in_specs=[slots(0), slots(1), tok(d), tok(LANES),
                  pl.BlockSpec((1, N_MOD, d), lambda b, i: (b, 0, 0)),
                  pl.BlockSpec((1, 2, d), lambda b, i: (b, 0, 0)),
                  pl.BlockSpec((1, d), lambda b, i: (0, 0)),
                  pl.BlockSpec(memory_space=pl.ANY)],
        out_specs=tok(d),
        out_shape=jax.ShapeDtypeStruct((nb, s, d), F32),
        scratch_shapes=[pltpu.VMEM((2, TOP_K, ts * SUBLANES, LANES), F32), pltpu.SemaphoreType.DMA((2,))],
        compiler_params=_params("arbitrary", "arbitrary"),
        name="final",
    )(slot_rows, slot_rows, h, route, mod3, modf3, gf.reshape(1, d), ys)


def _plan(routet, cnt, ts, ti, tmx, n_tiles):
    t = routet.shape[1]
    counts = cnt[0, :N_EXPERTS].astype(jnp.int32)
    tiles = (counts + (tmx - 1)) // tmx
    tile_end = jnp.cumsum(tiles)
    offs = ((tile_end - tiles) * tmx).astype(jnp.int32)
    n_used = tile_end[-1:]
    tile_ids = jnp.minimum(jnp.arange(n_tiles, dtype=jnp.int32), n_used[0] - 1)
    tile_expert = jnp.sum((tile_ids[:, None] >= tile_end[None, :]).astype(jnp.int32), axis=1)
    run_index = jnp.cumsum(jnp.concatenate([jnp.zeros((1,), jnp.int32),
                                            (tile_expert[1:] != tile_expert[:-1]).astype(jnp.int32)]))
    same_run = run_index[:, None] + 1 == run_index[None, :]
    has_next = jnp.any(same_run, axis=1)
    next_expert = jnp.where(has_next, jnp.max(jnp.where(same_run, tile_expert[None, :], 0), axis=1), tile_expert)

    eid = routet[0:2].astype(jnp.int32)
    experts = jnp.arange(N_EXPERTS, dtype=jnp.int32)[:, None, None]
    slot = routet[4:6].astype(jnp.int32) + jnp.sum(jnp.where(eid[None] == experts, offs[:, None, None], 0), axis=0)
    def tiled(tile):
        return slot.reshape(TOP_K, t // tile, tile).transpose(1, 0, 2).reshape(t // tile, 1, TOP_K * tile)

    fill_lo = jnp.concatenate([offs + counts, tile_end[-1:] * tmx]).astype(jnp.int32)
    fill_hi = jnp.concatenate([tile_end * tmx, jnp.full((1,), n_tiles * tmx, jnp.int32)]).astype(jnp.int32)
    src = _invert(fill_lo, fill_hi, tiled(ti), n_tiles * tmx)
    tables = [a.astype(jnp.int32) for a in (tile_expert, n_used, run_index, next_expert)]
    return tiled(ts) * SUBLANES, src.reshape(n_tiles, 1, tmx), tables


def kernel(x, c, w_ada, b_ada, g_norm1, w_in, w_dw, b_dw, g_conv_ln, b_conv_ln, w_conv_pw, b_conv_pw,
           w_a2, b_a2, g_gla_norm, w_gla_o, w_out, g_norm2, w_router_g, b_router_g, w_router_e,
           b_router_e, w1, w3, w2, w_ada_f, b_ada_f, g_final):
    nb, s, d = x.shape
    assert w_ada.shape[0] == 1, "single-layer block"
    assert d == 2 * PACK_ROWS * LANES, "packed token rows assume D_MODEL = 1024"
    tm = min(512, s)
    tc = min(256, s)
    tg = min(512, s)
    ts = min(256, s)
    ti = min(1024, s)
    tmx = 256
    t = nb * s
    n_tiles = (t * TOP_K) // tmx + N_EXPERTS
    mod3 = _ada(c, w_ada[0], b_ada[0]).reshape(nb, N_MOD, d)
    modf3 = _ada(c, w_ada_f, b_ada_f).reshape(nb, 2, d)
    z, q, k, v, rs, lg, gt, lg_sums = _proj(x, mod3, g_norm1[0], w_in[0], w_a2[0], b_a2[0], tm)
    og = _gla(q, k, lg, lg_sums, v, rs, g_gla_norm[0], tc)
    h, u2, logits = _merge(z, og, gt, x, mod3, w_dw[0], b_dw[0], g_conv_ln[0], b_conv_ln[0],
                           w_conv_pw[0], b_conv_pw[0], w_gla_o[0], w_out[0], g_norm2[0],
                           w_router_g[0], b_router_g[0], w_router_e[0], b_router_e[0], tg)
    route, routet, cnt = _route_call(logits.reshape(t, LANES), min(ROUTE_ROWS, t), ts)
    slot_rows, src_rows, tables = _plan(routet, cnt, ts, ti, tmx, n_tiles)
    ys = _experts(*tables, src_rows, u2, w1[0], w3[0], w2[0])
    return _final(slot_rows, h, route.reshape(nb, s, LANES), mod3, modf3, g_final, ys, ts)
```

```python
import functools

import jax
import jax.numpy as jnp
from jax import lax
from jax.experimental import pallas as pl
from jax.experimental.pallas import tpu as pltpu

F32 = jnp.float32
BF16 = jnp.bfloat16

EPS = 1e-6
CONV_DIM = 512
CONV_WIDTH = 31
GLA_HEADS = 4
GLA_DK = 128
GLA_DV = 256
GLA_LOWRANK = 16
GLA_TAU = 16.0
QK_DIM = GLA_HEADS * GLA_DK
V_DIM = GLA_HEADS * GLA_DV
N_GROUPS = 4
EXPERTS_PER_GROUP = 8
N_EXPERTS = N_GROUPS * EXPERTS_PER_GROUP
TOP_K = 2
N_MOD = 6

LANES = 128
SUBLANES = 8
PACK_ROWS = 4
CONV_ROWS = 32
ROUTE_ROWS = 2048
CLEAR_UNROLL = 16
CONV_HALO = 32
GLA_CHUNK = 128
GLA_FACTOR_RANGE = 60.0
VMEM_LIMIT = 56 * 1024 * 1024


def _bdot(a, b):
    return jnp.dot(a, b, preferred_element_type=F32)


def _split(a):
    hi = a.astype(BF16)
    lo = (a - hi.astype(F32)).astype(BF16)
    return hi, lo


def _dot3(a, b):
    ah, al = _split(a)
    bh, bl = _split(b)
    return _bdot(ah, bh) + (_bdot(ah, bl) + _bdot(al, bh))


def _sigmoid(x):
    return 1.0 / (1.0 + jnp.exp(-x))


def _rms(x, g):
    ms = jnp.mean(x * x, axis=-1, keepdims=True)
    return x * lax.rsqrt(ms + EPS) * g


def _rows_to_tiles(ref, val):
    n = val.shape[0]
    for j in range(val.shape[1] // LANES):
        ref[pl.ds(j, n, stride=SUBLANES), :] = val[:, j * LANES:(j + 1) * LANES]


def _tiles_to_rows(ref, n):
    return jnp.concatenate([ref[pl.ds(j, n, stride=SUBLANES), :] for j in range(SUBLANES)], axis=-1)


def _pack_rows(ref, val, tiles):
    _rows_to_tiles(tiles, val)
    ref[...] = pltpu.bitcast(tiles[...].astype(BF16), jnp.uint32)


def _unpack_rows(ref, n, tiles):
    tiles[...] = pltpu.bitcast(ref[...], BF16).astype(F32)
    return _tiles_to_rows(tiles, n).astype(BF16)


def _params(*sem):
    return pltpu.CompilerParams(dimension_semantics=sem, vmem_limit_bytes=VMEM_LIMIT)


def _ada_body(c_ref, w_ref, b_ref, o_ref):
    c = c_ref[...]
    o_ref[...] = _dot3(c * _sigmoid(c), w_ref[...]) + b_ref[...]


def _ada(c, w, b, tn=1024):
    nb, d = c.shape
    n = w.shape[1]
    return pl.pallas_call(
        _ada_body,
        grid=(n // tn,),
        in_specs=[pl.BlockSpec((nb, d), lambda j: (0, 0)),
                  pl.BlockSpec((d, tn), lambda j: (0, j)),
                  pl.BlockSpec((1, tn), lambda j: (0, j))],
        out_specs=pl.BlockSpec((nb, tn), lambda j: (0, j)),
        out_shape=jax.ShapeDtypeStruct((nb, n), F32),
        compiler_params=_params("arbitrary"),
        name="ada",
    )(c, w, b.reshape(1, n))


def _proj_body(x_ref, mod_ref, g1_ref, wm_ref, wa1_ref, wg_ref, wa2_ref, ba2_ref,
               z_ref, q_ref, k_ref, v_ref, rs_ref, lg_ref, gt_ref, ls_ref):
    x = x_ref[0]
    u = (_rms(x, g1_ref[...]) * (1.0 + mod_ref[0, 1:2, :]) + mod_ref[0, 0:1, :]).astype(BF16)
    c0 = 2 * CONV_DIM
    c1 = c0 + 2 * QK_DIM
    c2 = c1 + V_DIM
    c3 = c2 + V_DIM
    pc = _bdot(u, wm_ref[:, 0:c0])
    z_ref[0] = pc[:, :CONV_DIM] * _sigmoid(pc[:, CONV_DIM:])
    qk = _bdot(u, wm_ref[:, c0:c1])
    q_ref[0] = qk[:, :QK_DIM] * (GLA_DK ** -0.5)
    k_ref[0] = qk[:, QK_DIM:]
    v_ref[0] = _bdot(u, wm_ref[:, c1:c2]).astype(BF16)
    r = _bdot(u, wm_ref[:, c2:c3])
    rs_ref[0] = (r * _sigmoid(r)).astype(BF16)
    a1 = _bdot(u, wa1_ref[...])
    xg = _dot3(a1, wa2_ref[...]) + ba2_ref[...]
    lg = (jnp.minimum(xg, 0.0) - jnp.log1p(jnp.exp(-jnp.abs(xg)))) * (1.0 / GLA_TAU)
    lg_ref[0] = lg
    ls_ref[0, 0] = jnp.concatenate([jnp.sum(lg[lo:lo + GLA_CHUNK, :], axis=0, keepdims=True)
                                    for lo in range(0, lg.shape[0], GLA_CHUNK)], axis=0)
    gt_ref[0] = _sigmoid(_bdot(u, wg_ref[...])).astype(BF16)


def _wsplit_body(wt_ref, wm_ref, wa1_ref, wg_ref, buf, sem, *, c3):
    cw = buf.shape[1]
    n_main = c3 // cw
    j = pl.program_id(0)
    cur = lax.rem(j, 2)

    def chunk(step, half):
        start = jnp.where(step <= n_main, step * cw, c3 + GLA_LOWRANK + (step - n_main - 1) * cw)
        return pltpu.make_async_copy(wt_ref.at[pl.ds(pl.multiple_of(start, SUBLANES), cw), :], buf.at[half],
                                     sem.at[half])

    @pl.when(j == 0)
    def _():
        chunk(j, 0).start()

    @pl.when(j + 1 < pl.num_programs(0))
    def _():
        chunk(j + 1, 1 - cur).start()

    chunk(j, cur).wait()

    @pl.when(j < n_main)
    def _():
        wm_ref[...] = buf[cur].T.astype(BF16)

    @pl.when(j == n_main)
    def _():
        lane = lax.broadcasted_iota(jnp.int32, wa1_ref.shape, 1)
        wa1_ref[...] = jnp.where(lane < GLA_LOWRANK, buf[cur, 0:LANES, :].T, 0.0).astype(BF16)

    @pl.when(j > n_main)
    def _():
        wg_ref[...] = buf[cur].T.astype(BF16)


def _wsplit(wt, c3, cw=512):
    n, d = wt.shape
    ng = n - c3 - GLA_LOWRANK
    n_main = c3 // cw
    return pl.pallas_call(
        functools.partial(_wsplit_body, c3=c3),
        grid=(n_main + 1 + ng // cw,),
        in_specs=[pl.BlockSpec(memory_space=pl.ANY)],
        out_specs=[pl.BlockSpec((d, cw), lambda j: (0, jnp.minimum(j, n_main - 1))),
                   pl.BlockSpec((d, LANES), lambda j: (0, 0)),
                   pl.BlockSpec((d, cw), lambda j: (0, jnp.clip(j - n_main - 1, 0, ng // cw - 1)))],
        out_shape=[jax.ShapeDtypeStruct((d, c3), BF16), jax.ShapeDtypeStruct((d, LANES), BF16),
                   jax.ShapeDtypeStruct((d, ng), BF16)],
        scratch_shapes=[pltpu.VMEM((2, cw, d), F32), pltpu.SemaphoreType.DMA((2,))],
        compiler_params=_params("arbitrary"),
        name="wsplit",
    )(wt)


def _proj(x, mod3, g1, w_in, w_a2, b_a2, tm):
    nb, s, d = x.shape
    c3 = 2 * CONV_DIM + 2 * QK_DIM + 2 * V_DIM
    wm, wa1, wg = _wsplit(jnp.swapaxes(w_in, 0, 1), c3)
    wa2 = jnp.pad(w_a2, ((0, LANES - GLA_LOWRANK), (0, 0)))
    ng = wg.shape[1]

    def tok(width):
        return pl.BlockSpec((1, tm, width), lambda b, i: (b, i, 0))

    def const(shape):
        return pl.BlockSpec(shape, lambda b, i: (0,) * len(shape))

    def out(width, dt):
        return jax.ShapeDtypeStruct((nb, s, width), dt)

    return pl.pallas_call(
        _proj_body,
        grid=(nb, s // tm),
        in_specs=[tok(d),
                  pl.BlockSpec((1, N_MOD, d), lambda b, i: (b, 0, 0)),
                  const((1, d)), const(wm.shape), const(wa1.shape), const(wg.shape),
                  const(wa2.shape), const((1, QK_DIM))],
        out_specs=[tok(CONV_DIM), tok(QK_DIM), tok(QK_DIM), tok(V_DIM), tok(V_DIM), tok(QK_DIM), tok(ng),
                   pl.BlockSpec((1, 1, tm // GLA_CHUNK, QK_DIM), lambda b, i: (b, i, 0, 0))],
        out_shape=[out(CONV_DIM, F32), out(QK_DIM, F32), out(QK_DIM, F32), out(V_DIM, BF16),
                   out(V_DIM, BF16), out(QK_DIM, F32), out(ng, BF16),
                   jax.ShapeDtypeStruct((nb, s // tm, tm // GLA_CHUNK, QK_DIM), F32)],
        compiler_params=_params("arbitrary", "arbitrary"),
        name="proj",
    )(x, mod3, g1.reshape(1, d), wm, wa1, wg, wa2, b_a2.reshape(1, QK_DIM))


def _gla_body(ok_ref, q_ref, k_ref, lg_ref, v_ref, rs_ref, gn_ref, wo_ref, o_ref, st_ref, b_s, oi_s, *, n_chunks):
    cl = GLA_CHUNK

    @pl.when(pl.program_id(1) == 0)
    def _():
        st_ref[...] = jnp.zeros_like(st_ref)

    row = lax.broadcasted_iota(jnp.int32, (cl, cl), 0)
    col = lax.broadcasted_iota(jnp.int32, (cl, cl), 1)
    causal = col <= row
    tri = jnp.where(causal, 1.0, 0.0).astype(BF16)
    chunks = [slice(c * cl, (c + 1) * cl) for c in range(n_chunks)]
    heads = range(GLA_HEADS)
    ksl = [slice(h * GLA_DK, (h + 1) * GLA_DK) for h in heads]
    vsl = [slice(h * GLA_DV, (h + 1) * GLA_DV) for h in heads]

    def cumulative(rows):
        gh, gl = _split(lg_ref[0, rows, :])
        return _bdot(tri, gh) + _bdot(tri, gl)

    def intra_pairwise(h, c):
        kf = k_ref[0, chunks[c], ksl[h]]
        vf = v_ref[0, chunks[c], vsl[h]].astype(F32)
        bh = b_s[chunks[c], ksl[h]]
        key = lax.broadcasted_iota(jnp.int32, (cl, 1), 0)

        def group(g, carry):
            r0 = pl.multiple_of(g * SUBLANES, SUBLANES)
            q8 = q_ref[0, pl.ds(c * cl + r0, SUBLANES), ksl[h]]
            b8 = b_s[pl.ds(c * cl + r0, SUBLANES), ksl[h]]
            out_rows = []
            for r in range(SUBLANES):
                diff = jnp.where(key <= r0 + r, b8[r:r + 1, :] - bh, -jnp.inf)
                att = jnp.sum(jnp.exp(diff) * kf * q8[r:r + 1, :], axis=-1, keepdims=True)
                out_rows.append(jnp.sum(att * vf, axis=0, keepdims=True))
            oi_s[h, pl.ds(c * cl + r0, SUBLANES), :] = jnp.concatenate(out_rows, axis=0)
            return carry

        lax.fori_loop(0, cl // SUBLANES, group, 0)
        return oi_s[h, chunks[c], :]

    def run(factored):
        cums = [cumulative(rows) for rows in chunks]
        if not factored:
            for rows, b in zip(chunks, cums):
                b_s[rows, :] = b
        qes, kts, bts, klts, decays = [], [], [], [], []
        for rows, b in zip(chunks, cums):
            qes.append((q_ref[0, rows, :] * jnp.exp(b)).astype(BF16))
            kt = k_ref[0, rows, :].T
            bt = b.T
            bl = bt[:, cl - 1:cl]
            kts.append(kt)
            bts.append(bt)
            klts.append((kt * jnp.exp(bl - bt)).astype(BF16))
            decays.append(jnp.exp(bl))
        intra = {}
        if factored:
            kets = [(kt * jnp.exp(-bt)).astype(BF16) for kt, bt in zip(kts, bts)]
            atts = {}
            for c in range(n_chunks):
                for h in heads:
                    att = _bdot(qes[c][:, ksl[h]], kets[c][ksl[h], :])
                    atts[h, c] = jnp.where(causal, att, 0.0).astype(BF16)
            for c, rows in enumerate(chunks):
                for h in heads:
                    intra[h, c] = _bdot(atts[h, c], v_ref[0, rows, vsl[h]])
        else:
            for c in range(n_chunks):
                for h in heads:
                    intra[h, c] = intra_pairwise(h, c)
        updates = {}
        for c, rows in enumerate(chunks):
            for h in heads:
                updates[h, c] = _bdot(klts[c][ksl[h], :], v_ref[0, rows, vsl[h]])

        outs = {}
        states = []
        for h in heads:
            state = st_ref[h]
            for c, rows in enumerate(chunks):
                o = intra[h, c] + _bdot(qes[c][:, ksl[h]], state.astype(BF16))
                state = decays[c][ksl[h], :] * state + updates[h, c]
                outs[h, c] = (_rms(o, gn_ref[:, vsl[h]]) * rs_ref[0, rows, vsl[h]].astype(F32)).astype(BF16)
            states.append(state)
        for c, rows in enumerate(chunks):
            o_ref[0, rows, :] = _bdot(jnp.concatenate([outs[h, c] for h in heads], axis=-1), wo_ref[...])
        for h in heads:
            st_ref[h] = states[h]

    in_range = ok_ref[pl.program_id(0) * pl.num_programs(1) + pl.program_id(1)] != 0
    pl.when(in_range)(functools.partial(run, True))
    pl.when(jnp.logical_not(in_range))(functools.partial(run, False))


def _gla(q, k, lg, lg_sums, v, rs, gn, w_o, tc):
    nb, s, _ = q.shape
    d = w_o.shape[1]
    lowest = jnp.min(lg_sums.reshape(nb, s // tc, (tc // GLA_CHUNK) * QK_DIM), axis=-1)
    in_range = (lowest > -GLA_FACTOR_RANGE).astype(jnp.int32).reshape(-1)

    def tok(width):
        return pl.BlockSpec((1, tc, width), lambda b, i, ok: (b, i, 0))

    return pl.pallas_call(
        functools.partial(_gla_body, n_chunks=tc // GLA_CHUNK),
        grid_spec=pltpu.PrefetchScalarGridSpec(
            num_scalar_prefetch=1,
            grid=(nb, s // tc),
            in_specs=[tok(QK_DIM), tok(QK_DIM), tok(QK_DIM), tok(V_DIM), tok(V_DIM),
                      pl.BlockSpec((1, V_DIM), lambda b, i, ok: (0, 0)),
                      pl.BlockSpec((V_DIM, d), lambda b, i, ok: (0, 0))],
            out_specs=tok(d),
            scratch_shapes=[pltpu.VMEM((GLA_HEADS, GLA_DK, GLA_DV), F32),
                            pltpu.VMEM((tc, QK_DIM), F32),
                            pltpu.VMEM((GLA_HEADS, tc, GLA_DV), F32)]),
        out_shape=jax.ShapeDtypeStruct((nb, s, d), F32),
        compiler_params=_params("arbitrary", "arbitrary"),
        name="gla",
    )(in_range, q, k, lg, v, rs, gn.reshape(1, V_DIM), w_o.astype(BF16))


def _route(logits, cnt, sub):
    ts = logits.shape[0]
    lane = lax.broadcasted_iota(jnp.int32, (ts, LANES), 1).astype(F32)
    ninf = -jnp.inf
    lgm = jnp.where(lane < N_GROUPS, logits, ninf)
    gmax = jnp.max(lgm, axis=-1, keepdims=True)
    gsel = jnp.min(jnp.where(lgm == gmax, lane, float(LANES)), axis=-1, keepdims=True)
    wg = 1.0 / jnp.sum(jnp.exp(lgm - gmax), axis=-1, keepdims=True)
    base = N_GROUPS + EXPERTS_PER_GROUP * gsel
    le = jnp.where(lane >= base, jnp.where(lane < base + EXPERTS_PER_GROUP, logits, ninf), ninf)
    v1 = jnp.max(le, axis=-1, keepdims=True)
    i1 = jnp.min(jnp.where(le == v1, lane, float(LANES)), axis=-1, keepdims=True)
    le2 = jnp.where(lane == i1, ninf, le)
    v2 = jnp.max(le2, axis=-1, keepdims=True)
    i2 = jnp.min(jnp.where(le2 == v2, lane, float(LANES)), axis=-1, keepdims=True)
    e21 = jnp.exp(v2 - v1)
    w1 = wg / (1.0 + e21)
    w2 = w1 * e21
    eid1 = i1 - N_GROUPS
    eid2 = i2 - N_GROUPS
    oh1 = jnp.where(lane == eid1, 1.0, 0.0)
    oh2 = jnp.where(lane == eid2, 1.0, 0.0)
    ohs = oh1 + oh2
    row = lax.broadcasted_iota(jnp.int32, (sub, sub), 0)
    col = lax.broadcasted_iota(jnp.int32, (sub, sub), 1)
    before = jnp.where(col < row, 1.0, 0.0).astype(BF16)
    tots = []
    for lo in range(0, ts, sub):
        piece = ohs[lo:lo + sub, :]
        tots.append(cnt + _bdot(before, piece.astype(BF16)))
        cnt = cnt + jnp.sum(piece, axis=0, keepdims=True)
    tot = jnp.concatenate(tots, axis=0)
    rank1 = jnp.sum(oh1 * tot, axis=-1, keepdims=True)
    rank2 = jnp.sum(oh2 * tot, axis=-1, keepdims=True)
    packed = jnp.where(lane == 0.0, eid1,
             jnp.where(lane == 1.0, eid2,
             jnp.where(lane == 2.0, w1,
             jnp.where(lane == 3.0, w2,
             jnp.where(lane == 4.0, rank1,
             jnp.where(lane == 5.0, rank2, 0.0))))))
    return packed, cnt


def _merge_body(z_ref, yg_ref, gt_ref, x_ref, mod_ref, wdw_ref, bdw_ref, lng_ref, lnb_ref,
                wpw_ref, bpw_ref, wout_ref, g2_ref, wr_ref, br_ref,
                h_ref, u2_ref, logit_ref, zbuf, zsh, u2t):
    ts = z_ref.shape[1]
    d = x_ref.shape[2]
    first_tile = pl.program_id(1) == 0

    @pl.when(first_tile)
    def _():
        zbuf[0:CONV_HALO, :] = jnp.zeros((CONV_HALO, CONV_DIM), F32)

    zbuf[CONV_HALO:CONV_HALO + ts, :] = z_ref[0]
    span = ts + CONV_HALO - SUBLANES
    for r in range(1, SUBLANES):
        zsh[r - 1] = zbuf[r:r + span, :]
    off = CONV_HALO - (CONV_WIDTH - 1)
    pieces = []
    for blk in range(ts // CONV_ROWS):
        acc = None
        for j in range(CONV_WIDTH):
            a, r = divmod(off + j, SUBLANES)
            lo = a * SUBLANES + blk * CONV_ROWS
            src = zbuf[lo:lo + CONV_ROWS, :] if r == 0 else zsh[r - 1, lo:lo + CONV_ROWS, :]
            term = src * jnp.concatenate([wdw_ref[j]] * (CONV_ROWS // SUBLANES), axis=0)
            acc = term if acc is None else acc + term
        conv = acc + bdw_ref[...]
        mu = jnp.mean(conv, axis=-1, keepdims=True)
        xc = conv - mu
        var = jnp.mean(xc * xc, axis=-1, keepdims=True)
        ln = xc * lax.rsqrt(var + EPS) * lng_ref[...] + lnb_ref[...]
        pieces.append((ln * _sigmoid(ln)).astype(BF16))
    zbuf[0:CONV_HALO, :] = zbuf[ts:ts + CONV_HALO, :]
    y_conv = _bdot(jnp.concatenate(pieces, axis=0), wpw_ref[...]) + bpw_ref[...]
    y_gla = yg_ref[0]
    merged = gt_ref[0, :, 0:d].astype(F32) * y_conv + gt_ref[0, :, d:2 * d].astype(F32) * y_gla
    y = _bdot(merged.astype(BF16), wout_ref[...])
    h = x_ref[0] + mod_ref[0, 2:3, :] * y
    h_ref[0] = h
    u2 = _rms(h, g2_ref[...]) * (1.0 + mod_ref[0, 4:5, :]) + mod_ref[0, 3:4, :]
    _pack_rows(u2_ref, u2, u2t)
    logit_ref[0] = _dot3(u2, wr_ref[...]) + br_ref[...]


def _merge(z, y_gla, gt, x, mod3, w_dw, b_dw, ln_g, ln_b, w_pw, b_pw, w_out, g2, w_rg, b_rg, w_re, b_re, ts):
    nb, s, d = x.shape
    npad = LANES - N_GROUPS - N_EXPERTS
    wr = jnp.pad(jnp.concatenate([w_rg, w_re], axis=1), ((0, 0), (0, npad)))
    br = jnp.pad(jnp.concatenate([b_rg, b_re]), (0, npad)).reshape(1, LANES)

    def tok(width):
        return pl.BlockSpec((1, ts, width), lambda b, i: (b, i, 0))

    def const(shape):
        return pl.BlockSpec(shape, lambda b, i: (0,) * len(shape))

    def row(v):
        return v.reshape(1, v.shape[-1])

    return pl.pallas_call(
        _merge_body,
        grid=(nb, s // ts),
        in_specs=[tok(CONV_DIM), tok(d), tok(2 * d), tok(d),
                  pl.BlockSpec((1, N_MOD, d), lambda b, i: (b, 0, 0)),
                  const((CONV_WIDTH, SUBLANES, CONV_DIM)), const((1, CONV_DIM)), const((1, CONV_DIM)), const((1, CONV_DIM)),
                  const((CONV_DIM, d)), const((1, d)), const((d, d)), const((1, d)),
                  const((d, LANES)), const((1, LANES))],
        out_specs=[tok(d), pl.BlockSpec((ts * PACK_ROWS, LANES), lambda b, i: (b * (s // ts) + i, 0)), tok(LANES)],
        out_shape=[jax.ShapeDtypeStruct((nb, s, d), F32),
                   jax.ShapeDtypeStruct((nb * s * PACK_ROWS, LANES), jnp.uint32),
                   jax.ShapeDtypeStruct((nb, s, LANES), F32)],
        scratch_shapes=[pltpu.VMEM((CONV_HALO + ts, CONV_DIM), F32),
                        pltpu.VMEM((SUBLANES - 1, CONV_HALO + ts - SUBLANES, CONV_DIM), F32),
                        pltpu.VMEM((ts * SUBLANES, LANES), F32)],
        compiler_params=_params("arbitrary", "arbitrary"),
        name="merge",
    )(z, y_gla, gt, x, mod3, jnp.broadcast_to(w_dw.reshape(CONV_WIDTH, 1, CONV_DIM), (CONV_WIDTH, SUBLANES, CONV_DIM)), row(b_dw), row(ln_g), row(ln_b),
      w_pw.astype(BF16), row(b_pw), w_out.astype(BF16), row(g2), wr, br)


def _route_body(logit_ref, route_ref, routet_ref, cnt_ref, cnt_sc, *, sub):
    @pl.when(pl.program_id(0) == 0)
    def _():
        cnt_sc[...] = jnp.zeros_like(cnt_sc)

    packed, cnt = _route(logit_ref[...], cnt_sc[...], sub)
    route_ref[...] = packed
    routet_ref[...] = packed.T[0:SUBLANES, :]
    cnt_sc[...] = cnt
    cnt_ref[...] = jnp.broadcast_to(cnt, cnt_ref.shape)


def _route_call(logits, tr, sub):
    t = logits.shape[0]
    return pl.pallas_call(
        functools.partial(_route_body, sub=sub),
        grid=(t // tr,),
        in_specs=[pl.BlockSpec((tr, LANES), lambda i: (i, 0))],
        out_specs=[pl.BlockSpec((tr, LANES), lambda i: (i, 0)), pl.BlockSpec((SUBLANES, tr), lambda i: (0, i)),
                   pl.BlockSpec((SUBLANES, LANES), lambda i: (0, 0))],
        out_shape=[jax.ShapeDtypeStruct((t, LANES), F32), jax.ShapeDtypeStruct((SUBLANES, t), F32),
                   jax.ShapeDtypeStruct((SUBLANES, LANES), F32)],
        scratch_shapes=[pltpu.VMEM((1, LANES), F32)],
        compiler_params=_params("arbitrary"),
        name="route",
    )(logits)


def _row_copy(src, i, dst, j, sem):
    return pltpu.make_async_copy(src.at[pl.ds(pl.multiple_of(i, SUBLANES), SUBLANES), :],
                                 dst.at[pl.ds(pl.multiple_of(j, SUBLANES), SUBLANES), :], sem)


def _invert_body(lo_ref, hi_ref, slot_ref, src_ref):
    i = pl.program_id(0)
    ts = slot_ref.shape[-1] // TOP_K

    @pl.when(i == 0)
    def _():
        for e in range(lo_ref.shape[0]):
            lo = lo_ref[e]
            hi = hi_ref[e]

            def clear(p, carry, lo=lo, hi=hi):
                for j in range(CLEAR_UNROLL):
                    src_ref[jnp.minimum(lo + p * CLEAR_UNROLL + j, hi - 1)] = 0
                return carry

            trips = lax.shift_right_logical(hi - lo + (CLEAR_UNROLL - 1), CLEAR_UNROLL.bit_length() - 1)
            lax.fori_loop(0, trips, clear, 0)

    for r in range(ts):
        row = (i * ts + r) * PACK_ROWS
        for k in range(TOP_K):
            src_ref[slot_ref[0, 0, k * ts + r]] = row


def _invert(fill_lo, fill_hi, slots, n_slots):
    nt, _, width = slots.shape
    return pl.pallas_call(
        _invert_body,
        grid_spec=pltpu.PrefetchScalarGridSpec(
            num_scalar_prefetch=2,
            grid=(nt,),
            in_specs=[pl.BlockSpec((1, 1, width), lambda i, lo, hi: (i, 0, 0), memory_space=pltpu.SMEM)],
            out_specs=pl.BlockSpec(memory_space=pltpu.SMEM)),
        out_shape=jax.ShapeDtypeStruct((n_slots,), jnp.int32),
        compiler_params=_params("arbitrary"),
        name="invert",
    )(fill_lo, fill_hi, slots)


def _expert_body(te_ref, nu_ref, seg_ref, nxt_ref, src_ref, u2_ref, w1_ref, w3_ref, w2_ref, ys_ref,
                 u2v, xg, xt, w1f, w3f, w2f, w1b, w3b, w2b, sem, wsem):
    i = pl.program_id(0)
    tmx = src_ref.shape[-1]
    expert = te_ref[i]
    first = i == 0
    changed = jnp.logical_or(first, expert != te_ref[jnp.maximum(i - 1, 0)])
    slot = lax.rem(seg_ref[i], 2)

    def weight_copies(e, s):
        return [pltpu.make_async_copy(src.at[e], dst.at[s], wsem.at[s])
                for src, dst in ((w1_ref, w1f), (w3_ref, w3f), (w2_ref, w2f))]

    load = pltpu.make_async_copy(u2_ref, u2v, sem)

    @pl.when(first)
    def _():
        load.start()
        for cp in weight_copies(expert, 0):
            cp.start()

    @pl.when(changed)
    def _():
        for cp in weight_copies(expert, slot):
            cp.wait()
        w1b[...] = w1f[slot].astype(BF16)
        w3b[...] = w3f[slot].astype(BF16)
        w2b[...] = w2f[slot].astype(BF16)

    @pl.when(jnp.logical_and(changed, nxt_ref[i] != expert))
    def _():
        for cp in weight_copies(nxt_ref[i], 1 - slot):
            cp.start()

    @pl.when(first)
    def _():
        load.wait()

    @pl.when(i < nu_ref[0])
    def _():
        for r in range(tmx):
            row = pl.multiple_of(src_ref[0, 0, r], PACK_ROWS)
            xg[r * PACK_ROWS:(r + 1) * PACK_ROWS, :] = u2v[pl.ds(row, PACK_ROWS), :]
        x = _unpack_rows(xg, tmx, xt)
        h1 = _bdot(x, w1b[...])
        h3 = _bdot(x, w3b[...])
        hid = (h1 * _sigmoid(h1) * h3).astype(BF16)
        _rows_to_tiles(ys_ref, _bdot(hid, w2b[...]))

    @pl.when(i >= nu_ref[0])
    def _():
        ys_ref[...] = jnp.zeros_like(ys_ref)


def _experts(tile_expert, n_used, run_index, next_expert, src_rows, u2p, w1, w3, w2):
    n_tiles, _, tmx = src_rows.shape
    ne, d, f = w1.shape
    hbm = pl.BlockSpec(memory_space=pl.ANY)
    return pl.pallas_call(
        _expert_body,
        grid_spec=pltpu.PrefetchScalarGridSpec(
            num_scalar_prefetch=4,
            grid=(n_tiles,),
            in_specs=[pl.BlockSpec((1, 1, tmx), lambda i, *_: (i, 0, 0), memory_space=pltpu.SMEM),
                      hbm, hbm, hbm, hbm],
            out_specs=pl.BlockSpec((tmx * SUBLANES, LANES), lambda i, *_: (i, 0)),
            scratch_shapes=[pltpu.VMEM(u2p.shape, jnp.uint32),
                            pltpu.VMEM((tmx * PACK_ROWS, LANES), jnp.uint32),
                            pltpu.VMEM((tmx * SUBLANES, LANES), F32),
                            pltpu.VMEM((2, d, f), F32), pltpu.VMEM((2, d, f), F32), pltpu.VMEM((2, f, d), F32),
                            pltpu.VMEM((d, f), BF16), pltpu.VMEM((d, f), BF16), pltpu.VMEM((f, d), BF16),
                            pltpu.SemaphoreType.DMA(()), pltpu.SemaphoreType.DMA((2,))]),
        out_shape=jax.ShapeDtypeStruct((n_tiles * tmx * SUBLANES, LANES), F32),
        compiler_params=_params("arbitrary"),
        name="experts",
    )(tile_expert, n_used, run_index, next_expert, src_rows, u2p, w1, w3, w2)


def _final_body(p_ref, pn_ref, h_ref, route_ref, mod_ref, modf_ref, gf_ref, ys_ref, o_ref, y_buf, sem):
    ts = h_ref.shape[1]
    step = pl.program_id(0) * pl.num_programs(1) + pl.program_id(1)
    n_steps = pl.num_programs(0) * pl.num_programs(1)
    cur = lax.rem(step, 2)

    def request_rows(ref, half):
        for k in range(TOP_K):
            for r in range(ts):
                _row_copy(ys_ref, ref[0, 0, k * ts + r], y_buf.at[half, k], r * SUBLANES,
                          sem.at[half]).start(priority=r % 2)

    @pl.when(step == 0)
    def _():
        request_rows(p_ref, 0)

    @pl.when(step + 1 < n_steps)
    def _():
        request_rows(pn_ref, 1 - cur)

    for k in range(TOP_K):
        pltpu.make_async_copy(ys_ref.at[pl.ds(0, ts * SUBLANES), :], y_buf.at[cur, k], sem.at[cur]).wait()

    route = route_ref[0]
    y2 = (route[:, 2:3] * _tiles_to_rows(y_buf.at[cur, 0], ts)
          + route[:, 3:4] * _tiles_to_rows(y_buf.at[cur, 1], ts))
    h = h_ref[0] + mod_ref[0, 5:6, :] * y2
    o_ref[0] = _rms(h, gf_ref[...]) * (1.0 + modf_ref[0, 1:2, :]) + modf_ref[0, 0:1, :]


def _final(slot_rows, h, route, mod3, modf3, gf, ys, ts):
    nb, s, d = h.shape
    nt = s // ts
    last = nb * nt - 1

    def tok(width):
        return pl.BlockSpec((1, ts, width), lambda b, i: (b, i, 0))

    def slots(ahead):
        return pl.BlockSpec((1, 1, TOP_K * ts), lambda b, i: (jnp.minimum(b * nt + i + ahead, last), 0, 0),
                            memory_space=pltpu.SMEM)

    return pl.pallas_call(
        _final_body,
        grid=(nb, nt),
        in_specs=[slots(0), slots(1), tok(d), tok(LANES),
                  pl.BlockSpec((1, N_MOD, d), lambda b, i: (b, 0, 0)),
                  pl.BlockSpec((1, 2, d), lambda b, i: (b, 0, 0)),
                  pl.BlockSpec((1, d), lambda b, i: (0, 0)),
                  pl.BlockSpec(memory_space=pl.ANY)],
        out_specs=tok(d),
        out_shape=jax.ShapeDtypeStruct((nb, s, d), F32),
        scratch_shapes=[pltpu.VMEM((2, TOP_K, ts * SUBLANES, LANES), F32), pltpu.SemaphoreType.DMA((2,))],
        compiler_params=_params("arbitrary", "arbitrary"),
        name="final",
    )(slot_rows, slot_rows, h, route, mod3, modf3, gf.reshape(1, d), ys)


def _plan(routet, cnt, ts, ti, tmx, n_tiles):
    t = routet.shape[1]
    counts = cnt[0, :N_EXPERTS].astype(jnp.int32)
    tiles = (counts + (tmx - 1)) // tmx
    tile_end = jnp.cumsum(tiles)
    offs = ((tile_end - tiles) * tmx).astype(jnp.int32)
    n_used = tile_end[-1:]
    tile_ids = jnp.minimum(jnp.arange(n_tiles, dtype=jnp.int32), n_used[0] - 1)
    tile_expert = jnp.sum((tile_ids[:, None] >= tile_end[None, :]).astype(jnp.int32), axis=1)
    run_index = jnp.cumsum(jnp.concatenate([jnp.zeros((1,), jnp.int32),
                                            (tile_expert[1:] != tile_expert[:-1]).astype(jnp.int32)]))
    same_run = run_index[:, None] + 1 == run_index[None, :]
    has_next = jnp.any(same_run, axis=1)
    next_expert = jnp.where(has_next, jnp.max(jnp.where(same_run, tile_expert[None, :], 0), axis=1), tile_expert)

    eid = routet[0:2].astype(jnp.int32)
    experts = jnp.arange(N_EXPERTS, dtype=jnp.int32)[:, None, None]
    slot = routet[4:6].astype(jnp.int32) + jnp.sum(jnp.where(eid[None] == experts, offs[:, None, None], 0), axis=0)
    def tiled(tile):
        return slot.reshape(TOP_K, t // tile, tile).transpose(1, 0, 2).reshape(t // tile, 1, TOP_K * tile)

    fill_lo = jnp.concatenate([offs + counts, tile_end[-1:] * tmx]).astype(jnp.int32)
    fill_hi = jnp.concatenate([tile_end * tmx, jnp.full((1,), n_tiles * tmx, jnp.int32)]).astype(jnp.int32)
    src = _invert(fill_lo, fill_hi, tiled(ti), n_tiles * tmx)
    tables = [a.astype(jnp.int32) for a in (tile_expert, n_used, run_index, next_expert)]
    return tiled(ts) * SUBLANES, src.reshape(n_tiles, 1, tmx), tables


def kernel(x, c, w_ada, b_ada, g_norm1, w_in, w_dw, b_dw, g_conv_ln, b_conv_ln, w_conv_pw, b_conv_pw,
           w_a2, b_a2, g_gla_norm, w_gla_o, w_out, g_norm2, w_router_g, b_router_g, w_router_e,
           b_router_e, w1, w3, w2, w_ada_f, b_ada_f, g_final):
    nb, s, d = x.shape
    assert w_ada.shape[0] == 1, "single-layer block"
    assert d == 2 * PACK_ROWS * LANES, "packed token rows assume D_MODEL = 1024"
    tm = min(512, s)
    tc = min(256, s)
    tg = min(512, s)
    ts = min(256, s)
    ti = min(1024, s)
    tmx = 256
    t = nb * s
    n_tiles = (t * TOP_K) // tmx + N_EXPERTS
    mod3 = _ada(c, w_ada[0], b_ada[0]).reshape(nb, N_MOD, d)
    modf3 = _ada(c, w_ada_f, b_ada_f).reshape(nb, 2, d)
    z, q, k, v, rs, lg, gt, lg_sums = _proj(x, mod3, g_norm1[0], w_in[0], w_a2[0], b_a2[0], tm)
    y_gla = _gla(q, k, lg, lg_sums, v, rs, g_gla_norm[0], w_gla_o[0], tc)
    h, u2, logits = _merge(z, y_gla, gt, x, mod3, w_dw[0], b_dw[0], g_conv_ln[0], b_conv_ln[0],
                           w_conv_pw[0], b_conv_pw[0], w_out[0], g_norm2[0],
                           w_router_g[0], b_router_g[0], w_router_e[0], b_router_e[0], tg)
    route, routet, cnt = _route_call(logits.reshape(t, LANES), min(ROUTE_ROWS, t), ts)
    slot_rows, src_rows, tables = _plan(routet, cnt, ts, ti, tmx, n_tiles)
    ys = _experts(*tables, src_rows, u2, w1[0], w3[0], w2[0])
    return _final(slot_rows, h, route.reshape(nb, s, LANES), mod3, modf3, g_final, ys, ts)
```
